```python
import math
import numpy as np
import jax
import jax.numpy as jnp
from jax import lax

D_MODEL = 1024
BATCH = 4
SEQ = 4096
DEPTH = 4

HEAD_DIM = 64
QBLK = 128
SPARSE_QCHUNK = 32
NEG_INF = -1e30
EPS = 1e-6

NSA_HEADS = 4
CMP_BLOCK = 32
CMP_STRIDE = 16
CMP_HIDDEN = 256
SEL_BLOCK = 64
SEL_TOPK = 8
NSA_WINDOW = 512
FORCE_BONUS = 1e4
SWA_HEADS = 4
SWA_KV_HEADS = 2
SWA_WINDOW = 128
MOBA_HEADS = 4
MOBA_BLOCK = 256
MOBA_TOPK = 3
MLA_HEADS = 4
MLA_Q_RANK = 384
MLA_KV_RANK = 128
MLA_NOPE = 64
MLA_ROPE = 32
MLA_V = 64
ROPE_THETA = 10000.0
N_BRANCH = 4
BRANCH_WIDTH = 256
N_GROUPS = 4
EXPERTS_PER_GROUP = 8
N_EXPERTS = N_GROUPS * EXPERTS_PER_GROUP
TOPK_IN_GROUP = 2
D_EXPERT = 256
MOE_BLOCK = 128

QK_NSA_Q = 0
QK_NSA_KC = 1
QK_NSA_KS = 2
QK_NSA_KW = 3
QK_SWA_Q = 4
QK_SWA_K = 5
QK_MOBA_Q = 6
QK_MOBA_K = 7
QK_MLA_Q = 8
QK_MLA_K = 9
N_QK = 10

IN_SIZES = (
    NSA_HEADS * HEAD_DIM,
    HEAD_DIM, HEAD_DIM,
    HEAD_DIM, HEAD_DIM,
    HEAD_DIM, HEAD_DIM,
    3 * NSA_HEADS,
    SWA_HEADS * HEAD_DIM,
    SWA_KV_HEADS * HEAD_DIM,
    SWA_KV_HEADS * HEAD_DIM,
    MOBA_HEADS * HEAD_DIM,
    MOBA_HEADS * HEAD_DIM,
    MOBA_HEADS * HEAD_DIM,
    MLA_Q_RANK,
    MLA_KV_RANK,
    MLA_ROPE,
    N_BRANCH * D_MODEL,
)
D_IN = sum(IN_SIZES)

kernel_name = 'hybrid_nsa_swa_moba_mla_hmoe_trunk'


def rms_norm(x, g):
    xf = x.astype(jnp.float32)
    y = xf * lax.rsqrt(jnp.mean(xf * xf, axis=-1, keepdims=True) + EPS)
    return (y * g.astype(jnp.float32)).astype(x.dtype)


def alibi_slopes():
    n = NSA_HEADS + SWA_HEADS + MOBA_HEADS
    def pow2(m):
        start = 2.0 ** (-8.0 / m)
        return [start ** (i + 1) for i in range(m)]
    c = 2 ** int(math.floor(math.log2(n)))
    s = pow2(c) + (pow2(2 * c)[0::2][: n - c] if c < n else [])
    s = -np.sort(-np.asarray(s, np.float32))
    return s.reshape(NSA_HEADS, 3).T


def to_chunks(a, n_chunks):
    return jnp.moveaxis(a.reshape(a.shape[0], n_chunks, -1, *a.shape[2:]), 1, 0)


def from_chunks(a):
    return jnp.moveaxis(a, 0, 1).reshape(a.shape[1], -1, *a.shape[3:])


def apply_rope(x, pos):
    half = x.shape[-1] // 2
    inv = ROPE_THETA ** (-jnp.arange(half, dtype=jnp.float32) / half)
    ang = pos.astype(jnp.float32)[:, None] * inv[None, :]
    cos = jnp.cos(ang)[:, None, :]
    sin = jnp.sin(ang)[:, None, :]
    xf = x.astype(jnp.float32)
    x1, x2 = xf[..., :half], xf[..., half:]
    return jnp.concatenate([x1 * cos - x2 * sin, x1 * sin + x2 * cos], axis=-1).astype(x.dtype)


def banded_attention(q, k, v, window, slopes, sinks):
    B, S, H, dh = q.shape
    G = k.shape[2]
    R = H // G
    f32 = jnp.float32
    nq = S // QBLK
    span = window + QBLK
    pad = ((0, 0), (window, 0), (0, 0), (0, 0))
    idx = (jnp.arange(nq) * QBLK)[:, None] + jnp.arange(span)[None, :]
    kb = jnp.pad(k, pad)[:, idx]
    vb = jnp.pad(v, pad)[:, idx]
    qb = q.reshape(B, nq, QBLK, G, R, dh)
    s = jnp.einsum('bnqgrd,bnkgd->bngrqk', qb, kb, preferred_element_type=f32) * (dh ** -0.5)
    tq = (jnp.arange(nq) * QBLK)[:, None] + jnp.arange(QBLK)[None, :]
    kpos = idx - window
    dist = tq[:, :, None] - kpos[:, None, :]
    valid = (dist >= 0) & (dist < window) & (kpos[:, None, :] >= 0)
    sl = slopes.reshape(G, R)[None, None, :, :, None, None]
    s = s - sl * dist.astype(f32)[None, :, None, None]
    s = jnp.where(valid[None, :, None, None], s, NEG_INF)
    if sinks is None:
        p = jax.nn.softmax(s, axis=-1)
    else:
        sk = sinks.astype(f32).reshape(G, R)[None, None, :, :, None, None]
        m = jnp.maximum(jnp.max(s, axis=-1, keepdims=True), sk)
        e = jnp.exp(s - m)
        p = e / (jnp.sum(e, axis=-1, keepdims=True) + jnp.exp(sk - m))
    out = jnp.einsum('bngrqk,bnkgd->bnqgrd', p.astype(v.dtype), vb)
    return out.reshape(B, S, H, dh)


def compress_blocks(kv, pe, w1, w2):
    B, S, dk = kv.shape
    n_cmp = (S - CMP_BLOCK) // CMP_STRIDE + 1
    idx = (jnp.arange(n_cmp) * CMP_STRIDE)[:, None] + jnp.arange(CMP_BLOCK)[None, :]
    blk = (kv[:, idx] + pe).reshape(B, n_cmp, CMP_BLOCK * dk)
    return jax.nn.silu(blk @ w1) @ w2


def selected_block_attention(q, k, v, sel_idx, sel_ok, slopes):
    B, S, H, dk = q.shape
    n = sel_idx.shape[-1]
    f32 = jnp.float32
    nch = S // SPARSE_QCHUNK
    kblk = k.reshape(B, S // SEL_BLOCK, SEL_BLOCK, dk)
    vblk = v.reshape(B, S // SEL_BLOCK, SEL_BLOCK, dk)
    bi = jnp.arange(B)[:, None, None]
    off = jnp.arange(SEL_BLOCK)

    def chunk(args):
        qi, ii, oi, ci = args
        kg = kblk[bi, ii]
        vg = vblk[bi, ii]
        s = jnp.einsum('bchd,bcnld->bchnl', qi, kg, preferred_element_type=f32) * (dk ** -0.5)
        t = ci * SPARSE_QCHUNK + jnp.arange(SPARSE_QCHUNK)
        dist = t[None, :, None, None] - (ii[..., None] * SEL_BLOCK + off)
        valid = oi[..., None] & (dist >= 0)
        s = s - slopes[None, None, :, None, None] * dist[:, :, None].astype(f32)
        s = jnp.where(valid[:, :, None], s, NEG_INF).reshape(B, SPARSE_QCHUNK, H, n * SEL_BLOCK)
        p = jax.nn.softmax(s, axis=-1).astype(v.dtype)
        return jnp.einsum('bchk,bckd->bchd', p, vg.reshape(B, SPARSE_QCHUNK, n * SEL_BLOCK, dk))

    out = lax.map(chunk, (to_chunks(q, nch), to_chunks(sel_idx, nch), to_chunks(sel_ok, nch), jnp.arange(nch)))
    return from_chunks(out)


def nsa_attention(q, kc_raw, vc_raw, ks, vs, kw, vw, gate_logit, g_kc, pe, w1, w2, slopes):
    B, S, H, dk = q.shape
    f32 = jnp.float32
    t = jnp.arange(S)
    kc = rms_norm(compress_blocks(kc_raw, pe[0], w1[0], w2[0]), g_kc)
    vc = compress_blocks(vc_raw, pe[1], w1[1], w2[1])
    n_cmp = kc.shape[1]
    c_end = jnp.arange(n_cmp) * CMP_STRIDE + CMP_BLOCK - 1
    dist = t[:, None] - c_end[None, :]
    vis = dist >= 0
    s = jnp.einsum('bshd,bnd->bhsn', q, kc, preferred_element_type=f32) * (dk ** -0.5)
    s = s - slopes[None, :, None, None] * dist.astype(f32)
    s = jnp.where(vis, s, NEG_INF)
    e = jnp.exp(s - jnp.max(s, axis=-1, keepdims=True)) * vis
    p_cmp = e / jnp.maximum(jnp.sum(e, axis=-1, keepdims=True), 1e-30)
    o_cmp = jnp.einsum('bhsn,bnd->bshd', p_cmp.astype(vc.dtype), vc)
    n_sel = S // SEL_BLOCK
    starts = np.arange(n_cmp) * CMP_STRIDE
    jb = np.arange(n_sel) * SEL_BLOCK
    cover = ((starts[:, None] < jb[None, :] + SEL_BLOCK) & (starts[:, None] + CMP_BLOCK > jb[None, :])).astype(np.float32)
    p_slc = jnp.einsum('bhsn,nj->bsj', p_cmp, jnp.asarray(cover))
    cur = (t // SEL_BLOCK)[:, None]
    j = jnp.arange(n_sel)[None, :]
    forced = (j == 0) | (j == cur) | (j == cur - 1)
    score = jnp.where(j <= cur, p_slc + FORCE_BONUS * forced.astype(f32), NEG_INF)
    top_val, top_idx = lax.top_k(score, min(SEL_TOPK, n_sel))
    o_slc = selected_block_attention(q, ks, vs, top_idx, top_val > 0.5 * NEG_INF, slopes)
    o_win = banded_attention(q, kw[:, :, None], vw[:, :, None], NSA_WINDOW, slopes, None)
    g = jax.nn.sigmoid(gate_logit.astype(f32)).reshape(B, S, H, 3).astype(q.dtype)
    return g[..., 0:1] * o_cmp + g[..., 1:2] * o_slc + g[..., 2:3] * o_win


def moba_attention(q, k, v, slopes):
    B, S, H, dh = q.shape
    f32 = jnp.float32
    L = MOBA_BLOCK
    nb = -(-S // L)
    pad = ((0, 0), (0, nb * L - S), (0, 0), (0, 0))
    kp = jnp.pad(k, pad)
    vp = jnp.pad(v, pad)
    kblk = kp.reshape(B, nb, L, H, dh)
    kmean = jnp.mean(kblk.astype(f32), axis=2)
    t = jnp.arange(S)
    gs = jnp.einsum('bshd,bnhd->bshn', q.astype(f32), kmean)
    past = jnp.arange(nb)[None, :] < (t // L)[:, None]
    gs = jnp.where(past[None, :, None, :], gs, NEG_INF)
    kk = min(MOBA_TOPK, nb)
    top_val, top_idx = lax.top_k(gs, kk)
    top_ok = top_val > 0.5 * NEG_INF
    kbh = jnp.transpose(kblk, (0, 3, 1, 2, 4))
    vbh = jnp.transpose(vp.reshape(B, nb, L, H, dh), (0, 3, 1, 2, 4))
    bi = jnp.arange(B)[:, None, None, None]
    hi = jnp.arange(H)[None, None, :, None]
    off = jnp.arange(L)
    scale = dh ** -0.5
    C = SPARSE_QCHUNK
    nch = S // C

    def chunk(args):
        qi, ii, oi, ci = args
        tc = ci * C + jnp.arange(C)
        kg = kbh[bi, hi, ii]
        vg = vbh[bi, hi, ii]
        s_past = jnp.einsum('bchd,bchild->bchil', qi, kg, preferred_element_type=f32) * scale
        d_past = tc[None, :, None, None, None] - (ii[..., None] * L + off)
        s_past = s_past - slopes[None, None, :, None, None] * d_past.astype(f32)
        s_past = jnp.where(oi[..., None], s_past, NEG_INF).reshape(B, C, H, kk * L)
        start = (ci * C // L) * L
        ko = lax.dynamic_slice_in_dim(kp, start, L, axis=1)
        vo = lax.dynamic_slice_in_dim(vp, start, L, axis=1)
        s_own = jnp.einsum('bchd,blhd->bchl', qi, ko, preferred_element_type=f32) * scale
        d_own = tc[:, None] - (start + off)[None, :]
        s_own = s_own - slopes[None, None, :, None] * d_own.astype(f32)[None, :, None, :]
        s_own = jnp.where((d_own >= 0)[None, :, None, :], s_own, NEG_INF)
        p = jax.nn.softmax(jnp.concatenate([s_past, s_own], axis=-1), axis=-1).astype(v.dtype)
        o_past = jnp.einsum('bchk,bchkd->bchd', p[..., : kk * L], vg.reshape(B, C, H, kk * L, dh))
        o_own = jnp.einsum('bchl,blhd->bchd', p[..., kk * L:], vo)
        return o_past + o_own

    out = lax.map(chunk, (to_chunks(q, nch), to_chunks(top_idx, nch), to_chunks(top_ok, nch), jnp.arange(nch)))
    return from_chunks(out)


def causal_attention(q, k, v):
    B, S, H, dq = q.shape
    nq = S // QBLK
    kpos = jnp.arange(S)

    def block(args):
        qi, bidx = args
        s = jnp.einsum('bqhd,bkhd->bhqk', qi, k, preferred_element_type=jnp.float32) * (dq ** -0.5)
        tq = bidx * QBLK + jnp.arange(QBLK)
        s = jnp.where((kpos[None, :] <= tq[:, None])[None, None], s, NEG_INF)
        p = jax.nn.softmax(s, axis=-1).astype(v.dtype)
        return jnp.einsum('bhqk,bkhd->bqhd', p, v)

    out = lax.map(block, (to_chunks(q, nq), jnp.arange(nq)))
    return from_chunks(out)


def mla_attention(cq, ckv, k_rope, g_cq, g_ckv, w_uq, w_ukv, g_qn, g_kn, g_rope, pos):
    B, S, _ = cq.shape
    q = (rms_norm(cq, g_cq) @ w_uq).reshape(B, S, MLA_HEADS, MLA_NOPE + MLA_ROPE)
    kv = (rms_norm(ckv, g_ckv) @ w_ukv).reshape(B, S, MLA_HEADS, MLA_NOPE + MLA_V)
    q_nope = rms_norm(q[..., :MLA_NOPE], g_qn)
    q_rot = apply_rope(rms_norm(q[..., MLA_NOPE:], g_rope[0]), pos)
    k_nope = rms_norm(kv[..., :MLA_NOPE], g_kn)
    v = kv[..., MLA_NOPE:]
    k_rot = apply_rope(rms_norm(k_rope, g_rope[1])[:, :, None, :], pos)
    qf = jnp.concatenate([q_nope, q_rot], axis=-1)
    kf = jnp.concatenate([k_nope, jnp.broadcast_to(k_rot, (B, S, MLA_HEADS, MLA_ROPE))], axis=-1)
    return causal_attention(qf, kf, v)


def routed_experts(xf, eid, wts, w13, w2):
    N, D = xf.shape
    K = eid.shape[1]
    E = w13.shape[0]
    flat_e = eid.reshape(-1)
    order = jnp.argsort(flat_e)
    se = flat_e[order]
    counts = jnp.bincount(flat_e, length=E)
    padded = (counts + MOE_BLOCK - 1) // MOE_BLOCK * MOE_BLOCK
    pend = jnp.cumsum(padded)
    pstart = pend - padded
    start = jnp.cumsum(counts) - counts
    dest = pstart[se] + jnp.arange(N * K) - start[se]
    n_blocks = -(-(N * K) // MOE_BLOCK) + E
    R = n_blocks * MOE_BLOCK
    row_tok = jnp.full((R,), N, jnp.int32).at[dest].set((order // K).astype(jnp.int32))
    row_w = jnp.zeros((R,), xf.dtype).at[dest].set(wts.reshape(-1)[order])
    blk_e = jnp.minimum(jnp.searchsorted(pend, jnp.arange(n_blocks) * MOE_BLOCK, side='right'), E - 1)
    xpad = jnp.concatenate([xf, jnp.zeros((1, D), xf.dtype)], axis=0)
    xb = xpad[row_tok].reshape(n_blocks, MOE_BLOCK, D)

    def block(args):
        xi, e = args
        a, b = jnp.split(xi @ w13[e], 2, axis=-1)
        return (jax.nn.silu(a) * b) @ w2[e]

    yb = lax.map(block, (xb, blk_e)).reshape(R, D)
    y = jnp.zeros((N + 1, D), xf.dtype).at[row_tok].add(yb * row_w[:, None])
    return y[:N]


def hier_moe(h, w_coarse, b_coarse, w_fine, b_fine, w13, w2):
    B, S, D = h.shape
    N = B * S
    f32 = jnp.float32
    xf = h.reshape(N, D)
    lc = (xf @ w_coarse).astype(f32) + b_coarse.astype(f32)
    grp = jnp.argmax(lc, axis=-1)
    p_grp = jnp.take_along_axis(jax.nn.softmax(lc, axis=-1), grp[:, None], axis=-1)
    lf = ((xf @ w_fine).astype(f32) + b_fine.astype(f32)).reshape(N, N_GROUPS, EXPERTS_PER_GROUP)
    lf = jnp.take_along_axis(lf, grp[:, None, None], axis=1)[:, 0]
    top_p, top_e = lax.top_k(jax.nn.softmax(lf, axis=-1), TOPK_IN_GROUP)
    wts = p_grp * top_p / jnp.sum(top_p, axis=-1, keepdims=True)
    eid = (grp[:, None] * EXPERTS_PER_GROUP + top_e).astype(jnp.int32)
    return routed_experts(xf, eid, wts.astype(h.dtype), w13, w2).reshape(B, S, D)


def setup_inputs(seed: int = 0) -> dict:
    key = jax.random.key(seed)
    ks = jax.random.split(key, 24)
    f32 = jnp.float32
    L = DEPTH
    D = D_MODEL

    def nrm(k, shape, fan_in, s=1.0):
        return (s * fan_in ** -0.5) * jax.random.normal(k, shape, f32)

    def gain(k, shape):
        return 1.0 + 0.05 * jax.random.normal(k, shape, f32)

    return {
        'x': jax.random.normal(ks[0], (BATCH, SEQ, D), f32),
        'c': jax.random.normal(ks[1], (BATCH, D), f32),
        'w_ada': nrm(ks[2], (L, D, 6 * D), D, 0.5),
        'b_ada': 0.02 * jax.random.normal(ks[3], (L, 6 * D), f32),
        'norm_gain': gain(ks[4], (L, 2, D)),
        'w_in': nrm(ks[5], (L, D, D_IN), D),
        'qk_gain': gain(ks[6], (L, N_QK, HEAD_DIM)),
        'cmp_pe': 0.02 * jax.random.normal(ks[7], (L, 2, CMP_BLOCK, HEAD_DIM), f32),
        'cmp_w1': nrm(ks[8], (L, 2, CMP_BLOCK * HEAD_DIM, CMP_HIDDEN), CMP_BLOCK * HEAD_DIM),
        'cmp_w2': nrm(ks[9], (L, 2, CMP_HIDDEN, HEAD_DIM), CMP_HIDDEN),
        'swa_sinks': jax.random.normal(ks[10], (L, SWA_HEADS), f32),
        'lat_gain_q': gain(ks[11], (L, MLA_Q_RANK)),
        'lat_gain_kv': gain(ks[12], (L, MLA_KV_RANK)),
        'rope_gain': gain(ks[13], (L, 2, MLA_ROPE)),
        'w_uq': nrm(ks[14], (L, MLA_Q_RANK, MLA_HEADS * (MLA_NOPE + MLA_ROPE)), MLA_Q_RANK),
        'w_ukv': nrm(ks[15], (L, MLA_KV_RANK, MLA_HEADS * (MLA_NOPE + MLA_V)), MLA_KV_RANK),
        'w_branch': nrm(ks[16], (L, N_BRANCH, BRANCH_WIDTH, D), BRANCH_WIDTH),
        'w_out': nrm(ks[17], (L, D, D), D),
        'w_coarse': nrm(ks[18], (L, D, N_GROUPS), D),
        'b_coarse': 0.01 * jax.random.normal(ks[19], (L, N_GROUPS), f32),
        'w_fine': nrm(ks[20], (L, D, N_EXPERTS), D),
        'b_fine': 0.01 * jax.random.normal(ks[21], (L, N_EXPERTS), f32),
        'w13': nrm(ks[22], (L, N_EXPERTS, D, 2 * D_EXPERT), D),
        'w2': nrm(ks[23], (L, N_EXPERTS, D_EXPERT, D), D_EXPERT),
    }


def reference(x, c, w_ada, b_ada, norm_gain, w_in, qk_gain, cmp_pe, cmp_w1, cmp_w2, swa_sinks,
              lat_gain_q, lat_gain_kv, rope_gain, w_uq, w_ukv, w_branch, w_out,
              w_coarse, b_coarse, w_fine, b_fine, w13, w2):
    B, S, D = x.shape
    slopes = jnp.asarray(alibi_slopes())
    pos = jnp.arange(S)
    split_at = np.cumsum(IN_SIZES)[:-1].tolist()
    for l in range(DEPTH):
        mod = jax.nn.silu(c) @ w_ada[l] + b_ada[l]
        sh1, sc1, g1, sh2, sc2, g2 = jnp.split(mod[:, None, :], 6, axis=-1)
        h = rms_norm(x, norm_gain[l, 0]) * (1.0 + sc1) + sh1
        (q_a, kc_a, vc_a, ks_a, vs_a, kw_a, vw_a, gl_a,
         q_b, k_b, v_b, q_c, k_c, v_c, cq_d, ckv_d, kr_d, gate_in) = jnp.split(h @ w_in[l], split_at, axis=-1)
        gq = qk_gain[l]
        o_a = nsa_attention(rms_norm(q_a.reshape(B, S, NSA_HEADS, HEAD_DIM), gq[QK_NSA_Q]),
                            kc_a, vc_a, rms_norm(ks_a, gq[QK_NSA_KS]), vs_a,
                            rms_norm(kw_a, gq[QK_NSA_KW]), vw_a, gl_a, gq[QK_NSA_KC],
                            cmp_pe[l], cmp_w1[l], cmp_w2[l], slopes[0])
        o_b = banded_attention(rms_norm(q_b.reshape(B, S, SWA_HEADS, HEAD_DIM), gq[QK_SWA_Q]),
                               rms_norm(k_b.reshape(B, S, SWA_KV_HEADS, HEAD_DIM), gq[QK_SWA_K]),
                               v_b.reshape(B, S, SWA_KV_HEADS, HEAD_DIM), SWA_WINDOW, slopes[1], swa_sinks[l])
        o_c = moba_attention(rms_norm(q_c.reshape(B, S, MOBA_HEADS, HEAD_DIM), gq[QK_MOBA_Q]),
                             rms_norm(k_c.reshape(B, S, MOBA_HEADS, HEAD_DIM), gq[QK_MOBA_K]),
                             v_c.reshape(B, S, MOBA_HEADS, HEAD_DIM), slopes[2])
        o_d = mla_attention(cq_d, ckv_d, kr_d, lat_gain_q[l], lat_gain_kv[l], w_uq[l], w_ukv[l],
                            gq[QK_MLA_Q], gq[QK_MLA_K], rope_gain[l], pos)
        branches = jnp.stack([o_a.reshape(B, S, BRANCH_WIDTH), o_b.reshape(B, S, BRANCH_WIDTH),
                              o_c.reshape(B, S, BRANCH_WIDTH), o_d.reshape(B, S, BRANCH_WIDTH)], axis=2)
        proj = jnp.einsum('bsnw,nwd->bsnd', branches, w_branch[l])
        gates = jax.nn.sigmoid(gate_in.reshape(B, S, N_BRANCH, D))
        mixed = jnp.sum(gates * proj, axis=2) @ w_out[l]
        x = x + g1 * mixed
        h2 = rms_norm(x, norm_gain[l, 1]) * (1.0 + sc2) + sh2
        x = x + g2 * hier_moe(h2, w_coarse[l], b_coarse[l], w_fine[l], b_fine[l], w13[l], w2[l])
    return x
```

```python
import functools
import math

import numpy as np
import jax
import jax.numpy as jnp
from jax import lax
from jax.experimental import pallas as pl
from jax.experimental.pallas import tpu as pltpu

F32 = jnp.float32
BF16 = jnp.bfloat16

HEAD_DIM = 64
NEG_INF = -1e30
EPS = 1e-6
NSA_HEADS = 4
CMP_BLOCK = 32
CMP_STRIDE = 16
CMP_HIDDEN = 256
SEL_BLOCK = 64
SEL_TOPK = 8
NSA_WINDOW = 512
FORCE_BONUS = 1e4
SWA_HEADS = 4
SWA_KV_HEADS = 2
SWA_WINDOW = 128
MOBA_HEADS = 4
MOBA_BLOCK = 256
MOBA_TOPK = 3
MLA_HEADS = 4
MLA_Q_RANK = 384
MLA_KV_RANK = 128
MLA_NOPE = 64
MLA_ROPE = 32
MLA_V = 64
ROPE_THETA = 10000.0
N_BRANCH = 4
BRANCH_WIDTH = 256
N_GROUPS = 4
EXPERTS_PER_GROUP = 8
N_EXPERTS = N_GROUPS * EXPERTS_PER_GROUP
D_EXPERT = 256

QK_NSA_Q, QK_NSA_KC, QK_NSA_KS, QK_NSA_KW = 0, 1, 2, 3
QK_SWA_Q, QK_SWA_K, QK_MOBA_Q, QK_MOBA_K, QK_MLA_Q, QK_MLA_K = 4, 5, 6, 7, 8, 9

ATTN_OLD = 2476
ATTN_COLS = 2560
GATE_LANE0 = 32
EXPERT_ROWS = 256
VMEM_LIMIT = 56 * 1024 * 1024


def _alibi_slopes():
    n = NSA_HEADS + SWA_HEADS + MOBA_HEADS

    def pow2(m):
        start = 2.0 ** (-8.0 / m)
        return [start ** (i + 1) for i in range(m)]

    c = 2 ** int(math.floor(math.log2(n)))
    s = pow2(c) + (pow2(2 * c)[0::2][: n - c] if c < n else [])
    s = -np.sort(-np.asarray(s, np.float32))
    return s.reshape(NSA_HEADS, 3).T


def _dot(a, b):
    return jnp.dot(a, b, preferred_element_type=F32)


def _dot_nt(a, b):
    return lax.dot_general(a, b, (((1,), (1,)), ((), ())), preferred_element_type=F32)


def _split(a):
    hi = a.astype(BF16)
    lo = (a - hi.astype(F32)).astype(BF16)
    return hi, lo


def _dot3(a, b):
    ah, al = _split(a)
    bh, bl = _split(b)
    return _dot(ah, bh) + (_dot(ah, bl) + _dot(al, bh))


def _dot3_nt(a, b):
    ah, al = _split(a)
    bh, bl = _split(b)
    return _dot_nt(ah, bh) + (_dot_nt(ah, bl) + _dot_nt(al, bh))


def _rms(x, g):
    return x * lax.rsqrt(jnp.mean(x * x, axis=-1, keepdims=True) + EPS) * g


def _sigmoid(x):
    return 1.0 / (1.0 + jnp.exp(-x))


def _topk_mask(score, k):
    n = score.shape[-1]
    iota = lax.broadcasted_iota(jnp.int32, score.shape, 1).astype(F32)
    sel = jnp.zeros(score.shape, F32)
    for _ in range(k):
        m = jnp.max(score, axis=-1, keepdims=True)
        idx = jnp.min(jnp.where(score == m, iota, float(n)), axis=-1, keepdims=True)
        hit = iota == idx
        sel = jnp.where(hit, jnp.where(m > 0.5 * NEG_INF, 1.0, 0.0), sel)
        score = jnp.where(hit, -3e38, score)
    return sel


def _ada_kernel(c_ref, w_ref, b_ref, o_ref):
    c = c_ref[...]
    a = c * _sigmoid(c)
    o_ref[0] = _dot(a, w_ref[0]) + b_ref[0]


def _ada_mod(c, w_ada, b_ada):
    L, D, D6 = w_ada.shape
    B = c.shape[0]
    tn = 1024
    return pl.pallas_call(
        _ada_kernel,
        grid=(L, D6 // tn),
        in_specs=[
            pl.BlockSpec((B, D), lambda l, j: (0, 0)),
            pl.BlockSpec((1, D, tn), lambda l, j: (l, 0, j)),
            pl.BlockSpec((1, 1, tn), lambda l, j: (l, 0, j)),
        ],
        out_specs=pl.BlockSpec((1, B, tn), lambda l, j: (l, 0, j)),
        out_shape=jax.ShapeDtypeStruct((L, B, D6), F32),
        compiler_params=pltpu.CompilerParams(vmem_limit_bytes=VMEM_LIMIT),
        name="ada_mod",
    )(c, w_ada, b_ada.reshape(L, 1, D6))


def _proj_kernel(x_ref, sc_ref, sh_ref, ng_ref, w_ref, qkg_ref, lgq_ref, lgkv_ref, rg_ref,
                 wuq_ref, wukv_ref, cos_ref, sin_ref,
                 qa_ref, kcr_ref, vcr_ref, ks_ref, vs_ref, kw_ref, vw_ref, gs_ref,
                 qb_ref, kb_ref, vb_ref, qc_ref, kc_ref, vc_ref, mc_ref, qd_ref, kd_ref, vd_ref,
                 kmean_s):
    i = pl.program_id(1)
    hd = HEAD_DIM
    x = x_ref[0]
    h = _rms(x, ng_ref[...]) * (1.0 + sc_ref[0]) + sh_ref[0]
    hb = h.astype(BF16)
    gq = qkg_ref[...]

    def gain(slot):
        return gq[slot:slot + 1]

    def proj(a, b):
        return _dot(hb, w_ref[:, a:b])

    scale = hd ** -0.5

    p = proj(0, 256)
    for hh in range(NSA_HEADS):
        qa_ref[0, hh] = (_rms(p[:, hh * hd:(hh + 1) * hd], gain(QK_NSA_Q)) * scale).astype(BF16)
    p = proj(256, 640)
    kcr_ref[0] = p[:, 0:64]
    vcr_ref[0] = p[:, 64:128]
    ks_ref[0, 0] = _rms(p[:, 128:192], gain(QK_NSA_KS)).astype(BF16)
    vs_ref[0, 0] = p[:, 192:256].astype(BF16)
    kw_ref[0, 0] = _rms(p[:, 256:320], gain(QK_NSA_KW)).astype(BF16)
    vw_ref[0, 0] = p[:, 320:384].astype(BF16)

    p = proj(640, 1152)
    for hh in range(SWA_HEADS):
        qb_ref[0, hh] = (_rms(p[:, hh * hd:(hh + 1) * hd], gain(QK_SWA_Q)) * scale).astype(BF16)
    for gg in range(SWA_KV_HEADS):
        kb_ref[0, gg] = _rms(p[:, 256 + gg * hd:256 + (gg + 1) * hd], gain(QK_SWA_K)).astype(BF16)
        vb_ref[0, gg] = p[:, 384 + gg * hd:384 + (gg + 1) * hd].astype(BF16)

    @pl.when(i == 0)
    def _():
        kmean_s[...] = jnp.zeros(kmean_s.shape, F32)

    p = proj(1152, 1920)
    nblk = kmean_s.shape[0]
    blk_iota = lax.broadcasted_iota(jnp.int32, (1, nblk), 1)
    kmeans = []
    for hh in range(MOBA_HEADS):
        qn = _rms(p[:, hh * hd:(hh + 1) * hd], gain(QK_MOBA_Q))
        kn = _rms(p[:, 256 + hh * hd:256 + (hh + 1) * hd], gain(QK_MOBA_K))
        qc_ref[0, hh] = (qn * scale).astype(BF16)
        kc_ref[0, hh] = kn.astype(BF16)
        vc_ref[0, hh] = p[:, 512 + hh * hd:512 + (hh + 1) * hd].astype(BF16)
        kmeans.append(jnp.mean(kn, axis=0, keepdims=True))
        g = _dot3_nt(qn, kmean_s[:, hh * hd:(hh + 1) * hd])
        g = jnp.where(blk_iota < i, g, NEG_INF)
        mc_ref[0, hh] = jnp.where(blk_iota == i, 1.0, _topk_mask(g, MOBA_TOPK))
    kmean_s[pl.ds(i, 1), :] = jnp.concatenate(kmeans, axis=1)

    p = proj(1920, 2560)
    gs_ref[0] = _sigmoid(p[:, 512:640])
    cos = cos_ref[...]
    sin = sin_ref[...]
    half = MLA_ROPE // 2

    def rope(v):
        v1 = v[:, :half]
        v2 = v[:, half:]
        return jnp.concatenate([v1 * cos - v2 * sin, v1 * sin + v2 * cos], axis=1)

    rg = rg_ref[...]
    qlat = _dot(_rms(p[:, 0:384], lgq_ref[...]).astype(BF16), wuq_ref[...])
    kvlat = _dot(_rms(p[:, 384:512], lgkv_ref[...]).astype(BF16), wukv_ref[...])
    k_rot = rope(_rms(p[:, 512:544], rg[1:2]))
    dq = MLA_NOPE + MLA_ROPE
    dkv = MLA_NOPE + MLA_V
    mla_scale = dq ** -0.5
    for hh in range(MLA_HEADS):
        qh = qlat[:, hh * dq:(hh + 1) * dq]
        q_nope = _rms(qh[:, :MLA_NOPE], gain(QK_MLA_Q))
        q_rot = rope(_rms(qh[:, MLA_NOPE:], rg[0:1]))
        qd_ref[0, hh] = (jnp.concatenate([q_nope, q_rot], axis=1) * mla_scale).astype(BF16)
        kvh = kvlat[:, hh * dkv:(hh + 1) * dkv]
        k_nope = _rms(kvh[:, :MLA_NOPE], gain(QK_MLA_K))
        kd_ref[0, hh] = jnp.concatenate([k_nope, k_rot], axis=1).astype(BF16)
        vd_ref[0, hh] = kvh[:, MLA_NOPE:].astype(BF16)


def _proj_call(x, sc1, sh1, ng, w_attn, qkg, lgq, lgkv, rg, wuq, wukv, cos, sin):
    B, S, D = x.shape
    tm = MOBA_BLOCK
    nblk = S // tm
    hd = HEAD_DIM

    def full(shape):
        return pl.BlockSpec(shape, lambda b, i: (0,) * len(shape))

    def heads(nh, d):
        return pl.BlockSpec((1, nh, tm, d), lambda b, i: (b, 0, i, 0))

    in_specs = [
        pl.BlockSpec((1, tm, D), lambda b, i: (b, i, 0)),
        pl.BlockSpec((1, 1, D), lambda b, i: (b, 0, 0)),
        pl.BlockSpec((1, 1, D), lambda b, i: (b, 0, 0)),
        full((1, D)),
        full((D, ATTN_COLS)),
        full(qkg.shape),
        full(lgq.shape),
        full(lgkv.shape),
        full(rg.shape),
        full(wuq.shape),
        full(wukv.shape),
        pl.BlockSpec((tm, MLA_ROPE // 2), lambda b, i: (i, 0)),
        pl.BlockSpec((tm, MLA_ROPE // 2), lambda b, i: (i, 0)),
    ]
    row64 = pl.BlockSpec((1, tm, hd), lambda b, i: (b, i, 0))
    out_specs = [
        heads(4, hd), row64, row64, heads(1, hd), heads(1, hd), heads(1, hd), heads(1, hd),
        pl.BlockSpec((1, tm, 128), lambda b, i: (b, i, 0)),
        heads(4, hd), heads(2, hd), heads(2, hd),
        heads(4, hd), heads(4, hd), heads(4, hd), heads(4, nblk),
        heads(4, MLA_NOPE + MLA_ROPE), heads(4, MLA_NOPE + MLA_ROPE), heads(4, MLA_V),
    ]

    def sd(shape, dt):
        return jax.ShapeDtypeStruct(shape, dt)

    out_shape = [
        sd((B, 4, S, hd), BF16), sd((B, S, hd), F32), sd((B, S, hd), F32),
        sd((B, 1, S, hd), BF16), sd((B, 1, S, hd), BF16), sd((B, 1, S, hd), BF16), sd((B, 1, S, hd), BF16),
        sd((B, S, 128), F32),
        sd((B, 4, S, hd), BF16), sd((B, 2, S, hd), BF16), sd((B, 2, S, hd), BF16),
        sd((B, 4, S, hd), BF16), sd((B, 4, S, hd), BF16), sd((B, 4, S, hd), BF16), sd((B, 4, S, nblk), F32),
        sd((B, 4, S, 96), BF16), sd((B, 4, S, 96), BF16), sd((B, 4, S, MLA_V), BF16),
    ]
    return pl.pallas_call(
        _proj_kernel,
        grid=(B, nblk),
        in_specs=in_specs,
        out_specs=out_specs,
        out_shape=out_shape,
        scratch_shapes=[pltpu.VMEM((nblk, MOBA_HEADS * hd), F32)],
        compiler_params=pltpu.CompilerParams(
            dimension_semantics=("arbitrary", "arbitrary"), vmem_limit_bytes=VMEM_LIMIT),
        name="proj_prep",
    )(x, sc1, sh1, ng, w_attn, qkg, lgq, lgkv, rg, wuq, wukv, cos, sin)


def _compress_kernel(gk_ref, gv_ref, pe_ref, w1_ref, w2_ref, gkc_ref, kc_ref, vc_ref):
    half = w1_ref.shape[1] // 2
    outs = []
    for j, g_ref in enumerate((gk_ref, gv_ref)):
        g = g_ref[0].astype(BF16)
        top = _dot(g, w1_ref[j, :half].astype(BF16))
        bot = _dot(g, w1_ref[j, half:].astype(BF16))
        bot = jnp.concatenate([bot[1:], bot[:1]], axis=0)
        pe = jnp.broadcast_to(pe_ref[j], (8, pe_ref.shape[2]))
        bias = _dot3(pe, w1_ref[j])[0:1]
        hid = top + bot + bias
        hid = hid * _sigmoid(hid)
        outs.append(_dot(hid.astype(BF16), w2_ref[j].astype(BF16)))
    kc_ref[0] = _rms(outs[0], gkc_ref[...]).astype(BF16)
    vc_ref[0] = outs[1].astype(BF16)


def _compress_call(kc_raw, vc_raw, pe, w1, w2, g_kc):
    B, S, dk = kc_raw.shape
    n_grp = S // CMP_STRIDE
    gk = kc_raw.reshape(B, n_grp, CMP_STRIDE * dk)
    gv = vc_raw.reshape(B, n_grp, CMP_STRIDE * dk)
    pe_flat = pe.reshape(2, 1, CMP_BLOCK * dk)
    grp_spec = pl.BlockSpec((1, n_grp, CMP_STRIDE * dk), lambda b: (b, 0, 0))
    out_spec = pl.BlockSpec((1, n_grp, dk), lambda b: (b, 0, 0))
    return pl.pallas_call(
        _compress_kernel,
        grid=(B,),
        in_specs=[
            grp_spec, grp_spec,
            pl.BlockSpec(pe_flat.shape, lambda b: (0, 0, 0)),
            pl.BlockSpec(w1.shape, lambda b: (0, 0, 0)),
            pl.BlockSpec(w2.shape, lambda b: (0, 0, 0)),
            pl.BlockSpec((1, dk), lambda b: (0, 0)),
        ],
        out_specs=[out_spec, out_spec],
        out_shape=[jax.ShapeDtypeStruct((B, n_grp, dk), BF16)] * 2,
        compiler_params=pltpu.CompilerParams(vmem_limit_bytes=VMEM_LIMIT),
        name="nsa_compress",
    )(gk, gv, pe_flat, w1, w2, g_kc)


def _cmp_attn_kernel(q_ref, kc_ref, vc_ref, cover_ref, o_ref, sel_ref, *, slopes, tq, n_cmp):
    i = pl.program_id(1)
    kc = kc_ref[0]
    vc = vc_ref[0]
    ncp = kc.shape[0]
    t = (i * tq + lax.broadcasted_iota(jnp.int32, (tq, 1), 0))
    n_iota = lax.broadcasted_iota(jnp.int32, (1, ncp), 1)
    dist_i = t - (n_iota * CMP_STRIDE + (CMP_BLOCK - 1))
    vis = (n_iota < n_cmp) & (dist_i >= 0)
    dist = dist_i.astype(F32)
    visf = vis.astype(F32)
    psum = jnp.zeros((tq, ncp), F32)
    outs = []
    for hh in range(NSA_HEADS):
        s = _dot_nt(q_ref[0, hh], kc) - slopes[hh] * dist
        s = jnp.where(vis, s, NEG_INF)
        e = jnp.exp(s - jnp.max(s, axis=-1, keepdims=True)) * visf
        p = e / jnp.maximum(jnp.sum(e, axis=-1, keepdims=True), 1e-30)
        outs.append(_dot(p.astype(BF16), vc))
        psum = psum + p
    o_ref[0] = jnp.concatenate(outs, axis=1)
    ph, plo = _split(psum)
    cover = cover_ref[...]
    p_slc = _dot(ph, cover) + _dot(plo, cover)
    n_sel = cover.shape[1]
    cur = t // SEL_BLOCK
    j = lax.broadcasted_iota(jnp.int32, (1, n_sel), 1)
    forced = jnp.where(j == 0, 1.0, jnp.where(j == cur, 1.0, jnp.where(j == cur - 1, 1.0, 0.0)))
    score = jnp.where(j <= cur, p_slc + FORCE_BONUS * forced, NEG_INF)
    sel_ref[0, 0] = _topk_mask(score, min(SEL_TOPK, n_sel))


def _cmp_attn_call(qa, kc, vc, slopes):
    B, H, S, dk = qa.shape
    ncp = kc.shape[1]
    n_cmp = (S - CMP_BLOCK) // CMP_STRIDE + 1
    n_sel = S // SEL_BLOCK
    tq = 128
    starts = np.arange(ncp) * CMP_STRIDE
    jb = np.arange(n_sel) * SEL_BLOCK
    cover = ((starts[:, None] < jb[None, :] + SEL_BLOCK) & (starts[:, None] + CMP_BLOCK > jb[None, :])
             & (np.arange(ncp)[:, None] < n_cmp))
    cover = jnp.asarray(cover.astype(np.float32), dtype=BF16)
    return pl.pallas_call(
        functools.partial(_cmp_attn_kernel, slopes=tuple(float(s) for s in slopes), tq=tq, n_cmp=n_cmp),
        grid=(B, S // tq),
        in_specs=[
            pl.BlockSpec((1, H, tq, dk), lambda b, i: (b, 0, i, 0)),
            pl.BlockSpec((1, ncp, dk), lambda b, i: (b, 0, 0)),
            pl.BlockSpec((1, ncp, dk), lambda b, i: (b, 0, 0)),
            pl.BlockSpec((ncp, n_sel), lambda b, i: (0, 0)),
        ],
        out_specs=[
            pl.BlockSpec((1, tq, H * dk), lambda b, i: (b, i, 0)),
            pl.BlockSpec((1, 1, tq, n_sel), lambda b, i: (b, 0, i, 0)),
        ],
        out_shape=[jax.ShapeDtypeStruct((B, S, H * dk), F32),
                   jax.ShapeDtypeStruct((B, 1, S, n_sel), F32)],
        compiler_params=pltpu.CompilerParams(vmem_limit_bytes=VMEM_LIMIT),
        name="nsa_cmp_attn",
    )(qa, kc, vc, cover)


def _flash_kernel(*refs, H, G, tq, tk, slopes, window, mask_block, mask_per_head, has_sink):
    refs = list(refs)
    q_ref, k_ref, v_ref = refs[:3]
    pos = 3
    mask_ref = sink_ref = None
    if mask_block:
        mask_ref = refs[pos]
        pos += 1
    if has_sink:
        sink_ref = refs[pos]
        pos += 1
    o_ref, m_s, l_s, acc_s = refs[pos:pos + 4]
    R = H // G
    M = R * tq
    dv = v_ref.shape[-1]
    i = pl.program_id(1)
    q0 = i * tq
    hi = q0 // tk
    lo = jnp.maximum(q0 - (window - 1), 0) // tk if window else 0

    row_t = lax.broadcasted_iota(jnp.int32, (tq, 1), 0) + q0
    tpos = jnp.concatenate([row_t] * R, axis=0).astype(F32)
    lane = lax.broadcasted_iota(jnp.int32, (1, tk), 1)

    for g in range(G):
        if R == 1:
            q = q_ref[0, g]
        else:
            q = jnp.concatenate([q_ref[0, g * R + r] for r in range(R)], axis=0)
        if slopes is not None:
            slope_col = jnp.concatenate(
                [jnp.full((tq, 1), slopes[g * R + r], F32) for r in range(R)], axis=0)
        if mask_block:
            nblk = mask_ref.shape[-1]
            if mask_per_head:
                bm = jnp.concatenate([mask_ref[0, g * R + r] for r in range(R)], axis=0)
            else:
                bm = jnp.concatenate([mask_ref[0, 0]] * R, axis=0)
            bm = bm.astype(BF16)
            blk_iota = lax.broadcasted_iota(jnp.int32, (nblk, 1), 0)

        m_s[...] = jnp.full((M, 1), NEG_INF, F32)
        l_s[...] = jnp.zeros((M, 1), F32)
        acc_s[...] = jnp.zeros((M, dv), F32)

        def body(step, carry):
            j = hi - step
            k0 = pl.multiple_of(j * tk, tk)
            k = k_ref[0, g, pl.ds(k0, tk), :]
            v = v_ref[0, g, pl.ds(k0, tk), :]
            s = _dot_nt(q, k)
            dist = tpos - (lane + k0).astype(F32)
            if slopes is not None:
                s = s - slope_col * dist
            valid = dist >= 0.0
            if window:
                valid = valid & (dist < float(window))
            if mask_block:
                key_blk = lax.shift_right_logical(lane + k0, int(math.log2(mask_block)))
                expand = jnp.where(blk_iota == key_blk, 1.0, 0.0).astype(BF16)
                valid = valid & (_dot(bm, expand) > 0.5)
            s = jnp.where(valid, s, NEG_INF)
            m_prev = m_s[...]
            m_new = jnp.maximum(m_prev, jnp.max(s, axis=-1, keepdims=True))
            alpha = jnp.exp(m_prev - m_new)
            p = jnp.exp(s - m_new)
            l_s[...] = alpha * l_s[...] + jnp.sum(p, axis=-1, keepdims=True)
            acc_s[...] = alpha * acc_s[...] + _dot(p.astype(BF16), v)
            m_s[...] = m_new
            return carry

        lax.fori_loop(0, hi - lo + 1, body, 0)

        m = m_s[...]
        l = l_s[...]
        acc = acc_s[...]
        if has_sink:
            sk = jnp.concatenate(
                [jnp.broadcast_to(sink_ref[:, g * R + r:g * R + r + 1], (tq, 1)) for r in range(R)], axis=0)
            m_f = jnp.maximum(m, sk)
            a = jnp.exp(m - m_f)
            l = l * a + jnp.exp(sk - m_f)
            acc = acc * a
        out = acc / l
        for r in range(R):
            hh = g * R + r
            o_ref[0, :, hh * dv:(hh + 1) * dv] = out[r * tq:(r + 1) * tq].astype(o_ref.dtype)


def _flash_call(q, k, v, *, slopes=None, window=0, mask=None, mask_block=0, sinks=None,
                tq=128, tk=256, out_dtype=BF16, name="flash"):
    B, H, S, dq = q.shape
    G = k.shape[1]
    dv = v.shape[-1]
    tk = min(tk, S)
    assert tk % tq == 0 and S % tk == 0
    R = H // G
    in_specs = [
        pl.BlockSpec((1, H, tq, dq), lambda b, i: (b, 0, i, 0)),
        pl.BlockSpec((1, G, S, dq), lambda b, i: (b, 0, 0, 0)),
        pl.BlockSpec((1, G, S, dv), lambda b, i: (b, 0, 0, 0)),
    ]
    args = [q, k, v]
    mask_per_head = False
    if mask is not None:
        hm, nblk = mask.shape[1], mask.shape[3]
        mask_per_head = hm > 1
        in_specs.append(pl.BlockSpec((1, hm, tq, nblk), lambda b, i: (b, 0, i, 0)))
        args.append(mask)
    if sinks is not None:
        in_specs.append(pl.BlockSpec(sinks.shape, lambda b, i: (0, 0)))
        args.append(sinks)
    kern = functools.partial(
        _flash_kernel, H=H, G=G, tq=tq, tk=tk,
        slopes=None if slopes is None else tuple(float(s) for s in slopes),
        window=window, mask_block=mask_block if mask is not None else 0,
        mask_per_head=mask_per_head, has_sink=sinks is not None)
    return pl.pallas_call(
        kern,
        grid=(B, S // tq),
        in_specs=in_specs,
        out_specs=pl.BlockSpec((1, tq, H * dv), lambda b, i: (b, i, 0)),
        out_shape=jax.ShapeDtypeStruct((B, S, H * dv), out_dtype),
        scratch_shapes=[pltpu.VMEM((R * tq, 1), F32), pltpu.VMEM((R * tq, 1), F32),
                        pltpu.VMEM((R * tq, dv), F32)],
        compiler_params=pltpu.CompilerParams(vmem_limit_bytes=VMEM_LIMIT),
        name=name,
    )(*args)


def _merge_kernel(x_ref, sc1_ref, sh1_ref, g1_ref, sc2_ref, sh2_ref, ng_ref, wg_ref, wb_ref, wo_ref,
                  ocmp_ref, oslc_ref, owin_ref, gs_ref, ob_ref, oc_ref, od_ref, wr_ref, br_ref,
                  xo_ref, h2_ref, rt_ref):
    hd = HEAD_DIM
    x = x_ref[0]
    ng = ng_ref[...]
    h = _rms(x, ng[0:1]) * (1.0 + sc1_ref[0]) + sh1_ref[0]
    hb = h.astype(BF16)
    gs = gs_ref[0]
    ocmp = ocmp_ref[0]
    oslc = oslc_ref[0]
    owin = owin_ref[0]
    parts = []
    for hh in range(NSA_HEADS):
        c0 = GATE_LANE0 + 3 * hh
        sl = slice(hh * hd, (hh + 1) * hd)
        parts.append(gs[:, c0:c0 + 1] * ocmp[:, sl] + gs[:, c0 + 1:c0 + 2] * oslc[:, sl]
                     + gs[:, c0 + 2:c0 + 3] * owin[:, sl])
    o_a = jnp.concatenate(parts, axis=1).astype(BF16)
    branches = (o_a, ob_ref[0], oc_ref[0], od_ref[0])
    D = x.shape[1]
    mixed = None
    for n in range(N_BRANCH):
        gate = _sigmoid(_dot(hb, wg_ref[:, n * D:(n + 1) * D]))
        term = gate * _dot(branches[n], wb_ref[n])
        mixed = term if mixed is None else mixed + term
    xn = x + g1_ref[0] * _dot(mixed.astype(BF16), wo_ref[...])
    xo_ref[0] = xn
    h2 = _rms(xn, ng[1:2]) * (1.0 + sc2_ref[0]) + sh2_ref[0]
    h2_ref[0] = h2.astype(BF16)

    logits = _dot3(h2, wr_ref[...]) + br_ref[...]
    lane = lax.broadcasted_iota(jnp.int32, logits.shape, 1)
    lanef = lane.astype(F32)
    is_c = lane < N_GROUPS
    lc = jnp.where(is_c, logits, NEG_INF)
    mc = jnp.max(lc, axis=-1, keepdims=True)
    grp = jnp.min(jnp.where(lc == mc, lanef, 1e9), axis=-1, keepdims=True)
    p_grp = 1.0 / jnp.sum(jnp.where(is_c, jnp.exp(lc - mc), 0.0), axis=-1, keepdims=True)
    e_lane = lanef - float(N_GROUPS)
    in_grp = (lane >= N_GROUPS) & (lane < N_GROUPS + N_EXPERTS) & (
        jnp.floor(e_lane / EXPERTS_PER_GROUP) == grp)
    lf = jnp.where(in_grp, logits, NEG_INF)
    m1 = jnp.max(lf, axis=-1, keepdims=True)
    i1 = jnp.min(jnp.where(lf == m1, e_lane, 1e9), axis=-1, keepdims=True)
    lf2 = jnp.where(e_lane == i1, NEG_INF, lf)
    m2 = jnp.max(lf2, axis=-1, keepdims=True)
    i2 = jnp.min(jnp.where(lf2 == m2, e_lane, 1e9), axis=-1, keepdims=True)
    e2 = jnp.exp(m2 - m1)
    w1 = p_grp / (1.0 + e2)
    w2 = p_grp * e2 / (1.0 + e2)
    rt = jnp.where(lane == 0, i1, jnp.where(lane == 1, i2, jnp.where(lane == 2, w1, jnp.where(lane == 3, w2, 0.0))))
    rt_ref[0] = rt


def _merge_call(x, mods, ng, wg, wb, wo, ocmp, oslc, owin, gs, ob, oc, od, wr, br):
    B, S, D = x.shape
    tm = 256
    sc1, sh1, g1, sc2, sh2 = mods

    def full(a):
        return pl.BlockSpec(a.shape, lambda b, i: (0,) * a.ndim)

    modspec = pl.BlockSpec((1, 1, D), lambda b, i: (b, 0, 0))
    row = lambda w: pl.BlockSpec((1, tm, w), lambda b, i: (b, i, 0))
    in_specs = [row(D), modspec, modspec, modspec, modspec, modspec, full(ng), full(wg), full(wb), full(wo),
                row(256), row(256), row(256), row(128), row(256), row(256), row(256), full(wr), full(br)]
    return pl.pallas_call(
        _merge_kernel,
        grid=(B, S // tm),
        in_specs=in_specs,
        out_specs=[row(D), row(D), row(128)],
        out_shape=[jax.ShapeDtypeStruct((B, S, D), F32), jax.ShapeDtypeStruct((B, S, D), BF16),
                   jax.ShapeDtypeStruct((B, S, 128), F32)],
        compiler_params=pltpu.CompilerParams(vmem_limit_bytes=VMEM_LIMIT),
        name="merge_router",
    )(x, sc1, sh1, g1, sc2, sh2, ng, wg, wb, wo, ocmp, oslc, owin, gs, ob, oc, od, wr, br)


def _expert_kernel(blk_e_ref, x_ref, w13_ref, w2_ref, o_ref):
    del blk_e_ref
    xb = x_ref[...]
    ab = _dot(xb, w13_ref[0, 0].astype(BF16))
    de = ab.shape[1] // 2
    a = ab[:, :de]
    b = ab[:, de:]
    act = (a * _sigmoid(a)) * b
    o_ref[...] = _dot(act.astype(BF16), w2_ref[0, 0].astype(BF16))


def _expert_call(xb, blk_e, w13, w2, layer):
    R, D = xb.shape
    n_blocks = R // EXPERT_ROWS
    de2 = w13.shape[-1]
    grid_spec = pltpu.PrefetchScalarGridSpec(
        num_scalar_prefetch=1,
        grid=(n_blocks,),
        in_specs=[
            pl.BlockSpec((EXPERT_ROWS, D), lambda i, be: (i, 0)),
            pl.BlockSpec((1, 1, D, de2), lambda i, be: (layer, be[i], 0, 0)),
            pl.BlockSpec((1, 1, de2 // 2, D), lambda i, be: (layer, be[i], 0, 0)),
        ],
        out_specs=pl.BlockSpec((EXPERT_ROWS, D), lambda i, be: (i, 0)),
    )
    return pl.pallas_call(
        _expert_kernel,
        grid_spec=grid_spec,
        out_shape=jax.ShapeDtypeStruct((R, D), F32),
        compiler_params=pltpu.CompilerParams(vmem_limit_bytes=VMEM_LIMIT),
        name="experts",
    )(blk_e, xb, w13, w2)


def _moe(h2, route, w13, w2, layer):
    N, D = h2.shape
    K = 2
    E = N_EXPERTS
    eid = route[:, 0:2].astype(jnp.int32)
    wts = route[:, 2:4]
    flat_e = eid.reshape(-1)
    order = jnp.argsort(flat_e)
    se = flat_e[order]
    counts = jnp.bincount(flat_e, length=E)
    padded = (counts + EXPERT_ROWS - 1) // EXPERT_ROWS * EXPERT_ROWS
    pend = jnp.cumsum(padded)
    pstart = pend - padded
    start = jnp.cumsum(counts) - counts
    dest = pstart[se] + jnp.arange(N * K) - start[se]
    n_blocks = -(-(N * K) // EXPERT_ROWS) + E
    R = n_blocks * EXPERT_ROWS
    row_tok = jnp.full((R,), N, jnp.int32).at[dest].set((order // K).astype(jnp.int32))
    row_w = jnp.zeros((R,), F32).at[dest].set(wts.reshape(-1)[order])
    blk_e = jnp.minimum(jnp.searchsorted(pend, jnp.arange(n_blocks) * EXPERT_ROWS, side='right'), E - 1)
    xpad = jnp.concatenate([h2, jnp.zeros((1, D), h2.dtype)], axis=0)
    xb = xpad[row_tok]
    yb = _expert_call(xb, blk_e.astype(jnp.int32), w13, w2, layer)
    y = jnp.zeros((N + 1, D), F32).at[row_tok].add(yb * row_w[:, None])
    return y[:N]


def kernel(x, c, w_ada, b_ada, norm_gain, w_in, qk_gain, cmp_pe, cmp_w1, cmp_w2, swa_sinks,
           lat_gain_q, lat_gain_kv, rope_gain, w_uq, w_ukv, w_branch, w_out,
           w_coarse, b_coarse, w_fine, b_fine, w13, w2):
    B, S, D = x.shape
    L = w_in.shape[0]
    assert S % MOBA_BLOCK == 0 and D == 1024
    slopes = _alibi_slopes()

    half = MLA_ROPE // 2
    inv = ROPE_THETA ** (-jnp.arange(half, dtype=F32) / half)
    ang = jnp.arange(S).astype(F32)[:, None] * inv[None, :]
    cos, sin = jnp.cos(ang), jnp.sin(ang)

    mod = _ada_mod(c, w_ada, b_ada)

    w_attn = jnp.concatenate(
        [w_in[:, :, :640], w_in[:, :, 652:ATTN_OLD], w_in[:, :, 640:652],
         jnp.zeros((L, D, ATTN_COLS - ATTN_OLD), F32)], axis=2).astype(BF16)
    w_gate = w_in[:, :, ATTN_OLD:].astype(BF16)
    w_uq_b = w_uq.astype(BF16)
    w_ukv_b = w_ukv.astype(BF16)
    w_branch_b = w_branch.astype(BF16)
    w_out_b = w_out.astype(BF16)
    w_router = jnp.concatenate(
        [w_coarse, w_fine, jnp.zeros((L, D, 128 - N_GROUPS - N_EXPERTS), F32)], axis=2)
    b_router = jnp.concatenate(
        [b_coarse, b_fine, jnp.zeros((L, 128 - N_GROUPS - N_EXPERTS), F32)], axis=1)
    sinks_pad = jnp.concatenate([swa_sinks, jnp.zeros((L, 128 - SWA_HEADS), F32)], axis=1)

    for l in range(L):
        m6 = mod[l].reshape(B, 6, 1, D)
        sh1, sc1, g1, sh2, sc2, g2 = (m6[:, j] for j in range(6))
        (qa, kcr, vcr, ks, vs, kw, vw, gs, qb, kb, vb, qc, kc, vc, mc, qd, kd, vd) = _proj_call(
            x, sc1, sh1, norm_gain[l, 0:1], w_attn[l], qk_gain[l], lat_gain_q[l][None], lat_gain_kv[l][None],
            rope_gain[l], w_uq_b[l], w_ukv_b[l], cos, sin)
        kcmp, vcmp = _compress_call(kcr, vcr, cmp_pe[l], cmp_w1[l], cmp_w2[l], qk_gain[l, QK_NSA_KC][None])
        o_cmp, sel = _cmp_attn_call(qa, kcmp, vcmp, slopes[0])
        o_slc = _flash_call(qa, ks, vs, slopes=slopes[0], mask=sel, mask_block=SEL_BLOCK,
                            out_dtype=F32, name="nsa_slc")
        o_win = _flash_call(qa, kw, vw, slopes=slopes[0], window=NSA_WINDOW, out_dtype=F32, name="nsa_win")
        o_b = _flash_call(qb, kb, vb, slopes=slopes[1], window=SWA_WINDOW, sinks=sinks_pad[l][None],
                          name="swa")
        o_c = _flash_call(qc, kc, vc, slopes=slopes[2], mask=mc, mask_block=MOBA_BLOCK, name="moba")
        o_d = _flash_call(qd, kd, vd, name="mla")
        xn, h2, route = _merge_call(
            x, (sc1, sh1, g1, sc2, sh2), norm_gain[l], w_gate[l], w_branch_b[l], w_out_b[l],
            o_cmp, o_slc, o_win, gs, o_b, o_c, o_d, w_router[l], b_router[l][None])
        y = _moe(h2.reshape(B * S, D), route.reshape(B * S, 128), w13, w2, l)
        x = xn + g2 * y.reshape(B, S, D)
    return x
```

```python
import functools
import math

import numpy as np
import jax
import jax.numpy as jnp
from jax import lax
from jax.experimental import pallas as pl
from jax.experimental.pallas import tpu as pltpu

F32 = jnp.float32
BF16 = jnp.bfloat16

HEAD_DIM = 64
NEG_INF = -1e30
EPS = 1e-6
NSA_HEADS = 4
CMP_BLOCK = 32
CMP_STRIDE = 16
CMP_HIDDEN = 256
SEL_BLOCK = 64
SEL_TOPK = 8
NSA_WINDOW = 512
FORCE_BONUS = 1e4
SWA_HEADS = 4
SWA_KV_HEADS = 2
SWA_WINDOW = 128
MOBA_HEADS = 4
MOBA_BLOCK = 256
MOBA_TOPK = 3
MLA_HEADS = 4
MLA_Q_RANK = 384
MLA_KV_RANK = 128
MLA_NOPE = 64
MLA_ROPE = 32
MLA_V = 64
ROPE_THETA = 10000.0
N_BRANCH = 4
BRANCH_WIDTH = 256
N_GROUPS = 4
EXPERTS_PER_GROUP = 8
N_EXPERTS = N_GROUPS * EXPERTS_PER_GROUP
D_EXPERT = 256

QK_NSA_Q, QK_NSA_KC, QK_NSA_KS, QK_NSA_KW = 0, 1, 2, 3
QK_SWA_Q, QK_SWA_K, QK_MOBA_Q, QK_MOBA_K, QK_MLA_Q, QK_MLA_K = 4, 5, 6, 7, 8, 9

ATTN_OLD = 2476
ATTN_COLS = 2560
GATE_LANE0 = 32
EXPERT_ROWS = 256
VMEM_LIMIT = 56 * 1024 * 1024


def _alibi_slopes():
    n = NSA_HEADS + SWA_HEADS + MOBA_HEADS

    def pow2(m):
        start = 2.0 ** (-8.0 / m)
        return [start ** (i + 1) for i in range(m)]

    c = 2 ** int(math.floor(math.log2(n)))
    s = pow2(c) + (pow2(2 * c)[0::2][: n - c] if c < n else [])
    s = -np.sort(-np.asarray(s, np.float32))
    return s.reshape(NSA_HEADS, 3).T


def _dot(a, b):
    return jnp.dot(a, b, preferred_element_type=F32)


def _dot_nt(a, b):
    return lax.dot_general(a, b, (((1,), (1,)), ((), ())), preferred_element_type=F32)


def _split(a):
    hi = a.astype(BF16)
    lo = (a - hi.astype(F32)).astype(BF16)
    return hi, lo


def _dot3(a, b):
    ah, al = _split(a)
    bh, bl = _split(b)
    return _dot(ah, bh) + (_dot(ah, bl) + _dot(al, bh))


def _dot3_nt(a, b):
    ah, al = _split(a)
    bh, bl = _split(b)
    return _dot_nt(ah, bh) + (_dot_nt(ah, bl) + _dot_nt(al, bh))


def _rms(x, g):
    return x * lax.rsqrt(jnp.mean(x * x, axis=-1, keepdims=True) + EPS) * g


def _sigmoid(x):
    return 1.0 / (1.0 + jnp.exp(-x))


def _topk_mask(score, k):
    n = score.shape[-1]
    iota = lax.broadcasted_iota(jnp.int32, score.shape, 1).astype(F32)
    sel = jnp.zeros(score.shape, F32)
    for _ in range(k):
        m = jnp.max(score, axis=-1, keepdims=True)
        idx = jnp.min(jnp.where(score == m, iota, float(n)), axis=-1, keepdims=True)
        hit = iota == idx
        sel = jnp.where(hit, jnp.where(m > 0.5 * NEG_INF, 1.0, 0.0), sel)
        score = jnp.where(hit, -3e38, score)
    return sel


def _ada_kernel(c_ref, w_ref, b_ref, o_ref):
    c = c_ref[...]
    a = c * _sigmoid(c)
    o_ref[0] = _dot(a, w_ref[0]) + b_ref[0]


def _ada_mod(c, w_ada, b_ada):
    L, D, D6 = w_ada.shape
    B = c.shape[0]
    tn = 1024
    return pl.pallas_call(
        _ada_kernel,
        grid=(L, D6 // tn),
        in_specs=[
            pl.BlockSpec((B, D), lambda l, j: (0, 0)),
            pl.BlockSpec((1, D, tn), lambda l, j: (l, 0, j)),
            pl.BlockSpec((1, 1, tn), lambda l, j: (l, 0, j)),
        ],
        out_specs=pl.BlockSpec((1, B, tn), lambda l, j: (l, 0, j)),
        out_shape=jax.ShapeDtypeStruct((L, B, D6), F32),
        compiler_params=pltpu.CompilerParams(vmem_limit_bytes=VMEM_LIMIT),
        name="ada_mod",
    )(c, w_ada, b_ada.reshape(L, 1, D6))


def _proj_kernel(x_ref, sc_ref, sh_ref, ng_ref, w_ref, qkg_ref, lgq_ref, lgkv_ref, rg_ref,
                 wuq_ref, wukv_ref, cos_ref, sin_ref,
                 qa_ref, kcr_ref, vcr_ref, ks_ref, vs_ref, kw_ref, vw_ref, gs_ref,
                 qb_ref, kb_ref, vb_ref, qc_ref, kc_ref, vc_ref, mc_ref, qd_ref, kd_ref, vd_ref,
                 kmean_s):
    i = pl.program_id(1)
    hd = HEAD_DIM
    x = x_ref[0]
    h = _rms(x, ng_ref[...]) * (1.0 + sc_ref[0]) + sh_ref[0]
    hb = h.astype(BF16)
    gq = qkg_ref[...]

    def gain(slot):
        return gq[slot:slot + 1]

    def proj(a, b):
        return _dot(hb, w_ref[:, a:b])

    scale = hd ** -0.5

    p = proj(0, 256)
    for hh in range(NSA_HEADS):
        qa_ref[0, hh] = (_rms(p[:, hh * hd:(hh + 1) * hd], gain(QK_NSA_Q)) * scale).astype(BF16)
    p = proj(256, 640)
    kcr_ref[0] = p[:, 0:64]
    vcr_ref[0] = p[:, 64:128]
    ks_ref[0, 0] = _rms(p[:, 128:192], gain(QK_NSA_KS)).astype(BF16)
    vs_ref[0, 0] = p[:, 192:256].astype(BF16)
    kw_ref[0, 0] = _rms(p[:, 256:320], gain(QK_NSA_KW)).astype(BF16)
    vw_ref[0, 0] = p[:, 320:384].astype(BF16)

    p = proj(640, 1152)
    for hh in range(SWA_HEADS):
        qb_ref[0, hh] = (_rms(p[:, hh * hd:(hh + 1) * hd], gain(QK_SWA_Q)) * scale).astype(BF16)
    for gg in range(SWA_KV_HEADS):
        kb_ref[0, gg] = _rms(p[:, 256 + gg * hd:256 + (gg + 1) * hd], gain(QK_SWA_K)).astype(BF16)
        vb_ref[0, gg] = p[:, 384 + gg * hd:384 + (gg + 1) * hd].astype(BF16)

    @pl.when(i == 0)
    def _():
        kmean_s[...] = jnp.zeros(kmean_s.shape, F32)

    p = proj(1152, 1920)
    nblk = kmean_s.shape[0]
    blk_iota = lax.broadcasted_iota(jnp.int32, (1, nblk), 1)
    kmeans = []
    for hh in range(MOBA_HEADS):
        qn = _rms(p[:, hh * hd:(hh + 1) * hd], gain(QK_MOBA_Q))
        kn = _rms(p[:, 256 + hh * hd:256 + (hh + 1) * hd], gain(QK_MOBA_K))
        qc_ref[0, hh] = (qn * scale).astype(BF16)
        kc_ref[0, hh] = kn.astype(BF16)
        vc_ref[0, hh] = p[:, 512 + hh * hd:512 + (hh + 1) * hd].astype(BF16)
        kmeans.append(jnp.mean(kn, axis=0, keepdims=True))
        g = _dot3_nt(qn, kmean_s[:, hh * hd:(hh + 1) * hd])
        g = jnp.where(blk_iota < i, g, NEG_INF)
        mc_ref[0, hh] = jnp.where(blk_iota == i, 1.0, _topk_mask(g, MOBA_TOPK))
    kmean_s[pl.ds(i, 1), :] = jnp.concatenate(kmeans, axis=1)

    p = proj(1920, 2560)
    gs_ref[0] = _sigmoid(p[:, 512:640])
    cos = cos_ref[...]
    sin = sin_ref[...]
    half = MLA_ROPE // 2

    def rope(v):
        v1 = v[:, :half]
        v2 = v[:, half:]
        return jnp.concatenate([v1 * cos - v2 * sin, v1 * sin + v2 * cos], axis=1)

    rg = rg_ref[...]
    qlat = _dot(_rms(p[:, 0:384], lgq_ref[...]).astype(BF16), wuq_ref[...])
    kvlat = _dot(_rms(p[:, 384:512], lgkv_ref[...]).astype(BF16), wukv_ref[...])
    k_rot = rope(_rms(p[:, 512:544], rg[1:2]))
    dq = MLA_NOPE + MLA_ROPE
    dkv = MLA_NOPE + MLA_V
    mla_scale = dq ** -0.5
    for hh in range(MLA_HEADS):
        qh = qlat[:, hh * dq:(hh + 1) * dq]
        q_nope = _rms(qh[:, :MLA_NOPE], gain(QK_MLA_Q))
        q_rot = rope(_rms(qh[:, MLA_NOPE:], rg[0:1]))
        qd_ref[0, hh] = (jnp.concatenate([q_nope, q_rot], axis=1) * mla_scale).astype(BF16)
        kvh = kvlat[:, hh * dkv:(hh + 1) * dkv]
        k_nope = _rms(kvh[:, :MLA_NOPE], gain(QK_MLA_K))
        kd_ref[0, hh] = jnp.concatenate([k_nope, k_rot], axis=1).astype(BF16)
        vd_ref[0, hh] = kvh[:, MLA_NOPE:].astype(BF16)


def _proj_call(x, sc1, sh1, ng, w_attn, qkg, lgq, lgkv, rg, wuq, wukv, cos, sin):
    B, S, D = x.shape
    tm = MOBA_BLOCK
    nblk = S // tm
    hd = HEAD_DIM

    def full(shape):
        return pl.BlockSpec(shape, lambda b, i: (0,) * len(shape))

    def heads(nh, d):
        return pl.BlockSpec((1, nh, tm, d), lambda b, i: (b, 0, i, 0))

    in_specs = [
        pl.BlockSpec((1, tm, D), lambda b, i: (b, i, 0)),
        pl.BlockSpec((1, 1, D), lambda b, i: (b, 0, 0)),
        pl.BlockSpec((1, 1, D), lambda b, i: (b, 0, 0)),
        full((1, D)),
        full((D, ATTN_COLS)),
        full(qkg.shape),
        full(lgq.shape),
        full(lgkv.shape),
        full(rg.shape),
        full(wuq.shape),
        full(wukv.shape),
        pl.BlockSpec((tm, MLA_ROPE // 2), lambda b, i: (i, 0)),
        pl.BlockSpec((tm, MLA_ROPE // 2), lambda b, i: (i, 0)),
    ]
    row64 = pl.BlockSpec((1, tm, hd), lambda b, i: (b, i, 0))
    out_specs = [
        heads(4, hd), row64, row64, heads(1, hd), heads(1, hd), heads(1, hd), heads(1, hd),
        pl.BlockSpec((1, tm, 128), lambda b, i: (b, i, 0)),
        heads(4, hd), heads(2, hd), heads(2, hd),
        heads(4, hd), heads(4, hd), heads(4, hd), heads(4, nblk),
        heads(4, MLA_NOPE + MLA_ROPE), heads(4, MLA_NOPE + MLA_ROPE), heads(4, MLA_V),
    ]

    def sd(shape, dt):
        return jax.ShapeDtypeStruct(shape, dt)

    out_shape = [
        sd((B, 4, S, hd), BF16), sd((B, S, hd), F32), sd((B, S, hd), F32),
        sd((B, 1, S, hd), BF16), sd((B, 1, S, hd), BF16), sd((B, 1, S, hd), BF16), sd((B, 1, S, hd), BF16),
        sd((B, S, 128), F32),
        sd((B, 4, S, hd), BF16), sd((B, 2, S, hd), BF16), sd((B, 2, S, hd), BF16),
        sd((B, 4, S, hd), BF16), sd((B, 4, S, hd), BF16), sd((B, 4, S, hd), BF16), sd((B, 4, S, nblk), F32),
        sd((B, 4, S, 96), BF16), sd((B, 4, S, 96), BF16), sd((B, 4, S, MLA_V), BF16),
    ]
    return pl.pallas_call(
        _proj_kernel,
        grid=(B, nblk),
        in_specs=in_specs,
        out_specs=out_specs,
        out_shape=out_shape,
        scratch_shapes=[pltpu.VMEM((nblk, MOBA_HEADS * hd), F32)],
        compiler_params=pltpu.CompilerParams(
            dimension_semantics=("arbitrary", "arbitrary"), vmem_limit_bytes=VMEM_LIMIT),
        name="proj_prep",
    )(x, sc1, sh1, ng, w_attn, qkg, lgq, lgkv, rg, wuq, wukv, cos, sin)


def _compress_kernel(gk_ref, gv_ref, pe_ref, w1_ref, w2_ref, gkc_ref, kc_ref, vc_ref):
    half = w1_ref.shape[1] // 2
    outs = []
    for j, g_ref in enumerate((gk_ref, gv_ref)):
        g = g_ref[0].astype(BF16)
        top = _dot(g, w1_ref[j, :half].astype(BF16))
        bot = _dot(g, w1_ref[j, half:].astype(BF16))
        bot = jnp.concatenate([bot[1:], bot[:1]], axis=0)
        pe = jnp.broadcast_to(pe_ref[j], (8, pe_ref.shape[2]))
        bias = _dot3(pe, w1_ref[j])[0:1]
        hid = top + bot + bias
        hid = hid * _sigmoid(hid)
        outs.append(_dot(hid.astype(BF16), w2_ref[j].astype(BF16)))
    kc_ref[0] = _rms(outs[0], gkc_ref[...]).astype(BF16)
    vc_ref[0] = outs[1].astype(BF16)


def _compress_call(kc_raw, vc_raw, pe, w1, w2, g_kc):
    B, S, dk = kc_raw.shape
    n_grp = S // CMP_STRIDE
    gk = kc_raw.reshape(B, n_grp, CMP_STRIDE * dk)
    gv = vc_raw.reshape(B, n_grp, CMP_STRIDE * dk)
    pe_flat = pe.reshape(2, 1, CMP_BLOCK * dk)
    grp_spec = pl.BlockSpec((1, n_grp, CMP_STRIDE * dk), lambda b: (b, 0, 0))
    out_spec = pl.BlockSpec((1, n_grp, dk), lambda b: (b, 0, 0))
    return pl.pallas_call(
        _compress_kernel,
        grid=(B,),
        in_specs=[
            grp_spec, grp_spec,
            pl.BlockSpec(pe_flat.shape, lambda b: (0, 0, 0)),
            pl.BlockSpec(w1.shape, lambda b: (0, 0, 0)),
            pl.BlockSpec(w2.shape, lambda b: (0, 0, 0)),
            pl.BlockSpec((1, dk), lambda b: (0, 0)),
        ],
        out_specs=[out_spec, out_spec],
        out_shape=[jax.ShapeDtypeStruct((B, n_grp, dk), BF16)] * 2,
        compiler_params=pltpu.CompilerParams(vmem_limit_bytes=VMEM_LIMIT),
        name="nsa_compress",
    )(gk, gv, pe_flat, w1, w2, g_kc)


def _cmp_attn_kernel(q_ref, kc_ref, vc_ref, cover_ref, o_ref, sel_ref, *, slopes, tq, n_cmp):
    i = pl.program_id(1)
    kc = kc_ref[0]
    vc = vc_ref[0]
    ncp = kc.shape[0]
    t = (i * tq + lax.broadcasted_iota(jnp.int32, (tq, 1), 0))
    n_iota = lax.broadcasted_iota(jnp.int32, (1, ncp), 1)
    dist_i = t - (n_iota * CMP_STRIDE + (CMP_BLOCK - 1))
    vis = (n_iota < n_cmp) & (dist_i >= 0)
    dist = dist_i.astype(F32)
    visf = vis.astype(F32)
    psum = jnp.zeros((tq, ncp), F32)
    outs = []
    for hh in range(NSA_HEADS):
        s = _dot_nt(q_ref[0, hh], kc) - slopes[hh] * dist
        s = jnp.where(vis, s, NEG_INF)
        e = jnp.exp(s - jnp.max(s, axis=-1, keepdims=True)) * visf
        p = e / jnp.maximum(jnp.sum(e, axis=-1, keepdims=True), 1e-30)
        outs.append(_dot(p.astype(BF16), vc))
        psum = psum + p
    o_ref[0] = jnp.concatenate(outs, axis=1)
    ph, plo = _split(psum)
    cover = cover_ref[...]
    p_slc = _dot(ph, cover) + _dot(plo, cover)
    n_sel = cover.shape[1]
    cur = t // SEL_BLOCK
    j = lax.broadcasted_iota(jnp.int32, (1, n_sel), 1)
    forced = jnp.where(j == 0, 1.0, jnp.where(j == cur, 1.0, jnp.where(j == cur - 1, 1.0, 0.0)))
    score = jnp.where(j <= cur, p_slc + FORCE_BONUS * forced, NEG_INF)
    sel_ref[0, 0] = _topk_mask(score, min(SEL_TOPK, n_sel))


def _cmp_attn_call(qa, kc, vc, slopes):
    B, H, S, dk = qa.shape
    ncp = kc.shape[1]
    n_cmp = (S - CMP_BLOCK) // CMP_STRIDE + 1
    n_sel = S // SEL_BLOCK
    tq = 128
    starts = np.arange(ncp) * CMP_STRIDE
    jb = np.arange(n_sel) * SEL_BLOCK
    cover = ((starts[:, None] < jb[None, :] + SEL_BLOCK) & (starts[:, None] + CMP_BLOCK > jb[None, :])
             & (np.arange(ncp)[:, None] < n_cmp))
    cover = jnp.asarray(cover.astype(np.float32), dtype=BF16)
    return pl.pallas_call(
        functools.partial(_cmp_attn_kernel, slopes=tuple(float(s) for s in slopes), tq=tq, n_cmp=n_cmp),
        grid=(B, S // tq),
        in_specs=[
            pl.BlockSpec((1, H, tq, dk), lambda b, i: (b, 0, i, 0)),
            pl.BlockSpec((1, ncp, dk), lambda b, i: (b, 0, 0)),
            pl.BlockSpec((1, ncp, dk), lambda b, i: (b, 0, 0)),
            pl.BlockSpec((ncp, n_sel), lambda b, i: (0, 0)),
        ],
        out_specs=[
            pl.BlockSpec((1, tq, H * dk), lambda b, i: (b, i, 0)),
            pl.BlockSpec((1, 1, tq, n_sel), lambda b, i: (b, 0, i, 0)),
        ],
        out_shape=[jax.ShapeDtypeStruct((B, S, H * dk), F32),
                   jax.ShapeDtypeStruct((B, 1, S, n_sel), F32)],
        compiler_params=pltpu.CompilerParams(vmem_limit_bytes=VMEM_LIMIT),
        name="nsa_cmp_attn",
    )(qa, kc, vc, cover)


def _flash_kernel(*refs, H, G, tq, tk, slopes, window, mask_block, mask_per_head, has_sink):
    refs = list(refs)
    q_ref, k_ref, v_ref = refs[:3]
    pos = 3
    mask_ref = sink_ref = None
    if mask_block:
        mask_ref = refs[pos]
        pos += 1
    if has_sink:
        sink_ref = refs[pos]
        pos += 1
    o_ref, m_s, l_s, acc_s = refs[pos:pos + 4]
    pos += 4
    bias_s = need_s = None
    if slopes is not None:
        bias_s = refs[pos]
        pos += 1
    if mask_block:
        need_s = refs[pos]
    R = H // G
    M = R * tq
    dv = v_ref.shape[-1]
    i = pl.program_id(1)
    q0 = i * tq
    hi = q0 // tk
    lo = jnp.maximum(q0 - (window - 1), 0) // tk if window else 0

    def rel_pos():
        r_row = jnp.concatenate([lax.broadcasted_iota(jnp.int32, (1, tq), 1)] * R, axis=1)
        return (r_row - lax.broadcasted_iota(jnp.int32, (tk, M), 0)).astype(F32)

    qs, slope_rows, bms = [], [], []
    for g in range(G):
        if R == 1:
            qs.append(q_ref[0, g])
        else:
            qs.append(jnp.concatenate([q_ref[0, g * R + r] for r in range(R)], axis=0))
        if slopes is not None:
            slope_rows.append(jnp.concatenate(
                [jnp.full((1, tq), slopes[g * R + r], F32) for r in range(R)], axis=1))
            bias_s[g] = slope_rows[g] * rel_pos()
        if mask_block:
            if mask_per_head:
                bm = jnp.concatenate([mask_ref[0, g * R + r] for r in range(R)], axis=0)
            else:
                bm = jnp.concatenate([mask_ref[0, 0]] * R, axis=0)
            bms.append(bm)
    m_s[...] = jnp.full(m_s.shape, NEG_INF, F32)
    l_s[...] = jnp.zeros(l_s.shape, F32)
    acc_s[...] = jnp.zeros(acc_s.shape, F32)

    if mask_block:
        nblk = mask_ref.shape[-1]
        col = bms[0]
        for bm in bms[1:]:
            col = jnp.maximum(col, bm)
        col = jnp.max(col, axis=0, keepdims=True)
        bpt = tk // mask_block
        for jt in range(nblk // bpt):
            need_s[jt] = jnp.max(col[:, jt * bpt:(jt + 1) * bpt]).astype(jnp.int32)
        bms = [bm.T.astype(BF16) for bm in bms]

    def tile(j, edge):
        k0 = pl.multiple_of(j * tk, tk)
        off = (q0 - k0).astype(F32)
        valid = None
        if edge:
            dist = rel_pos() + off
            valid = dist >= 0.0
            if window:
                valid = valid & (dist < float(window))
        if mask_block:
            key_blk = lax.shift_right_logical(
                lax.broadcasted_iota(jnp.int32, (tk, nblk), 0) + k0, int(math.log2(mask_block)))
            expand = jnp.where(key_blk == lax.broadcasted_iota(jnp.int32, (tk, nblk), 1), 1.0, 0.0).astype(BF16)
        for g in range(G):
            k = k_ref[0, g, pl.ds(k0, tk), :]
            v = v_ref[0, g, pl.ds(k0, tk), :]
            s = _dot_nt(k, qs[g])
            if slopes is not None:
                s = s - bias_s[g]
                shift = slope_rows[g] * off
            ok = valid
            if mask_block:
                sel = _dot(expand, bms[g]) > 0.5
                ok = sel if ok is None else ok & sel
            if ok is not None:
                s = jnp.where(ok, s, NEG_INF)
            m_prev = m_s[g]
            s_max = jnp.max(s, axis=0, keepdims=True)
            if slopes is not None:
                s_max = s_max - shift
            m_new = jnp.maximum(m_prev, s_max)
            alpha = jnp.exp(m_prev - m_new)
            p = jnp.exp(s - (m_new + shift if slopes is not None else m_new))
            l_s[g] = alpha * l_s[g] + jnp.sum(p, axis=0, keepdims=True)
            pv = lax.dot_general(v, p.astype(BF16), (((0,), (0,)), ((), ())), preferred_element_type=F32)
            acc_s[g] = alpha * acc_s[g] + pv
            m_s[g] = m_new

    def body(step, carry):
        j = hi - step
        is_edge = step == 0
        if window:
            is_edge = is_edge | (q0 - j * tk + (tq - 1) >= window)
        run = (step == 0) | (need_s[j] > 0) if mask_block else None

        def when(c):
            return pl.when(c if run is None else c & run)

        @when(is_edge)
        def _():
            tile(j, True)

        @when(jnp.logical_not(is_edge))
        def _():
            tile(j, False)

        return carry

    lax.fori_loop(0, hi - lo + 1, body, 0)

    for g in range(G):
        m = m_s[g]
        l = l_s[g]
        acc = acc_s[g]
        if has_sink:
            sk = jnp.concatenate(
                [jnp.broadcast_to(sink_ref[:, g * R + r:g * R + r + 1], (1, tq)) for r in range(R)], axis=1)
            m_f = jnp.maximum(m, sk)
            a = jnp.exp(m - m_f)
            l = l * a + jnp.exp(sk - m_f)
            acc = acc * a
        out = acc / l
        for r in range(R):
            hh = g * R + r
            o_ref[0, :, hh * dv:(hh + 1) * dv] = out[:, r * tq:(r + 1) * tq].T.astype(o_ref.dtype)


def _flash_call(q, k, v, *, slopes=None, window=0, mask=None, mask_block=0, sinks=None,
                tq=128, tk=256, out_dtype=BF16, name="flash"):
    B, H, S, dq = q.shape
    G = k.shape[1]
    dv = v.shape[-1]
    tk = min(tk, S)
    assert tk % tq == 0 and S % tk == 0
    R = H // G
    in_specs = [
        pl.BlockSpec((1, H, tq, dq), lambda b, i: (b, 0, i, 0)),
        pl.BlockSpec((1, G, S, dq), lambda b, i: (b, 0, 0, 0)),
        pl.BlockSpec((1, G, S, dv), lambda b, i: (b, 0, 0, 0)),
    ]
    args = [q, k, v]
    mask_per_head = False
    if mask is not None:
        hm, nblk = mask.shape[1], mask.shape[3]
        mask_per_head = hm > 1
        in_specs.append(pl.BlockSpec((1, hm, tq, nblk), lambda b, i: (b, 0, i, 0)))
        args.append(mask)
    if sinks is not None:
        in_specs.append(pl.BlockSpec(sinks.shape, lambda b, i: (0, 0)))
        args.append(sinks)
    kern = functools.partial(
        _flash_kernel, H=H, G=G, tq=tq, tk=tk,
        slopes=None if slopes is None else tuple(float(s) for s in slopes),
        window=window, mask_block=mask_block if mask is not None else 0,
        mask_per_head=mask_per_head, has_sink=sinks is not None)
    scratch = [pltpu.VMEM((G, 1, R * tq), F32), pltpu.VMEM((G, 1, R * tq), F32),
               pltpu.VMEM((G, dv, R * tq), F32)]
    if slopes is not None:
        scratch.append(pltpu.VMEM((G, tk, R * tq), F32))
    if mask is not None:
        assert tk % mask_block == 0
        scratch.append(pltpu.SMEM((S // tk,), jnp.int32))
    return pl.pallas_call(
        kern,
        grid=(B, S // tq),
        in_specs=in_specs,
        out_specs=pl.BlockSpec((1, tq, H * dv), lambda b, i: (b, i, 0)),
        out_shape=jax.ShapeDtypeStruct((B, S, H * dv), out_dtype),
        scratch_shapes=scratch,
        compiler_params=pltpu.CompilerParams(vmem_limit_bytes=VMEM_LIMIT),
        name=name,
    )(*args)


def _merge_kernel(x_ref, sc1_ref, sh1_ref, g1_ref, sc2_ref, sh2_ref, ng_ref, wg_ref, wb_ref, wo_ref,
                  ocmp_ref, oslc_ref, owin_ref, gs_ref, ob_ref, oc_ref, od_ref, wr_ref, br_ref,
                  xo_ref, h2_ref, rt_ref):
    hd = HEAD_DIM
    x = x_ref[0]
    ng = ng_ref[...]
    h = _rms(x, ng[0:1]) * (1.0 + sc1_ref[0]) + sh1_ref[0]
    hb = h.astype(BF16)
    gs = gs_ref[0]
    ocmp = ocmp_ref[0]
    oslc = oslc_ref[0]
    owin = owin_ref[0]
    parts = []
    for hh in range(NSA_HEADS):
        c0 = GATE_LANE0 + 3 * hh
        sl = slice(hh * hd, (hh + 1) * hd)
        parts.append(gs[:, c0:c0 + 1] * ocmp[:, sl] + gs[:, c0 + 1:c0 + 2] * oslc[:, sl]
                     + gs[:, c0 + 2:c0 + 3] * owin[:, sl])
    o_a = jnp.concatenate(parts, axis=1).astype(BF16)
    branches = (o_a, ob_ref[0], oc_ref[0], od_ref[0])
    D = x.shape[1]
    mixed = None
    for n in range(N_BRANCH):
        gate = _sigmoid(_dot(hb, wg_ref[:, n * D:(n + 1) * D]))
        term = gate * _dot(branches[n], wb_ref[n])
        mixed = term if mixed is None else mixed + term
    xn = x + g1_ref[0] * _dot(mixed.astype(BF16), wo_ref[...])
    xo_ref[0] = xn
    h2 = _rms(xn, ng[1:2]) * (1.0 + sc2_ref[0]) + sh2_ref[0]
    h2_ref[0] = h2.astype(BF16)

    logits = _dot3(h2, wr_ref[...]) + br_ref[...]
    lane = lax.broadcasted_iota(jnp.int32, logits.shape, 1)
    lanef = lane.astype(F32)
    is_c = lane < N_GROUPS
    lc = jnp.where(is_c, logits, NEG_INF)
    mc = jnp.max(lc, axis=-1, keepdims=True)
    grp = jnp.min(jnp.where(lc == mc, lanef, 1e9), axis=-1, keepdims=True)
    p_grp = 1.0 / jnp.sum(jnp.where(is_c, jnp.exp(lc - mc), 0.0), axis=-1, keepdims=True)
    e_lane = lanef - float(N_GROUPS)
    in_grp = (lane >= N_GROUPS) & (lane < N_GROUPS + N_EXPERTS) & (
        jnp.floor(e_lane / EXPERTS_PER_GROUP) == grp)
    lf = jnp.where(in_grp, logits, NEG_INF)
    m1 = jnp.max(lf, axis=-1, keepdims=True)
    i1 = jnp.min(jnp.where(lf == m1, e_lane, 1e9), axis=-1, keepdims=True)
    lf2 = jnp.where(e_lane == i1, NEG_INF, lf)
    m2 = jnp.max(lf2, axis=-1, keepdims=True)
    i2 = jnp.min(jnp.where(lf2 == m2, e_lane, 1e9), axis=-1, keepdims=True)
    e2 = jnp.exp(m2 - m1)
    w1 = p_grp / (1.0 + e2)
    w2 = p_grp * e2 / (1.0 + e2)
    rt = jnp.where(lane == 0, i1, jnp.where(lane == 1, i2, jnp.where(lane == 2, w1, jnp.where(lane == 3, w2, 0.0))))
    rt_ref[0] = rt


def _merge_call(x, mods, ng, wg, wb, wo, ocmp, oslc, owin, gs, ob, oc, od, wr, br):
    B, S, D = x.shape
    tm = 256
    sc1, sh1, g1, sc2, sh2 = mods

    def full(a):
        return pl.BlockSpec(a.shape, lambda b, i: (0,) * a.ndim)

    modspec = pl.BlockSpec((1, 1, D), lambda b, i: (b, 0, 0))
    row = lambda w: pl.BlockSpec((1, tm, w), lambda b, i: (b, i, 0))
    in_specs = [row(D), modspec, modspec, modspec, modspec, modspec, full(ng), full(wg), full(wb), full(wo),
                row(256), row(256), row(256), row(128), row(256), row(256), row(256), full(wr), full(br)]
    return pl.pallas_call(
        _merge_kernel,
        grid=(B, S // tm),
        in_specs=in_specs,
        out_specs=[row(D), row(D), row(128)],
        out_shape=[jax.ShapeDtypeStruct((B, S, D), F32), jax.ShapeDtypeStruct((B, S, D), BF16),
                   jax.ShapeDtypeStruct((B, S, 128), F32)],
        compiler_params=pltpu.CompilerParams(vmem_limit_bytes=VMEM_LIMIT),
        name="merge_router",
    )(x, sc1, sh1, g1, sc2, sh2, ng, wg, wb, wo, ocmp, oslc, owin, gs, ob, oc, od, wr, br)


def _expert_kernel(blk_e_ref, x_ref, w13_ref, w2_ref, o_ref):
    del blk_e_ref
    xb = x_ref[...]
    ab = _dot(xb, w13_ref[0, 0].astype(BF16))
    de = ab.shape[1] // 2
    a = ab[:, :de]
    b = ab[:, de:]
    act = (a * _sigmoid(a)) * b
    o_ref[...] = _dot(act.astype(BF16), w2_ref[0, 0].astype(BF16))


def _expert_call(xb, blk_e, w13, w2, layer):
    R, D = xb.shape
    n_blocks = R // EXPERT_ROWS
    de2 = w13.shape[-1]
    grid_spec = pltpu.PrefetchScalarGridSpec(
        num_scalar_prefetch=1,
        grid=(n_blocks,),
        in_specs=[
            pl.BlockSpec((EXPERT_ROWS, D), lambda i, be: (i, 0)),
            pl.BlockSpec((1, 1, D, de2), lambda i, be: (layer, be[i], 0, 0)),
            pl.BlockSpec((1, 1, de2 // 2, D), lambda i, be: (layer, be[i], 0, 0)),
        ],
        out_specs=pl.BlockSpec((EXPERT_ROWS, D), lambda i, be: (i, 0)),
    )
    return pl.pallas_call(
        _expert_kernel,
        grid_spec=grid_spec,
        out_shape=jax.ShapeDtypeStruct((R, D), F32),
        compiler_params=pltpu.CompilerParams(vmem_limit_bytes=VMEM_LIMIT),
        name="experts",
    )(blk_e, xb, w13, w2)


def _moe(h2, route, w13, w2, layer):
    N, D = h2.shape
    K = 2
    E = N_EXPERTS
    eid = route[:, 0:2].astype(jnp.int32)
    wts = route[:, 2:4]
    flat_e = eid.reshape(-1)
    order = jnp.argsort(flat_e)
    se = flat_e[order]
    counts = jnp.bincount(flat_e, length=E)
    padded = (counts + EXPERT_ROWS - 1) // EXPERT_ROWS * EXPERT_ROWS
    pend = jnp.cumsum(padded)
    pstart = pend - padded
    start = jnp.cumsum(counts) - counts
    dest = pstart[se] + jnp.arange(N * K) - start[se]
    n_blocks = -(-(N * K) // EXPERT_ROWS) + E
    R = n_blocks * EXPERT_ROWS
    row_tok = jnp.full((R,), N, jnp.int32).at[dest].set((order // K).astype(jnp.int32))
    row_w = jnp.zeros((R,), F32).at[dest].set(wts.reshape(-1)[order])
    blk_e = jnp.minimum(jnp.searchsorted(pend, jnp.arange(n_blocks) * EXPERT_ROWS, side='right'), E - 1)
    xpad = jnp.concatenate([h2, jnp.zeros((1, D), h2.dtype)], axis=0)
    xb = xpad[row_tok]
    yb = _expert_call(xb, blk_e.astype(jnp.int32), w13, w2, layer)
    y = jnp.zeros((N + 1, D), F32).at[row_tok].add(yb * row_w[:, None])
    return y[:N]


def kernel(x, c, w_ada, b_ada, norm_gain, w_in, qk_gain, cmp_pe, cmp_w1, cmp_w2, swa_sinks,
           lat_gain_q, lat_gain_kv, rope_gain, w_uq, w_ukv, w_branch, w_out,
           w_coarse, b_coarse, w_fine, b_fine, w13, w2):
    B, S, D = x.shape
    L = w_in.shape[0]
    assert S % MOBA_BLOCK == 0 and D == 1024
    slopes = _alibi_slopes()

    half = MLA_ROPE // 2
    inv = ROPE_THETA ** (-jnp.arange(half, dtype=F32) / half)
    ang = jnp.arange(S).astype(F32)[:, None] * inv[None, :]
    cos, sin = jnp.cos(ang), jnp.sin(ang)

    mod = _ada_mod(c, w_ada, b_ada)

    w_attn = jnp.concatenate(
        [w_in[:, :, :640], w_in[:, :, 652:ATTN_OLD], w_in[:, :, 640:652],
         jnp.zeros((L, D, ATTN_COLS - ATTN_OLD), F32)], axis=2).astype(BF16)
    w_gate = w_in[:, :, ATTN_OLD:].astype(BF16)
    w_uq_b = w_uq.astype(BF16)
    w_ukv_b = w_ukv.astype(BF16)
    w_branch_b = w_branch.astype(BF16)
    w_out_b = w_out.astype(BF16)
    w_router = jnp.concatenate(
        [w_coarse, w_fine, jnp.zeros((L, D, 128 - N_GROUPS - N_EXPERTS), F32)], axis=2)
    b_router = jnp.concatenate(
        [b_coarse, b_fine, jnp.zeros((L, 128 - N_GROUPS - N_EXPERTS), F32)], axis=1)
    sinks_pad = jnp.concatenate([swa_sinks, jnp.zeros((L, 128 - SWA_HEADS), F32)], axis=1)

    for l in range(L):
        m6 = mod[l].reshape(B, 6, 1, D)
        sh1, sc1, g1, sh2, sc2, g2 = (m6[:, j] for j in range(6))
        (qa, kcr, vcr, ks, vs, kw, vw, gs, qb, kb, vb, qc, kc, vc, mc, qd, kd, vd) = _proj_call(
            x, sc1, sh1, norm_gain[l, 0:1], w_attn[l], qk_gain[l], lat_gain_q[l][None], lat_gain_kv[l][None],
            rope_gain[l], w_uq_b[l], w_ukv_b[l], cos, sin)
        kcmp, vcmp = _compress_call(kcr, vcr, cmp_pe[l], cmp_w1[l], cmp_w2[l], qk_gain[l, QK_NSA_KC][None])
        o_cmp, sel = _cmp_attn_call(qa, kcmp, vcmp, slopes[0])
        o_slc = _flash_call(qa, ks, vs, slopes=slopes[0], mask=sel, mask_block=SEL_BLOCK,
                            tq=256, tk=256, out_dtype=F32, name="nsa_slc")
        o_win = _flash_call(qa, kw, vw, slopes=slopes[0], window=NSA_WINDOW,
                            tq=256, tk=256, out_dtype=F32, name="nsa_win")
        o_b = _flash_call(qb, kb, vb, slopes=slopes[1], window=SWA_WINDOW, sinks=sinks_pad[l][None],
                          tq=256, tk=256, name="swa")
        o_c = _flash_call(qc, kc, vc, slopes=slopes[2], mask=mc, mask_block=MOBA_BLOCK,
                          tq=256, tk=256, name="moba")
        o_d = _flash_call(qd, kd, vd, tq=256, tk=512, name="mla")
        xn, h2, route = _merge_call(
            x, (sc1, sh1, g1, sc2, sh2), norm_gain[l], w_gate[l], w_branch_b[l], w_out_b[l],
            o_cmp, o_slc, o_win, gs, o_b, o_c, o_d, w_router[l], b_router[l][None])
        y = _moe(h2.reshape(B * S, D), route.reshape(B * S, 128), w13, w2, l)
        x = xn + g2 * y.reshape(B, S, D)
    return x
```

```python
import functools
import math

import numpy as np
import jax
import jax.numpy as jnp
from jax import lax
from jax.experimental import pallas as pl
from jax.experimental.pallas import tpu as pltpu

F32 = jnp.float32
BF16 = jnp.bfloat16

HEAD_DIM = 64
NEG_INF = -1e30
EPS = 1e-6
NSA_HEADS = 4
CMP_BLOCK = 32
CMP_STRIDE = 16
CMP_HIDDEN = 256
SEL_BLOCK = 64
SEL_TOPK = 8
NSA_WINDOW = 512
FORCE_BONUS = 1e4
SWA_HEADS = 4
SWA_KV_HEADS = 2
SWA_WINDOW = 128
MOBA_HEADS = 4
MOBA_BLOCK = 256
MOBA_TOPK = 3
MLA_HEADS = 4
MLA_Q_RANK = 384
MLA_KV_RANK = 128
MLA_NOPE = 64
MLA_ROPE = 32
MLA_V = 64
ROPE_THETA = 10000.0
N_BRANCH = 4
BRANCH_WIDTH = 256
N_GROUPS = 4
EXPERTS_PER_GROUP = 8
N_EXPERTS = N_GROUPS * EXPERTS_PER_GROUP
D_EXPERT = 256

QK_NSA_Q, QK_NSA_KC, QK_NSA_KS, QK_NSA_KW = 0, 1, 2, 3
QK_SWA_Q, QK_SWA_K, QK_MOBA_Q, QK_MOBA_K, QK_MLA_Q, QK_MLA_K = 4, 5, 6, 7, 8, 9

ATTN_OLD = 2476
ATTN_COLS = 2560
GATE_LANE0 = 32
EXPERT_ROWS = 256
VMEM_LIMIT = 56 * 1024 * 1024


def _alibi_slopes():
    n = NSA_HEADS + SWA_HEADS + MOBA_HEADS

    def pow2(m):
        start = 2.0 ** (-8.0 / m)
        return [start ** (i + 1) for i in range(m)]

    c = 2 ** int(math.floor(math.log2(n)))
    s = pow2(c) + (pow2(2 * c)[0::2][: n - c] if c < n else [])
    s = -np.sort(-np.asarray(s, np.float32))
    return s.reshape(NSA_HEADS, 3).T


def _dot(a, b):
    return jnp.dot(a, b, preferred_element_type=F32)


def _dot_nt(a, b):
    return lax.dot_general(a, b, (((1,), (1,)), ((), ())), preferred_element_type=F32)


def _split(a):
    hi = a.astype(BF16)
    lo = (a - hi.astype(F32)).astype(BF16)
    return hi, lo


def _dot3(a, b):
    ah, al = _split(a)
    bh, bl = _split(b)
    return _dot(ah, bh) + (_dot(ah, bl) + _dot(al, bh))


def _dot3_nt(a, b):
    ah, al = _split(a)
    bh, bl = _split(b)
    return _dot_nt(ah, bh) + (_dot_nt(ah, bl) + _dot_nt(al, bh))


def _rms(x, g):
    return x * lax.rsqrt(jnp.mean(x * x, axis=-1, keepdims=True) + EPS) * g


def _sigmoid(x):
    return 1.0 / (1.0 + jnp.exp(-x))


def _topk_mask(score, k):
    n = score.shape[-1]
    iota = lax.broadcasted_iota(jnp.int32, score.shape, 1).astype(F32)
    sel = jnp.zeros(score.shape, F32)
    for _ in range(k):
        m = jnp.max(score, axis=-1, keepdims=True)
        idx = jnp.min(jnp.where(score == m, iota, float(n)), axis=-1, keepdims=True)
        hit = iota == idx
        sel = jnp.where(hit, jnp.where(m > 0.5 * NEG_INF, 1.0, 0.0), sel)
        score = jnp.where(hit, -3e38, score)
    return sel


def _ada_kernel(c_ref, w_ref, b_ref, o_ref):
    c = c_ref[...]
    a = c * _sigmoid(c)
    o_ref[0] = _dot(a, w_ref[0]) + b_ref[0]


def _ada_mod(c, w_ada, b_ada):
    L, D, D6 = w_ada.shape
    B = c.shape[0]
    tn = 1024
    return pl.pallas_call(
        _ada_kernel,
        grid=(L, D6 // tn),
        in_specs=[
            pl.BlockSpec((B, D), lambda l, j: (0, 0)),
            pl.BlockSpec((1, D, tn), lambda l, j: (l, 0, j)),
            pl.BlockSpec((1, 1, tn), lambda l, j: (l, 0, j)),
        ],
        out_specs=pl.BlockSpec((1, B, tn), lambda l, j: (l, 0, j)),
        out_shape=jax.ShapeDtypeStruct((L, B, D6), F32),
        compiler_params=pltpu.CompilerParams(vmem_limit_bytes=VMEM_LIMIT),
        name="ada_mod",
    )(c, w_ada, b_ada.reshape(L, 1, D6))


def _proj_kernel(x_ref, sc_ref, sh_ref, ng_ref, w_ref, qkg_ref, lgq_ref, lgkv_ref, rg_ref,
                 wuq_ref, wukv_ref, cos_ref, sin_ref,
                 qa_ref, kcr_ref, vcr_ref, ks_ref, vs_ref, kw_ref, vw_ref, gs_ref,
                 qb_ref, kb_ref, vb_ref, qc_ref, kc_ref, vc_ref, mc_ref, qd_ref, kd_ref, vd_ref,
                 kmean_s):
    i = pl.program_id(1)
    hd = HEAD_DIM
    x = x_ref[0]
    h = _rms(x, ng_ref[...]) * (1.0 + sc_ref[0]) + sh_ref[0]
    hb = h.astype(BF16)
    gq = qkg_ref[...]

    def gain(slot):
        return gq[slot:slot + 1]

    def proj(a, b):
        return _dot(hb, w_ref[:, a:b])

    scale = hd ** -0.5

    p = proj(0, 256)
    for hh in range(NSA_HEADS):
        qa_ref[0, hh] = (_rms(p[:, hh * hd:(hh + 1) * hd], gain(QK_NSA_Q)) * scale).astype(BF16)
    p = proj(256, 640)
    kcr_ref[0] = p[:, 0:64]
    vcr_ref[0] = p[:, 64:128]
    ks_ref[0, 0] = _rms(p[:, 128:192], gain(QK_NSA_KS)).astype(BF16)
    vs_ref[0, 0] = p[:, 192:256].astype(BF16)
    kw_ref[0, 0] = _rms(p[:, 256:320], gain(QK_NSA_KW)).astype(BF16)
    vw_ref[0, 0] = p[:, 320:384].astype(BF16)

    p = proj(640, 1152)
    for hh in range(SWA_HEADS):
        qb_ref[0, hh] = (_rms(p[:, hh * hd:(hh + 1) * hd], gain(QK_SWA_Q)) * scale).astype(BF16)
    for gg in range(SWA_KV_HEADS):
        kb_ref[0, gg] = _rms(p[:, 256 + gg * hd:256 + (gg + 1) * hd], gain(QK_SWA_K)).astype(BF16)
        vb_ref[0, gg] = p[:, 384 + gg * hd:384 + (gg + 1) * hd].astype(BF16)

    @pl.when(i == 0)
    def _():
        kmean_s[...] = jnp.zeros(kmean_s.shape, F32)

    p = proj(1152, 1920)
    nblk = kmean_s.shape[0]
    blk_iota = lax.broadcasted_iota(jnp.int32, (1, nblk), 1)
    kmeans = []
    for hh in range(MOBA_HEADS):
        qn = _rms(p[:, hh * hd:(hh + 1) * hd], gain(QK_MOBA_Q))
        kn = _rms(p[:, 256 + hh * hd:256 + (hh + 1) * hd], gain(QK_MOBA_K))
        qc_ref[0, hh] = (qn * scale).astype(BF16)
        kc_ref[0, hh] = kn.astype(BF16)
        vc_ref[0, hh] = p[:, 512 + hh * hd:512 + (hh + 1) * hd].astype(BF16)
        kmeans.append(jnp.mean(kn, axis=0, keepdims=True))
        g = _dot3_nt(qn, kmean_s[:, hh * hd:(hh + 1) * hd])
        g = jnp.where(blk_iota < i, g, NEG_INF)
        mc_ref[0, hh] = jnp.where(blk_iota == i, 1.0, _topk_mask(g, MOBA_TOPK))
    kmean_s[pl.ds(i, 1), :] = jnp.concatenate(kmeans, axis=1)

    p = proj(1920, 2560)
    gs_ref[0] = _sigmoid(p[:, 512:640])
    cos = cos_ref[...]
    sin = sin_ref[...]
    half = MLA_ROPE // 2

    def rope(v):
        v1 = v[:, :half]
        v2 = v[:, half:]
        return jnp.concatenate([v1 * cos - v2 * sin, v1 * sin + v2 * cos], axis=1)

    rg = rg_ref[...]
    qlat = _dot(_rms(p[:, 0:384], lgq_ref[...]).astype(BF16), wuq_ref[...])
    kvlat = _dot(_rms(p[:, 384:512], lgkv_ref[...]).astype(BF16), wukv_ref[...])
    k_rot = rope(_rms(p[:, 512:544], rg[1:2]))
    dq = MLA_NOPE + MLA_ROPE
    dkv = MLA_NOPE + MLA_V
    mla_scale = dq ** -0.5
    for hh in range(MLA_HEADS):
        qh = qlat[:, hh * dq:(hh + 1) * dq]
        q_nope = _rms(qh[:, :MLA_NOPE], gain(QK_MLA_Q))
        q_rot = rope(_rms(qh[:, MLA_NOPE:], rg[0:1]))
        qd_ref[0, hh] = (jnp.concatenate([q_nope, q_rot], axis=1) * mla_scale).astype(BF16)
        kvh = kvlat[:, hh * dkv:(hh + 1) * dkv]
        k_nope = _rms(kvh[:, :MLA_NOPE], gain(QK_MLA_K))
        kd_ref[0, hh] = jnp.concatenate([k_nope, k_rot], axis=1).astype(BF16)
        vd_ref[0, hh] = kvh[:, MLA_NOPE:].astype(BF16)


def _proj_call(x, sc1, sh1, ng, w_attn, qkg, lgq, lgkv, rg, wuq, wukv, cos, sin):
    B, S, D = x.shape
    tm = MOBA_BLOCK
    nblk = S // tm
    hd = HEAD_DIM

    def full(shape):
        return pl.BlockSpec(shape, lambda b, i: (0,) * len(shape))

    def heads(nh, d):
        return pl.BlockSpec((1, nh, tm, d), lambda b, i: (b, 0, i, 0))

    in_specs = [
        pl.BlockSpec((1, tm, D), lambda b, i: (b, i, 0)),
        pl.BlockSpec((1, 1, D), lambda b, i: (b, 0, 0)),
        pl.BlockSpec((1, 1, D), lambda b, i: (b, 0, 0)),
        full((1, D)),
        full((D, ATTN_COLS)),
        full(qkg.shape),
        full(lgq.shape),
        full(lgkv.shape),
        full(rg.shape),
        full(wuq.shape),
        full(wukv.shape),
        pl.BlockSpec((tm, MLA_ROPE // 2), lambda b, i: (i, 0)),
        pl.BlockSpec((tm, MLA_ROPE // 2), lambda b, i: (i, 0)),
    ]
    row64 = pl.BlockSpec((1, tm, hd), lambda b, i: (b, i, 0))
    out_specs = [
        heads(4, hd), row64, row64, heads(1, hd), heads(1, hd), heads(1, hd), heads(1, hd),
        pl.BlockSpec((1, tm, 128), lambda b, i: (b, i, 0)),
        heads(4, hd), heads(2, hd), heads(2, hd),
        heads(4, hd), heads(4, hd), heads(4, hd), heads(4, nblk),
        heads(4, MLA_NOPE + MLA_ROPE), heads(4, MLA_NOPE + MLA_ROPE), heads(4, MLA_V),
    ]

    def sd(shape, dt):
        return jax.ShapeDtypeStruct(shape, dt)

    out_shape = [
        sd((B, 4, S, hd), BF16), sd((B, S, hd), F32), sd((B, S, hd), F32),
        sd((B, 1, S, hd), BF16), sd((B, 1, S, hd), BF16), sd((B, 1, S, hd), BF16), sd((B, 1, S, hd), BF16),
        sd((B, S, 128), F32),
        sd((B, 4, S, hd), BF16), sd((B, 2, S, hd), BF16), sd((B, 2, S, hd), BF16),
        sd((B, 4, S, hd), BF16), sd((B, 4, S, hd), BF16), sd((B, 4, S, hd), BF16), sd((B, 4, S, nblk), F32),
        sd((B, 4, S, 96), BF16), sd((B, 4, S, 96), BF16), sd((B, 4, S, MLA_V), BF16),
    ]
    return pl.pallas_call(
        _proj_kernel,
        grid=(B, nblk),
        in_specs=in_specs,
        out_specs=out_specs,
        out_shape=out_shape,
        scratch_shapes=[pltpu.VMEM((nblk, MOBA_HEADS * hd), F32)],
        compiler_params=pltpu.CompilerParams(
            dimension_semantics=("arbitrary", "arbitrary"), vmem_limit_bytes=VMEM_LIMIT),
        name="proj_prep",
    )(x, sc1, sh1, ng, w_attn, qkg, lgq, lgkv, rg, wuq, wukv, cos, sin)


def _compress_kernel(gk_ref, gv_ref, pe_ref, w1_ref, w2_ref, gkc_ref, kc_ref, vc_ref):
    half = w1_ref.shape[1] // 2
    outs = []
    for j, g_ref in enumerate((gk_ref, gv_ref)):
        g = g_ref[0].astype(BF16)
        top = _dot(g, w1_ref[j, :half].astype(BF16))
        bot = _dot(g, w1_ref[j, half:].astype(BF16))
        bot = jnp.concatenate([bot[1:], bot[:1]], axis=0)
        pe = jnp.broadcast_to(pe_ref[j], (8, pe_ref.shape[2]))
        bias = _dot3(pe, w1_ref[j])[0:1]
        hid = top + bot + bias
        hid = hid * _sigmoid(hid)
        outs.append(_dot(hid.astype(BF16), w2_ref[j].astype(BF16)))
    kc_ref[0] = _rms(outs[0], gkc_ref[...]).astype(BF16)
    vc_ref[0] = outs[1].astype(BF16)


def _compress_call(kc_raw, vc_raw, pe, w1, w2, g_kc):
    B, S, dk = kc_raw.shape
    n_grp = S // CMP_STRIDE
    gk = kc_raw.reshape(B, n_grp, CMP_STRIDE * dk)
    gv = vc_raw.reshape(B, n_grp, CMP_STRIDE * dk)
    pe_flat = pe.reshape(2, 1, CMP_BLOCK * dk)
    grp_spec = pl.BlockSpec((1, n_grp, CMP_STRIDE * dk), lambda b: (b, 0, 0))
    out_spec = pl.BlockSpec((1, n_grp, dk), lambda b: (b, 0, 0))
    return pl.pallas_call(
        _compress_kernel,
        grid=(B,),
        in_specs=[
            grp_spec, grp_spec,
            pl.BlockSpec(pe_flat.shape, lambda b: (0, 0, 0)),
            pl.BlockSpec(w1.shape, lambda b: (0, 0, 0)),
            pl.BlockSpec(w2.shape, lambda b: (0, 0, 0)),
            pl.BlockSpec((1, dk), lambda b: (0, 0)),
        ],
        out_specs=[out_spec, out_spec],
        out_shape=[jax.ShapeDtypeStruct((B, n_grp, dk), BF16)] * 2,
        compiler_params=pltpu.CompilerParams(vmem_limit_bytes=VMEM_LIMIT),
        name="nsa_compress",
    )(gk, gv, pe_flat, w1, w2, g_kc)


def _cmp_attn_kernel(q_ref, kc_ref, vc_ref, cover_ref, o_ref, sel_ref, *, slopes, tq, n_cmp):
    i = pl.program_id(1)
    kc = kc_ref[0]
    vc = vc_ref[0]
    ncp = kc.shape[0]
    t = (i * tq + lax.broadcasted_iota(jnp.int32, (tq, 1), 0))
    n_iota = lax.broadcasted_iota(jnp.int32, (1, ncp), 1)
    dist_i = t - (n_iota * CMP_STRIDE + (CMP_BLOCK - 1))
    vis = (n_iota < n_cmp) & (dist_i >= 0)
    dist = dist_i.astype(F32)
    visf = vis.astype(F32)
    psum = jnp.zeros((tq, ncp), F32)
    outs = []
    for hh in range(NSA_HEADS):
        s = _dot_nt(q_ref[0, hh], kc) - slopes[hh] * dist
        s = jnp.where(vis, s, NEG_INF)
        e = jnp.exp(s - jnp.max(s, axis=-1, keepdims=True)) * visf
        p = e / jnp.maximum(jnp.sum(e, axis=-1, keepdims=True), 1e-30)
        outs.append(_dot(p.astype(BF16), vc))
        psum = psum + p
    o_ref[0] = jnp.concatenate(outs, axis=1)
    ph, plo = _split(psum)
    cover = cover_ref[...]
    p_slc = _dot(ph, cover) + _dot(plo, cover)
    n_sel = cover.shape[1]
    cur = t // SEL_BLOCK
    j = lax.broadcasted_iota(jnp.int32, (1, n_sel), 1)
    forced = jnp.where(j == 0, 1.0, jnp.where(j == cur, 1.0, jnp.where(j == cur - 1, 1.0, 0.0)))
    score = jnp.where(j <= cur, p_slc + FORCE_BONUS * forced, NEG_INF)
    sel_ref[0, 0] = _topk_mask(score, min(SEL_TOPK, n_sel))


def _cmp_attn_call(qa, kc, vc, slopes):
    B, H, S, dk = qa.shape
    ncp = kc.shape[1]
    n_cmp = (S - CMP_BLOCK) // CMP_STRIDE + 1
    n_sel = S // SEL_BLOCK
    tq = 128
    starts = np.arange(ncp) * CMP_STRIDE
    jb = np.arange(n_sel) * SEL_BLOCK
    cover = ((starts[:, None] < jb[None, :] + SEL_BLOCK) & (starts[:, None] + CMP_BLOCK > jb[None, :])
             & (np.arange(ncp)[:, None] < n_cmp))
    cover = jnp.asarray(cover.astype(np.float32), dtype=BF16)
    return pl.pallas_call(
        functools.partial(_cmp_attn_kernel, slopes=tuple(float(s) for s in slopes), tq=tq, n_cmp=n_cmp),
        grid=(B, S // tq),
        in_specs=[
            pl.BlockSpec((1, H, tq, dk), lambda b, i: (b, 0, i, 0)),
            pl.BlockSpec((1, ncp, dk), lambda b, i: (b, 0, 0)),
            pl.BlockSpec((1, ncp, dk), lambda b, i: (b, 0, 0)),
            pl.BlockSpec((ncp, n_sel), lambda b, i: (0, 0)),
        ],
        out_specs=[
            pl.BlockSpec((1, tq, H * dk), lambda b, i: (b, i, 0)),
            pl.BlockSpec((1, 1, tq, n_sel), lambda b, i: (b, 0, i, 0)),
        ],
        out_shape=[jax.ShapeDtypeStruct((B, S, H * dk), F32),
                   jax.ShapeDtypeStruct((B, 1, S, n_sel), F32)],
        compiler_params=pltpu.CompilerParams(vmem_limit_bytes=VMEM_LIMIT),
        name="nsa_cmp_attn",
    )(qa, kc, vc, cover)


def _flash_kernel(*refs, H, G, tq, tk, slopes, window, mask_block, mask_per_head, has_sink):
    refs = list(refs)
    q_ref, k_ref, v_ref = refs[:3]
    pos = 3
    mask_ref = sink_ref = None
    if mask_block:
        mask_ref = refs[pos]
        pos += 1
    if has_sink:
        sink_ref = refs[pos]
        pos += 1
    o_ref, m_s, l_s, acc_s = refs[pos:pos + 4]
    pos += 4
    bias_s = need_s = None
    if slopes is not None:
        bias_s = refs[pos]
        pos += 1
    if mask_block:
        need_s = refs[pos]
    R = H // G
    M = R * tq
    dv = v_ref.shape[-1]
    i = pl.program_id(1)
    q0 = i * tq
    hi = q0 // tk
    lo = jnp.maximum(q0 - (window - 1), 0) // tk if window else 0

    def rel_pos():
        r_row = jnp.concatenate([lax.broadcasted_iota(jnp.int32, (1, tq), 1)] * R, axis=1)
        return (r_row - lax.broadcasted_iota(jnp.int32, (tk, M), 0)).astype(F32)

    qs, slope_rows, bms = [], [], []
    for g in range(G):
        if R == 1:
            qs.append(q_ref[0, g])
        else:
            qs.append(jnp.concatenate([q_ref[0, g * R + r] for r in range(R)], axis=0))
        if slopes is not None:
            slope_rows.append(jnp.concatenate(
                [jnp.full((1, tq), slopes[g * R + r], F32) for r in range(R)], axis=1))
            bias_s[g] = slope_rows[g] * rel_pos()
        if mask_block:
            if mask_per_head:
                bm = jnp.concatenate([mask_ref[0, g * R + r] for r in range(R)], axis=0)
            else:
                bm = jnp.concatenate([mask_ref[0, 0]] * R, axis=0)
            bms.append(bm)
    m_s[...] = jnp.full(m_s.shape, NEG_INF, F32)
    l_s[...] = jnp.zeros(l_s.shape, F32)
    acc_s[...] = jnp.zeros(acc_s.shape, F32)

    if mask_block:
        nblk = mask_ref.shape[-1]
        col = bms[0]
        for bm in bms[1:]:
            col = jnp.maximum(col, bm)
        col = jnp.max(col, axis=0, keepdims=True)
        bpt = tk // mask_block
        for jt in range(nblk // bpt):
            need_s[jt] = jnp.max(col[:, jt * bpt:(jt + 1) * bpt]).astype(jnp.int32)
        bms = [bm.T.astype(BF16) for bm in bms]

    def tile(j, edge):
        k0 = pl.multiple_of(j * tk, tk)
        off = (q0 - k0).astype(F32)
        valid = None
        if edge:
            dist = rel_pos() + off
            valid = dist >= 0.0
            if window:
                valid = valid & (dist < float(window))
        if mask_block:
            key_blk = lax.shift_right_logical(
                lax.broadcasted_iota(jnp.int32, (tk, nblk), 0) + k0, int(math.log2(mask_block)))
            expand = jnp.where(key_blk == lax.broadcasted_iota(jnp.int32, (tk, nblk), 1), 1.0, 0.0).astype(BF16)
        for g in range(G):
            k = k_ref[0, g, pl.ds(k0, tk), :]
            v = v_ref[0, g, pl.ds(k0, tk), :]
            s = _dot_nt(k, qs[g])
            if slopes is not None:
                s = s - bias_s[g]
                shift = slope_rows[g] * off
            ok = valid
            if mask_block:
                sel = _dot(expand, bms[g]) > 0.5
                ok = sel if ok is None else ok & sel
            if ok is not None:
                s = jnp.where(ok, s, NEG_INF)
            m_prev = m_s[g]
            s_max = jnp.max(s, axis=0, keepdims=True)
            if slopes is not None:
                s_max = s_max - shift
            m_new = jnp.maximum(m_prev, s_max)
            alpha = jnp.exp(m_prev - m_new)
            p = jnp.exp(s - (m_new + shift if slopes is not None else m_new))
            l_s[g] = alpha * l_s[g] + jnp.sum(p, axis=0, keepdims=True)
            pv = lax.dot_general(v, p.astype(BF16), (((0,), (0,)), ((), ())), preferred_element_type=F32)
            acc_s[g] = alpha * acc_s[g] + pv
            m_s[g] = m_new

    def body(step, carry):
        j = hi - step
        is_edge = step == 0
        if window:
            is_edge = is_edge | (q0 - j * tk + (tq - 1) >= window)
        run = (step == 0) | (need_s[j] > 0) if mask_block else None

        def when(c):
            return pl.when(c if run is None else c & run)

        @when(is_edge)
        def _():
            tile(j, True)

        @when(jnp.logical_not(is_edge))
        def _():
            tile(j, False)

        return carry

    lax.fori_loop(0, hi - lo + 1, body, 0)

    for g in range(G):
        m = m_s[g]
        l = l_s[g]
        acc = acc_s[g]
        if has_sink:
            sk = jnp.concatenate(
                [jnp.broadcast_to(sink_ref[:, g * R + r:g * R + r + 1], (1, tq)) for r in range(R)], axis=1)
            m_f = jnp.maximum(m, sk)
            a = jnp.exp(m - m_f)
            l = l * a + jnp.exp(sk - m_f)
            acc = acc * a
        out = acc / l
        for r in range(R):
            hh = g * R + r
            o_ref[0, :, hh * dv:(hh + 1) * dv] = out[:, r * tq:(r + 1) * tq].T.astype(o_ref.dtype)


def _flash_call(q, k, v, *, slopes=None, window=0, mask=None, mask_block=0, sinks=None,
                tq=128, tk=256, out_dtype=BF16, name="flash"):
    B, H, S, dq = q.shape
    G = k.shape[1]
    dv = v.shape[-1]
    tk = min(tk, S)
    assert tk % tq == 0 and S % tk == 0
    R = H // G
    in_specs = [
        pl.BlockSpec((1, H, tq, dq), lambda b, i: (b, 0, i, 0)),
        pl.BlockSpec((1, G, S, dq), lambda b, i: (b, 0, 0, 0)),
        pl.BlockSpec((1, G, S, dv), lambda b, i: (b, 0, 0, 0)),
    ]
    args = [q, k, v]
    mask_per_head = False
    if mask is not None:
        hm, nblk = mask.shape[1], mask.shape[3]
        mask_per_head = hm > 1
        in_specs.append(pl.BlockSpec((1, hm, tq, nblk), lambda b, i: (b, 0, i, 0)))
        args.append(mask)
    if sinks is not None:
        in_specs.append(pl.BlockSpec(sinks.shape, lambda b, i: (0, 0)))
        args.append(sinks)
    kern = functools.partial(
        _flash_kernel, H=H, G=G, tq=tq, tk=tk,
        slopes=None if slopes is None else tuple(float(s) for s in slopes),
        window=window, mask_block=mask_block if mask is not None else 0,
        mask_per_head=mask_per_head, has_sink=sinks is not None)
    scratch = [pltpu.VMEM((G, 1, R * tq), F32), pltpu.VMEM((G, 1, R * tq), F32),
               pltpu.VMEM((G, dv, R * tq), F32)]
    if slopes is not None:
        scratch.append(pltpu.VMEM((G, tk, R * tq), F32))
    if mask is not None:
        assert tk % mask_block == 0
        scratch.append(pltpu.SMEM((S // tk,), jnp.int32))
    return pl.pallas_call(
        kern,
        grid=(B, S // tq),
        in_specs=in_specs,
        out_specs=pl.BlockSpec((1, tq, H * dv), lambda b, i: (b, i, 0)),
        out_shape=jax.ShapeDtypeStruct((B, S, H * dv), out_dtype),
        scratch_shapes=scratch,
        compiler_params=pltpu.CompilerParams(vmem_limit_bytes=VMEM_LIMIT),
        name=name,
    )(*args)


def _merge_kernel(x_ref, sc1_ref, sh1_ref, g1_ref, sc2_ref, sh2_ref, ng_ref, wg_ref, wb_ref, wo_ref,
                  ocmp_ref, oslc_ref, owin_ref, gs_ref, ob_ref, oc_ref, od_ref, wr_ref, br_ref,
                  xo_ref, h2_ref, rt_ref):
    hd = HEAD_DIM
    x = x_ref[0]
    ng = ng_ref[...]
    h = _rms(x, ng[0:1]) * (1.0 + sc1_ref[0]) + sh1_ref[0]
    hb = h.astype(BF16)
    gs = gs_ref[0]
    ocmp = ocmp_ref[0]
    oslc = oslc_ref[0]
    owin = owin_ref[0]
    parts = []
    for hh in range(NSA_HEADS):
        c0 = GATE_LANE0 + 3 * hh
        sl = slice(hh * hd, (hh + 1) * hd)
        parts.append(gs[:, c0:c0 + 1] * ocmp[:, sl] + gs[:, c0 + 1:c0 + 2] * oslc[:, sl]
                     + gs[:, c0 + 2:c0 + 3] * owin[:, sl])
    o_a = jnp.concatenate(parts, axis=1).astype(BF16)
    branches = (o_a, ob_ref[0], oc_ref[0], od_ref[0])
    D = x.shape[1]
    mixed = None
    for n in range(N_BRANCH):
        gate = _sigmoid(_dot(hb, wg_ref[:, n * D:(n + 1) * D]))
        term = gate * _dot(branches[n], wb_ref[n])
        mixed = term if mixed is None else mixed + term
    xn = x + g1_ref[0] * _dot(mixed.astype(BF16), wo_ref[...])
    xo_ref[0] = xn
    h2 = _rms(xn, ng[1:2]) * (1.0 + sc2_ref[0]) + sh2_ref[0]
    h2_ref[0] = h2

    logits = _dot3(h2, wr_ref[...]) + br_ref[...]
    lane = lax.broadcasted_iota(jnp.int32, logits.shape, 1)
    lanef = lane.astype(F32)
    is_c = lane < N_GROUPS
    lc = jnp.where(is_c, logits, NEG_INF)
    mc = jnp.max(lc, axis=-1, keepdims=True)
    grp = jnp.min(jnp.where(lc == mc, lanef, 1e9), axis=-1, keepdims=True)
    p_grp = 1.0 / jnp.sum(jnp.where(is_c, jnp.exp(lc - mc), 0.0), axis=-1, keepdims=True)
    e_lane = lanef - float(N_GROUPS)
    in_grp = (lane >= N_GROUPS) & (lane < N_GROUPS + N_EXPERTS) & (
        jnp.floor(e_lane / EXPERTS_PER_GROUP) == grp)
    lf = jnp.where(in_grp, logits, NEG_INF)
    m1 = jnp.max(lf, axis=-1, keepdims=True)
    i1 = jnp.min(jnp.where(lf == m1, e_lane, 1e9), axis=-1, keepdims=True)
    lf2 = jnp.where(e_lane == i1, NEG_INF, lf)
    m2 = jnp.max(lf2, axis=-1, keepdims=True)
    i2 = jnp.min(jnp.where(lf2 == m2, e_lane, 1e9), axis=-1, keepdims=True)
    e2 = jnp.exp(m2 - m1)
    w1 = p_grp / (1.0 + e2)
    w2 = p_grp * e2 / (1.0 + e2)
    rt = jnp.where(lane == 0, i1, jnp.where(lane == 1, i2, jnp.where(lane == 2, w1, jnp.where(lane == 3, w2, 0.0))))
    rt_ref[0] = rt


def _merge_call(x, mods, ng, wg, wb, wo, ocmp, oslc, owin, gs, ob, oc, od, wr, br):
    B, S, D = x.shape
    tm = 256
    sc1, sh1, g1, sc2, sh2 = mods

    def full(a):
        return pl.BlockSpec(a.shape, lambda b, i: (0,) * a.ndim)

    modspec = pl.BlockSpec((1, 1, D), lambda b, i: (b, 0, 0))
    row = lambda w: pl.BlockSpec((1, tm, w), lambda b, i: (b, i, 0))
    in_specs = [row(D), modspec, modspec, modspec, modspec, modspec, full(ng), full(wg), full(wb), full(wo),
                row(256), row(256), row(256), row(128), row(256), row(256), row(256), full(wr), full(br)]
    return pl.pallas_call(
        _merge_kernel,
        grid=(B, S // tm),
        in_specs=in_specs,
        out_specs=[row(D), row(D), row(128)],
        out_shape=[jax.ShapeDtypeStruct((B, S, D), F32), jax.ShapeDtypeStruct((B, S, D), F32),
                   jax.ShapeDtypeStruct((B, S, 128), F32)],
        compiler_params=pltpu.CompilerParams(vmem_limit_bytes=VMEM_LIMIT),
        name="merge_router",
    )(x, sc1, sh1, g1, sc2, sh2, ng, wg, wb, wo, ocmp, oslc, owin, gs, ob, oc, od, wr, br)


def _expert_kernel(blk_e_ref, tok_ref, dst_ref, h2_hbm, w13_ref, w2_ref, y2_hbm, xbuf, ybuf, gsem, ssem):
    del blk_e_ref
    i = pl.program_id(0)
    nb = pl.num_programs(0)
    slot = i % 2
    rows = EXPERT_ROWS

    def gather_copy(blk, s, r):
        tok = tok_ref[blk * rows + r]
        return pltpu.make_async_copy(h2_hbm.at[pl.ds(tok, 1), :], xbuf.at[s, pl.ds(r, 1), :], gsem.at[s])

    def scatter_copy(blk, s, r):
        d = dst_ref[blk * rows + r]
        return pltpu.make_async_copy(ybuf.at[s, pl.ds(r, 1), :], y2_hbm.at[pl.ds(d, 1), :], ssem.at[s])

    def for_rows(fn):
        def body(r, carry):
            fn(r)
            return carry
        lax.fori_loop(0, rows, body, 0, unroll=8)

    @pl.when(i == 0)
    def _():
        for_rows(lambda r: gather_copy(0, 0, r).start())

    @pl.when(i + 1 < nb)
    def _():
        for_rows(lambda r: gather_copy(i + 1, 1 - slot, r).start())

    for_rows(lambda r: gather_copy(i, slot, r).wait())

    @pl.when(i >= 2)
    def _():
        for_rows(lambda r: scatter_copy(i - 2, slot, r).wait())

    xb = xbuf[slot].astype(BF16)
    ab = _dot(xb, w13_ref[0, 0].astype(BF16))
    de = ab.shape[1] // 2
    a = ab[:, :de]
    b = ab[:, de:]
    act = (a * _sigmoid(a)) * b
    ybuf[slot] = _dot(act.astype(BF16), w2_ref[0, 0].astype(BF16))
    for_rows(lambda r: scatter_copy(i, slot, r).start())

    @pl.when(i == nb - 1)
    def _():
        @pl.when(nb >= 2)
        def _():
            for_rows(lambda r: scatter_copy(i - 1, 1 - slot, r).wait())
        for_rows(lambda r: scatter_copy(i, slot, r).wait())


def _expert_call(h2, blk_e, row_tok, row_dst, w13, w2, layer):
    N, D = h2.shape
    R = row_tok.shape[0]
    n_blocks = R // EXPERT_ROWS
    de2 = w13.shape[-1]
    grid_spec = pltpu.PrefetchScalarGridSpec(
        num_scalar_prefetch=3,
        grid=(n_blocks,),
        in_specs=[
            pl.BlockSpec(memory_space=pl.ANY),
            pl.BlockSpec((1, 1, D, de2), lambda i, be, rt, rd: (layer, be[i], 0, 0)),
            pl.BlockSpec((1, 1, de2 // 2, D), lambda i, be, rt, rd: (layer, be[i], 0, 0)),
        ],
        out_specs=pl.BlockSpec(memory_space=pl.ANY),
        scratch_shapes=[pltpu.VMEM((2, EXPERT_ROWS, D), F32), pltpu.VMEM((2, EXPERT_ROWS, D), F32),
                        pltpu.SemaphoreType.DMA((2,)), pltpu.SemaphoreType.DMA((2,))],
    )
    return pl.pallas_call(
        _expert_kernel,
        grid_spec=grid_spec,
        out_shape=jax.ShapeDtypeStruct((R, D), F32),
        compiler_params=pltpu.CompilerParams(
            dimension_semantics=("arbitrary",), vmem_limit_bytes=VMEM_LIMIT),
        name="experts",
    )(blk_e, row_tok, row_dst, h2, w13, w2)


def _combine_kernel(xn_ref, g2_ref, rt_ref, y_ref, o_ref):
    D = xn_ref.shape[-1]
    rt = rt_ref[...]
    y = y_ref[...]
    o_ref[...] = xn_ref[...] + g2_ref[0] * (rt[:, 2:3] * y[:, :D] + rt[:, 3:4] * y[:, D:])


def _combine_call(xn, g2, route, y2):
    B, S, D = xn.shape
    tm = 512
    spb = S // tm
    y2v = y2.reshape(y2.shape[0] // 2, 2 * D)
    return pl.pallas_call(
        _combine_kernel,
        grid=(B, spb),
        in_specs=[
            pl.BlockSpec((tm, D), lambda b, i: (b * spb + i, 0)),
            pl.BlockSpec((1, 1, D), lambda b, i: (b, 0, 0)),
            pl.BlockSpec((tm, 128), lambda b, i: (b * spb + i, 0)),
            pl.BlockSpec((tm, 2 * D), lambda b, i: (b * spb + i, 0)),
        ],
        out_specs=pl.BlockSpec((tm, D), lambda b, i: (b * spb + i, 0)),
        out_shape=jax.ShapeDtypeStruct((B * S, D), F32),
        compiler_params=pltpu.CompilerParams(vmem_limit_bytes=VMEM_LIMIT),
        name="moe_combine",
    )(xn.reshape(B * S, D), g2, route, y2v).reshape(B, S, D)


def _moe(h2, route, w13, w2, layer):
    N, D = h2.shape
    K = 2
    E = N_EXPERTS
    rows = EXPERT_ROWS
    flat_e = route[:, 0:2].astype(jnp.int32).reshape(-1)
    order = jnp.argsort(flat_e).astype(jnp.int32)
    counts = jnp.sum((flat_e[:, None] == jnp.arange(E)[None, :]).astype(jnp.int32), axis=0)
    padded = (counts + rows - 1) // rows * rows
    pend = jnp.cumsum(padded)
    pstart = pend - padded
    start = jnp.cumsum(counts) - counts
    n_blocks = (N * K) // rows + E
    R = n_blocks * rows
    blk_e = jnp.minimum(
        jnp.sum((jnp.arange(n_blocks)[:, None] * rows >= pend[None, :]).astype(jnp.int32), axis=1), E - 1)
    pos = jnp.arange(R, dtype=jnp.int32)
    e_pos = jnp.repeat(blk_e, rows)
    local = pos - pstart[e_pos]
    is_real = (pos < pend[E - 1]) & (local < counts[e_pos])
    rank = jnp.clip(start[e_pos] + local, 0, N * K - 1)
    pair = order[rank]
    real_before = jnp.cumsum(is_real.astype(jnp.int32)) - is_real.astype(jnp.int32)
    row_tok = jnp.where(is_real, pair // K, 0).astype(jnp.int32)
    row_dst = jnp.where(is_real, pair, N * K + pos - real_before).astype(jnp.int32)
    return _expert_call(h2, blk_e.astype(jnp.int32), row_tok, row_dst, w13, w2, layer)


def kernel(x, c, w_ada, b_ada, norm_gain, w_in, qk_gain, cmp_pe, cmp_w1, cmp_w2, swa_sinks,
           lat_gain_q, lat_gain_kv, rope_gain, w_uq, w_ukv, w_branch, w_out,
           w_coarse, b_coarse, w_fine, b_fine, w13, w2):
    B, S, D = x.shape
    L = w_in.shape[0]
    assert S % MOBA_BLOCK == 0 and D == 1024
    slopes = _alibi_slopes()

    half = MLA_ROPE // 2
    inv = ROPE_THETA ** (-jnp.arange(half, dtype=F32) / half)
    ang = jnp.arange(S).astype(F32)[:, None] * inv[None, :]
    cos, sin = jnp.cos(ang), jnp.sin(ang)

    mod = _ada_mod(c, w_ada, b_ada)

    w_attn = jnp.concatenate(
        [w_in[:, :, :640], w_in[:, :, 652:ATTN_OLD], w_in[:, :, 640:652],
         jnp.zeros((L, D, ATTN_COLS - ATTN_OLD), F32)], axis=2).astype(BF16)
    w_gate = w_in[:, :, ATTN_OLD:].astype(BF16)
    w_uq_b = w_uq.astype(BF16)
    w_ukv_b = w_ukv.astype(BF16)
    w_branch_b = w_branch.astype(BF16)
    w_out_b = w_out.astype(BF16)
    w_router = jnp.concatenate(
        [w_coarse, w_fine, jnp.zeros((L, D, 128 - N_GROUPS - N_EXPERTS), F32)], axis=2)
    b_router = jnp.concatenate(
        [b_coarse, b_fine, jnp.zeros((L, 128 - N_GROUPS - N_EXPERTS), F32)], axis=1)
    sinks_pad = jnp.concatenate([swa_sinks, jnp.zeros((L, 128 - SWA_HEADS), F32)], axis=1)

    for l in range(L):
        m6 = mod[l].reshape(B, 6, 1, D)
        sh1, sc1, g1, sh2, sc2, g2 = (m6[:, j] for j in range(6))
        (qa, kcr, vcr, ks, vs, kw, vw, gs, qb, kb, vb, qc, kc, vc, mc, qd, kd, vd) = _proj_call(
            x, sc1, sh1, norm_gain[l, 0:1], w_attn[l], qk_gain[l], lat_gain_q[l][None], lat_gain_kv[l][None],
            rope_gain[l], w_uq_b[l], w_ukv_b[l], cos, sin)
        kcmp, vcmp = _compress_call(kcr, vcr, cmp_pe[l], cmp_w1[l], cmp_w2[l], qk_gain[l, QK_NSA_KC][None])
        o_cmp, sel = _cmp_attn_call(qa, kcmp, vcmp, slopes[0])
        o_slc = _flash_call(qa, ks, vs, slopes=slopes[0], mask=sel, mask_block=SEL_BLOCK,
                            tq=256, tk=256, out_dtype=F32, name="nsa_slc")
        o_win = _flash_call(qa, kw, vw, slopes=slopes[0], window=NSA_WINDOW,
                            tq=256, tk=256, out_dtype=F32, name="nsa_win")
        o_b = _flash_call(qb, kb, vb, slopes=slopes[1], window=SWA_WINDOW, sinks=sinks_pad[l][None],
                          tq=256, tk=256, name="swa")
        o_c = _flash_call(qc, kc, vc, slopes=slopes[2], mask=mc, mask_block=MOBA_BLOCK,
                          tq=256, tk=256, name="moba")
        o_d = _flash_call(qd, kd, vd, tq=256, tk=512, name="mla")
        xn, h2, route = _merge_call(
            x, (sc1, sh1, g1, sc2, sh2), norm_gain[l], w_gate[l], w_branch_b[l], w_out_b[l],
            o_cmp, o_slc, o_win, gs, o_b, o_c, o_d, w_router[l], b_router[l][None])
        route = route.reshape(B * S, 128)
        y2 = _moe(h2.reshape(B * S, D), route, w13, w2, l)
        x = _combine_call(xn, g2, route, y2)
    return x
```

```python
import functools
import math

import numpy as np
import jax
import jax.numpy as jnp
from jax import lax
from jax.experimental import pallas as pl
from jax.experimental.pallas import tpu as pltpu

F32 = jnp.float32
BF16 = jnp.bfloat16

HEAD_DIM = 64
NEG_INF = -1e30
EPS = 1e-6
NSA_HEADS = 4
CMP_BLOCK = 32
CMP_STRIDE = 16
CMP_HIDDEN = 256
SEL_BLOCK = 64
SEL_TOPK = 8
NSA_WINDOW = 512
FORCE_BONUS = 1e4
SWA_HEADS = 4
SWA_KV_HEADS = 2
SWA_WINDOW = 128
MOBA_HEADS = 4
MOBA_BLOCK = 256
MOBA_TOPK = 3
MLA_HEADS = 4
MLA_Q_RANK = 384
MLA_KV_RANK = 128
MLA_NOPE = 64
MLA_ROPE = 32
MLA_V = 64
ROPE_THETA = 10000.0
N_BRANCH = 4
BRANCH_WIDTH = 256
N_GROUPS = 4
EXPERTS_PER_GROUP = 8
N_EXPERTS = N_GROUPS * EXPERTS_PER_GROUP
D_EXPERT = 256

QK_NSA_Q, QK_NSA_KC, QK_NSA_KS, QK_NSA_KW = 0, 1, 2, 3
QK_SWA_Q, QK_SWA_K, QK_MOBA_Q, QK_MOBA_K, QK_MLA_Q, QK_MLA_K = 4, 5, 6, 7, 8, 9

ATTN_OLD = 2476
ATTN_COLS = 2560
GATE_LANE0 = 32
LOG2E = math.log2(math.e)
EXPERT_ROWS = 256
VMEM_LIMIT = 56 * 1024 * 1024


def _alibi_slopes():
    n = NSA_HEADS + SWA_HEADS + MOBA_HEADS

    def pow2(m):
        start = 2.0 ** (-8.0 / m)
        return [start ** (i + 1) for i in range(m)]

    c = 2 ** int(math.floor(math.log2(n)))
    s = pow2(c) + (pow2(2 * c)[0::2][: n - c] if c < n else [])
    s = -np.sort(-np.asarray(s, np.float32))
    return s.reshape(NSA_HEADS, 3).T


def _dot(a, b):
    return jnp.dot(a, b, preferred_element_type=F32)


def _dot_nt(a, b):
    return lax.dot_general(a, b, (((1,), (1,)), ((), ())), preferred_element_type=F32)


def _split(a):
    hi = a.astype(BF16)
    lo = (a - hi.astype(F32)).astype(BF16)
    return hi, lo


def _dot3(a, b):
    ah, al = _split(a)
    bh, bl = _split(b)
    return _dot(ah, bh) + (_dot(ah, bl) + _dot(al, bh))


def _dot3_nt(a, b):
    ah, al = _split(a)
    bh, bl = _split(b)
    return _dot_nt(ah, bh) + (_dot_nt(ah, bl) + _dot_nt(al, bh))


def _rms(x, g):
    return x * lax.rsqrt(jnp.mean(x * x, axis=-1, keepdims=True) + EPS) * g


def _sigmoid(x):
    return 1.0 / (1.0 + jnp.exp(-x))


def _topk_mask(score, k):
    n = score.shape[-1]
    iota = lax.broadcasted_iota(jnp.int32, score.shape, 1).astype(F32)
    sel = jnp.zeros(score.shape, F32)
    for _ in range(k):
        m = jnp.max(score, axis=-1, keepdims=True)
        idx = jnp.min(jnp.where(score == m, iota, float(n)), axis=-1, keepdims=True)
        hit = iota == idx
        sel = jnp.where(hit, jnp.where(m > 0.5 * NEG_INF, 1.0, 0.0), sel)
        score = jnp.where(hit, -3e38, score)
    return sel


def _ada_kernel(c_ref, w_ref, b_ref, o_ref):
    c = c_ref[...]
    a = c * _sigmoid(c)
    o_ref[0] = _dot(a, w_ref[0]) + b_ref[0]


def _ada_mod(c, w_ada, b_ada):
    L, D, D6 = w_ada.shape
    B = c.shape[0]
    tn = 1024
    return pl.pallas_call(
        _ada_kernel,
        grid=(L, D6 // tn),
        in_specs=[
            pl.BlockSpec((B, D), lambda l, j: (0, 0)),
            pl.BlockSpec((1, D, tn), lambda l, j: (l, 0, j)),
            pl.BlockSpec((1, 1, tn), lambda l, j: (l, 0, j)),
        ],
        out_specs=pl.BlockSpec((1, B, tn), lambda l, j: (l, 0, j)),
        out_shape=jax.ShapeDtypeStruct((L, B, D6), F32),
        compiler_params=pltpu.CompilerParams(vmem_limit_bytes=VMEM_LIMIT),
        name="ada_mod",
    )(c, w_ada, b_ada.reshape(L, 1, D6))


def _proj_kernel(x_ref, sc_ref, sh_ref, ng_ref, w_ref, qkg_ref, lgq_ref, lgkv_ref, rg_ref,
                 wuq_ref, wukv_ref, cos_ref, sin_ref,
                 qa_ref, kcr_ref, vcr_ref, ks_ref, vs_ref, kw_ref, vw_ref, gs_ref,
                 qb_ref, kb_ref, vb_ref, qc_ref, kc_ref, vc_ref, mc_ref, qd_ref, kd_ref, vd_ref,
                 kmean_s):
    i = pl.program_id(1)
    hd = HEAD_DIM
    x = x_ref[0]
    h = _rms(x, ng_ref[...]) * (1.0 + sc_ref[0]) + sh_ref[0]
    hb = h.astype(BF16)
    gq = qkg_ref[...]

    def gain(slot):
        return gq[slot:slot + 1]

    def proj(a, b):
        return _dot(hb, w_ref[:, a:b])

    scale = hd ** -0.5 * LOG2E

    p = proj(0, 256)
    for hh in range(NSA_HEADS):
        qa_ref[0, hh] = (_rms(p[:, hh * hd:(hh + 1) * hd], gain(QK_NSA_Q)) * scale).astype(BF16)
    p = proj(256, 640)
    kcr_ref[0] = p[:, 0:64]
    vcr_ref[0] = p[:, 64:128]
    ks_ref[0, 0] = _rms(p[:, 128:192], gain(QK_NSA_KS)).astype(BF16)
    vs_ref[0, 0] = p[:, 128:256].T[hd:].astype(BF16)
    kw_ref[0, 0] = _rms(p[:, 256:320], gain(QK_NSA_KW)).astype(BF16)
    vw_ref[0, 0] = p[:, 256:384].T[hd:].astype(BF16)

    p = proj(640, 1152)
    for hh in range(SWA_HEADS):
        qb_ref[0, hh] = (_rms(p[:, hh * hd:(hh + 1) * hd], gain(QK_SWA_Q)) * scale).astype(BF16)
    vt = p[:, 384:512].T
    for gg in range(SWA_KV_HEADS):
        kb_ref[0, gg] = _rms(p[:, 256 + gg * hd:256 + (gg + 1) * hd], gain(QK_SWA_K)).astype(BF16)
        vb_ref[0, gg] = vt[gg * hd:(gg + 1) * hd].astype(BF16)

    @pl.when(i == 0)
    def _():
        kmean_s[...] = jnp.zeros(kmean_s.shape, F32)

    p = proj(1152, 1920)
    nblk = kmean_s.shape[0]
    blk_iota = lax.broadcasted_iota(jnp.int32, (1, nblk), 1)
    kmeans = []
    vt = p[:, 512:768].T
    for hh in range(MOBA_HEADS):
        qn = _rms(p[:, hh * hd:(hh + 1) * hd], gain(QK_MOBA_Q))
        kn = _rms(p[:, 256 + hh * hd:256 + (hh + 1) * hd], gain(QK_MOBA_K))
        qc_ref[0, hh] = (qn * scale).astype(BF16)
        kc_ref[0, hh] = kn.astype(BF16)
        vc_ref[0, hh] = vt[hh * hd:(hh + 1) * hd].astype(BF16)
        kmeans.append(jnp.mean(kn, axis=0, keepdims=True))
        g = _dot3_nt(qn, kmean_s[:, hh * hd:(hh + 1) * hd])
        g = jnp.where(blk_iota < i, g, NEG_INF)
        mc_ref[0, hh] = jnp.where(blk_iota == i, 1.0, _topk_mask(g, MOBA_TOPK))
    kmean_s[pl.ds(i, 1), :] = jnp.concatenate(kmeans, axis=1)

    p = proj(1920, 2560)
    gs_ref[0] = _sigmoid(p[:, 512:640])
    cos = cos_ref[...]
    sin = sin_ref[...]
    half = MLA_ROPE // 2

    def rope(v):
        v1 = v[:, :half]
        v2 = v[:, half:]
        return jnp.concatenate([v1 * cos - v2 * sin, v1 * sin + v2 * cos], axis=1)

    rg = rg_ref[...]
    qlat = _dot(_rms(p[:, 0:384], lgq_ref[...]).astype(BF16), wuq_ref[...])
    kvlat = _dot(_rms(p[:, 384:512], lgkv_ref[...]).astype(BF16), wukv_ref[...])
    k_rot = rope(_rms(p[:, 512:544], rg[1:2]))
    dq = MLA_NOPE + MLA_ROPE
    dkv = MLA_NOPE + MLA_V
    mla_scale = dq ** -0.5 * LOG2E
    kvt = kvlat.T
    for hh in range(MLA_HEADS):
        qh = qlat[:, hh * dq:(hh + 1) * dq]
        q_nope = _rms(qh[:, :MLA_NOPE], gain(QK_MLA_Q))
        q_rot = rope(_rms(qh[:, MLA_NOPE:], rg[0:1]))
        qd_ref[0, hh] = (jnp.concatenate([q_nope, q_rot], axis=1) * mla_scale).astype(BF16)
        kvh = kvlat[:, hh * dkv:(hh + 1) * dkv]
        k_nope = _rms(kvh[:, :MLA_NOPE], gain(QK_MLA_K))
        kd_ref[0, hh] = jnp.concatenate([k_nope, k_rot], axis=1).astype(BF16)
        vd_ref[0, hh] = kvt[hh * dkv + MLA_NOPE:(hh + 1) * dkv].astype(BF16)


def _proj_call(x, sc1, sh1, ng, w_attn, qkg, lgq, lgkv, rg, wuq, wukv, cos, sin):
    B, S, D = x.shape
    tm = MOBA_BLOCK
    nblk = S // tm
    hd = HEAD_DIM

    def full(shape):
        return pl.BlockSpec(shape, lambda b, i: (0,) * len(shape))

    def heads(nh, d):
        return pl.BlockSpec((1, nh, tm, d), lambda b, i: (b, 0, i, 0))

    in_specs = [
        pl.BlockSpec((1, tm, D), lambda b, i: (b, i, 0)),
        pl.BlockSpec((1, 1, D), lambda b, i: (b, 0, 0)),
        pl.BlockSpec((1, 1, D), lambda b, i: (b, 0, 0)),
        full((1, D)),
        full((D, ATTN_COLS)),
        full(qkg.shape),
        full(lgq.shape),
        full(lgkv.shape),
        full(rg.shape),
        full(wuq.shape),
        full(wukv.shape),
        pl.BlockSpec((tm, MLA_ROPE // 2), lambda b, i: (i, 0)),
        pl.BlockSpec((tm, MLA_ROPE // 2), lambda b, i: (i, 0)),
    ]
    row64 = pl.BlockSpec((1, tm, hd), lambda b, i: (b, i, 0))
    def heads_t(nh, d):
        return pl.BlockSpec((1, nh, d, tm), lambda b, i: (b, 0, 0, i))

    out_specs = [
        heads(4, hd), row64, row64, heads(1, hd), heads_t(1, hd), heads(1, hd), heads_t(1, hd),
        pl.BlockSpec((1, tm, 128), lambda b, i: (b, i, 0)),
        heads(4, hd), heads(2, hd), heads_t(2, hd),
        heads(4, hd), heads(4, hd), heads_t(4, hd), heads(4, nblk),
        heads(4, MLA_NOPE + MLA_ROPE), heads(4, MLA_NOPE + MLA_ROPE), heads_t(4, MLA_V),
    ]

    def sd(shape, dt):
        return jax.ShapeDtypeStruct(shape, dt)

    out_shape = [
        sd((B, 4, S, hd), BF16), sd((B, S, hd), F32), sd((B, S, hd), F32),
        sd((B, 1, S, hd), BF16), sd((B, 1, hd, S), BF16), sd((B, 1, S, hd), BF16), sd((B, 1, hd, S), BF16),
        sd((B, S, 128), F32),
        sd((B, 4, S, hd), BF16), sd((B, 2, S, hd), BF16), sd((B, 2, hd, S), BF16),
        sd((B, 4, S, hd), BF16), sd((B, 4, S, hd), BF16), sd((B, 4, hd, S), BF16), sd((B, 4, S, nblk), F32),
        sd((B, 4, S, 96), BF16), sd((B, 4, S, 96), BF16), sd((B, 4, MLA_V, S), BF16),
    ]
    return pl.pallas_call(
        _proj_kernel,
        grid=(B, nblk),
        in_specs=in_specs,
        out_specs=out_specs,
        out_shape=out_shape,
        scratch_shapes=[pltpu.VMEM((nblk, MOBA_HEADS * hd), F32)],
        compiler_params=pltpu.CompilerParams(
            dimension_semantics=("arbitrary", "arbitrary"), vmem_limit_bytes=VMEM_LIMIT),
        name="proj_prep",
    )(x, sc1, sh1, ng, w_attn, qkg, lgq, lgkv, rg, wuq, wukv, cos, sin)


def _compress_kernel(gk_ref, gv_ref, pe_ref, w1_ref, w2_ref, gkc_ref, kc_ref, vc_ref):
    half = w1_ref.shape[1] // 2
    outs = []
    for j, g_ref in enumerate((gk_ref, gv_ref)):
        g = g_ref[0].astype(BF16)
        top = _dot(g, w1_ref[j, :half].astype(BF16))
        bot = _dot(g, w1_ref[j, half:].astype(BF16))
        bot = jnp.concatenate([bot[1:], bot[:1]], axis=0)
        pe = jnp.broadcast_to(pe_ref[j], (8, pe_ref.shape[2]))
        bias = _dot3(pe, w1_ref[j])[0:1]
        hid = top + bot + bias
        hid = hid * _sigmoid(hid)
        outs.append(_dot(hid.astype(BF16), w2_ref[j].astype(BF16)))
    kc_ref[0] = _rms(outs[0], gkc_ref[...]).astype(BF16)
    vc_ref[0] = outs[1].astype(BF16)


def _compress_call(kc_raw, vc_raw, pe, w1, w2, g_kc):
    B, S, dk = kc_raw.shape
    n_grp = S // CMP_STRIDE
    gk = kc_raw.reshape(B, n_grp, CMP_STRIDE * dk)
    gv = vc_raw.reshape(B, n_grp, CMP_STRIDE * dk)
    pe_flat = pe.reshape(2, 1, CMP_BLOCK * dk)
    grp_spec = pl.BlockSpec((1, n_grp, CMP_STRIDE * dk), lambda b: (b, 0, 0))
    out_spec = pl.BlockSpec((1, n_grp, dk), lambda b: (b, 0, 0))
    return pl.pallas_call(
        _compress_kernel,
        grid=(B,),
        in_specs=[
            grp_spec, grp_spec,
            pl.BlockSpec(pe_flat.shape, lambda b: (0, 0, 0)),
            pl.BlockSpec(w1.shape, lambda b: (0, 0, 0)),
            pl.BlockSpec(w2.shape, lambda b: (0, 0, 0)),
            pl.BlockSpec((1, dk), lambda b: (0, 0)),
        ],
        out_specs=[out_spec, out_spec],
        out_shape=[jax.ShapeDtypeStruct((B, n_grp, dk), BF16)] * 2,
        compiler_params=pltpu.CompilerParams(vmem_limit_bytes=VMEM_LIMIT),
        name="nsa_compress",
    )(gk, gv, pe_flat, w1, w2, g_kc)


def _cmp_attn_kernel(q_ref, kc_ref, vc_ref, cover_ref, o_ref, sel_ref, *, slopes, tq, n_cmp):
    i = pl.program_id(1)
    kc = kc_ref[0]
    vc = vc_ref[0]
    ncp = kc.shape[0]
    t = (i * tq + lax.broadcasted_iota(jnp.int32, (tq, 1), 0))
    n_iota = lax.broadcasted_iota(jnp.int32, (1, ncp), 1)
    dist_i = t - (n_iota * CMP_STRIDE + (CMP_BLOCK - 1))
    vis = (n_iota < n_cmp) & (dist_i >= 0)
    dist = dist_i.astype(F32)
    visf = vis.astype(F32)
    psum = jnp.zeros((tq, ncp), F32)
    outs = []
    for hh in range(NSA_HEADS):
        s = _dot_nt(q_ref[0, hh], kc) - (slopes[hh] * LOG2E) * dist
        s = jnp.where(vis, s, NEG_INF)
        e = jnp.exp2(s - jnp.max(s, axis=-1, keepdims=True)) * visf
        p = e / jnp.maximum(jnp.sum(e, axis=-1, keepdims=True), 1e-30)
        outs.append(_dot(p.astype(BF16), vc))
        psum = psum + p
    o_ref[0] = jnp.concatenate(outs, axis=1)
    ph, plo = _split(psum)
    cover = cover_ref[...]
    p_slc = _dot(ph, cover) + _dot(plo, cover)
    n_sel = cover.shape[1]
    cur = t // SEL_BLOCK
    j = lax.broadcasted_iota(jnp.int32, (1, n_sel), 1)
    forced = jnp.where(j == 0, 1.0, jnp.where(j == cur, 1.0, jnp.where(j == cur - 1, 1.0, 0.0)))
    score = jnp.where(j <= cur, p_slc + FORCE_BONUS * forced, NEG_INF)
    sel_ref[0, 0] = _topk_mask(score, min(SEL_TOPK, n_sel))


def _cmp_attn_call(qa, kc, vc, slopes):
    B, H, S, dk = qa.shape
    ncp = kc.shape[1]
    n_cmp = (S - CMP_BLOCK) // CMP_STRIDE + 1
    n_sel = S // SEL_BLOCK
    tq = 128
    starts = np.arange(ncp) * CMP_STRIDE
    jb = np.arange(n_sel) * SEL_BLOCK
    cover = ((starts[:, None] < jb[None, :] + SEL_BLOCK) & (starts[:, None] + CMP_BLOCK > jb[None, :])
             & (np.arange(ncp)[:, None] < n_cmp))
    cover = jnp.asarray(cover.astype(np.float32), dtype=BF16)
    return pl.pallas_call(
        functools.partial(_cmp_attn_kernel, slopes=tuple(float(s) for s in slopes), tq=tq, n_cmp=n_cmp),
        grid=(B, S // tq),
        in_specs=[
            pl.BlockSpec((1, H, tq, dk), lambda b, i: (b, 0, i, 0)),
            pl.BlockSpec((1, ncp, dk), lambda b, i: (b, 0, 0)),
            pl.BlockSpec((1, ncp, dk), lambda b, i: (b, 0, 0)),
            pl.BlockSpec((ncp, n_sel), lambda b, i: (0, 0)),
        ],
        out_specs=[
            pl.BlockSpec((1, tq, H * dk), lambda b, i: (b, i, 0)),
            pl.BlockSpec((1, 1, tq, n_sel), lambda b, i: (b, 0, i, 0)),
        ],
        out_shape=[jax.ShapeDtypeStruct((B, S, H * dk), F32),
                   jax.ShapeDtypeStruct((B, 1, S, n_sel), F32)],
        compiler_params=pltpu.CompilerParams(vmem_limit_bytes=VMEM_LIMIT),
        name="nsa_cmp_attn",
    )(qa, kc, vc, cover)


def _flash_kernel(*refs, H, G, hpc, tq, tk, slopes, window, mask_block, mask_per_head, has_sink):
    refs = list(refs)
    q_ref, k_ref, v_ref = refs[:3]
    pos = 3
    mask_ref = sink_ref = None
    if mask_block:
        mask_ref = refs[pos]
        pos += 1
    if has_sink:
        sink_ref = refs[pos]
        pos += 1
    o_ref, m_s, l_s, acc_s = refs[pos:pos + 4]
    pos += 4
    bias_s = need_s = None
    if slopes is not None:
        bias_s = refs[pos]
        pos += 1
    if mask_block:
        need_s = refs[pos]
    R = H // G
    C = H // hpc
    M = hpc * tq
    dv = v_ref.shape[-2]
    i = pl.program_id(1)
    q0 = i * tq
    hi = q0 // tk
    lo = jnp.maximum(q0 - (window - 1), 0) // tk if window else 0

    def rel_pos():
        r_row = jnp.concatenate([lax.broadcasted_iota(jnp.int32, (1, tq), 1)] * hpc, axis=1)
        return (r_row - lax.broadcasted_iota(jnp.int32, (tk, M), 0)).astype(F32)

    qs, slope_rows, bms = [], [], []
    for c in range(C):
        heads = [c * hpc + r for r in range(hpc)]
        if hpc == 1:
            qs.append(q_ref[0, heads[0]])
        else:
            qs.append(jnp.concatenate([q_ref[0, hh] for hh in heads], axis=0))
        if slopes is not None:
            slope_rows.append(jnp.concatenate([jnp.full((1, tq), slopes[hh], F32) for hh in heads], axis=1))
            bias_s[c] = slope_rows[c] * rel_pos()
        if mask_block:
            if mask_per_head:
                bm = jnp.concatenate([mask_ref[0, hh] for hh in heads], axis=0)
            else:
                bm = jnp.concatenate([mask_ref[0, 0]] * hpc, axis=0)
            bms.append(bm)
    m_s[...] = jnp.full(m_s.shape, NEG_INF, F32)
    l_s[...] = jnp.zeros(l_s.shape, F32)
    acc_s[...] = jnp.zeros(acc_s.shape, F32)

    if mask_block:
        nblk = mask_ref.shape[-1]
        col = bms[0]
        for bm in bms[1:]:
            col = jnp.maximum(col, bm)
        col = jnp.max(col, axis=0, keepdims=True)
        bpt = tk // mask_block
        for jt in range(nblk // bpt):
            need_s[jt] = jnp.max(col[:, jt * bpt:(jt + 1) * bpt]).astype(jnp.int32)
        bms = [bm.T.astype(BF16) for bm in bms]

    def tile(j, edge):
        k0 = pl.multiple_of(j * tk, tk)
        off = (q0 - k0).astype(F32)
        valid = None
        if edge:
            dist = rel_pos() + off
            valid = dist >= 0.0
            if window:
                valid = valid & (dist < float(window))
        if mask_block:
            key_blk = lax.shift_right_logical(
                lax.broadcasted_iota(jnp.int32, (tk, nblk), 0) + k0, int(math.log2(mask_block)))
            expand = jnp.where(key_blk == lax.broadcasted_iota(jnp.int32, (tk, nblk), 1), 1.0, 0.0).astype(BF16)
        scores = [_dot_nt(k_ref[0, (c * hpc) // R, pl.ds(k0, tk), :], qs[c]) for c in range(C)]
        sels = [_dot(expand, bms[c]) for c in range(C)] if mask_block else None
        probs, alphas = [], []
        for g in range(C):
            s = scores[g]
            if slopes is not None:
                s = s - bias_s[g]
                shift = slope_rows[g] * off
            ok = valid
            if mask_block:
                sel = sels[g] > 0.5
                ok = sel if ok is None else ok & sel
            if ok is not None:
                s = jnp.where(ok, s, NEG_INF)
            m_prev = m_s[g]
            s_max = jnp.max(s, axis=0, keepdims=True)
            if slopes is not None:
                s_max = s_max - shift
            m_new = jnp.maximum(m_prev, s_max)
            alpha = jnp.exp2(m_prev - m_new)
            p = jnp.exp2(s - (m_new + shift if slopes is not None else m_new))
            l_s[g] = alpha * l_s[g] + jnp.sum(p, axis=0, keepdims=True)
            m_s[g] = m_new
            probs.append(p.astype(BF16))
            alphas.append(alpha)
        for c in range(C):
            v = v_ref[0, (c * hpc) // R, :, pl.ds(k0, tk)]
            acc_s[c] = alphas[c] * acc_s[c] + _dot(v, probs[c])

    def body(step, carry):
        j = hi - step
        is_edge = step == 0
        if window:
            is_edge = is_edge | (q0 - j * tk + (tq - 1) >= window)
        run = (step == 0) | (need_s[j] > 0) if mask_block else None

        def when(c):
            return pl.when(c if run is None else c & run)

        @when(is_edge)
        def _():
            tile(j, True)

        @when(jnp.logical_not(is_edge))
        def _():
            tile(j, False)

        return carry

    lax.fori_loop(0, hi - lo + 1, body, 0)

    for c in range(C):
        m = m_s[c]
        l = l_s[c]
        acc = acc_s[c]
        if has_sink:
            sk = LOG2E * jnp.concatenate(
                [jnp.broadcast_to(sink_ref[:, c * hpc + r:c * hpc + r + 1], (1, tq)) for r in range(hpc)], axis=1)
            m_f = jnp.maximum(m, sk)
            a = jnp.exp2(m - m_f)
            l = l * a + jnp.exp2(sk - m_f)
            acc = acc * a
        out = acc / l
        for r in range(hpc):
            hh = c * hpc + r
            o_ref[0, :, hh * dv:(hh + 1) * dv] = out[:, r * tq:(r + 1) * tq].T.astype(o_ref.dtype)


def _flash_call(q, k, v, *, slopes=None, window=0, mask=None, mask_block=0, sinks=None,
                tq=128, tk=256, hpc=None, out_dtype=BF16, name="flash"):
    B, H, S, dq = q.shape
    G = k.shape[1]
    dv = v.shape[-2]
    tk = min(tk, S)
    assert tk % tq == 0 and S % tk == 0
    R = H // G
    hpc = R if hpc is None else hpc
    assert R % hpc == 0
    C = H // hpc
    in_specs = [
        pl.BlockSpec((1, H, tq, dq), lambda b, i: (b, 0, i, 0)),
        pl.BlockSpec((1, G, S, dq), lambda b, i: (b, 0, 0, 0)),
        pl.BlockSpec((1, G, dv, S), lambda b, i: (b, 0, 0, 0)),
    ]
    args = [q, k, v]
    mask_per_head = False
    if mask is not None:
        hm, nblk = mask.shape[1], mask.shape[3]
        mask_per_head = hm > 1
        in_specs.append(pl.BlockSpec((1, hm, tq, nblk), lambda b, i: (b, 0, i, 0)))
        args.append(mask)
    if sinks is not None:
        in_specs.append(pl.BlockSpec(sinks.shape, lambda b, i: (0, 0)))
        args.append(sinks)
    kern = functools.partial(
        _flash_kernel, H=H, G=G, hpc=hpc, tq=tq, tk=tk,
        slopes=None if slopes is None else tuple(float(s) * LOG2E for s in slopes),
        window=window, mask_block=mask_block if mask is not None else 0,
        mask_per_head=mask_per_head, has_sink=sinks is not None)
    scratch = [pltpu.VMEM((C, 1, hpc * tq), F32), pltpu.VMEM((C, 1, hpc * tq), F32),
               pltpu.VMEM((C, dv, hpc * tq), F32)]
    if slopes is not None:
        scratch.append(pltpu.VMEM((C, tk, hpc * tq), F32))
    if mask is not None:
        assert tk % mask_block == 0
        scratch.append(pltpu.SMEM((S // tk,), jnp.int32))
    return pl.pallas_call(
        kern,
        grid=(B, S // tq),
        in_specs=in_specs,
        out_specs=pl.BlockSpec((1, tq, H * dv), lambda b, i: (b, i, 0)),
        out_shape=jax.ShapeDtypeStruct((B, S, H * dv), out_dtype),
        scratch_shapes=scratch,
        compiler_params=pltpu.CompilerParams(vmem_limit_bytes=VMEM_LIMIT),
        name=name,
    )(*args)


def _merge_kernel(x_ref, sc1_ref, sh1_ref, g1_ref, sc2_ref, sh2_ref, ng_ref, wg_ref, wb_ref, wo_ref,
                  ocmp_ref, oslc_ref, owin_ref, gs_ref, ob_ref, oc_ref, od_ref, wr_ref, br_ref,
                  xo_ref, h2_ref, rt_ref):
    hd = HEAD_DIM
    x = x_ref[0]
    ng = ng_ref[...]
    h = _rms(x, ng[0:1]) * (1.0 + sc1_ref[0]) + sh1_ref[0]
    hb = h.astype(BF16)
    gs = gs_ref[0]
    ocmp = ocmp_ref[0]
    oslc = oslc_ref[0]
    owin = owin_ref[0]
    parts = []
    for hh in range(NSA_HEADS):
        c0 = GATE_LANE0 + 3 * hh
        sl = slice(hh * hd, (hh + 1) * hd)
        parts.append(gs[:, c0:c0 + 1] * ocmp[:, sl] + gs[:, c0 + 1:c0 + 2] * oslc[:, sl]
                     + gs[:, c0 + 2:c0 + 3] * owin[:, sl])
    o_a = jnp.concatenate(parts, axis=1).astype(BF16)
    branches = (o_a, ob_ref[0], oc_ref[0], od_ref[0])
    D = x.shape[1]
    mixed = None
    for n in range(N_BRANCH):
        gate = _sigmoid(_dot(hb, wg_ref[:, n * D:(n + 1) * D]))
        term = gate * _dot(branches[n], wb_ref[n])
        mixed = term if mixed is None else mixed + term
    xn = x + g1_ref[0] * _dot(mixed.astype(BF16), wo_ref[...])
    xo_ref[0] = xn
    h2 = _rms(xn, ng[1:2]) * (1.0 + sc2_ref[0]) + sh2_ref[0]
    h2_ref[0] = h2

    logits = _dot3(h2, wr_ref[...]) + br_ref[...]
    lane = lax.broadcasted_iota(jnp.int32, logits.shape, 1)
    lanef = lane.astype(F32)
    is_c = lane < N_GROUPS
    lc = jnp.where(is_c, logits, NEG_INF)
    mc = jnp.max(lc, axis=-1, keepdims=True)
    grp = jnp.min(jnp.where(lc == mc, lanef, 1e9), axis=-1, keepdims=True)
    p_grp = 1.0 / jnp.sum(jnp.where(is_c, jnp.exp(lc - mc), 0.0), axis=-1, keepdims=True)
    e_lane = lanef - float(N_GROUPS)
    in_grp = (lane >= N_GROUPS) & (lane < N_GROUPS + N_EXPERTS) & (
        jnp.floor(e_lane / EXPERTS_PER_GROUP) == grp)
    lf = jnp.where(in_grp, logits, NEG_INF)
    m1 = jnp.max(lf, axis=-1, keepdims=True)
    i1 = jnp.min(jnp.where(lf == m1, e_lane, 1e9), axis=-1, keepdims=True)
    lf2 = jnp.where(e_lane == i1, NEG_INF, lf)
    m2 = jnp.max(lf2, axis=-1, keepdims=True)
    i2 = jnp.min(jnp.where(lf2 == m2, e_lane, 1e9), axis=-1, keepdims=True)
    e2 = jnp.exp(m2 - m1)
    w1 = p_grp / (1.0 + e2)
    w2 = p_grp * e2 / (1.0 + e2)
    rt = jnp.where(lane == 0, i1, jnp.where(lane == 1, i2, jnp.where(lane == 2, w1, jnp.where(lane == 3, w2, 0.0))))
    rt_ref[0] = rt


def _merge_call(x, mods, ng, wg, wb, wo, ocmp, oslc, owin, gs, ob, oc, od, wr, br):
    B, S, D = x.shape
    tm = 256
    sc1, sh1, g1, sc2, sh2 = mods

    def full(a):
        return pl.BlockSpec(a.shape, lambda b, i: (0,) * a.ndim)

    modspec = pl.BlockSpec((1, 1, D), lambda b, i: (b, 0, 0))
    row = lambda w: pl.BlockSpec((1, tm, w), lambda b, i: (b, i, 0))
    in_specs = [row(D), modspec, modspec, modspec, modspec, modspec, full(ng), full(wg), full(wb), full(wo),
                row(256), row(256), row(256), row(128), row(256), row(256), row(256), full(wr), full(br)]
    return pl.pallas_call(
        _merge_kernel,
        grid=(B, S // tm),
        in_specs=in_specs,
        out_specs=[row(D), row(D), row(128)],
        out_shape=[jax.ShapeDtypeStruct((B, S, D), F32), jax.ShapeDtypeStruct((B, S, D), F32),
                   jax.ShapeDtypeStruct((B, S, 128), F32)],
        compiler_params=pltpu.CompilerParams(vmem_limit_bytes=VMEM_LIMIT),
        name="merge_router",
    )(x, sc1, sh1, g1, sc2, sh2, ng, wg, wb, wo, ocmp, oslc, owin, gs, ob, oc, od, wr, br)


def _expert_kernel(blk_e_ref, tok_ref, dst_ref, h2_hbm, w13_ref, w2_ref, y2_hbm, xbuf, ybuf, gsem, ssem):
    del blk_e_ref
    i = pl.program_id(0)
    nb = pl.num_programs(0)
    slot = i % 2
    rows = EXPERT_ROWS

    def gather_copy(blk, s, r):
        tok = tok_ref[blk * rows + r]
        return pltpu.make_async_copy(h2_hbm.at[pl.ds(tok, 1), :], xbuf.at[s, pl.ds(r, 1), :], gsem.at[s])

    def scatter_copy(blk, s, r):
        d = dst_ref[blk * rows + r]
        return pltpu.make_async_copy(ybuf.at[s, pl.ds(r, 1), :], y2_hbm.at[pl.ds(d, 1), :], ssem.at[s])

    def for_rows(fn):
        def body(r, carry):
            fn(r)
            return carry
        lax.fori_loop(0, rows, body, 0, unroll=8)

    @pl.when(i == 0)
    def _():
        for_rows(lambda r: gather_copy(0, 0, r).start())

    @pl.when(i + 1 < nb)
    def _():
        for_rows(lambda r: gather_copy(i + 1, 1 - slot, r).start())

    def wait_gather(s):
        pltpu.make_async_copy(h2_hbm.at[pl.ds(0, rows), :], xbuf.at[s], gsem.at[s]).wait()

    def wait_scatter(s):
        pltpu.make_async_copy(ybuf.at[s], y2_hbm.at[pl.ds(0, rows), :], ssem.at[s]).wait()

    wait_gather(slot)

    @pl.when(i >= 2)
    def _():
        wait_scatter(slot)

    xb = xbuf[slot].astype(BF16)
    ab = _dot(xb, w13_ref[0, 0].astype(BF16))
    de = ab.shape[1] // 2
    a = ab[:, :de]
    b = ab[:, de:]
    act = (a * _sigmoid(a)) * b
    ybuf[slot] = _dot(act.astype(BF16), w2_ref[0, 0].astype(BF16))
    for_rows(lambda r: scatter_copy(i, slot, r).start())

    @pl.when(i == nb - 1)
    def _():
        @pl.when(nb >= 2)
        def _():
            wait_scatter(1 - slot)
        wait_scatter(slot)


def _expert_call(h2, blk_e, row_tok, row_dst, w13, w2, layer):
    N, D = h2.shape
    R = row_tok.shape[0]
    n_blocks = R // EXPERT_ROWS
    de2 = w13.shape[-1]
    grid_spec = pltpu.PrefetchScalarGridSpec(
        num_scalar_prefetch=3,
        grid=(n_blocks,),
        in_specs=[
            pl.BlockSpec(memory_space=pl.ANY),
            pl.BlockSpec((1, 1, D, de2), lambda i, be, rt, rd: (layer, be[i], 0, 0)),
            pl.BlockSpec((1, 1, de2 // 2, D), lambda i, be, rt, rd: (layer, be[i], 0, 0)),
        ],
        out_specs=pl.BlockSpec(memory_space=pl.ANY),
        scratch_shapes=[pltpu.VMEM((2, EXPERT_ROWS, D), F32), pltpu.VMEM((2, EXPERT_ROWS, D), F32),
                        pltpu.SemaphoreType.DMA((2,)), pltpu.SemaphoreType.DMA((2,))],
    )
    return pl.pallas_call(
        _expert_kernel,
        grid_spec=grid_spec,
        out_shape=jax.ShapeDtypeStruct((R, D), F32),
        compiler_params=pltpu.CompilerParams(
            dimension_semantics=("arbitrary",), vmem_limit_bytes=VMEM_LIMIT),
        name="experts",
    )(blk_e, row_tok, row_dst, h2, w13, w2)


def _combine_kernel(xn_ref, g2_ref, rt_ref, y0_ref, y1_ref, o_ref):
    rt = rt_ref[...]
    o_ref[...] = xn_ref[...] + g2_ref[0] * (rt[:, 2:3] * y0_ref[...] + rt[:, 3:4] * y1_ref[...])


def _combine_call(xn, g2, route, y2):
    B, S, D = xn.shape
    tm = 512
    spb = S // tm
    nt = B * spb
    return pl.pallas_call(
        _combine_kernel,
        grid=(B, spb),
        in_specs=[
            pl.BlockSpec((tm, D), lambda b, i: (b * spb + i, 0)),
            pl.BlockSpec((1, 1, D), lambda b, i: (b, 0, 0)),
            pl.BlockSpec((tm, 128), lambda b, i: (b * spb + i, 0)),
            pl.BlockSpec((tm, D), lambda b, i: (b * spb + i, 0)),
            pl.BlockSpec((tm, D), lambda b, i: (nt + b * spb + i, 0)),
        ],
        out_specs=pl.BlockSpec((tm, D), lambda b, i: (b * spb + i, 0)),
        out_shape=jax.ShapeDtypeStruct((B * S, D), F32),
        compiler_params=pltpu.CompilerParams(vmem_limit_bytes=VMEM_LIMIT),
        name="moe_combine",
    )(xn.reshape(B * S, D), g2, route, y2, y2).reshape(B, S, D)


def _moe(h2, route, w13, w2, layer):
    N, D = h2.shape
    K = 2
    E = N_EXPERTS
    rows = EXPERT_ROWS
    flat_e = route[:, 0:2].astype(jnp.int32).reshape(-1)
    order = jnp.argsort(flat_e).astype(jnp.int32)
    counts = jnp.sum((flat_e[:, None] == jnp.arange(E)[None, :]).astype(jnp.int32), axis=0)
    padded = (counts + rows - 1) // rows * rows
    pend = jnp.cumsum(padded)
    pstart = pend - padded
    start = jnp.cumsum(counts) - counts
    n_blocks = (N * K) // rows + E
    R = n_blocks * rows
    blk_e = jnp.minimum(
        jnp.sum((jnp.arange(n_blocks)[:, None] * rows >= pend[None, :]).astype(jnp.int32), axis=1), E - 1)
    pos = jnp.arange(R, dtype=jnp.int32)
    e_pos = jnp.repeat(blk_e, rows)
    local = pos - pstart[e_pos]
    is_real = (pos < pend[E - 1]) & (local < counts[e_pos])
    rank = jnp.clip(start[e_pos] + local, 0, N * K - 1)
    pair = order[rank]
    real_before = jnp.cumsum(is_real.astype(jnp.int32)) - is_real.astype(jnp.int32)
    row_tok = jnp.where(is_real, pair // K, 0).astype(jnp.int32)
    row_dst = jnp.where(is_real, (pair % K) * N + pair // K, N * K + pos - real_before).astype(jnp.int32)
    return _expert_call(h2, blk_e.astype(jnp.int32), row_tok, row_dst, w13, w2, layer)


def kernel(x, c, w_ada, b_ada, norm_gain, w_in, qk_gain, cmp_pe, cmp_w1, cmp_w2, swa_sinks,
           lat_gain_q, lat_gain_kv, rope_gain, w_uq, w_ukv, w_branch, w_out,
           w_coarse, b_coarse, w_fine, b_fine, w13, w2):
    B, S, D = x.shape
    L = w_in.shape[0]
    assert S % MOBA_BLOCK == 0 and D == 1024
    slopes = _alibi_slopes()

    half = MLA_ROPE // 2
    inv = ROPE_THETA ** (-jnp.arange(half, dtype=F32) / half)
    ang = jnp.arange(S).astype(F32)[:, None] * inv[None, :]
    cos, sin = jnp.cos(ang), jnp.sin(ang)

    mod = _ada_mod(c, w_ada, b_ada)

    w_attn = jnp.concatenate(
        [w_in[:, :, :640], w_in[:, :, 652:ATTN_OLD], w_in[:, :, 640:652],
         jnp.zeros((L, D, ATTN_COLS - ATTN_OLD), F32)], axis=2).astype(BF16)
    w_gate = w_in[:, :, ATTN_OLD:].astype(BF16)
    w_uq_b = w_uq.astype(BF16)
    w_ukv_b = w_ukv.astype(BF16)
    w_branch_b = w_branch.astype(BF16)
    w_out_b = w_out.astype(BF16)
    w_router = jnp.concatenate(
        [w_coarse, w_fine, jnp.zeros((L, D, 128 - N_GROUPS - N_EXPERTS), F32)], axis=2)
    b_router = jnp.concatenate(
        [b_coarse, b_fine, jnp.zeros((L, 128 - N_GROUPS - N_EXPERTS), F32)], axis=1)
    sinks_pad = jnp.concatenate([swa_sinks, jnp.zeros((L, 128 - SWA_HEADS), F32)], axis=1)

    for l in range(L):
        m6 = mod[l].reshape(B, 6, 1, D)
        sh1, sc1, g1, sh2, sc2, g2 = (m6[:, j] for j in range(6))
        (qa, kcr, vcr, ks, vs, kw, vw, gs, qb, kb, vb, qc, kc, vc, mc, qd, kd, vd) = _proj_call(
            x, sc1, sh1, norm_gain[l, 0:1], w_attn[l], qk_gain[l], lat_gain_q[l][None], lat_gain_kv[l][None],
            rope_gain[l], w_uq_b[l], w_ukv_b[l], cos, sin)
        kcmp, vcmp = _compress_call(kcr, vcr, cmp_pe[l], cmp_w1[l], cmp_w2[l], qk_gain[l, QK_NSA_KC][None])
        o_cmp, sel = _cmp_attn_call(qa, kcmp, vcmp, slopes[0])
        o_slc = _flash_call(qa, ks, vs, slopes=slopes[0], mask=sel, mask_block=SEL_BLOCK,
                            tq=256, tk=256, hpc=2, out_dtype=F32, name="nsa_slc")
        o_win = _flash_call(qa, kw, vw, slopes=slopes[0], window=NSA_WINDOW,
                            tq=256, tk=256, hpc=2, out_dtype=F32, name="nsa_win")
        o_b = _flash_call(qb, kb, vb, slopes=slopes[1], window=SWA_WINDOW, sinks=sinks_pad[l][None],
                          tq=256, tk=256, name="swa")
        o_c = _flash_call(qc, kc, vc, slopes=slopes[2], mask=mc, mask_block=MOBA_BLOCK,
                          tq=256, tk=256, name="moba")
        o_d = _flash_call(qd, kd, vd, tq=256, tk=512, name="mla")
        xn, h2, route = _merge_call(
            x, (sc1, sh1, g1, sc2, sh2), norm_gain[l], w_gate[l], w_branch_b[l], w_out_b[l],
            o_cmp, o_slc, o_win, gs, o_b, o_c, o_d, w_router[l], b_router[l][None])
        route = route.reshape(B * S, 128)
        y2 = _moe(h2.reshape(B * S, D), route, w13, w2, l)
        x = _combine_call(xn, g2, route, y2)
    return x
```

```python
import functools
import math

import numpy as np
import jax
import jax.numpy as jnp
from jax import lax
from jax.experimental import pallas as pl
from jax.experimental.pallas import tpu as pltpu

F32 = jnp.float32
BF16 = jnp.bfloat16

HEAD_DIM = 64
NEG_INF = -1e30
EPS = 1e-6
NSA_HEADS = 4
CMP_BLOCK = 32
CMP_STRIDE = 16
CMP_HIDDEN = 256
SEL_BLOCK = 64
SEL_TOPK = 8
NSA_WINDOW = 512
FORCE_BONUS = 1e4
SWA_HEADS = 4
SWA_KV_HEADS = 2
SWA_WINDOW = 128
MOBA_HEADS = 4
MOBA_BLOCK = 256
MOBA_TOPK = 3
MLA_HEADS = 4
MLA_Q_RANK = 384
MLA_KV_RANK = 128
MLA_NOPE = 64
MLA_ROPE = 32
MLA_V = 64
ROPE_THETA = 10000.0
N_BRANCH = 4
BRANCH_WIDTH = 256
N_GROUPS = 4
EXPERTS_PER_GROUP = 8
N_EXPERTS = N_GROUPS * EXPERTS_PER_GROUP
D_EXPERT = 256

QK_NSA_Q, QK_NSA_KC, QK_NSA_KS, QK_NSA_KW = 0, 1, 2, 3
QK_SWA_Q, QK_SWA_K, QK_MOBA_Q, QK_MOBA_K, QK_MLA_Q, QK_MLA_K = 4, 5, 6, 7, 8, 9

ATTN_OLD = 2476
ATTN_COLS = 2560
GATE_LANE0 = 32
LOG2E = math.log2(math.e)
SUB, LANE = 8, 128
EXPERT_ROWS = 256
VMEM_LIMIT = 56 * 1024 * 1024


def _alibi_slopes():
    n = NSA_HEADS + SWA_HEADS + MOBA_HEADS

    def pow2(m):
        start = 2.0 ** (-8.0 / m)
        return [start ** (i + 1) for i in range(m)]

    c = 2 ** int(math.floor(math.log2(n)))
    s = pow2(c) + (pow2(2 * c)[0::2][: n - c] if c < n else [])
    s = -np.sort(-np.asarray(s, np.float32))
    return s.reshape(NSA_HEADS, 3).T


def _dot(a, b):
    return jnp.dot(a, b, preferred_element_type=F32)


def _dot_nt(a, b):
    return lax.dot_general(a, b, (((1,), (1,)), ((), ())), preferred_element_type=F32)


def _split(a):
    hi = a.astype(BF16)
    lo = (a - hi.astype(F32)).astype(BF16)
    return hi, lo


def _dot3(a, b):
    ah, al = _split(a)
    bh, bl = _split(b)
    return _dot(ah, bh) + (_dot(ah, bl) + _dot(al, bh))


def _dot3_nt(a, b):
    ah, al = _split(a)
    bh, bl = _split(b)
    return _dot_nt(ah, bh) + (_dot_nt(ah, bl) + _dot_nt(al, bh))


def _rms(x, g):
    return x * lax.rsqrt(jnp.mean(x * x, axis=-1, keepdims=True) + EPS) * g


def _sigmoid(x):
    return 1.0 / (1.0 + jnp.exp(-x))


def _topk_mask(score, k):
    n = score.shape[-1]
    iota = lax.broadcasted_iota(jnp.int32, score.shape, 1).astype(F32)
    sel = jnp.zeros(score.shape, F32)
    for _ in range(k):
        m = jnp.max(score, axis=-1, keepdims=True)
        idx = jnp.min(jnp.where(score == m, iota, float(n)), axis=-1, keepdims=True)
        hit = iota == idx
        sel = jnp.where(hit, jnp.where(m > 0.5 * NEG_INF, 1.0, 0.0), sel)
        score = jnp.where(hit, -3e38, score)
    return sel


def _ada_kernel(c_ref, w_ref, b_ref, o_ref):
    c = c_ref[...]
    a = c * _sigmoid(c)
    o_ref[0] = _dot(a, w_ref[0]) + b_ref[0]


def _ada_mod(c, w_ada, b_ada):
    L, D, D6 = w_ada.shape
    B = c.shape[0]
    tn = 1024
    return pl.pallas_call(
        _ada_kernel,
        grid=(L, D6 // tn),
        in_specs=[
            pl.BlockSpec((B, D), lambda l, j: (0, 0)),
            pl.BlockSpec((1, D, tn), lambda l, j: (l, 0, j)),
            pl.BlockSpec((1, 1, tn), lambda l, j: (l, 0, j)),
        ],
        out_specs=pl.BlockSpec((1, B, tn), lambda l, j: (l, 0, j)),
        out_shape=jax.ShapeDtypeStruct((L, B, D6), F32),
        compiler_params=pltpu.CompilerParams(vmem_limit_bytes=VMEM_LIMIT),
        name="ada_mod",
    )(c, w_ada, b_ada.reshape(L, 1, D6))


def _proj_kernel(x_ref, sc_ref, sh_ref, ng_ref, w_ref, qkg_ref, lgq_ref, lgkv_ref, rg_ref,
                 wuq_ref, wukv_ref, cos_ref, sin_ref,
                 qa_ref, kcr_ref, vcr_ref, ks_ref, vs_ref, kw_ref, vw_ref, gs_ref,
                 qb_ref, kb_ref, vb_ref, qc_ref, kc_ref, vc_ref, mc_ref, qd_ref, kd_ref, vd_ref,
                 kmean_s):
    i = pl.program_id(1)
    hd = HEAD_DIM
    x = x_ref[0]
    h = _rms(x, ng_ref[...]) * (1.0 + sc_ref[0]) + sh_ref[0]
    hb = h.astype(BF16)
    gq = qkg_ref[...]

    def gain(slot):
        return gq[slot:slot + 1]

    def proj(a, b):
        return _dot(hb, w_ref[:, a:b])

    scale = hd ** -0.5 * LOG2E

    p = proj(0, 256)
    for hh in range(NSA_HEADS):
        qa_ref[0, hh] = (_rms(p[:, hh * hd:(hh + 1) * hd], gain(QK_NSA_Q)) * scale).astype(BF16)
    p = proj(256, 640)
    kcr_ref[0] = p[:, 0:64]
    vcr_ref[0] = p[:, 64:128]
    ks_ref[0, 0] = _rms(p[:, 128:192], gain(QK_NSA_KS)).astype(BF16)
    vs_ref[0, 0] = p[:, 128:256].T[hd:].astype(BF16)
    kw_ref[0, 0] = _rms(p[:, 256:320], gain(QK_NSA_KW)).astype(BF16)
    vw_ref[0, 0] = p[:, 256:384].T[hd:].astype(BF16)

    p = proj(640, 1152)
    for hh in range(SWA_HEADS):
        qb_ref[0, hh] = (_rms(p[:, hh * hd:(hh + 1) * hd], gain(QK_SWA_Q)) * scale).astype(BF16)
    vt = p[:, 384:512].T
    for gg in range(SWA_KV_HEADS):
        kb_ref[0, gg] = _rms(p[:, 256 + gg * hd:256 + (gg + 1) * hd], gain(QK_SWA_K)).astype(BF16)
        vb_ref[0, gg] = vt[gg * hd:(gg + 1) * hd].astype(BF16)

    @pl.when(i == 0)
    def _():
        kmean_s[...] = jnp.zeros(kmean_s.shape, F32)

    p = proj(1152, 1920)
    nblk = kmean_s.shape[0]
    blk_iota = lax.broadcasted_iota(jnp.int32, (1, nblk), 1)
    kmeans = []
    vt = p[:, 512:768].T
    for hh in range(MOBA_HEADS):
        qn = _rms(p[:, hh * hd:(hh + 1) * hd], gain(QK_MOBA_Q))
        kn = _rms(p[:, 256 + hh * hd:256 + (hh + 1) * hd], gain(QK_MOBA_K))
        qc_ref[0, hh] = (qn * scale).astype(BF16)
        kc_ref[0, hh] = kn.astype(BF16)
        vc_ref[0, hh] = vt[hh * hd:(hh + 1) * hd].astype(BF16)
        kmeans.append(jnp.mean(kn, axis=0, keepdims=True))
        g = _dot3_nt(qn, kmean_s[:, hh * hd:(hh + 1) * hd])
        g = jnp.where(blk_iota < i, g, NEG_INF)
        mc_ref[0, hh] = jnp.where(blk_iota == i, 1.0, _topk_mask(g, MOBA_TOPK))
    kmean_s[pl.ds(i, 1), :] = jnp.concatenate(kmeans, axis=1)

    p = proj(1920, 2560)
    gs_ref[0] = _sigmoid(p[:, 512:640])
    cos = cos_ref[...]
    sin = sin_ref[...]
    half = MLA_ROPE // 2

    def rope(v):
        v1 = v[:, :half]
        v2 = v[:, half:]
        return jnp.concatenate([v1 * cos - v2 * sin, v1 * sin + v2 * cos], axis=1)

    rg = rg_ref[...]
    qlat = _dot(_rms(p[:, 0:384], lgq_ref[...]).astype(BF16), wuq_ref[...])
    kvlat = _dot(_rms(p[:, 384:512], lgkv_ref[...]).astype(BF16), wukv_ref[...])
    k_rot = rope(_rms(p[:, 512:544], rg[1:2]))
    dq = MLA_NOPE + MLA_ROPE
    dkv = MLA_NOPE + MLA_V
    mla_scale = dq ** -0.5 * LOG2E
    kvt = kvlat.T
    for hh in range(MLA_HEADS):
        qh = qlat[:, hh * dq:(hh + 1) * dq]
        q_nope = _rms(qh[:, :MLA_NOPE], gain(QK_MLA_Q))
        q_rot = rope(_rms(qh[:, MLA_NOPE:], rg[0:1]))
        qd_ref[0, hh] = (jnp.concatenate([q_nope, q_rot], axis=1) * mla_scale).astype(BF16)
        kvh = kvlat[:, hh * dkv:(hh + 1) * dkv]
        k_nope = _rms(kvh[:, :MLA_NOPE], gain(QK_MLA_K))
        kd_ref[0, hh] = jnp.concatenate([k_nope, k_rot], axis=1).astype(BF16)
        vd_ref[0, hh] = kvt[hh * dkv + MLA_NOPE:(hh + 1) * dkv].astype(BF16)


def _proj_call(x, sc1, sh1, ng, w_attn, qkg, lgq, lgkv, rg, wuq, wukv, cos, sin):
    B, S, D = x.shape
    tm = MOBA_BLOCK
    nblk = S // tm
    hd = HEAD_DIM

    def full(shape):
        return pl.BlockSpec(shape, lambda b, i: (0,) * len(shape))

    def heads(nh, d):
        return pl.BlockSpec((1, nh, tm, d), lambda b, i: (b, 0, i, 0))

    in_specs = [
        pl.BlockSpec((1, tm, D), lambda b, i: (b, i, 0)),
        pl.BlockSpec((1, 1, D), lambda b, i: (b, 0, 0)),
        pl.BlockSpec((1, 1, D), lambda b, i: (b, 0, 0)),
        full((1, D)),
        full((D, ATTN_COLS)),
        full(qkg.shape),
        full(lgq.shape),
        full(lgkv.shape),
        full(rg.shape),
        full(wuq.shape),
        full(wukv.shape),
        pl.BlockSpec((tm, MLA_ROPE // 2), lambda b, i: (i, 0)),
        pl.BlockSpec((tm, MLA_ROPE // 2), lambda b, i: (i, 0)),
    ]
    row64 = pl.BlockSpec((1, tm, hd), lambda b, i: (b, i, 0))
    def heads_t(nh, d):
        return pl.BlockSpec((1, nh, d, tm), lambda b, i: (b, 0, 0, i))

    out_specs = [
        heads(4, hd), row64, row64, heads(1, hd), heads_t(1, hd), heads(1, hd), heads_t(1, hd),
        pl.BlockSpec((1, tm, 128), lambda b, i: (b, i, 0)),
        heads(4, hd), heads(2, hd), heads_t(2, hd),
        heads(4, hd), heads(4, hd), heads_t(4, hd), heads(4, nblk),
        heads(4, MLA_NOPE + MLA_ROPE), heads(4, MLA_NOPE + MLA_ROPE), heads_t(4, MLA_V),
    ]

    def sd(shape, dt):
        return jax.ShapeDtypeStruct(shape, dt)

    out_shape = [
        sd((B, 4, S, hd), BF16), sd((B, S, hd), F32), sd((B, S, hd), F32),
        sd((B, 1, S, hd), BF16), sd((B, 1, hd, S), BF16), sd((B, 1, S, hd), BF16), sd((B, 1, hd, S), BF16),
        sd((B, S, 128), F32),
        sd((B, 4, S, hd), BF16), sd((B, 2, S, hd), BF16), sd((B, 2, hd, S), BF16),
        sd((B, 4, S, hd), BF16), sd((B, 4, S, hd), BF16), sd((B, 4, hd, S), BF16), sd((B, 4, S, nblk), F32),
        sd((B, 4, S, 96), BF16), sd((B, 4, S, 96), BF16), sd((B, 4, MLA_V, S), BF16),
    ]
    return pl.pallas_call(
        _proj_kernel,
        grid=(B, nblk),
        in_specs=in_specs,
        out_specs=out_specs,
        out_shape=out_shape,
        scratch_shapes=[pltpu.VMEM((nblk, MOBA_HEADS * hd), F32)],
        compiler_params=pltpu.CompilerParams(
            dimension_semantics=("arbitrary", "arbitrary"), vmem_limit_bytes=VMEM_LIMIT),
        name="proj_prep",
    )(x, sc1, sh1, ng, w_attn, qkg, lgq, lgkv, rg, wuq, wukv, cos, sin)


def _compress_kernel(gk_ref, gv_ref, pe_ref, w1_ref, w2_ref, gkc_ref, kc_ref, vc_ref):
    half = w1_ref.shape[1] // 2
    outs = []
    for j, g_ref in enumerate((gk_ref, gv_ref)):
        g = g_ref[0].astype(BF16)
        top = _dot(g, w1_ref[j, :half].astype(BF16))
        bot = _dot(g, w1_ref[j, half:].astype(BF16))
        bot = jnp.concatenate([bot[1:], bot[:1]], axis=0)
        pe = jnp.broadcast_to(pe_ref[j], (8, pe_ref.shape[2]))
        bias = _dot3(pe, w1_ref[j])[0:1]
        hid = top + bot + bias
        hid = hid * _sigmoid(hid)
        outs.append(_dot(hid.astype(BF16), w2_ref[j].astype(BF16)))
    kc_ref[0] = _rms(outs[0], gkc_ref[...]).astype(BF16)
    vc_ref[0] = outs[1].astype(BF16)


def _compress_call(kc_raw, vc_raw, pe, w1, w2, g_kc):
    B, S, dk = kc_raw.shape
    n_grp = S // CMP_STRIDE
    gk = kc_raw.reshape(B, n_grp, CMP_STRIDE * dk)
    gv = vc_raw.reshape(B, n_grp, CMP_STRIDE * dk)
    pe_flat = pe.reshape(2, 1, CMP_BLOCK * dk)
    grp_spec = pl.BlockSpec((1, n_grp, CMP_STRIDE * dk), lambda b: (b, 0, 0))
    out_spec = pl.BlockSpec((1, n_grp, dk), lambda b: (b, 0, 0))
    return pl.pallas_call(
        _compress_kernel,
        grid=(B,),
        in_specs=[
            grp_spec, grp_spec,
            pl.BlockSpec(pe_flat.shape, lambda b: (0, 0, 0)),
            pl.BlockSpec(w1.shape, lambda b: (0, 0, 0)),
            pl.BlockSpec(w2.shape, lambda b: (0, 0, 0)),
            pl.BlockSpec((1, dk), lambda b: (0, 0)),
        ],
        out_specs=[out_spec, out_spec],
        out_shape=[jax.ShapeDtypeStruct((B, n_grp, dk), BF16)] * 2,
        compiler_params=pltpu.CompilerParams(vmem_limit_bytes=VMEM_LIMIT),
        name="nsa_compress",
    )(gk, gv, pe_flat, w1, w2, g_kc)


def _cmp_attn_kernel(q_ref, kc_ref, vc_ref, cover_ref, o_ref, sel_ref, *, slopes, tq, n_cmp):
    i = pl.program_id(1)
    kc = kc_ref[0]
    vc = vc_ref[0]
    ncp = kc.shape[0]
    t = (i * tq + lax.broadcasted_iota(jnp.int32, (tq, 1), 0))
    n_iota = lax.broadcasted_iota(jnp.int32, (1, ncp), 1)
    dist_i = t - (n_iota * CMP_STRIDE + (CMP_BLOCK - 1))
    vis = (n_iota < n_cmp) & (dist_i >= 0)
    dist = dist_i.astype(F32)
    visf = vis.astype(F32)
    psum = jnp.zeros((tq, ncp), F32)
    outs = []
    for hh in range(NSA_HEADS):
        s = _dot_nt(q_ref[0, hh], kc) - (slopes[hh] * LOG2E) * dist
        s = jnp.where(vis, s, NEG_INF)
        e = jnp.exp2(s - jnp.max(s, axis=-1, keepdims=True)) * visf
        p = e / jnp.maximum(jnp.sum(e, axis=-1, keepdims=True), 1e-30)
        outs.append(_dot(p.astype(BF16), vc))
        psum = psum + p
    o_ref[0] = jnp.concatenate(outs, axis=1)
    ph, plo = _split(psum)
    cover = cover_ref[...]
    p_slc = _dot(ph, cover) + _dot(plo, cover)
    n_sel = cover.shape[1]
    cur = t // SEL_BLOCK
    j = lax.broadcasted_iota(jnp.int32, (1, n_sel), 1)
    forced = jnp.where(j == 0, 1.0, jnp.where(j == cur, 1.0, jnp.where(j == cur - 1, 1.0, 0.0)))
    score = jnp.where(j <= cur, p_slc + FORCE_BONUS * forced, NEG_INF)
    sel_ref[0, 0] = _topk_mask(score, min(SEL_TOPK, n_sel))


def _cmp_attn_call(qa, kc, vc, slopes):
    B, H, S, dk = qa.shape
    ncp = kc.shape[1]
    n_cmp = (S - CMP_BLOCK) // CMP_STRIDE + 1
    n_sel = S // SEL_BLOCK
    tq = 128
    starts = np.arange(ncp) * CMP_STRIDE
    jb = np.arange(n_sel) * SEL_BLOCK
    cover = ((starts[:, None] < jb[None, :] + SEL_BLOCK) & (starts[:, None] + CMP_BLOCK > jb[None, :])
             & (np.arange(ncp)[:, None] < n_cmp))
    cover = jnp.asarray(cover.astype(np.float32), dtype=BF16)
    return pl.pallas_call(
        functools.partial(_cmp_attn_kernel, slopes=tuple(float(s) for s in slopes), tq=tq, n_cmp=n_cmp),
        grid=(B, S // tq),
        in_specs=[
            pl.BlockSpec((1, H, tq, dk), lambda b, i: (b, 0, i, 0)),
            pl.BlockSpec((1, ncp, dk), lambda b, i: (b, 0, 0)),
            pl.BlockSpec((1, ncp, dk), lambda b, i: (b, 0, 0)),
            pl.BlockSpec((ncp, n_sel), lambda b, i: (0, 0)),
        ],
        out_specs=[
            pl.BlockSpec((1, tq, H * dk), lambda b, i: (b, i, 0)),
            pl.BlockSpec((1, 1, tq, n_sel), lambda b, i: (b, 0, i, 0)),
        ],
        out_shape=[jax.ShapeDtypeStruct((B, S, H * dk), F32),
                   jax.ShapeDtypeStruct((B, 1, S, n_sel), F32)],
        compiler_params=pltpu.CompilerParams(vmem_limit_bytes=VMEM_LIMIT),
        name="nsa_cmp_attn",
    )(qa, kc, vc, cover)


def _flash_kernel(*refs, H, G, hpc, tq, tk, slopes, window, mask_block, mask_per_head, has_sink):
    refs = list(refs)
    q_ref, k_ref, v_ref = refs[:3]
    pos = 3
    mask_ref = sink_ref = None
    if mask_block:
        mask_ref = refs[pos]
        pos += 1
    if has_sink:
        sink_ref = refs[pos]
        pos += 1
    o_ref, m_s, l_s, acc_s = refs[pos:pos + 4]
    pos += 4
    bias_s = need_s = None
    if slopes is not None:
        bias_s = refs[pos]
        pos += 1
    if mask_block:
        need_s = refs[pos]
    R = H // G
    C = H // hpc
    M = hpc * tq
    dv = v_ref.shape[-2]
    i = pl.program_id(1)
    q0 = i * tq
    hi = q0 // tk
    lo = jnp.maximum(q0 - (window - 1), 0) // tk if window else 0

    def rel_pos():
        r_row = jnp.concatenate([lax.broadcasted_iota(jnp.int32, (1, tq), 1)] * hpc, axis=1)
        return (r_row - lax.broadcasted_iota(jnp.int32, (tk, M), 0)).astype(F32)

    qs, slope_rows, bms = [], [], []
    for c in range(C):
        heads = [c * hpc + r for r in range(hpc)]
        if hpc == 1:
            qs.append(q_ref[0, heads[0]])
        else:
            qs.append(jnp.concatenate([q_ref[0, hh] for hh in heads], axis=0))
        if slopes is not None:
            slope_rows.append(jnp.concatenate([jnp.full((1, tq), slopes[hh], F32) for hh in heads], axis=1))
            bias_s[c] = slope_rows[c] * rel_pos()
        if mask_block:
            if mask_per_head:
                bm = jnp.concatenate([mask_ref[0, hh] for hh in heads], axis=0)
            else:
                bm = jnp.concatenate([mask_ref[0, 0]] * hpc, axis=0)
            bms.append(bm)
    m_s[...] = jnp.full(m_s.shape, NEG_INF, F32)
    l_s[...] = jnp.zeros(l_s.shape, F32)
    acc_s[...] = jnp.zeros(acc_s.shape, F32)

    if mask_block:
        nblk = mask_ref.shape[-1]
        col = bms[0]
        for bm in bms[1:]:
            col = jnp.maximum(col, bm)
        col = jnp.max(col, axis=0, keepdims=True)
        bpt = tk // mask_block
        for jt in range(nblk // bpt):
            need_s[jt] = jnp.max(col[:, jt * bpt:(jt + 1) * bpt]).astype(jnp.int32)
        bms = [bm.T.astype(BF16) for bm in bms]

    def tile(j, edge):
        k0 = pl.multiple_of(j * tk, tk)
        off = (q0 - k0).astype(F32)
        valid = None
        if edge:
            dist = rel_pos() + off
            valid = dist >= 0.0
            if window:
                valid = valid & (dist < float(window))
        if mask_block:
            key_blk = lax.shift_right_logical(
                lax.broadcasted_iota(jnp.int32, (tk, nblk), 0) + k0, int(math.log2(mask_block)))
            expand = jnp.where(key_blk == lax.broadcasted_iota(jnp.int32, (tk, nblk), 1), 1.0, 0.0).astype(BF16)
        scores = [_dot_nt(k_ref[0, (c * hpc) // R, pl.ds(k0, tk), :], qs[c]) for c in range(C)]
        sels = [_dot(expand, bms[c]) for c in range(C)] if mask_block else None
        probs, alphas = [], []
        for g in range(C):
            s = scores[g]
            if slopes is not None:
                s = s - bias_s[g]
                shift = slope_rows[g] * off
            ok = valid
            if mask_block:
                sel = sels[g] > 0.5
                ok = sel if ok is None else ok & sel
            if ok is not None:
                s = jnp.where(ok, s, NEG_INF)
            m_prev = m_s[g]
            s_max = jnp.max(s, axis=0, keepdims=True)
            if slopes is not None:
                s_max = s_max - shift
            m_new = jnp.maximum(m_prev, s_max)
            alpha = jnp.exp2(m_prev - m_new)
            p = jnp.exp2(s - (m_new + shift if slopes is not None else m_new))
            l_s[g] = alpha * l_s[g] + jnp.sum(p, axis=0, keepdims=True)
            m_s[g] = m_new
            probs.append(p.astype(BF16))
            alphas.append(alpha)
        for c in range(C):
            v = v_ref[0, (c * hpc) // R, :, pl.ds(k0, tk)]
            acc_s[c] = alphas[c] * acc_s[c] + _dot(v, probs[c])

    def body(step, carry):
        j = hi - step
        is_edge = step == 0
        if window:
            is_edge = is_edge | (q0 - j * tk + (tq - 1) >= window)
        run = (step == 0) | (need_s[j] > 0) if mask_block else None

        def when(c):
            return pl.when(c if run is None else c & run)

        @when(is_edge)
        def _():
            tile(j, True)

        @when(jnp.logical_not(is_edge))
        def _():
            tile(j, False)

        return carry

    lax.fori_loop(0, hi - lo + 1, body, 0)

    for c in range(C):
        m = m_s[c]
        l = l_s[c]
        acc = acc_s[c]
        if has_sink:
            sk = LOG2E * jnp.concatenate(
                [jnp.broadcast_to(sink_ref[:, c * hpc + r:c * hpc + r + 1], (1, tq)) for r in range(hpc)], axis=1)
            m_f = jnp.maximum(m, sk)
            a = jnp.exp2(m - m_f)
            l = l * a + jnp.exp2(sk - m_f)
            acc = acc * a
        out = acc / l
        for r in range(hpc):
            hh = c * hpc + r
            o_ref[0, :, hh * dv:(hh + 1) * dv] = out[:, r * tq:(r + 1) * tq].T.astype(o_ref.dtype)


def _flash_call(q, k, v, *, slopes=None, window=0, mask=None, mask_block=0, sinks=None,
                tq=128, tk=256, hpc=None, out_dtype=BF16, name="flash"):
    B, H, S, dq = q.shape
    G = k.shape[1]
    dv = v.shape[-2]
    tk = min(tk, S)
    assert tk % tq == 0 and S % tk == 0
    R = H // G
    hpc = R if hpc is None else hpc
    assert R % hpc == 0
    C = H // hpc
    in_specs = [
        pl.BlockSpec((1, H, tq, dq), lambda b, i: (b, 0, i, 0)),
        pl.BlockSpec((1, G, S, dq), lambda b, i: (b, 0, 0, 0)),
        pl.BlockSpec((1, G, dv, S), lambda b, i: (b, 0, 0, 0)),
    ]
    args = [q, k, v]
    mask_per_head = False
    if mask is not None:
        hm, nblk = mask.shape[1], mask.shape[3]
        mask_per_head = hm > 1
        in_specs.append(pl.BlockSpec((1, hm, tq, nblk), lambda b, i: (b, 0, i, 0)))
        args.append(mask)
    if sinks is not None:
        in_specs.append(pl.BlockSpec(sinks.shape, lambda b, i: (0, 0)))
        args.append(sinks)
    kern = functools.partial(
        _flash_kernel, H=H, G=G, hpc=hpc, tq=tq, tk=tk,
        slopes=None if slopes is None else tuple(float(s) * LOG2E for s in slopes),
        window=window, mask_block=mask_block if mask is not None else 0,
        mask_per_head=mask_per_head, has_sink=sinks is not None)
    scratch = [pltpu.VMEM((C, 1, hpc * tq), F32), pltpu.VMEM((C, 1, hpc * tq), F32),
               pltpu.VMEM((C, dv, hpc * tq), F32)]
    if slopes is not None:
        scratch.append(pltpu.VMEM((C, tk, hpc * tq), F32))
    if mask is not None:
        assert tk % mask_block == 0
        scratch.append(pltpu.SMEM((S // tk,), jnp.int32))
    return pl.pallas_call(
        kern,
        grid=(B, S // tq),
        in_specs=in_specs,
        out_specs=pl.BlockSpec((1, tq, H * dv), lambda b, i: (b, i, 0)),
        out_shape=jax.ShapeDtypeStruct((B, S, H * dv), out_dtype),
        scratch_shapes=scratch,
        compiler_params=pltpu.CompilerParams(vmem_limit_bytes=VMEM_LIMIT),
        name=name,
    )(*args)


def _merge_kernel(x_ref, sc1_ref, sh1_ref, g1_ref, sc2_ref, sh2_ref, ng_ref, wg_ref, wb_ref, wo_ref,
                  ocmp_ref, oslc_ref, owin_ref, gs_ref, ob_ref, oc_ref, od_ref, wr_ref, br_ref,
                  xo_ref, h2_ref, rt_ref):
    hd = HEAD_DIM
    x = x_ref[0]
    ng = ng_ref[...]
    h = _rms(x, ng[0:1]) * (1.0 + sc1_ref[0]) + sh1_ref[0]
    hb = h.astype(BF16)
    gs = gs_ref[0]
    ocmp = ocmp_ref[0]
    oslc = oslc_ref[0]
    owin = owin_ref[0]
    parts = []
    for hh in range(NSA_HEADS):
        c0 = GATE_LANE0 + 3 * hh
        sl = slice(hh * hd, (hh + 1) * hd)
        parts.append(gs[:, c0:c0 + 1] * ocmp[:, sl] + gs[:, c0 + 1:c0 + 2] * oslc[:, sl]
                     + gs[:, c0 + 2:c0 + 3] * owin[:, sl])
    o_a = jnp.concatenate(parts, axis=1).astype(BF16)
    branches = (o_a, ob_ref[0], oc_ref[0], od_ref[0])
    D = x.shape[1]
    mixed = None
    for n in range(N_BRANCH):
        gate = _sigmoid(_dot(hb, wg_ref[:, n * D:(n + 1) * D]))
        term = gate * _dot(branches[n], wb_ref[n])
        mixed = term if mixed is None else mixed + term
    xn = x + g1_ref[0] * _dot(mixed.astype(BF16), wo_ref[...])
    xo_ref[0] = xn
    h2 = _rms(xn, ng[1:2]) * (1.0 + sc2_ref[0]) + sh2_ref[0]
    for c in range(SUB):
        h2_ref[pl.ds(c, h2.shape[0], stride=SUB), :] = h2[:, c * LANE:(c + 1) * LANE]

    logits = _dot3(h2, wr_ref[...]) + br_ref[...]
    lane = lax.broadcasted_iota(jnp.int32, logits.shape, 1)
    lanef = lane.astype(F32)
    is_c = lane < N_GROUPS
    lc = jnp.where(is_c, logits, NEG_INF)
    mc = jnp.max(lc, axis=-1, keepdims=True)
    grp = jnp.min(jnp.where(lc == mc, lanef, 1e9), axis=-1, keepdims=True)
    p_grp = 1.0 / jnp.sum(jnp.where(is_c, jnp.exp(lc - mc), 0.0), axis=-1, keepdims=True)
    e_lane = lanef - float(N_GROUPS)
    in_grp = (lane >= N_GROUPS) & (lane < N_GROUPS + N_EXPERTS) & (
        jnp.floor(e_lane / EXPERTS_PER_GROUP) == grp)
    lf = jnp.where(in_grp, logits, NEG_INF)
    m1 = jnp.max(lf, axis=-1, keepdims=True)
    i1 = jnp.min(jnp.where(lf == m1, e_lane, 1e9), axis=-1, keepdims=True)
    lf2 = jnp.where(e_lane == i1, NEG_INF, lf)
    m2 = jnp.max(lf2, axis=-1, keepdims=True)
    i2 = jnp.min(jnp.where(lf2 == m2, e_lane, 1e9), axis=-1, keepdims=True)
    e2 = jnp.exp(m2 - m1)
    w1 = p_grp / (1.0 + e2)
    w2 = p_grp * e2 / (1.0 + e2)
    rt = jnp.where(lane == 0, i1, jnp.where(lane == 1, i2, jnp.where(lane == 2, w1, jnp.where(lane == 3, w2, 0.0))))
    rt_ref[0] = rt


def _merge_call(x, mods, ng, wg, wb, wo, ocmp, oslc, owin, gs, ob, oc, od, wr, br):
    B, S, D = x.shape
    tm = 256
    sc1, sh1, g1, sc2, sh2 = mods

    def full(a):
        return pl.BlockSpec(a.shape, lambda b, i: (0,) * a.ndim)

    modspec = pl.BlockSpec((1, 1, D), lambda b, i: (b, 0, 0))
    row = lambda w: pl.BlockSpec((1, tm, w), lambda b, i: (b, i, 0))
    in_specs = [row(D), modspec, modspec, modspec, modspec, modspec, full(ng), full(wg), full(wb), full(wo),
                row(256), row(256), row(256), row(128), row(256), row(256), row(256), full(wr), full(br)]
    return pl.pallas_call(
        _merge_kernel,
        grid=(B, S // tm),
        in_specs=in_specs,
        out_specs=[row(D), pl.BlockSpec((tm * SUB, LANE), lambda b, i: (b * (S // tm) + i, 0)), row(128)],
        out_shape=[jax.ShapeDtypeStruct((B, S, D), F32), jax.ShapeDtypeStruct((B * S * SUB, LANE), F32),
                   jax.ShapeDtypeStruct((B, S, 128), F32)],
        compiler_params=pltpu.CompilerParams(vmem_limit_bytes=VMEM_LIMIT),
        name="merge_router",
    )(x, sc1, sh1, g1, sc2, sh2, ng, wg, wb, wo, ocmp, oslc, owin, gs, ob, oc, od, wr, br)


def _expert_kernel(blk_e_ref, tok_ref, dst_ref, h2_hbm, w13_ref, w2_ref, y2_hbm, xbuf, ybuf, gsem, ssem):
    del blk_e_ref
    i = pl.program_id(0)
    nb = pl.num_programs(0)
    slot = i % 2
    rows = EXPERT_ROWS

    def tile_rows(t):
        return pl.ds(pl.multiple_of(t * SUB, SUB), SUB)

    def gather_copy(blk, s, r):
        tok = tok_ref[blk * rows + r]
        return pltpu.make_async_copy(h2_hbm.at[tile_rows(tok), :], xbuf.at[s, tile_rows(r), :], gsem.at[s])

    def scatter_copy(blk, s, r):
        d = dst_ref[blk * rows + r]
        return pltpu.make_async_copy(ybuf.at[s, tile_rows(r), :], y2_hbm.at[tile_rows(d), :], ssem.at[s])

    def for_rows(fn):
        def body(r, carry):
            fn(r)
            return carry
        lax.fori_loop(0, rows, body, 0, unroll=8)

    @pl.when(i == 0)
    def _():
        for_rows(lambda r: gather_copy(0, 0, r).start())

    @pl.when(i + 1 < nb)
    def _():
        for_rows(lambda r: gather_copy(i + 1, 1 - slot, r).start())

    def wait_gather(s):
        pltpu.make_async_copy(h2_hbm.at[pl.ds(0, rows * SUB), :], xbuf.at[s], gsem.at[s]).wait()

    def wait_scatter(s):
        pltpu.make_async_copy(ybuf.at[s], y2_hbm.at[pl.ds(0, rows * SUB), :], ssem.at[s]).wait()

    wait_gather(slot)

    @pl.when(i >= 2)
    def _():
        wait_scatter(slot)

    ab = None
    for c in range(0, SUB, 2):
        xc = jnp.concatenate([xbuf[slot, pl.ds(c, rows, stride=SUB), :],
                              xbuf[slot, pl.ds(c + 1, rows, stride=SUB), :]], axis=1).astype(BF16)
        part = _dot(xc, w13_ref[0, 0, c * LANE:(c + 2) * LANE, :].astype(BF16))
        ab = part if ab is None else ab + part
    de = ab.shape[1] // 2
    a = ab[:, :de]
    b = ab[:, de:]
    act = ((a * _sigmoid(a)) * b).astype(BF16)
    for c in range(0, SUB, 2):
        y = _dot(act, w2_ref[0, 0, :, c * LANE:(c + 2) * LANE].astype(BF16))
        ybuf[slot, pl.ds(c, rows, stride=SUB), :] = y[:, :LANE]
        ybuf[slot, pl.ds(c + 1, rows, stride=SUB), :] = y[:, LANE:]
    for_rows(lambda r: scatter_copy(i, slot, r).start())

    @pl.when(i == nb - 1)
    def _():
        @pl.when(nb >= 2)
        def _():
            wait_scatter(1 - slot)
        wait_scatter(slot)


def _expert_call(h2, blk_e, row_tok, row_dst, w13, w2, layer):
    D = SUB * LANE
    assert h2.shape[1] == LANE and w13.shape[2] == D
    R = row_tok.shape[0]
    n_blocks = R // EXPERT_ROWS
    de2 = w13.shape[-1]
    grid_spec = pltpu.PrefetchScalarGridSpec(
        num_scalar_prefetch=3,
        grid=(n_blocks,),
        in_specs=[
            pl.BlockSpec(memory_space=pl.ANY),
            pl.BlockSpec((1, 1, D, de2), lambda i, be, rt, rd: (layer, be[i], 0, 0)),
            pl.BlockSpec((1, 1, de2 // 2, D), lambda i, be, rt, rd: (layer, be[i], 0, 0)),
        ],
        out_specs=pl.BlockSpec(memory_space=pl.ANY),
        scratch_shapes=[pltpu.VMEM((2, EXPERT_ROWS * SUB, LANE), F32), pltpu.VMEM((2, EXPERT_ROWS * SUB, LANE), F32),
                        pltpu.SemaphoreType.DMA((2,)), pltpu.SemaphoreType.DMA((2,))],
    )
    return pl.pallas_call(
        _expert_kernel,
        grid_spec=grid_spec,
        out_shape=jax.ShapeDtypeStruct((R * SUB, LANE), F32),
        compiler_params=pltpu.CompilerParams(
            dimension_semantics=("arbitrary",), vmem_limit_bytes=VMEM_LIMIT),
        name="experts",
    )(blk_e, row_tok, row_dst, h2, w13, w2)


def _combine_kernel(xn_ref, g2_ref, rt_ref, y0_ref, y1_ref, o_ref):
    rt = rt_ref[...]
    tm = xn_ref.shape[0]
    w0 = rt[:, 2:3]
    w1 = rt[:, 3:4]
    g2 = g2_ref[0]
    for c in range(SUB):
        cols = slice(c * LANE, (c + 1) * LANE)
        y = w0 * y0_ref[pl.ds(c, tm, stride=SUB), :] + w1 * y1_ref[pl.ds(c, tm, stride=SUB), :]
        o_ref[:, cols] = xn_ref[:, cols] + g2[:, cols] * y


def _combine_call(xn, g2, route, y2):
    B, S, D = xn.shape
    tm = 512
    spb = S // tm
    nt = B * spb
    return pl.pallas_call(
        _combine_kernel,
        grid=(B, spb),
        in_specs=[
            pl.BlockSpec((tm, D), lambda b, i: (b * spb + i, 0)),
            pl.BlockSpec((1, 1, D), lambda b, i: (b, 0, 0)),
            pl.BlockSpec((tm, 128), lambda b, i: (b * spb + i, 0)),
            pl.BlockSpec((tm * SUB, LANE), lambda b, i: (b * spb + i, 0)),
            pl.BlockSpec((tm * SUB, LANE), lambda b, i: (nt + b * spb + i, 0)),
        ],
        out_specs=pl.BlockSpec((tm, D), lambda b, i: (b * spb + i, 0)),
        out_shape=jax.ShapeDtypeStruct((B * S, D), F32),
        compiler_params=pltpu.CompilerParams(vmem_limit_bytes=VMEM_LIMIT),
        name="moe_combine",
    )(xn.reshape(B * S, D), g2, route, y2, y2).reshape(B, S, D)


def _moe(h2, route, w13, w2, layer):
    N = h2.shape[0] // SUB
    K = 2
    E = N_EXPERTS
    rows = EXPERT_ROWS
    flat_e = route[:, 0:2].astype(jnp.int32).reshape(-1)
    order = jnp.argsort(flat_e).astype(jnp.int32)
    counts = jnp.sum((flat_e[:, None] == jnp.arange(E)[None, :]).astype(jnp.int32), axis=0)
    padded = (counts + rows - 1) // rows * rows
    pend = jnp.cumsum(padded)
    pstart = pend - padded
    start = jnp.cumsum(counts) - counts
    n_blocks = (N * K) // rows + E
    R = n_blocks * rows
    blk_e = jnp.minimum(
        jnp.sum((jnp.arange(n_blocks)[:, None] * rows >= pend[None, :]).astype(jnp.int32), axis=1), E - 1)
    pos = jnp.arange(R, dtype=jnp.int32)
    e_pos = jnp.repeat(blk_e, rows)
    local = pos - pstart[e_pos]
    is_real = (pos < pend[E - 1]) & (local < counts[e_pos])
    rank = jnp.clip(start[e_pos] + local, 0, N * K - 1)
    pair = order[rank]
    real_before = jnp.cumsum(is_real.astype(jnp.int32)) - is_real.astype(jnp.int32)
    row_tok = jnp.where(is_real, pair // K, 0).astype(jnp.int32)
    row_dst = jnp.where(is_real, (pair % K) * N + pair // K, N * K + pos - real_before).astype(jnp.int32)
    return _expert_call(h2, blk_e.astype(jnp.int32), row_tok, row_dst, w13, w2, layer)


def kernel(x, c, w_ada, b_ada, norm_gain, w_in, qk_gain, cmp_pe, cmp_w1, cmp_w2, swa_sinks,
           lat_gain_q, lat_gain_kv, rope_gain, w_uq, w_ukv, w_branch, w_out,
           w_coarse, b_coarse, w_fine, b_fine, w13, w2):
    B, S, D = x.shape
    L = w_in.shape[0]
    assert S % MOBA_BLOCK == 0 and D == 1024
    slopes = _alibi_slopes()

    half = MLA_ROPE // 2
    inv = ROPE_THETA ** (-jnp.arange(half, dtype=F32) / half)
    ang = jnp.arange(S).astype(F32)[:, None] * inv[None, :]
    cos, sin = jnp.cos(ang), jnp.sin(ang)

    mod = _ada_mod(c, w_ada, b_ada)

    w_attn = jnp.concatenate(
        [w_in[:, :, :640], w_in[:, :, 652:ATTN_OLD], w_in[:, :, 640:652],
         jnp.zeros((L, D, ATTN_COLS - ATTN_OLD), F32)], axis=2).astype(BF16)
    w_gate = w_in[:, :, ATTN_OLD:].astype(BF16)
    w_uq_b = w_uq.astype(BF16)
    w_ukv_b = w_ukv.astype(BF16)
    w_branch_b = w_branch.astype(BF16)
    w_out_b = w_out.astype(BF16)
    w_router = jnp.concatenate(
        [w_coarse, w_fine, jnp.zeros((L, D, 128 - N_GROUPS - N_EXPERTS), F32)], axis=2)
    b_router = jnp.concatenate(
        [b_coarse, b_fine, jnp.zeros((L, 128 - N_GROUPS - N_EXPERTS), F32)], axis=1)
    sinks_pad = jnp.concatenate([swa_sinks, jnp.zeros((L, 128 - SWA_HEADS), F32)], axis=1)

    for l in range(L):
        m6 = mod[l].reshape(B, 6, 1, D)
        sh1, sc1, g1, sh2, sc2, g2 = (m6[:, j] for j in range(6))
        (qa, kcr, vcr, ks, vs, kw, vw, gs, qb, kb, vb, qc, kc, vc, mc, qd, kd, vd) = _proj_call(
            x, sc1, sh1, norm_gain[l, 0:1], w_attn[l], qk_gain[l], lat_gain_q[l][None], lat_gain_kv[l][None],
            rope_gain[l], w_uq_b[l], w_ukv_b[l], cos, sin)
        kcmp, vcmp = _compress_call(kcr, vcr, cmp_pe[l], cmp_w1[l], cmp_w2[l], qk_gain[l, QK_NSA_KC][None])
        o_cmp, sel = _cmp_attn_call(qa, kcmp, vcmp, slopes[0])
        o_slc = _flash_call(qa, ks, vs, slopes=slopes[0], mask=sel, mask_block=SEL_BLOCK,
                            tq=256, tk=256, hpc=2, out_dtype=F32, name="nsa_slc")
        o_win = _flash_call(qa, kw, vw, slopes=slopes[0], window=NSA_WINDOW,
                            tq=256, tk=256, hpc=2, out_dtype=F32, name="nsa_win")
        o_b = _flash_call(qb, kb, vb, slopes=slopes[1], window=SWA_WINDOW, sinks=sinks_pad[l][None],
                          tq=256, tk=256, name="swa")
        o_c = _flash_call(qc, kc, vc, slopes=slopes[2], mask=mc, mask_block=MOBA_BLOCK,
                          tq=256, tk=512, name="moba")
        o_d = _flash_call(qd, kd, vd, tq=256, tk=512, name="mla")
        xn, h2, route = _merge_call(
            x, (sc1, sh1, g1, sc2, sh2), norm_gain[l], w_gate[l], w_branch_b[l], w_out_b[l],
            o_cmp, o_slc, o_win, gs, o_b, o_c, o_d, w_router[l], b_router[l][None])
        route = route.reshape(B * S, 128)
        y2 = _moe(h2, route, w13, w2, l)
        x = _combine_call(xn, g2, route, y2)
    return x
```

```python
import functools
import math

import numpy as np
import jax
import jax.numpy as jnp
from jax import lax
from jax.experimental import pallas as pl
from jax.experimental.pallas import tpu as pltpu

F32 = jnp.float32
BF16 = jnp.bfloat16

HEAD_DIM = 64
NEG_INF = -1e30
EPS = 1e-6
NSA_HEADS = 4
CMP_BLOCK = 32
CMP_STRIDE = 16
CMP_HIDDEN = 256
SEL_BLOCK = 64
SEL_TOPK = 8
NSA_WINDOW = 512
FORCE_BONUS = 1e4
SWA_HEADS = 4
SWA_KV_HEADS = 2
SWA_WINDOW = 128
MOBA_HEADS = 4
MOBA_BLOCK = 256
MOBA_TOPK = 3
MLA_HEADS = 4
MLA_Q_RANK = 384
MLA_KV_RANK = 128
MLA_NOPE = 64
MLA_ROPE = 32
MLA_V = 64
ROPE_THETA = 10000.0
N_BRANCH = 4
BRANCH_WIDTH = 256
N_GROUPS = 4
EXPERTS_PER_GROUP = 8
N_EXPERTS = N_GROUPS * EXPERTS_PER_GROUP
D_EXPERT = 256

QK_NSA_Q, QK_NSA_KC, QK_NSA_KS, QK_NSA_KW = 0, 1, 2, 3
QK_SWA_Q, QK_SWA_K, QK_MOBA_Q, QK_MOBA_K, QK_MLA_Q, QK_MLA_K = 4, 5, 6, 7, 8, 9

ATTN_OLD = 2476
ATTN_COLS = 2560
GATE_LANE0 = 32
LOG2E = math.log2(math.e)
SUB, LANE = 8, 128
EXPERT_ROWS = 256
VMEM_LIMIT = 56 * 1024 * 1024


def _alibi_slopes():
    n = NSA_HEADS + SWA_HEADS + MOBA_HEADS

    def pow2(m):
        start = 2.0 ** (-8.0 / m)
        return [start ** (i + 1) for i in range(m)]

    c = 2 ** int(math.floor(math.log2(n)))
    s = pow2(c) + (pow2(2 * c)[0::2][: n - c] if c < n else [])
    s = -np.sort(-np.asarray(s, np.float32))
    return s.reshape(NSA_HEADS, 3).T


def _dot(a, b):
    return jnp.dot(a, b, preferred_element_type=F32)


def _dot_nt(a, b):
    return lax.dot_general(a, b, (((1,), (1,)), ((), ())), preferred_element_type=F32)


def _split(a):
    hi = a.astype(BF16)
    lo = (a - hi.astype(F32)).astype(BF16)
    return hi, lo


def _dot3(a, b):
    ah, al = _split(a)
    bh, bl = _split(b)
    return _dot(ah, bh) + (_dot(ah, bl) + _dot(al, bh))


def _dot3_nt(a, b):
    ah, al = _split(a)
    bh, bl = _split(b)
    return _dot_nt(ah, bh) + (_dot_nt(ah, bl) + _dot_nt(al, bh))


def _rms(x, g):
    return x * lax.rsqrt(jnp.mean(x * x, axis=-1, keepdims=True) + EPS) * g


def _sigmoid(x):
    return 1.0 / (1.0 + jnp.exp(-x))


def _topk_mask(score, k, axis=1):
    n = score.shape[axis]
    iota = lax.broadcasted_iota(jnp.int32, score.shape, axis).astype(F32)
    sel = jnp.zeros(score.shape, F32)
    for _ in range(k):
        m = jnp.max(score, axis=axis, keepdims=True)
        idx = jnp.min(jnp.where(score == m, iota, float(n)), axis=axis, keepdims=True)
        hit = iota == idx
        sel = jnp.where(hit, jnp.where(m > 0.5 * NEG_INF, 1.0, 0.0), sel)
        score = jnp.where(hit, -3e38, score)
    return sel


def _ada_kernel(c_ref, w_ref, b_ref, o_ref):
    c = c_ref[...]
    a = c * _sigmoid(c)
    o_ref[0] = _dot(a, w_ref[0]) + b_ref[0]


def _ada_mod(c, w_ada, b_ada):
    L, D, D6 = w_ada.shape
    B = c.shape[0]
    tn = 1024
    return pl.pallas_call(
        _ada_kernel,
        grid=(L, D6 // tn),
        in_specs=[
            pl.BlockSpec((B, D), lambda l, j: (0, 0)),
            pl.BlockSpec((1, D, tn), lambda l, j: (l, 0, j)),
            pl.BlockSpec((1, 1, tn), lambda l, j: (l, 0, j)),
        ],
        out_specs=pl.BlockSpec((1, B, tn), lambda l, j: (l, 0, j)),
        out_shape=jax.ShapeDtypeStruct((L, B, D6), F32),
        compiler_params=pltpu.CompilerParams(vmem_limit_bytes=VMEM_LIMIT),
        name="ada_mod",
    )(c, w_ada, b_ada.reshape(L, 1, D6))


def _proj_kernel(x_ref, sc_ref, sh_ref, ng_ref, w_ref, qkg_ref, lgq_ref, lgkv_ref, rg_ref,
                 wuq_ref, wukv_ref, cos_ref, sin_ref,
                 qa_ref, kcr_ref, vcr_ref, ks_ref, vs_ref, kw_ref, vw_ref, gs_ref,
                 qb_ref, kb_ref, vb_ref, qc_ref, kc_ref, vc_ref, mc_ref, qd_ref, kd_ref, vd_ref,
                 kmean_s):
    i = pl.program_id(1)
    hd = HEAD_DIM
    x = x_ref[0]
    h = _rms(x, ng_ref[...]) * (1.0 + sc_ref[0]) + sh_ref[0]
    hb = h.astype(BF16)
    gq = qkg_ref[...]

    def gain(slot):
        return gq[slot:slot + 1]

    def proj(a, b):
        return _dot(hb, w_ref[:, a:b])

    scale = hd ** -0.5 * LOG2E

    p = proj(0, 256)
    for hh in range(NSA_HEADS):
        qa_ref[0, hh] = (_rms(p[:, hh * hd:(hh + 1) * hd], gain(QK_NSA_Q)) * scale).astype(BF16)
    p = proj(256, 640)
    kcr_ref[0] = p[:, 0:64]
    vcr_ref[0] = p[:, 64:128]
    ks_ref[0, 0] = _rms(p[:, 128:192], gain(QK_NSA_KS)).astype(BF16)
    vs_ref[0, 0] = p[:, 128:256].T[hd:].astype(BF16)
    kw_ref[0, 0] = _rms(p[:, 256:320], gain(QK_NSA_KW)).astype(BF16)
    vw_ref[0, 0] = p[:, 256:384].T[hd:].astype(BF16)

    p = proj(640, 1152)
    for hh in range(SWA_HEADS):
        qb_ref[0, hh] = (_rms(p[:, hh * hd:(hh + 1) * hd], gain(QK_SWA_Q)) * scale).astype(BF16)
    vt = p[:, 384:512].T
    for gg in range(SWA_KV_HEADS):
        kb_ref[0, gg] = _rms(p[:, 256 + gg * hd:256 + (gg + 1) * hd], gain(QK_SWA_K)).astype(BF16)
        vb_ref[0, gg] = vt[gg * hd:(gg + 1) * hd].astype(BF16)

    @pl.when(i == 0)
    def _():
        kmean_s[...] = jnp.zeros(kmean_s.shape, F32)

    p = proj(1152, 1920)
    nblk = kmean_s.shape[0]
    blk_iota = lax.broadcasted_iota(jnp.int32, (1, nblk), 1)
    kmeans = []
    vt = p[:, 512:768].T
    for hh in range(MOBA_HEADS):
        qn = _rms(p[:, hh * hd:(hh + 1) * hd], gain(QK_MOBA_Q))
        kn = _rms(p[:, 256 + hh * hd:256 + (hh + 1) * hd], gain(QK_MOBA_K))
        qc_ref[0, hh] = (qn * scale).astype(BF16)
        kc_ref[0, hh] = kn.astype(BF16)
        vc_ref[0, hh] = vt[hh * hd:(hh + 1) * hd].astype(BF16)
        kmeans.append(jnp.mean(kn, axis=0, keepdims=True))
        g = _dot3_nt(qn, kmean_s[:, hh * hd:(hh + 1) * hd])
        g = jnp.where(blk_iota < i, g, NEG_INF)
        mc_ref[0, hh] = jnp.where(blk_iota == i, 1.0, _topk_mask(g, MOBA_TOPK))
    kmean_s[pl.ds(i, 1), :] = jnp.concatenate(kmeans, axis=1)

    p = proj(1920, 2560)
    gs_ref[0] = _sigmoid(p[:, 512:640])
    cos = cos_ref[...]
    sin = sin_ref[...]
    half = MLA_ROPE // 2

    def rope(v):
        v1 = v[:, :half]
        v2 = v[:, half:]
        return jnp.concatenate([v1 * cos - v2 * sin, v1 * sin + v2 * cos], axis=1)

    rg = rg_ref[...]
    qlat = _dot(_rms(p[:, 0:384], lgq_ref[...]).astype(BF16), wuq_ref[...])
    kvlat = _dot(_rms(p[:, 384:512], lgkv_ref[...]).astype(BF16), wukv_ref[...])
    k_rot = rope(_rms(p[:, 512:544], rg[1:2]))
    dq = MLA_NOPE + MLA_ROPE
    dkv = MLA_NOPE + MLA_V
    mla_scale = dq ** -0.5 * LOG2E
    kvt = kvlat.T
    for hh in range(MLA_HEADS):
        qh = qlat[:, hh * dq:(hh + 1) * dq]
        q_nope = _rms(qh[:, :MLA_NOPE], gain(QK_MLA_Q))
        q_rot = rope(_rms(qh[:, MLA_NOPE:], rg[0:1]))
        qd_ref[0, hh] = (jnp.concatenate([q_nope, q_rot], axis=1) * mla_scale).astype(BF16)
        kvh = kvlat[:, hh * dkv:(hh + 1) * dkv]
        k_nope = _rms(kvh[:, :MLA_NOPE], gain(QK_MLA_K))
        kd_ref[0, hh] = jnp.concatenate([k_nope, k_rot], axis=1).astype(BF16)
        vd_ref[0, hh] = kvt[hh * dkv + MLA_NOPE:(hh + 1) * dkv].astype(BF16)


def _proj_call(x, sc1, sh1, ng, w_attn, qkg, lgq, lgkv, rg, wuq, wukv, cos, sin):
    B, S, D = x.shape
    tm = MOBA_BLOCK
    nblk = S // tm
    hd = HEAD_DIM

    def full(shape):
        return pl.BlockSpec(shape, lambda b, i: (0,) * len(shape))

    def heads(nh, d):
        return pl.BlockSpec((1, nh, tm, d), lambda b, i: (b, 0, i, 0))

    in_specs = [
        pl.BlockSpec((1, tm, D), lambda b, i: (b, i, 0)),
        pl.BlockSpec((1, 1, D), lambda b, i: (b, 0, 0)),
        pl.BlockSpec((1, 1, D), lambda b, i: (b, 0, 0)),
        full((1, D)),
        full((D, ATTN_COLS)),
        full(qkg.shape),
        full(lgq.shape),
        full(lgkv.shape),
        full(rg.shape),
        full(wuq.shape),
        full(wukv.shape),
        pl.BlockSpec((tm, MLA_ROPE // 2), lambda b, i: (i, 0)),
        pl.BlockSpec((tm, MLA_ROPE // 2), lambda b, i: (i, 0)),
    ]
    row64 = pl.BlockSpec((1, tm, hd), lambda b, i: (b, i, 0))
    def heads_t(nh, d):
        return pl.BlockSpec((1, nh, d, tm), lambda b, i: (b, 0, 0, i))

    out_specs = [
        heads(4, hd), row64, row64, heads(1, hd), heads_t(1, hd), heads(1, hd), heads_t(1, hd),
        pl.BlockSpec((1, tm, 128), lambda b, i: (b, i, 0)),
        heads(4, hd), heads(2, hd), heads_t(2, hd),
        heads(4, hd), heads(4, hd), heads_t(4, hd), heads(4, nblk),
        heads(4, MLA_NOPE + MLA_ROPE), heads(4, MLA_NOPE + MLA_ROPE), heads_t(4, MLA_V),
    ]

    def sd(shape, dt):
        return jax.ShapeDtypeStruct(shape, dt)

    out_shape = [
        sd((B, 4, S, hd), BF16), sd((B, S, hd), F32), sd((B, S, hd), F32),
        sd((B, 1, S, hd), BF16), sd((B, 1, hd, S), BF16), sd((B, 1, S, hd), BF16), sd((B, 1, hd, S), BF16),
        sd((B, S, 128), F32),
        sd((B, 4, S, hd), BF16), sd((B, 2, S, hd), BF16), sd((B, 2, hd, S), BF16),
        sd((B, 4, S, hd), BF16), sd((B, 4, S, hd), BF16), sd((B, 4, hd, S), BF16), sd((B, 4, S, nblk), F32),
        sd((B, 4, S, 96), BF16), sd((B, 4, S, 96), BF16), sd((B, 4, MLA_V, S), BF16),
    ]
    return pl.pallas_call(
        _proj_kernel,
        grid=(B, nblk),
        in_specs=in_specs,
        out_specs=out_specs,
        out_shape=out_shape,
        scratch_shapes=[pltpu.VMEM((nblk, MOBA_HEADS * hd), F32)],
        compiler_params=pltpu.CompilerParams(
            dimension_semantics=("arbitrary", "arbitrary"), vmem_limit_bytes=VMEM_LIMIT),
        name="proj_prep",
    )(x, sc1, sh1, ng, w_attn, qkg, lgq, lgkv, rg, wuq, wukv, cos, sin)


def _compress_kernel(gk_ref, gv_ref, pe_ref, w1_ref, w2_ref, gkc_ref, kc_ref, vc_ref):
    half = w1_ref.shape[1] // 2
    outs = []
    for j, g_ref in enumerate((gk_ref, gv_ref)):
        g = g_ref[0].astype(BF16)
        top = _dot(g, w1_ref[j, :half].astype(BF16))
        bot = _dot(g, w1_ref[j, half:].astype(BF16))
        bot = jnp.concatenate([bot[1:], bot[:1]], axis=0)
        pe = jnp.broadcast_to(pe_ref[j], (8, pe_ref.shape[2]))
        bias = _dot3(pe, w1_ref[j])[0:1]
        hid = top + bot + bias
        hid = hid * _sigmoid(hid)
        outs.append(_dot(hid.astype(BF16), w2_ref[j].astype(BF16)))
    kc_ref[0] = _rms(outs[0], gkc_ref[...]).astype(BF16)
    dk = outs[1].shape[1]
    vc_ref[0] = jnp.concatenate([outs[1], outs[1]], axis=1).T[:dk].astype(BF16)


def _compress_call(kc_raw, vc_raw, pe, w1, w2, g_kc):
    B, S, dk = kc_raw.shape
    n_grp = S // CMP_STRIDE
    gk = kc_raw.reshape(B, n_grp, CMP_STRIDE * dk)
    gv = vc_raw.reshape(B, n_grp, CMP_STRIDE * dk)
    pe_flat = pe.reshape(2, 1, CMP_BLOCK * dk)
    grp_spec = pl.BlockSpec((1, n_grp, CMP_STRIDE * dk), lambda b: (b, 0, 0))
    out_spec = pl.BlockSpec((1, n_grp, dk), lambda b: (b, 0, 0))
    return pl.pallas_call(
        _compress_kernel,
        grid=(B,),
        in_specs=[
            grp_spec, grp_spec,
            pl.BlockSpec(pe_flat.shape, lambda b: (0, 0, 0)),
            pl.BlockSpec(w1.shape, lambda b: (0, 0, 0)),
            pl.BlockSpec(w2.shape, lambda b: (0, 0, 0)),
            pl.BlockSpec((1, dk), lambda b: (0, 0)),
        ],
        out_specs=[out_spec, pl.BlockSpec((1, dk, n_grp), lambda b: (b, 0, 0))],
        out_shape=[jax.ShapeDtypeStruct((B, n_grp, dk), BF16), jax.ShapeDtypeStruct((B, dk, n_grp), BF16)],
        compiler_params=pltpu.CompilerParams(vmem_limit_bytes=VMEM_LIMIT),
        name="nsa_compress",
    )(gk, gv, pe_flat, w1, w2, g_kc)


def _cmp_attn_kernel(q_ref, kc_ref, vct_ref, cover_ref, o_ref, sel_ref, *, slopes, tq, n_cmp):
    i = pl.program_id(1)
    kc = kc_ref[0]
    vct = vct_ref[0]
    ncp = kc.shape[0]
    t_full = i * tq + lax.broadcasted_iota(jnp.int32, (ncp, tq), 1)
    n_iota = lax.broadcasted_iota(jnp.int32, (ncp, tq), 0)
    dist_i = t_full - (n_iota * CMP_STRIDE + (CMP_BLOCK - 1))
    vis = (n_iota < n_cmp) & (dist_i >= 0)
    dist = dist_i.astype(F32)
    visf = vis.astype(F32)
    psum = jnp.zeros((ncp, tq), F32)
    outs = []
    for hh in range(NSA_HEADS):
        s = _dot_nt(kc, q_ref[0, hh]) - (slopes[hh] * LOG2E) * dist
        s = jnp.where(vis, s, NEG_INF)
        e = jnp.exp2(s - jnp.max(s, axis=0, keepdims=True)) * visf
        p = e / jnp.maximum(jnp.sum(e, axis=0, keepdims=True), 1e-30)
        outs.append(_dot(vct, p.astype(BF16)))
        psum = psum + p
    o_ref[0] = jnp.concatenate(outs, axis=0).T
    ph, plo = _split(psum)
    cover = cover_ref[...]
    p_slc = _dot(cover, ph) + _dot(cover, plo)
    n_sel = cover.shape[0]
    cur = (i * tq + lax.broadcasted_iota(jnp.int32, (1, tq), 1)) // SEL_BLOCK
    j = lax.broadcasted_iota(jnp.int32, (n_sel, 1), 0)
    forced = jnp.where(j == 0, 1.0, jnp.where(j == cur, 1.0, jnp.where(j == cur - 1, 1.0, 0.0)))
    score = jnp.where(j <= cur, p_slc + FORCE_BONUS * forced, NEG_INF)
    sel_ref[0, 0] = _topk_mask(score, min(SEL_TOPK, n_sel), axis=0)


def _cmp_attn_call(qa, kc, vc, slopes):
    B, H, S, dk = qa.shape
    ncp = kc.shape[1]
    n_cmp = (S - CMP_BLOCK) // CMP_STRIDE + 1
    n_sel = S // SEL_BLOCK
    tq = 256
    starts = np.arange(ncp) * CMP_STRIDE
    jb = np.arange(n_sel) * SEL_BLOCK
    cover = ((starts[:, None] < jb[None, :] + SEL_BLOCK) & (starts[:, None] + CMP_BLOCK > jb[None, :])
             & (np.arange(ncp)[:, None] < n_cmp))
    cover = jnp.asarray(cover.T.astype(np.float32), dtype=BF16)
    return pl.pallas_call(
        functools.partial(_cmp_attn_kernel, slopes=tuple(float(s) for s in slopes), tq=tq, n_cmp=n_cmp),
        grid=(B, S // tq),
        in_specs=[
            pl.BlockSpec((1, H, tq, dk), lambda b, i: (b, 0, i, 0)),
            pl.BlockSpec((1, ncp, dk), lambda b, i: (b, 0, 0)),
            pl.BlockSpec((1, dk, ncp), lambda b, i: (b, 0, 0)),
            pl.BlockSpec((n_sel, ncp), lambda b, i: (0, 0)),
        ],
        out_specs=[
            pl.BlockSpec((1, tq, H * dk), lambda b, i: (b, i, 0)),
            pl.BlockSpec((1, 1, n_sel, tq), lambda b, i: (b, 0, 0, i)),
        ],
        out_shape=[jax.ShapeDtypeStruct((B, S, H * dk), F32),
                   jax.ShapeDtypeStruct((B, 1, n_sel, S), F32)],
        compiler_params=pltpu.CompilerParams(vmem_limit_bytes=VMEM_LIMIT),
        name="nsa_cmp_attn",
    )(qa, kc, vc, cover)


def _flash_kernel(*refs, H, G, hpc, tq, tk, slopes, window, mask_block, mask_per_head, mask_t, has_sink):
    refs = list(refs)
    q_ref, k_ref, v_ref = refs[:3]
    pos = 3
    mask_ref = sink_ref = None
    if mask_block:
        mask_ref = refs[pos]
        pos += 1
    if has_sink:
        sink_ref = refs[pos]
        pos += 1
    o_ref, m_s, l_s, acc_s = refs[pos:pos + 4]
    pos += 4
    bias_s = need_s = None
    if slopes is not None:
        bias_s = refs[pos]
        pos += 1
    if mask_block:
        need_s = refs[pos]
    R = H // G
    C = H // hpc
    M = hpc * tq
    dv = v_ref.shape[-2]
    i = pl.program_id(1)
    q0 = i * tq
    hi = q0 // tk
    lo = jnp.maximum(q0 - (window - 1), 0) // tk if window else 0

    def rel_pos():
        r_row = jnp.concatenate([lax.broadcasted_iota(jnp.int32, (1, tq), 1)] * hpc, axis=1)
        return (r_row - lax.broadcasted_iota(jnp.int32, (tk, M), 0)).astype(F32)

    qs, slope_rows, bms = [], [], []
    for c in range(C):
        heads = [c * hpc + r for r in range(hpc)]
        if hpc == 1:
            qs.append(q_ref[0, heads[0]])
        else:
            qs.append(jnp.concatenate([q_ref[0, hh] for hh in heads], axis=0))
        if slopes is not None:
            slope_rows.append(jnp.concatenate([jnp.full((1, tq), slopes[hh], F32) for hh in heads], axis=1))
            bias_s[c] = slope_rows[c] * rel_pos()
        if mask_block:
            parts = [mask_ref[0, hh if mask_per_head else 0] for hh in heads]
            bms.append(jnp.concatenate(parts, axis=1) if mask_t else jnp.concatenate(parts, axis=0).T)
    m_s[...] = jnp.full(m_s.shape, NEG_INF, F32)
    l_s[...] = jnp.zeros(l_s.shape, F32)
    acc_s[...] = jnp.zeros(acc_s.shape, F32)

    if mask_block:
        nblk = bms[0].shape[0]
        col = bms[0]
        for bm in bms[1:]:
            col = jnp.maximum(col, bm)
        col = jnp.max(col, axis=1, keepdims=True)
        bpt = tk // mask_block
        for jt in range(nblk // bpt):
            need_s[jt] = jnp.max(col[jt * bpt:(jt + 1) * bpt]).astype(jnp.int32)
        bms = [bm.astype(BF16) for bm in bms]

    def tile(j, edge):
        k0 = pl.multiple_of(j * tk, tk)
        off = (q0 - k0).astype(F32)
        valid = None
        if edge:
            dist = rel_pos() + off
            valid = dist >= 0.0
            if window:
                valid = valid & (dist < float(window))
        if mask_block:
            key_blk = lax.shift_right_logical(
                lax.broadcasted_iota(jnp.int32, (tk, nblk), 0) + k0, int(math.log2(mask_block)))
            expand = jnp.where(key_blk == lax.broadcasted_iota(jnp.int32, (tk, nblk), 1), 1.0, 0.0).astype(BF16)
        scores = [_dot_nt(k_ref[0, (c * hpc) // R, pl.ds(k0, tk), :], qs[c]) for c in range(C)]
        sels = [_dot(expand, bms[c]) for c in range(C)] if mask_block else None
        probs, alphas = [], []
        for g in range(C):
            s = scores[g]
            if slopes is not None:
                s = s - bias_s[g]
                shift = slope_rows[g] * off
            ok = valid
            if mask_block:
                sel = sels[g] > 0.5
                ok = sel if ok is None else ok & sel
            if ok is not None:
                s = jnp.where(ok, s, NEG_INF)
            m_prev = m_s[g]
            s_max = jnp.max(s, axis=0, keepdims=True)
            if slopes is not None:
                s_max = s_max - shift
            m_new = jnp.maximum(m_prev, s_max)
            alpha = jnp.exp2(m_prev - m_new)
            p = jnp.exp2(s - (m_new + shift if slopes is not None else m_new))
            l_s[g] = alpha * l_s[g] + jnp.sum(p, axis=0, keepdims=True)
            m_s[g] = m_new
            probs.append(p.astype(BF16))
            alphas.append(alpha)
        for c in range(C):
            v = v_ref[0, (c * hpc) // R, :, pl.ds(k0, tk)]
            acc_s[c] = alphas[c] * acc_s[c] + _dot(v, probs[c])

    def body(step, carry):
        j = hi - step
        is_edge = step == 0
        if window:
            is_edge = is_edge | (q0 - j * tk + (tq - 1) >= window)
        run = (step == 0) | (need_s[j] > 0) if mask_block else None

        def when(c):
            return pl.when(c if run is None else c & run)

        @when(is_edge)
        def _():
            tile(j, True)

        @when(jnp.logical_not(is_edge))
        def _():
            tile(j, False)

        return carry

    lax.fori_loop(0, hi - lo + 1, body, 0)

    for c in range(C):
        m = m_s[c]
        l = l_s[c]
        acc = acc_s[c]
        if has_sink:
            sk = LOG2E * jnp.concatenate(
                [jnp.broadcast_to(sink_ref[:, c * hpc + r:c * hpc + r + 1], (1, tq)) for r in range(hpc)], axis=1)
            m_f = jnp.maximum(m, sk)
            a = jnp.exp2(m - m_f)
            l = l * a + jnp.exp2(sk - m_f)
            acc = acc * a
        out = acc / l
        for r in range(hpc):
            hh = c * hpc + r
            o_ref[0, :, hh * dv:(hh + 1) * dv] = out[:, r * tq:(r + 1) * tq].T.astype(o_ref.dtype)


def _flash_call(q, k, v, *, slopes=None, window=0, mask=None, mask_block=0, mask_t=False, sinks=None,
                tq=128, tk=256, hpc=None, out_dtype=BF16, name="flash"):
    B, H, S, dq = q.shape
    G = k.shape[1]
    dv = v.shape[-2]
    tk = min(tk, S)
    assert tk % tq == 0 and S % tk == 0
    R = H // G
    hpc = R if hpc is None else hpc
    assert R % hpc == 0
    C = H // hpc
    in_specs = [
        pl.BlockSpec((1, H, tq, dq), lambda b, i: (b, 0, i, 0)),
        pl.BlockSpec((1, G, S, dq), lambda b, i: (b, 0, 0, 0)),
        pl.BlockSpec((1, G, dv, S), lambda b, i: (b, 0, 0, 0)),
    ]
    args = [q, k, v]
    mask_per_head = False
    if mask is not None:
        hm = mask.shape[1]
        mask_per_head = hm > 1
        if mask_t:
            in_specs.append(pl.BlockSpec((1, hm, mask.shape[2], tq), lambda b, i: (b, 0, 0, i)))
        else:
            in_specs.append(pl.BlockSpec((1, hm, tq, mask.shape[3]), lambda b, i: (b, 0, i, 0)))
        args.append(mask)
    if sinks is not None:
        in_specs.append(pl.BlockSpec(sinks.shape, lambda b, i: (0, 0)))
        args.append(sinks)
    kern = functools.partial(
        _flash_kernel, H=H, G=G, hpc=hpc, tq=tq, tk=tk,
        slopes=None if slopes is None else tuple(float(s) * LOG2E for s in slopes),
        window=window, mask_block=mask_block if mask is not None else 0,
        mask_per_head=mask_per_head, mask_t=mask_t, has_sink=sinks is not None)
    scratch =[pltpu.VMEM((C, 1, hpc * tq), F32), pltpu.VMEM((C, 1, hpc * tq), F32),
               pltpu.VMEM((C, dv, hpc * tq), F32)]
    if slopes is not None:
        scratch.append(pltpu.VMEM((C, tk, hpc * tq), F32))
    if mask is not None:
        assert tk % mask_block == 0
        scratch.append(pltpu.SMEM((S // tk,), jnp.int32))
    return pl.pallas_call(
        kern,
        grid=(B, S // tq),
        in_specs=in_specs,
        out_specs=pl.BlockSpec((1, tq, H * dv), lambda b, i: (b, i, 0)),
        out_shape=jax.ShapeDtypeStruct((B, S, H * dv), out_dtype),
        scratch_shapes=scratch,
        compiler_params=pltpu.CompilerParams(vmem_limit_bytes=VMEM_LIMIT),
        name=name,
    )(*args)


def _merge_kernel(x_ref, sc1_ref, sh1_ref, g1_ref, sc2_ref, sh2_ref, ng_ref, wg_ref, wb_ref, wo_ref,
                  ocmp_ref, oslc_ref, owin_ref, gs_ref, ob_ref, oc_ref, od_ref, wr_ref, br_ref,
                  xo_ref, h2_ref, rt_ref):
    hd = HEAD_DIM
    x = x_ref[0]
    ng = ng_ref[...]
    h = _rms(x, ng[0:1]) * (1.0 + sc1_ref[0]) + sh1_ref[0]
    hb = h.astype(BF16)
    gs = gs_ref[0]
    ocmp = ocmp_ref[0]
    oslc = oslc_ref[0]
    owin = owin_ref[0]
    parts = []
    for hh in range(NSA_HEADS):
        c0 = GATE_LANE0 + 3 * hh
        sl = slice(hh * hd, (hh + 1) * hd)
        parts.append(gs[:, c0:c0 + 1] * ocmp[:, sl] + gs[:, c0 + 1:c0 + 2] * oslc[:, sl]
                     + gs[:, c0 + 2:c0 + 3] * owin[:, sl])
    o_a = jnp.concatenate(parts, axis=1).astype(BF16)
    branches = (o_a, ob_ref[0], oc_ref[0], od_ref[0])
    D = x.shape[1]
    mixed = None
    for n in range(N_BRANCH):
        gate = _sigmoid(_dot(hb, wg_ref[:, n * D:(n + 1) * D]))
        term = gate * _dot(branches[n], wb_ref[n])
        mixed = term if mixed is None else mixed + term
    xn = x + g1_ref[0] * _dot(mixed.astype(BF16), wo_ref[...])
    xo_ref[0] = xn
    h2 = _rms(xn, ng[1:2]) * (1.0 + sc2_ref[0]) + sh2_ref[0]
    for c in range(SUB):
        h2_ref[pl.ds(c, h2.shape[0], stride=SUB), :] = h2[:, c * LANE:(c + 1) * LANE]

    logits = _dot3(h2, wr_ref[...]) + br_ref[...]
    lane = lax.broadcasted_iota(jnp.int32, logits.shape, 1)
    lanef = lane.astype(F32)
    is_c = lane < N_GROUPS
    lc = jnp.where(is_c, logits, NEG_INF)
    mc = jnp.max(lc, axis=-1, keepdims=True)
    grp = jnp.min(jnp.where(lc == mc, lanef, 1e9), axis=-1, keepdims=True)
    p_grp = 1.0 / jnp.sum(jnp.where(is_c, jnp.exp(lc - mc), 0.0), axis=-1, keepdims=True)
    e_lane = lanef - float(N_GROUPS)
    in_grp = (lane >= N_GROUPS) & (lane < N_GROUPS + N_EXPERTS) & (
        jnp.floor(e_lane / EXPERTS_PER_GROUP) == grp)
    lf = jnp.where(in_grp, logits, NEG_INF)
    m1 = jnp.max(lf, axis=-1, keepdims=True)
    i1 = jnp.min(jnp.where(lf == m1, e_lane, 1e9), axis=-1, keepdims=True)
    lf2 = jnp.where(e_lane == i1, NEG_INF, lf)
    m2 = jnp.max(lf2, axis=-1, keepdims=True)
    i2 = jnp.min(jnp.where(lf2 == m2, e_lane, 1e9), axis=-1, keepdims=True)
    e2 = jnp.exp(m2 - m1)
    w1 = p_grp / (1.0 + e2)
    w2 = p_grp * e2 / (1.0 + e2)
    rt = jnp.where(lane == 0, i1, jnp.where(lane == 1, i2, jnp.where(lane == 2, w1, jnp.where(lane == 3, w2, 0.0))))
    rt_ref[0] = rt


def _merge_call(x, mods, ng, wg, wb, wo, ocmp, oslc, owin, gs, ob, oc, od, wr, br):
    B, S, D = x.shape
    tm = 256
    sc1, sh1, g1, sc2, sh2 = mods

    def full(a):
        return pl.BlockSpec(a.shape, lambda b, i: (0,) * a.ndim)

    modspec = pl.BlockSpec((1, 1, D), lambda b, i: (b, 0, 0))
    row = lambda w: pl.BlockSpec((1, tm, w), lambda b, i: (b, i, 0))
    in_specs = [row(D), modspec, modspec, modspec, modspec, modspec, full(ng), full(wg), full(wb), full(wo),
                row(256), row(256), row(256), row(128), row(256), row(256), row(256), full(wr), full(br)]
    return pl.pallas_call(
        _merge_kernel,
        grid=(B, S // tm),
        in_specs=in_specs,
        out_specs=[row(D), pl.BlockSpec((tm * SUB, LANE), lambda b, i: (b * (S // tm) + i, 0)), row(128)],
        out_shape=[jax.ShapeDtypeStruct((B, S, D), F32), jax.ShapeDtypeStruct((B * S * SUB, LANE), F32),
                   jax.ShapeDtypeStruct((B, S, 128), F32)],
        compiler_params=pltpu.CompilerParams(vmem_limit_bytes=VMEM_LIMIT),
        name="merge_router",
    )(x, sc1, sh1, g1, sc2, sh2, ng, wg, wb, wo, ocmp, oslc, owin, gs, ob, oc, od, wr, br)


def _expert_kernel(blk_e_ref, nreal_ref, tok_ref, dst_ref, h2_hbm, w13_ref, w2_ref, y2_hbm, xbuf, ybuf, gsem, ssem):
    del blk_e_ref
    i = pl.program_id(0)
    nb = pl.num_programs(0)
    slot = i % 2
    rows = EXPERT_ROWS
    grp = 8

    def tile_rows(t):
        return pl.ds(pl.multiple_of(t * SUB, SUB), SUB)

    def gather_copy(blk, s, r):
        tok = tok_ref[blk * rows + r]
        return pltpu.make_async_copy(h2_hbm.at[tile_rows(tok), :], xbuf.at[s, tile_rows(r), :], gsem.at[s])

    def scatter_copy(blk, s, r):
        d = dst_ref[blk * rows + r]
        return pltpu.make_async_copy(ybuf.at[s, tile_rows(r), :], y2_hbm.at[tile_rows(d), :], ssem.at[s])

    def for_real_rows(blk, per_row, per_group):
        n = nreal_ref[blk]
        nfull = n // grp

        def body(g, carry):
            per_group(g)
            return carry
        lax.fori_loop(0, nfull, body, 0)
        for u in range(grp - 1):
            r = nfull * grp + u

            @pl.when(r < n)
            def _():
                per_row(r)

    def start_gather(blk, s):
        def group(g):
            for u in range(grp):
                gather_copy(blk, s, g * grp + u).start()
        for_real_rows(blk, lambda r: gather_copy(blk, s, r).start(), group)

    def start_scatter(blk, s):
        def group(g):
            for u in range(grp):
                scatter_copy(blk, s, g * grp + u).start()
        for_real_rows(blk, lambda r: scatter_copy(blk, s, r).start(), group)

    def wait_gather(blk, s):
        def tiles(k):
            return pltpu.make_async_copy(h2_hbm.at[pl.ds(0, k * SUB), :], xbuf.at[s, pl.ds(0, k * SUB), :], gsem.at[s])
        for_real_rows(blk, lambda r: tiles(1).wait(), lambda g: tiles(grp).wait())

    def wait_scatter(blk, s):
        def tiles(k):
            return pltpu.make_async_copy(ybuf.at[s, pl.ds(0, k * SUB), :], y2_hbm.at[pl.ds(0, k * SUB), :], ssem.at[s])
        for_real_rows(blk, lambda r: tiles(1).wait(), lambda g: tiles(grp).wait())

    @pl.when(i == 0)
    def _():
        xbuf[...] = jnp.zeros(xbuf.shape, F32)
        start_gather(0, 0)

    @pl.when(i + 1 < nb)
    def _():
        start_gather(i + 1, 1 - slot)

    wait_gather(i, slot)

    @pl.when(i >= 2)
    def _():
        wait_scatter(i - 2, slot)

    ab = None
    for c in range(0, SUB, 2):
        xc = jnp.concatenate([xbuf[slot, pl.ds(c, rows, stride=SUB), :],
                              xbuf[slot, pl.ds(c + 1, rows, stride=SUB), :]], axis=1).astype(BF16)
        part = _dot(xc, w13_ref[0, 0, c * LANE:(c + 2) * LANE, :].astype(BF16))
        ab = part if ab is None else ab + part
    de = ab.shape[1] // 2
    a = ab[:, :de]
    b = ab[:, de:]
    act = ((a * _sigmoid(a)) * b).astype(BF16)
    for c in range(0, SUB, 2):
        y = _dot(act, w2_ref[0, 0, :, c * LANE:(c + 2) * LANE].astype(BF16))
        ybuf[slot, pl.ds(c, rows, stride=SUB), :] = y[:, :LANE]
        ybuf[slot, pl.ds(c + 1, rows, stride=SUB), :] = y[:, LANE:]
    start_scatter(i, slot)

    @pl.when(i == nb - 1)
    def _():
        @pl.when(nb >= 2)
        def _():
            wait_scatter(i - 1, 1 - slot)
        wait_scatter(i, slot)


def _expert_call(h2, blk_e, n_real, row_tok, row_dst, w13, w2, layer):
    D = SUB * LANE
    assert h2.shape[1] == LANE and w13.shape[2] == D
    n_blocks = blk_e.shape[0]
    de2 = w13.shape[-1]
    grid_spec = pltpu.PrefetchScalarGridSpec(
        num_scalar_prefetch=4,
        grid=(n_blocks,),
        in_specs=[
            pl.BlockSpec(memory_space=pl.ANY),
            pl.BlockSpec((1, 1, D, de2), lambda i, be, nr, rt, rd: (layer, be[i], 0, 0)),
            pl.BlockSpec((1, 1, de2 // 2, D), lambda i, be, nr, rt, rd: (layer, be[i], 0, 0)),
        ],
        out_specs=pl.BlockSpec(memory_space=pl.ANY),
        scratch_shapes=[pltpu.VMEM((2, EXPERT_ROWS * SUB, LANE), F32), pltpu.VMEM((2, EXPERT_ROWS * SUB, LANE), F32),
                        pltpu.SemaphoreType.DMA((2,)), pltpu.SemaphoreType.DMA((2,))],
    )
    return pl.pallas_call(
        _expert_kernel,
        grid_spec=grid_spec,
        out_shape=jax.ShapeDtypeStruct((2 * h2.shape[0], LANE), F32),
        compiler_params=pltpu.CompilerParams(
            dimension_semantics=("arbitrary",), vmem_limit_bytes=VMEM_LIMIT),
        name="experts",
    )(blk_e, n_real, row_tok, row_dst, h2, w13, w2)


def _combine_kernel(xn_ref, g2_ref, rt_ref, y0_ref, y1_ref, o_ref):
    rt = rt_ref[...]
    tm = xn_ref.shape[0]
    w0 = rt[:, 2:3]
    w1 = rt[:, 3:4]
    g2 = g2_ref[0]
    for c in range(SUB):
        cols = slice(c * LANE, (c + 1) * LANE)
        y = w0 * y0_ref[pl.ds(c, tm, stride=SUB), :] + w1 * y1_ref[pl.ds(c, tm, stride=SUB), :]
        o_ref[:, cols] = xn_ref[:, cols] + g2[:, cols] * y


def _combine_call(xn, g2, route, y2):
    B, S, D = xn.shape
    tm = 512
    spb = S // tm
    nt = B * spb
    return pl.pallas_call(
        _combine_kernel,
        grid=(B, spb),
        in_specs=[
            pl.BlockSpec((tm, D), lambda b, i: (b * spb + i, 0)),
            pl.BlockSpec((1, 1, D), lambda b, i: (b, 0, 0)),
            pl.BlockSpec((tm, 128), lambda b, i: (b * spb + i, 0)),
            pl.BlockSpec((tm * SUB, LANE), lambda b, i: (b * spb + i, 0)),
            pl.BlockSpec((tm * SUB, LANE), lambda b, i: (nt + b * spb + i, 0)),
        ],
        out_specs=pl.BlockSpec((tm, D), lambda b, i: (b * spb + i, 0)),
        out_shape=jax.ShapeDtypeStruct((B * S, D), F32),
        compiler_params=pltpu.CompilerParams(vmem_limit_bytes=VMEM_LIMIT),
        name="moe_combine",
    )(xn.reshape(B * S, D), g2, route, y2, y2).reshape(B, S, D)


def _moe(h2, route, w13, w2, layer):
    N = h2.shape[0] // SUB
    K = 2
    E = N_EXPERTS
    rows = EXPERT_ROWS
    flat_e = route[:, 0:2].astype(jnp.int32).reshape(-1)
    order = jnp.argsort(flat_e).astype(jnp.int32)
    counts = jnp.sum((flat_e[:, None] == jnp.arange(E)[None, :]).astype(jnp.int32), axis=0)
    padded = (counts + rows - 1) // rows * rows
    pend = jnp.cumsum(padded)
    pstart = pend - padded
    start = jnp.cumsum(counts) - counts
    n_blocks = (N * K) // rows + E
    R = n_blocks * rows
    blk_e = jnp.minimum(
        jnp.sum((jnp.arange(n_blocks)[:, None] * rows >= pend[None, :]).astype(jnp.int32), axis=1), E - 1)
    pos = jnp.arange(R, dtype=jnp.int32)
    e_pos = jnp.repeat(blk_e, rows)
    local = pos - pstart[e_pos]
    is_real = (pos < pend[E - 1]) & (local < counts[e_pos])
    rank = jnp.clip(start[e_pos] + local, 0, N * K - 1)
    pair = order[rank]
    row_tok = jnp.where(is_real, pair // K, 0).astype(jnp.int32)
    row_dst = jnp.where(is_real, (pair % K) * N + pair // K, 0).astype(jnp.int32)
    n_real = jnp.sum(is_real.reshape(n_blocks, rows).astype(jnp.int32), axis=1)
    return _expert_call(h2, blk_e.astype(jnp.int32), n_real, row_tok, row_dst, w13, w2, layer)


def kernel(x, c, w_ada, b_ada, norm_gain, w_in, qk_gain, cmp_pe, cmp_w1, cmp_w2, swa_sinks,
           lat_gain_q, lat_gain_kv, rope_gain, w_uq, w_ukv, w_branch, w_out,
           w_coarse, b_coarse, w_fine, b_fine, w13, w2):
    B, S, D = x.shape
    L = w_in.shape[0]
    assert S % MOBA_BLOCK == 0 and D == 1024
    slopes = _alibi_slopes()

    half = MLA_ROPE // 2
    inv = ROPE_THETA ** (-jnp.arange(half, dtype=F32) / half)
    ang = jnp.arange(S).astype(F32)[:, None] * inv[None, :]
    cos, sin = jnp.cos(ang), jnp.sin(ang)

    mod = _ada_mod(c, w_ada, b_ada)

    w_attn = jnp.concatenate(
        [w_in[:, :, :640], w_in[:, :, 652:ATTN_OLD], w_in[:, :, 640:652],
         jnp.zeros((L, D, ATTN_COLS - ATTN_OLD), F32)], axis=2).astype(BF16)
    w_gate = w_in[:, :, ATTN_OLD:].astype(BF16)
    w_uq_b = w_uq.astype(BF16)
    w_ukv_b = w_ukv.astype(BF16)
    w_branch_b = w_branch.astype(BF16)
    w_out_b = w_out.astype(BF16)
    w_router = jnp.concatenate(
        [w_coarse, w_fine, jnp.zeros((L, D, 128 - N_GROUPS - N_EXPERTS), F32)], axis=2)
    b_router = jnp.concatenate(
        [b_coarse, b_fine, jnp.zeros((L, 128 - N_GROUPS - N_EXPERTS), F32)], axis=1)
    sinks_pad = jnp.concatenate([swa_sinks, jnp.zeros((L, 128 - SWA_HEADS), F32)], axis=1)

    for l in range(L):
        m6 = mod[l].reshape(B, 6, 1, D)
        sh1, sc1, g1, sh2, sc2, g2 = (m6[:, j] for j in range(6))
        (qa, kcr, vcr, ks, vs, kw, vw, gs, qb, kb, vb, qc, kc, vc, mc, qd, kd, vd) = _proj_call(
            x, sc1, sh1, norm_gain[l, 0:1], w_attn[l], qk_gain[l], lat_gain_q[l][None], lat_gain_kv[l][None],
            rope_gain[l], w_uq_b[l], w_ukv_b[l], cos, sin)
        kcmp, vcmp = _compress_call(kcr, vcr, cmp_pe[l], cmp_w1[l], cmp_w2[l], qk_gain[l, QK_NSA_KC][None])
        o_cmp, sel = _cmp_attn_call(qa, kcmp, vcmp, slopes[0])
        o_slc = _flash_call(qa, ks, vs, slopes=slopes[0], mask=sel, mask_block=SEL_BLOCK, mask_t=True,
                            tq=256, tk=256, hpc=2, out_dtype=F32, name="nsa_slc")
        o_win = _flash_call(qa, kw, vw, slopes=slopes[0], window=NSA_WINDOW,
                            tq=256, tk=256, hpc=2, out_dtype=F32, name="nsa_win")
        o_b = _flash_call(qb, kb, vb, slopes=slopes[1], window=SWA_WINDOW, sinks=sinks_pad[l][None],
                          tq=256, tk=256, name="swa")
        o_c = _flash_call(qc, kc, vc, slopes=slopes[2], mask=mc, mask_block=MOBA_BLOCK,
                          tq=256, tk=512, name="moba")
        o_d = _flash_call(qd, kd, vd, tq=256, tk=512, name="mla")
        xn, h2, route = _merge_call(
            x, (sc1, sh1, g1, sc2, sh2), norm_gain[l], w_gate[l], w_branch_b[l], w_out_b[l],
            o_cmp, o_slc, o_win, gs, o_b, o_c, o_d, w_router[l], b_router[l][None])
        route = route.reshape(B * S, 128)
        y2 = _moe(h2, route, w13, w2, l)
        x = _combine_call(xn, g2, route, y2)
    return x
```

```python
import functools
import math

import numpy as np
import jax
import jax.numpy as jnp
from jax import lax
from jax.experimental import pallas as pl
from jax.experimental.pallas import tpu as pltpu

F32 = jnp.float32
BF16 = jnp.bfloat16

HEAD_DIM = 64
NEG_INF = -1e30
EPS = 1e-6
NSA_HEADS = 4
CMP_BLOCK = 32
CMP_STRIDE = 16
CMP_HIDDEN = 256
SEL_BLOCK = 64
SEL_TOPK = 8
NSA_WINDOW = 512
FORCE_BONUS = 1e4
SWA_HEADS = 4
SWA_KV_HEADS = 2
SWA_WINDOW = 128
MOBA_HEADS = 4
MOBA_BLOCK = 256
MOBA_TOPK = 3
MLA_HEADS = 4
MLA_Q_RANK = 384
MLA_KV_RANK = 128
MLA_NOPE = 64
MLA_ROPE = 32
MLA_V = 64
ROPE_THETA = 10000.0
N_BRANCH = 4
BRANCH_WIDTH = 256
N_GROUPS = 4
EXPERTS_PER_GROUP = 8
N_EXPERTS = N_GROUPS * EXPERTS_PER_GROUP
D_EXPERT = 256

(GR_NSA_Q, GR_NSA_K, GR_SWA_Q, GR_SWA_K, GR_MOBA_Q, GR_MOBA_K,
 GR_MLA_QN, GR_MLA_QR, GR_MLA_KN, GR_MLA_KR) = range(10)
QK_NSA_Q, QK_NSA_KC, QK_NSA_KS, QK_NSA_KW = 0, 1, 2, 3
QK_SWA_Q, QK_SWA_K, QK_MOBA_Q, QK_MOBA_K, QK_MLA_Q, QK_MLA_K = 4, 5, 6, 7, 8, 9

ATTN_OLD = 2476
ATTN_COLS = 2560
GATE_LANE0 = 32
LOG2E = math.log2(math.e)
SUB, LANE = 8, 128
EXPERT_ROWS = 256
VMEM_LIMIT = 56 * 1024 * 1024


def _alibi_slopes():
    n = NSA_HEADS + SWA_HEADS + MOBA_HEADS

    def pow2(m):
        start = 2.0 ** (-8.0 / m)
        return [start ** (i + 1) for i in range(m)]

    c = 2 ** int(math.floor(math.log2(n)))
    s = pow2(c) + (pow2(2 * c)[0::2][: n - c] if c < n else [])
    s = -np.sort(-np.asarray(s, np.float32))
    return s.reshape(NSA_HEADS, 3).T


def _dot(a, b):
    return jnp.dot(a, b, preferred_element_type=F32)


def _dot_nt(a, b):
    return lax.dot_general(a, b, (((1,), (1,)), ((), ())), preferred_element_type=F32)


def _split(a):
    hi = a.astype(BF16)
    lo = (a - hi.astype(F32)).astype(BF16)
    return hi, lo


def _dot3(a, b):
    ah, al = _split(a)
    bh, bl = _split(b)
    return _dot(ah, bh) + (_dot(ah, bl) + _dot(al, bh))


def _dot3_nt(a, b):
    ah, al = _split(a)
    bh, bl = _split(b)
    return _dot_nt(ah, bh) + (_dot_nt(ah, bl) + _dot_nt(al, bh))


def _rms(x, g):
    return x * lax.rsqrt(jnp.mean(x * x, axis=-1, keepdims=True) + EPS) * g


def _sigmoid(x):
    return 1.0 / (1.0 + jnp.exp(-x))


def _topk_mask(score, k, axis=1):
    n = score.shape[axis]
    iota = lax.broadcasted_iota(jnp.int32, score.shape, axis).astype(F32)
    sel = jnp.zeros(score.shape, F32)
    for _ in range(k):
        m = jnp.max(score, axis=axis, keepdims=True)
        idx = jnp.min(jnp.where(score == m, iota, float(n)), axis=axis, keepdims=True)
        hit = iota == idx
        sel = jnp.where(hit, jnp.where(m > 0.5 * NEG_INF, 1.0, 0.0), sel)
        score = jnp.where(hit, -3e38, score)
    return sel


def _ada_kernel(c_ref, w_ref, b_ref, o_ref):
    c = c_ref[...]
    a = c * _sigmoid(c)
    o_ref[0] = _dot(a, w_ref[0]) + b_ref[0]


def _ada_mod(c, w_ada, b_ada):
    L, D, D6 = w_ada.shape
    B = c.shape[0]
    tn = 1024
    return pl.pallas_call(
        _ada_kernel,
        grid=(L, D6 // tn),
        in_specs=[
            pl.BlockSpec((B, D), lambda l, j: (0, 0)),
            pl.BlockSpec((1, D, tn), lambda l, j: (l, 0, j)),
            pl.BlockSpec((1, 1, tn), lambda l, j: (l, 0, j)),
        ],
        out_specs=pl.BlockSpec((1, B, tn), lambda l, j: (l, 0, j)),
        out_shape=jax.ShapeDtypeStruct((L, B, D6), F32),
        compiler_params=pltpu.CompilerParams(vmem_limit_bytes=VMEM_LIMIT),
        name="ada_mod",
    )(c, w_ada, b_ada.reshape(L, 1, D6))


def _rms_blocks(x, bd, g):
    hi, lo = _split(x * x)
    ms = _dot(hi, bd) + _dot(lo, bd)
    return x * lax.rsqrt(ms + EPS) * g


def _proj_kernel(x_ref, sc_ref, sh_ref, ng_ref, w_ref, gr_ref, bd64_ref, bd32_ref, lgq_ref, lgkv_ref,
                 wuq_ref, wukv_ref, cos_ref, sin_ref,
                 qa_ref, kcr_ref, vcr_ref, ks_ref, vs_ref, kw_ref, vw_ref, gs_ref,
                 qb_ref, kb_ref, vb_ref, qc_ref, kc_ref, vc_ref, mc_ref, qd_ref, kd_ref, vd_ref,
                 kmean_s):
    i = pl.program_id(1)
    hd = HEAD_DIM
    x = x_ref[0]
    h = _rms(x, ng_ref[...]) * (1.0 + sc_ref[0]) + sh_ref[0]
    hb = h.astype(BF16)
    bd64 = bd64_ref[...]
    bd64h = bd64_ref[0:2 * hd, 0:2 * hd]
    bd32 = bd32_ref[...]

    def grow(r, w=4 * hd):
        return gr_ref[r:r + 1, 0:w]

    def proj(a, b):
        return _dot(hb, w_ref[:, a:b])

    def store_heads(ref, slab, n):
        for hh in range(n):
            ref[0, hh] = slab[:, hh * hd:(hh + 1) * hd].astype(BF16)

    def store_heads_t(ref, slab_t, n):
        for hh in range(n):
            ref[0, hh] = slab_t[hh * hd:(hh + 1) * hd].astype(BF16)

    store_heads(qa_ref, _rms_blocks(proj(0, 256), bd64, grow(GR_NSA_Q)), NSA_HEADS)
    p = proj(256, 640)
    kcr_ref[0] = p[:, 0:64]
    vcr_ref[0] = p[:, 64:128]
    kk = _rms_blocks(p[:, 128:256], bd64h, grow(GR_NSA_K, 2 * hd))
    ks_ref[0, 0] = kk[:, :hd].astype(BF16)
    kw_ref[0, 0] = kk[:, hd:].astype(BF16)
    vt = p[:, 256:384].T
    vs_ref[0, 0] = vt[:hd].astype(BF16)
    vw_ref[0, 0] = vt[hd:].astype(BF16)

    p = proj(640, 1152)
    store_heads(qb_ref, _rms_blocks(p[:, 0:256], bd64, grow(GR_SWA_Q)), SWA_HEADS)
    store_heads(kb_ref, _rms_blocks(p[:, 256:384], bd64h, grow(GR_SWA_K, 2 * hd)), SWA_KV_HEADS)
    store_heads_t(vb_ref, p[:, 384:512].T, SWA_KV_HEADS)

    @pl.when(i == 0)
    def _():
        kmean_s[...] = jnp.zeros(kmean_s.shape, F32)

    p = proj(1152, 1920)
    qn = _rms_blocks(p[:, 0:256], bd64, grow(GR_MOBA_Q))
    kn = _rms_blocks(p[:, 256:512], bd64, grow(GR_MOBA_K))
    store_heads(qc_ref, qn, MOBA_HEADS)
    store_heads(kc_ref, kn, MOBA_HEADS)
    store_heads_t(vc_ref, p[:, 512:768].T, MOBA_HEADS)
    nblk = kmean_s.shape[0]
    blk_iota = lax.broadcasted_iota(jnp.int32, (1, nblk), 1)
    head_of_lane = lax.broadcasted_iota(jnp.int32, (1, 4 * hd), 1) // hd
    kmeans = kmean_s[...]
    for hh in range(MOBA_HEADS):
        g = _dot3_nt(jnp.where(head_of_lane == hh, qn, 0.0), kmeans)
        g = jnp.where(blk_iota < i, g, NEG_INF)
        mc_ref[0, hh] = jnp.where(blk_iota == i, 1.0, _topk_mask(g, MOBA_TOPK))
    kmean_s[pl.ds(i, 1), :] = jnp.mean(kn, axis=0, keepdims=True)

    p = proj(1920, 2560)
    gs_ref[0] = _sigmoid(p[:, 512:640])
    cos = cos_ref[...]
    sin = sin_ref[...]
    half = MLA_ROPE // 2
    first_half = lax.broadcasted_iota(jnp.int32, (1, 4 * MLA_ROPE), 1) % MLA_ROPE < half

    def rope(v):
        swapped = jnp.where(first_half, pltpu.roll(v, 4 * MLA_ROPE - half, 1), pltpu.roll(v, half, 1))
        return v * cos + swapped * sin

    qlat = _dot(_rms(p[:, 0:384], lgq_ref[...]).astype(BF16), wuq_ref[...])
    kvlat = _dot(_rms(p[:, 384:512], lgkv_ref[...]).astype(BF16), wukv_ref[...])
    q_nope = _rms_blocks(qlat[:, 0:256], bd64, grow(GR_MLA_QN))
    q_rot = rope(_rms_blocks(qlat[:, 256:384], bd32, grow(GR_MLA_QR, 2 * hd)))
    k_nope = _rms_blocks(kvlat[:, 0:256], bd64, grow(GR_MLA_KN))
    k_rot = rope(_rms_blocks(p[:, 512:640], bd32, grow(GR_MLA_KR, 2 * hd)))[:, :MLA_ROPE]
    store_heads_t(vd_ref, kvlat[:, 256:512].T, MLA_HEADS)
    for hh in range(MLA_HEADS):
        qd_ref[0, hh] = jnp.concatenate(
            [q_nope[:, hh * hd:(hh + 1) * hd], q_rot[:, hh * MLA_ROPE:(hh + 1) * MLA_ROPE]], axis=1).astype(BF16)
        kd_ref[0, hh] = jnp.concatenate([k_nope[:, hh * hd:(hh + 1) * hd], k_rot], axis=1).astype(BF16)


def _proj_call(x, sc1, sh1, ng, w_attn, gain_rows, bd64, bd32, lgq, lgkv, wuq, wukv, cos, sin):
    B, S, D = x.shape
    tm = MOBA_BLOCK
    nblk = S // tm
    hd = HEAD_DIM

    def full(shape):
        return pl.BlockSpec(shape, lambda b, i: (0,) * len(shape))

    def heads(nh, d):
        return pl.BlockSpec((1, nh, tm, d), lambda b, i: (b, 0, i, 0))

    in_specs = [
        pl.BlockSpec((1, tm, D), lambda b, i: (b, i, 0)),
        pl.BlockSpec((1, 1, D), lambda b, i: (b, 0, 0)),
        pl.BlockSpec((1, 1, D), lambda b, i: (b, 0, 0)),
        full((1, D)),
        full((D, ATTN_COLS)),
        full(gain_rows.shape),
        full(bd64.shape),
        full(bd32.shape),
        full(lgq.shape),
        full(lgkv.shape),
        full(wuq.shape),
        full(wukv.shape),
        pl.BlockSpec((tm, 4 * MLA_ROPE), lambda b, i: (i, 0)),
        pl.BlockSpec((tm, 4 * MLA_ROPE), lambda b, i: (i, 0)),
    ]
    row64 = pl.BlockSpec((1, tm, hd), lambda b, i: (b, i, 0))
    def heads_t(nh, d):
        return pl.BlockSpec((1, nh, d, tm), lambda b, i: (b, 0, 0, i))

    out_specs = [
        heads(4, hd), row64, row64, heads(1, hd), heads_t(1, hd), heads(1, hd), heads_t(1, hd),
        pl.BlockSpec((1, tm, 128), lambda b, i: (b, i, 0)),
        heads(4, hd), heads(2, hd), heads_t(2, hd),
        heads(4, hd), heads(4, hd), heads_t(4, hd), heads(4, nblk),
        heads(4, MLA_NOPE + MLA_ROPE), heads(4, MLA_NOPE + MLA_ROPE), heads_t(4, MLA_V),
    ]

    def sd(shape, dt):
        return jax.ShapeDtypeStruct(shape, dt)

    out_shape = [
        sd((B, 4, S, hd), BF16), sd((B, S, hd), F32), sd((B, S, hd), F32),
        sd((B, 1, S, hd), BF16), sd((B, 1, hd, S), BF16), sd((B, 1, S, hd), BF16), sd((B, 1, hd, S), BF16),
        sd((B, S, 128), F32),
        sd((B, 4, S, hd), BF16), sd((B, 2, S, hd), BF16), sd((B, 2, hd, S), BF16),
        sd((B, 4, S, hd), BF16), sd((B, 4, S, hd), BF16), sd((B, 4, hd, S), BF16), sd((B, 4, S, nblk), F32),
        sd((B, 4, S, 96), BF16), sd((B, 4, S, 96), BF16), sd((B, 4, MLA_V, S), BF16),
    ]
    return pl.pallas_call(
        _proj_kernel,
        grid=(B, nblk),
        in_specs=in_specs,
        out_specs=out_specs,
        out_shape=out_shape,
        scratch_shapes=[pltpu.VMEM((nblk, MOBA_HEADS * hd), F32)],
        compiler_params=pltpu.CompilerParams(
            dimension_semantics=("arbitrary", "arbitrary"), vmem_limit_bytes=VMEM_LIMIT),
        name="proj_prep",
    )(x, sc1, sh1, ng, w_attn, gain_rows, bd64, bd32, lgq, lgkv, wuq, wukv, cos, sin)


def _compress_kernel(gk_ref, gv_ref, pe_ref, w1_ref, w2_ref, gkc_ref, kc_ref, vc_ref):
    half = w1_ref.shape[1] // 2
    outs = []
    for j, g_ref in enumerate((gk_ref, gv_ref)):
        g = g_ref[0].astype(BF16)
        top = _dot(g, w1_ref[j, :half].astype(BF16))
        bot = _dot(g, w1_ref[j, half:].astype(BF16))
        bot = jnp.concatenate([bot[1:], bot[:1]], axis=0)
        pe = jnp.broadcast_to(pe_ref[j], (8, pe_ref.shape[2]))
        bias = _dot3(pe, w1_ref[j])[0:1]
        hid = top + bot + bias
        hid = hid * _sigmoid(hid)
        outs.append(_dot(hid.astype(BF16), w2_ref[j].astype(BF16)))
    kc_ref[0] = _rms(outs[0], gkc_ref[...]).astype(BF16)
    dk = outs[1].shape[1]
    vc_ref[0] = jnp.concatenate([outs[1], outs[1]], axis=1).T[:dk].astype(BF16)


def _compress_call(kc_raw, vc_raw, pe, w1, w2, g_kc):
    B, S, dk = kc_raw.shape
    n_grp = S // CMP_STRIDE
    gk = kc_raw.reshape(B, n_grp, CMP_STRIDE * dk)
    gv = vc_raw.reshape(B, n_grp, CMP_STRIDE * dk)
    pe_flat = pe.reshape(2, 1, CMP_BLOCK * dk)
    grp_spec = pl.BlockSpec((1, n_grp, CMP_STRIDE * dk), lambda b: (b, 0, 0))
    out_spec = pl.BlockSpec((1, n_grp, dk), lambda b: (b, 0, 0))
    return pl.pallas_call(
        _compress_kernel,
        grid=(B,),
        in_specs=[
            grp_spec, grp_spec,
            pl.BlockSpec(pe_flat.shape, lambda b: (0, 0, 0)),
            pl.BlockSpec(w1.shape, lambda b: (0, 0, 0)),
            pl.BlockSpec(w2.shape, lambda b: (0, 0, 0)),
            pl.BlockSpec((1, dk), lambda b: (0, 0)),
        ],
        out_specs=[out_spec, pl.BlockSpec((1, dk, n_grp), lambda b: (b, 0, 0))],
        out_shape=[jax.ShapeDtypeStruct((B, n_grp, dk), BF16), jax.ShapeDtypeStruct((B, dk, n_grp), BF16)],
        compiler_params=pltpu.CompilerParams(vmem_limit_bytes=VMEM_LIMIT),
        name="nsa_compress",
    )(gk, gv, pe_flat, w1, w2, g_kc)


def _cmp_attn_kernel(q_ref, kc_ref, vct_ref, cover_ref, o_ref, sel_ref, *, slopes, tq, n_cmp):
    i = pl.program_id(1)
    kc = kc_ref[0]
    vct = vct_ref[0]
    ncp = kc.shape[0]
    t_full = i * tq + lax.broadcasted_iota(jnp.int32, (ncp, tq), 1)
    n_iota = lax.broadcasted_iota(jnp.int32, (ncp, tq), 0)
    dist_i = t_full - (n_iota * CMP_STRIDE + (CMP_BLOCK - 1))
    vis = (n_iota < n_cmp) & (dist_i >= 0)
    dist = dist_i.astype(F32)
    visf = vis.astype(F32)
    psum = jnp.zeros((ncp, tq), F32)
    outs = []
    for hh in range(NSA_HEADS):
        s = _dot_nt(kc, q_ref[0, hh]) - (slopes[hh] * LOG2E) * dist
        s = jnp.where(vis, s, NEG_INF)
        e = jnp.exp2(s - jnp.max(s, axis=0, keepdims=True)) * visf
        p = e / jnp.maximum(jnp.sum(e, axis=0, keepdims=True), 1e-30)
        outs.append(_dot(vct, p.astype(BF16)))
        psum = psum + p
    o_ref[0] = jnp.concatenate(outs, axis=0).T
    ph, plo = _split(psum)
    cover = cover_ref[...]
    p_slc = _dot(cover, ph) + _dot(cover, plo)
    n_sel = cover.shape[0]
    cur = (i * tq + lax.broadcasted_iota(jnp.int32, (1, tq), 1)) // SEL_BLOCK
    j = lax.broadcasted_iota(jnp.int32, (n_sel, 1), 0)
    forced = jnp.where(j == 0, 1.0, jnp.where(j == cur, 1.0, jnp.where(j == cur - 1, 1.0, 0.0)))
    score = jnp.where(j <= cur, p_slc + FORCE_BONUS * forced, NEG_INF)
    sel_ref[0, 0] = _topk_mask(score, min(SEL_TOPK, n_sel), axis=0)


def _cmp_attn_call(qa, kc, vc, slopes):
    B, H, S, dk = qa.shape
    ncp = kc.shape[1]
    n_cmp = (S - CMP_BLOCK) // CMP_STRIDE + 1
    n_sel = S // SEL_BLOCK
    tq = 256
    starts = np.arange(ncp) * CMP_STRIDE
    jb = np.arange(n_sel) * SEL_BLOCK
    cover = ((starts[:, None] < jb[None, :] + SEL_BLOCK) & (starts[:, None] + CMP_BLOCK > jb[None, :])
             & (np.arange(ncp)[:, None] < n_cmp))
    cover = jnp.asarray(cover.T.astype(np.float32), dtype=BF16)
    return pl.pallas_call(
        functools.partial(_cmp_attn_kernel, slopes=tuple(float(s) for s in slopes), tq=tq, n_cmp=n_cmp),
        grid=(B, S // tq),
        in_specs=[
            pl.BlockSpec((1, H, tq, dk), lambda b, i: (b, 0, i, 0)),
            pl.BlockSpec((1, ncp, dk), lambda b, i: (b, 0, 0)),
            pl.BlockSpec((1, dk, ncp), lambda b, i: (b, 0, 0)),
            pl.BlockSpec((n_sel, ncp), lambda b, i: (0, 0)),
        ],
        out_specs=[
            pl.BlockSpec((1, tq, H * dk), lambda b, i: (b, i, 0)),
            pl.BlockSpec((1, 1, n_sel, tq), lambda b, i: (b, 0, 0, i)),
        ],
        out_shape=[jax.ShapeDtypeStruct((B, S, H * dk), F32),
                   jax.ShapeDtypeStruct((B, 1, n_sel, S), F32)],
        compiler_params=pltpu.CompilerParams(vmem_limit_bytes=VMEM_LIMIT),
        name="nsa_cmp_attn",
    )(qa, kc, vc, cover)


def _flash_kernel(*refs, H, G, hpc, tq, tk, slopes, window, mask_block, mask_per_head, mask_t, has_sink):
    refs = list(refs)
    q_ref, k_ref, v_ref = refs[:3]
    pos = 3
    mask_ref = sink_ref = None
    if mask_block:
        mask_ref = refs[pos]
        pos += 1
    if has_sink:
        sink_ref = refs[pos]
        pos += 1
    o_ref, m_s, l_s, acc_s = refs[pos:pos + 4]
    pos += 4
    bias_s = need_s = None
    if slopes is not None:
        bias_s = refs[pos]
        pos += 1
    if mask_block:
        need_s = refs[pos]
    R = H // G
    C = H // hpc
    M = hpc * tq
    dv = v_ref.shape[-2]
    i = pl.program_id(1)
    q0 = i * tq
    hi = q0 // tk
    lo = jnp.maximum(q0 - (window - 1), 0) // tk if window else 0

    def rel_pos():
        r_row = jnp.concatenate([lax.broadcasted_iota(jnp.int32, (1, tq), 1)] * hpc, axis=1)
        return (r_row - lax.broadcasted_iota(jnp.int32, (tk, M), 0)).astype(F32)

    qs, slope_rows, bms = [], [], []
    for c in range(C):
        heads = [c * hpc + r for r in range(hpc)]
        if hpc == 1:
            qs.append(q_ref[0, heads[0]])
        else:
            qs.append(jnp.concatenate([q_ref[0, hh] for hh in heads], axis=0))
        if slopes is not None:
            slope_rows.append(jnp.concatenate([jnp.full((1, tq), slopes[hh], F32) for hh in heads], axis=1))
            bias_s[c] = slope_rows[c] * rel_pos()
        if mask_block:
            parts = [mask_ref[0, hh if mask_per_head else 0] for hh in heads]
            bms.append(jnp.concatenate(parts, axis=1) if mask_t else jnp.concatenate(parts, axis=0).T)
    m_s[...] = jnp.full(m_s.shape, NEG_INF, F32)
    l_s[...] = jnp.zeros(l_s.shape, F32)
    acc_s[...] = jnp.zeros(acc_s.shape, F32)

    if mask_block:
        nblk = bms[0].shape[0]
        col = bms[0]
        for bm in bms[1:]:
            col = jnp.maximum(col, bm)
        col = jnp.max(col, axis=1, keepdims=True)
        bpt = tk // mask_block
        for jt in range(nblk // bpt):
            need_s[jt] = jnp.max(col[jt * bpt:(jt + 1) * bpt]).astype(jnp.int32)
        bms = [bm.astype(BF16) for bm in bms]

    def tile(j, edge):
        k0 = pl.multiple_of(j * tk, tk)
        off = (q0 - k0).astype(F32)
        valid = None
        if edge:
            dist = rel_pos() + off
            valid = dist >= 0.0
            if window:
                valid = valid & (dist < float(window))
        if mask_block:
            key_blk = lax.shift_right_logical(
                lax.broadcasted_iota(jnp.int32, (tk, nblk), 0) + k0, int(math.log2(mask_block)))
            expand = jnp.where(key_blk == lax.broadcasted_iota(jnp.int32, (tk, nblk), 1), 1.0, 0.0).astype(BF16)
        scores = [_dot_nt(k_ref[0, (c * hpc) // R, pl.ds(k0, tk), :], qs[c]) for c in range(C)]
        sels = [_dot(expand, bms[c]) for c in range(C)] if mask_block else None
        probs, alphas = [], []
        for g in range(C):
            s = scores[g]
            if slopes is not None:
                s = s - bias_s[g]
                shift = slope_rows[g] * off
            ok = valid
            if mask_block:
                sel = sels[g] > 0.5
                ok = sel if ok is None else ok & sel
            if ok is not None:
                s = jnp.where(ok, s, NEG_INF)
            m_prev = m_s[g]
            s_max = jnp.max(s, axis=0, keepdims=True)
            if slopes is not None:
                s_max = s_max - shift
            m_new = jnp.maximum(m_prev, s_max)
            alpha = jnp.exp2(m_prev - m_new)
            p = jnp.exp2(s - (m_new + shift if slopes is not None else m_new))
            l_s[g] = alpha * l_s[g] + jnp.sum(p, axis=0, keepdims=True)
            m_s[g] = m_new
            probs.append(p.astype(BF16))
            alphas.append(alpha)
        for c in range(C):
            v = v_ref[0, (c * hpc) // R, :, pl.ds(k0, tk)]
            acc_s[c] = alphas[c] * acc_s[c] + _dot(v, probs[c])

    def body(step, carry):
        j = hi - step
        is_edge = step == 0
        if window:
            is_edge = is_edge | (q0 - j * tk + (tq - 1) >= window)
        run = (step == 0) | (need_s[j] > 0) if mask_block else None

        def when(c):
            return pl.when(c if run is None else c & run)

        @when(is_edge)
        def _():
            tile(j, True)

        @when(jnp.logical_not(is_edge))
        def _():
            tile(j, False)

        return carry

    lax.fori_loop(0, hi - lo + 1, body, 0)

    for c in range(C):
        m = m_s[c]
        l = l_s[c]
        acc = acc_s[c]
        if has_sink:
            sk = LOG2E * jnp.concatenate(
                [jnp.broadcast_to(sink_ref[:, c * hpc + r:c * hpc + r + 1], (1, tq)) for r in range(hpc)], axis=1)
            m_f = jnp.maximum(m, sk)
            a = jnp.exp2(m - m_f)
            l = l * a + jnp.exp2(sk - m_f)
            acc = acc * a
        out = acc / l
        for r in range(hpc):
            hh = c * hpc + r
            o_ref[0, :, hh * dv:(hh + 1) * dv] = out[:, r * tq:(r + 1) * tq].T.astype(o_ref.dtype)


def _flash_call(q, k, v, *, slopes=None, window=0, mask=None, mask_block=0, mask_t=False, sinks=None,
                tq=128, tk=256, hpc=None, out_dtype=BF16, name="flash"):
    B, H, S, dq = q.shape
    G = k.shape[1]
    dv = v.shape[-2]
    tk = min(tk, S)
    assert tk % tq == 0 and S % tk == 0
    R = H // G
    hpc = R if hpc is None else hpc
    assert R % hpc == 0
    C = H // hpc
    in_specs = [
        pl.BlockSpec((1, H, tq, dq), lambda b, i: (b, 0, i, 0)),
        pl.BlockSpec((1, G, S, dq), lambda b, i: (b, 0, 0, 0)),
        pl.BlockSpec((1, G, dv, S), lambda b, i: (b, 0, 0, 0)),
    ]
    args = [q, k, v]
    mask_per_head = False
    if mask is not None:
        hm = mask.shape[1]
        mask_per_head = hm > 1
        if mask_t:
            in_specs.append(pl.BlockSpec((1, hm, mask.shape[2], tq), lambda b, i: (b, 0, 0, i)))
        else:
            in_specs.append(pl.BlockSpec((1, hm, tq, mask.shape[3]), lambda b, i: (b, 0, i, 0)))
        args.append(mask)
    if sinks is not None:
        in_specs.append(pl.BlockSpec(sinks.shape, lambda b, i: (0, 0)))
        args.append(sinks)
    kern = functools.partial(
        _flash_kernel, H=H, G=G, hpc=hpc, tq=tq, tk=tk,
        slopes=None if slopes is None else tuple(float(s) * LOG2E for s in slopes),
        window=window, mask_block=mask_block if mask is not None else 0,
        mask_per_head=mask_per_head, mask_t=mask_t, has_sink=sinks is not None)
    scratch =[pltpu.VMEM((C, 1, hpc * tq), F32), pltpu.VMEM((C, 1, hpc * tq), F32),
               pltpu.VMEM((C, dv, hpc * tq), F32)]
    if slopes is not None:
        scratch.append(pltpu.VMEM((C, tk, hpc * tq), F32))
    if mask is not None:
        assert tk % mask_block == 0
        scratch.append(pltpu.SMEM((S // tk,), jnp.int32))
    return pl.pallas_call(
        kern,
        grid=(B, S // tq),
        in_specs=in_specs,
        out_specs=pl.BlockSpec((1, tq, H * dv), lambda b, i: (b, i, 0)),
        out_shape=jax.ShapeDtypeStruct((B, S, H * dv), out_dtype),
        scratch_shapes=scratch,
        compiler_params=pltpu.CompilerParams(vmem_limit_bytes=VMEM_LIMIT),
        name=name,
    )(*args)


def _merge_kernel(x_ref, sc1_ref, sh1_ref, g1_ref, sc2_ref, sh2_ref, ng_ref, wg_ref, wb_ref, wo_ref,
                  ocmp_ref, oslc_ref, owin_ref, gs_ref, ob_ref, oc_ref, od_ref, wr_ref, br_ref,
                  xo_ref, h2_ref, rt_ref):
    hd = HEAD_DIM
    x = x_ref[0]
    ng = ng_ref[...]
    h = _rms(x, ng[0:1]) * (1.0 + sc1_ref[0]) + sh1_ref[0]
    hb = h.astype(BF16)
    gs = gs_ref[0]
    ocmp = ocmp_ref[0]
    oslc = oslc_ref[0]
    owin = owin_ref[0]
    parts = []
    for hh in range(NSA_HEADS):
        c0 = GATE_LANE0 + 3 * hh
        sl = slice(hh * hd, (hh + 1) * hd)
        parts.append(gs[:, c0:c0 + 1] * ocmp[:, sl] + gs[:, c0 + 1:c0 + 2] * oslc[:, sl]
                     + gs[:, c0 + 2:c0 + 3] * owin[:, sl])
    o_a = jnp.concatenate(parts, axis=1).astype(BF16)
    branches = (o_a, ob_ref[0], oc_ref[0], od_ref[0])
    D = x.shape[1]
    mixed = None
    for n in range(N_BRANCH):
        gate = _sigmoid(_dot(hb, wg_ref[:, n * D:(n + 1) * D]))
        term = gate * _dot(branches[n], wb_ref[n])
        mixed = term if mixed is None else mixed + term
    xn = x + g1_ref[0] * _dot(mixed.astype(BF16), wo_ref[...])
    xo_ref[0] = xn
    h2 = _rms(xn, ng[1:2]) * (1.0 + sc2_ref[0]) + sh2_ref[0]
    for c in range(SUB):
        h2_ref[pl.ds(c, h2.shape[0], stride=SUB), :] = h2[:, c * LANE:(c + 1) * LANE]

    logits = _dot3(h2, wr_ref[...]) + br_ref[...]
    lane = lax.broadcasted_iota(jnp.int32, logits.shape, 1)
    lanef = lane.astype(F32)
    is_c = lane < N_GROUPS
    lc = jnp.where(is_c, logits, NEG_INF)
    mc = jnp.max(lc, axis=-1, keepdims=True)
    grp = jnp.min(jnp.where(lc == mc, lanef, 1e9), axis=-1, keepdims=True)
    p_grp = 1.0 / jnp.sum(jnp.where(is_c, jnp.exp(lc - mc), 0.0), axis=-1, keepdims=True)
    e_lane = lanef - float(N_GROUPS)
    in_grp = (lane >= N_GROUPS) & (lane < N_GROUPS + N_EXPERTS) & (
        jnp.floor(e_lane / EXPERTS_PER_GROUP) == grp)
    lf = jnp.where(in_grp, logits, NEG_INF)
    m1 = jnp.max(lf, axis=-1, keepdims=True)
    i1 = jnp.min(jnp.where(lf == m1, e_lane, 1e9), axis=-1, keepdims=True)
    lf2 = jnp.where(e_lane == i1, NEG_INF, lf)
    m2 = jnp.max(lf2, axis=-1, keepdims=True)
    i2 = jnp.min(jnp.where(lf2 == m2, e_lane, 1e9), axis=-1, keepdims=True)
    e2 = jnp.exp(m2 - m1)
    w1 = p_grp / (1.0 + e2)
    w2 = p_grp * e2 / (1.0 + e2)
    rt = jnp.where(lane == 0, i1, jnp.where(lane == 1, i2, jnp.where(lane == 2, w1, jnp.where(lane == 3, w2, 0.0))))
    rt_ref[0] = rt


def _merge_call(x, mods, ng, wg, wb, wo, ocmp, oslc, owin, gs, ob, oc, od, wr, br):
    B, S, D = x.shape
    tm = 256
    sc1, sh1, g1, sc2, sh2 = mods

    def full(a):
        return pl.BlockSpec(a.shape, lambda b, i: (0,) * a.ndim)

    modspec = pl.BlockSpec((1, 1, D), lambda b, i: (b, 0, 0))
    row = lambda w: pl.BlockSpec((1, tm, w), lambda b, i: (b, i, 0))
    in_specs = [row(D), modspec, modspec, modspec, modspec, modspec, full(ng), full(wg), full(wb), full(wo),
                row(256), row(256), row(256), row(128), row(256), row(256), row(256), full(wr), full(br)]
    return pl.pallas_call(
        _merge_kernel,
        grid=(B, S // tm),
        in_specs=in_specs,
        out_specs=[row(D), pl.BlockSpec((tm * SUB, LANE), lambda b, i: (b * (S // tm) + i, 0)), row(128)],
        out_shape=[jax.ShapeDtypeStruct((B, S, D), F32), jax.ShapeDtypeStruct((B * S * SUB, LANE), F32),
                   jax.ShapeDtypeStruct((B, S, 128), F32)],
        compiler_params=pltpu.CompilerParams(vmem_limit_bytes=VMEM_LIMIT),
        name="merge_router",
    )(x, sc1, sh1, g1, sc2, sh2, ng, wg, wb, wo, ocmp, oslc, owin, gs, ob, oc, od, wr, br)


def _expert_kernel(blk_e_ref, nreal_ref, tok_ref, dst_ref, h2_hbm, w13_ref, w2_ref, y2_hbm, xbuf, ybuf, gsem, ssem):
    del blk_e_ref
    i = pl.program_id(0)
    nb = pl.num_programs(0)
    slot = i % 2
    rows = EXPERT_ROWS
    grp = 8

    def tile_rows(t):
        return pl.ds(pl.multiple_of(t * SUB, SUB), SUB)

    def gather_copy(blk, s, r):
        tok = tok_ref[blk * rows + r]
        return pltpu.make_async_copy(h2_hbm.at[tile_rows(tok), :], xbuf.at[s, tile_rows(r), :], gsem.at[s])

    def scatter_copy(blk, s, r):
        d = dst_ref[blk * rows + r]
        return pltpu.make_async_copy(ybuf.at[s, tile_rows(r), :], y2_hbm.at[tile_rows(d), :], ssem.at[s])

    def for_real_rows(blk, per_row, per_group):
        n = nreal_ref[blk]
        nfull = n // grp

        def body(g, carry):
            per_group(g)
            return carry
        lax.fori_loop(0, nfull, body, 0)
        for u in range(grp - 1):
            r = nfull * grp + u

            @pl.when(r < n)
            def _():
                per_row(r)

    def start_gather(blk, s):
        def group(g):
            for u in range(grp):
                gather_copy(blk, s, g * grp + u).start()
        for_real_rows(blk, lambda r: gather_copy(blk, s, r).start(), group)

    def start_scatter(blk, s):
        def group(g):
            for u in range(grp):
                scatter_copy(blk, s, g * grp + u).start()
        for_real_rows(blk, lambda r: scatter_copy(blk, s, r).start(), group)

    def wait_gather(blk, s):
        def tiles(k):
            return pltpu.make_async_copy(h2_hbm.at[pl.ds(0, k * SUB), :], xbuf.at[s, pl.ds(0, k * SUB), :], gsem.at[s])
        for_real_rows(blk, lambda r: tiles(1).wait(), lambda g: tiles(grp).wait())

    def wait_scatter(blk, s):
        def tiles(k):
            return pltpu.make_async_copy(ybuf.at[s, pl.ds(0, k * SUB), :], y2_hbm.at[pl.ds(0, k * SUB), :], ssem.at[s])
        for_real_rows(blk, lambda r: tiles(1).wait(), lambda g: tiles(grp).wait())

    @pl.when(i == 0)
    def _():
        xbuf[...] = jnp.zeros(xbuf.shape, F32)
        start_gather(0, 0)

    @pl.when(i + 1 < nb)
    def _():
        start_gather(i + 1, 1 - slot)

    wait_gather(i, slot)

    @pl.when(i >= 2)
    def _():
        wait_scatter(i - 2, slot)

    ab = None
    for c in range(0, SUB, 2):
        xc = jnp.concatenate([xbuf[slot, pl.ds(c, rows, stride=SUB), :],
                              xbuf[slot, pl.ds(c + 1, rows, stride=SUB), :]], axis=1).astype(BF16)
        part = _dot(xc, w13_ref[0, 0, c * LANE:(c + 2) * LANE, :].astype(BF16))
        ab = part if ab is None else ab + part
    de = ab.shape[1] // 2
    a = ab[:, :de]
    b = ab[:, de:]
    act = ((a * _sigmoid(a)) * b).astype(BF16)
    for c in range(0, SUB, 2):
        y = _dot(act, w2_ref[0, 0, :, c * LANE:(c + 2) * LANE].astype(BF16))
        ybuf[slot, pl.ds(c, rows, stride=SUB), :] = y[:, :LANE]
        ybuf[slot, pl.ds(c + 1, rows, stride=SUB), :] = y[:, LANE:]
    start_scatter(i, slot)

    @pl.when(i == nb - 1)
    def _():
        @pl.when(nb >= 2)
        def _():
            wait_scatter(i - 1, 1 - slot)
        wait_scatter(i, slot)


def _expert_call(h2, blk_e, n_real, row_tok, row_dst, w13, w2, layer):
    D = SUB * LANE
    assert h2.shape[1] == LANE and w13.shape[2] == D
    n_blocks = blk_e.shape[0]
    de2 = w13.shape[-1]
    grid_spec = pltpu.PrefetchScalarGridSpec(
        num_scalar_prefetch=4,
        grid=(n_blocks,),
        in_specs=[
            pl.BlockSpec(memory_space=pl.ANY),
            pl.BlockSpec((1, 1, D, de2), lambda i, be, nr, rt, rd: (layer, be[i], 0, 0)),
            pl.BlockSpec((1, 1, de2 // 2, D), lambda i, be, nr, rt, rd: (layer, be[i], 0, 0)),
        ],
        out_specs=pl.BlockSpec(memory_space=pl.ANY),
        scratch_shapes=[pltpu.VMEM((2, EXPERT_ROWS * SUB, LANE), F32), pltpu.VMEM((2, EXPERT_ROWS * SUB, LANE), F32),
                        pltpu.SemaphoreType.DMA((2,)), pltpu.SemaphoreType.DMA((2,))],
    )
    return pl.pallas_call(
        _expert_kernel,
        grid_spec=grid_spec,
        out_shape=jax.ShapeDtypeStruct((2 * h2.shape[0], LANE), F32),
        compiler_params=pltpu.CompilerParams(
            dimension_semantics=("arbitrary",), vmem_limit_bytes=VMEM_LIMIT),
        name="experts",
    )(blk_e, n_real, row_tok, row_dst, h2, w13, w2)


def _combine_kernel(xn_ref, g2_ref, rt_ref, y0_ref, y1_ref, o_ref):
    rt = rt_ref[...]
    tm = xn_ref.shape[0]
    w0 = rt[:, 2:3]
    w1 = rt[:, 3:4]
    g2 = g2_ref[0]
    for c in range(SUB):
        cols = slice(c * LANE, (c + 1) * LANE)
        y = w0 * y0_ref[pl.ds(c, tm, stride=SUB), :] + w1 * y1_ref[pl.ds(c, tm, stride=SUB), :]
        o_ref[:, cols] = xn_ref[:, cols] + g2[:, cols] * y


def _combine_call(xn, g2, route, y2):
    B, S, D = xn.shape
    tm = 512
    spb = S // tm
    nt = B * spb
    return pl.pallas_call(
        _combine_kernel,
        grid=(B, spb),
        in_specs=[
            pl.BlockSpec((tm, D), lambda b, i: (b * spb + i, 0)),
            pl.BlockSpec((1, 1, D), lambda b, i: (b, 0, 0)),
            pl.BlockSpec((tm, 128), lambda b, i: (b * spb + i, 0)),
            pl.BlockSpec((tm * SUB, LANE), lambda b, i: (b * spb + i, 0)),
            pl.BlockSpec((tm * SUB, LANE), lambda b, i: (nt + b * spb + i, 0)),
        ],
        out_specs=pl.BlockSpec((tm, D), lambda b, i: (b * spb + i, 0)),
        out_shape=jax.ShapeDtypeStruct((B * S, D), F32),
        compiler_params=pltpu.CompilerParams(vmem_limit_bytes=VMEM_LIMIT),
        name="moe_combine",
    )(xn.reshape(B * S, D), g2, route, y2, y2).reshape(B, S, D)


def _moe(h2, route, w13, w2, layer):
    N = h2.shape[0] // SUB
    K = 2
    E = N_EXPERTS
    rows = EXPERT_ROWS
    flat_e = route[:, 0:2].astype(jnp.int32).reshape(-1)
    order = jnp.argsort(flat_e).astype(jnp.int32)
    counts = jnp.sum((flat_e[:, None] == jnp.arange(E)[None, :]).astype(jnp.int32), axis=0)
    padded = (counts + rows - 1) // rows * rows
    pend = jnp.cumsum(padded)
    pstart = pend - padded
    start = jnp.cumsum(counts) - counts
    n_blocks = (N * K) // rows + E
    R = n_blocks * rows
    blk_e = jnp.minimum(
        jnp.sum((jnp.arange(n_blocks)[:, None] * rows >= pend[None, :]).astype(jnp.int32), axis=1), E - 1)
    pos = jnp.arange(R, dtype=jnp.int32)
    e_pos = jnp.repeat(blk_e, rows)
    local = pos - pstart[e_pos]
    is_real = (pos < pend[E - 1]) & (local < counts[e_pos])
    rank = jnp.clip(start[e_pos] + local, 0, N * K - 1)
    pair = order[rank]
    row_tok = jnp.where(is_real, pair // K, 0).astype(jnp.int32)
    row_dst = jnp.where(is_real, (pair % K) * N + pair // K, 0).astype(jnp.int32)
    n_real = jnp.sum(is_real.reshape(n_blocks, rows).astype(jnp.int32), axis=1)
    return _expert_call(h2, blk_e.astype(jnp.int32), n_real, row_tok, row_dst, w13, w2, layer)


def kernel(x, c, w_ada, b_ada, norm_gain, w_in, qk_gain, cmp_pe, cmp_w1, cmp_w2, swa_sinks,
           lat_gain_q, lat_gain_kv, rope_gain, w_uq, w_ukv, w_branch, w_out,
           w_coarse, b_coarse, w_fine, b_fine, w13, w2):
    B, S, D = x.shape
    L = w_in.shape[0]
    assert S % MOBA_BLOCK == 0 and D == 1024
    slopes = _alibi_slopes()

    half = MLA_ROPE // 2
    inv = ROPE_THETA ** (-jnp.arange(half, dtype=F32) / half)
    ang = jnp.arange(S).astype(F32)[:, None] * inv[None, :]
    cos = jnp.tile(jnp.cos(ang), (1, 8))
    sin = jnp.tile(jnp.concatenate([-jnp.sin(ang), jnp.sin(ang)], axis=1), (1, 4))

    mod = _ada_mod(c, w_ada, b_ada)

    w_attn = jnp.concatenate(
        [w_in[:, :, :448], w_in[:, :, 512:576], w_in[:, :, 448:512], w_in[:, :, 576:640],
         w_in[:, :, 652:ATTN_OLD], w_in[:, :, 640:652],
         jnp.zeros((L, D, ATTN_COLS - ATTN_OLD), F32)], axis=2).astype(BF16)
    w_gate = w_in[:, :, ATTN_OLD:].astype(BF16)
    dq, dkv = MLA_NOPE + MLA_ROPE, MLA_NOPE + MLA_V
    uq = w_uq.reshape(L, MLA_Q_RANK, MLA_HEADS, dq)
    w_uq_b = jnp.concatenate([uq[..., :MLA_NOPE].reshape(L, MLA_Q_RANK, -1),
                              uq[..., MLA_NOPE:].reshape(L, MLA_Q_RANK, -1)], axis=2).astype(BF16)
    ukv = w_ukv.reshape(L, MLA_KV_RANK, MLA_HEADS, dkv)
    w_ukv_b = jnp.concatenate([ukv[..., :MLA_NOPE].reshape(L, MLA_KV_RANK, -1),
                               ukv[..., MLA_NOPE:].reshape(L, MLA_KV_RANK, -1)], axis=2).astype(BF16)

    qs = HEAD_DIM ** -0.5 * LOG2E
    ms = dq ** -0.5 * LOG2E

    def lanes(v, reps, scale=1.0):
        r = jnp.tile(v, (1, reps)) * scale
        return jnp.pad(r, ((0, 0), (0, 4 * HEAD_DIM - r.shape[1])))

    gq = qk_gain
    gain_rows = jnp.stack([
        lanes(gq[:, QK_NSA_Q], 4, qs),
        lanes(jnp.concatenate([gq[:, QK_NSA_KS], gq[:, QK_NSA_KW]], axis=1), 1),
        lanes(gq[:, QK_SWA_Q], 4, qs),
        lanes(gq[:, QK_SWA_K], 2),
        lanes(gq[:, QK_MOBA_Q], 4, qs),
        lanes(gq[:, QK_MOBA_K], 4),
        lanes(gq[:, QK_MLA_Q], 4, ms),
        lanes(rope_gain[:, 0], 4, ms),
        lanes(gq[:, QK_MLA_K], 4),
        lanes(rope_gain[:, 1], 4),
    ], axis=1)
    bd64 = jnp.asarray(np.kron(np.eye(4), np.ones((HEAD_DIM, HEAD_DIM))) / HEAD_DIM, dtype=BF16)
    bd32 = jnp.asarray(np.kron(np.eye(4), np.ones((MLA_ROPE, MLA_ROPE))) / MLA_ROPE, dtype=BF16)
    w_branch_b = w_branch.astype(BF16)
    w_out_b = w_out.astype(BF16)
    w_router = jnp.concatenate(
        [w_coarse, w_fine, jnp.zeros((L, D, 128 - N_GROUPS - N_EXPERTS), F32)], axis=2)
    b_router = jnp.concatenate(
        [b_coarse, b_fine, jnp.zeros((L, 128 - N_GROUPS - N_EXPERTS), F32)], axis=1)
    sinks_pad = jnp.concatenate([swa_sinks, jnp.zeros((L, 128 - SWA_HEADS), F32)], axis=1)

    for l in range(L):
        m6 = mod[l].reshape(B, 6, 1, D)
        sh1, sc1, g1, sh2, sc2, g2 = (m6[:, j] for j in range(6))
        (qa, kcr, vcr, ks, vs, kw, vw, gs, qb, kb, vb, qc, kc, vc, mc, qd, kd, vd) = _proj_call(
            x, sc1, sh1, norm_gain[l, 0:1], w_attn[l], gain_rows[l], bd64, bd32, lat_gain_q[l][None],
            lat_gain_kv[l][None], w_uq_b[l], w_ukv_b[l], cos, sin)
        kcmp, vcmp = _compress_call(kcr, vcr, cmp_pe[l], cmp_w1[l], cmp_w2[l], qk_gain[l, QK_NSA_KC][None])
        o_cmp, sel = _cmp_attn_call(qa, kcmp, vcmp, slopes[0])
        o_slc = _flash_call(qa, ks, vs, slopes=slopes[0], mask=sel, mask_block=SEL_BLOCK, mask_t=True,
                            tq=256, tk=256, hpc=2, out_dtype=F32, name="nsa_slc")
        o_win = _flash_call(qa, kw, vw, slopes=slopes[0], window=NSA_WINDOW,
                            tq=256, tk=256, hpc=2, out_dtype=F32, name="nsa_win")
        o_b = _flash_call(qb, kb, vb, slopes=slopes[1], window=SWA_WINDOW, sinks=sinks_pad[l][None],
                          tq=256, tk=256, name="swa")
        o_c = _flash_call(qc, kc, vc, slopes=slopes[2], mask=mc, mask_block=MOBA_BLOCK,
                          tq=256, tk=512, name="moba")
        o_d = _flash_call(qd, kd, vd, tq=256, tk=512, name="mla")
        xn, h2, route = _merge_call(
            x, (sc1, sh1, g1, sc2, sh2), norm_gain[l], w_gate[l], w_branch_b[l], w_out_b[l],
            o_cmp, o_slc, o_win, gs, o_b, o_c, o_d, w_router[l], b_router[l][None])
        route = route.reshape(B * S, 128)
        y2 = _moe(h2, route, w13, w2, l)
        x = _combine_call(xn, g2, route, y2)
    return x
```

```python
import functools
import math

import numpy as np
import jax
import jax.numpy as jnp
from jax import lax
from jax.experimental import pallas as pl
from jax.experimental.pallas import tpu as pltpu

F32 = jnp.float32
BF16 = jnp.bfloat16

HEAD_DIM = 64
NEG_INF = -1e30
EPS = 1e-6
NSA_HEADS = 4
CMP_BLOCK = 32
CMP_STRIDE = 16
CMP_HIDDEN = 256
SEL_BLOCK = 64
SEL_TOPK = 8
NSA_WINDOW = 512
FORCE_BONUS = 1e4
SWA_HEADS = 4
SWA_KV_HEADS = 2
SWA_WINDOW = 128
MOBA_HEADS = 4
MOBA_BLOCK = 256
MOBA_TOPK = 3
MLA_HEADS = 4
MLA_Q_RANK = 384
MLA_KV_RANK = 128
MLA_NOPE = 64
MLA_ROPE = 32
MLA_V = 64
ROPE_THETA = 10000.0
N_BRANCH = 4
BRANCH_WIDTH = 256
N_GROUPS = 4
EXPERTS_PER_GROUP = 8
N_EXPERTS = N_GROUPS * EXPERTS_PER_GROUP
D_EXPERT = 256

(GR_NSA_Q, GR_NSA_K, GR_SWA_Q, GR_SWA_K, GR_MOBA_Q, GR_MOBA_K,
 GR_MLA_QN, GR_MLA_QR, GR_MLA_KN, GR_MLA_KR) = range(10)
QK_NSA_Q, QK_NSA_KC, QK_NSA_KS, QK_NSA_KW = 0, 1, 2, 3
QK_SWA_Q, QK_SWA_K, QK_MOBA_Q, QK_MOBA_K, QK_MLA_Q, QK_MLA_K = 4, 5, 6, 7, 8, 9

ATTN_OLD = 2476
ATTN_COLS = 2560
GATE_LANE0 = 32
LOG2E = math.log2(math.e)
SUB, LANE = 8, 128
EXPERT_ROWS = 256
VMEM_LIMIT = 56 * 1024 * 1024


def _alibi_slopes():
    n = NSA_HEADS + SWA_HEADS + MOBA_HEADS

    def pow2(m):
        start = 2.0 ** (-8.0 / m)
        return [start ** (i + 1) for i in range(m)]

    c = 2 ** int(math.floor(math.log2(n)))
    s = pow2(c) + (pow2(2 * c)[0::2][: n - c] if c < n else [])
    s = -np.sort(-np.asarray(s, np.float32))
    return s.reshape(NSA_HEADS, 3).T


def _dot(a, b):
    return jnp.dot(a, b, preferred_element_type=F32)


def _dot_nt(a, b):
    return lax.dot_general(a, b, (((1,), (1,)), ((), ())), preferred_element_type=F32)


def _split(a):
    hi = a.astype(BF16)
    lo = (a - hi.astype(F32)).astype(BF16)
    return hi, lo


def _dot3(a, b):
    ah, al = _split(a)
    bh, bl = _split(b)
    return _dot(ah, bh) + (_dot(ah, bl) + _dot(al, bh))


def _dot3_nt(a, b):
    ah, al = _split(a)
    bh, bl = _split(b)
    return _dot_nt(ah, bh) + (_dot_nt(ah, bl) + _dot_nt(al, bh))


def _rms(x, g):
    return x * lax.rsqrt(jnp.mean(x * x, axis=-1, keepdims=True) + EPS) * g


def _sigmoid(x):
    return 1.0 / (1.0 + jnp.exp(-x))


AUG_MASK0 = 6


def _bf16_pieces(x, n=3):
    out, r = [], float(x)
    for _ in range(n):
        piece = float(np.asarray(r, np.float32).astype(jnp.bfloat16).astype(np.float32))
        out.append(piece)
        r -= piece
    return out


def _topk_mask(score, k, axis=1):
    n = score.shape[axis]
    iota = lax.broadcasted_iota(jnp.int32, score.shape, axis).astype(F32)
    sel = jnp.zeros(score.shape, F32)
    for _ in range(k):
        m = jnp.max(score, axis=axis, keepdims=True)
        idx = jnp.min(jnp.where(score == m, iota, float(n)), axis=axis, keepdims=True)
        hit = iota == idx
        sel = jnp.where(hit, jnp.where(m > 0.5 * NEG_INF, 1.0, 0.0), sel)
        score = jnp.where(hit, -3e38, score)
    return sel


def _ada_kernel(c_ref, w_ref, b_ref, o_ref):
    c = c_ref[...]
    a = c * _sigmoid(c)
    o_ref[0] = _dot(a, w_ref[0]) + b_ref[0]


def _ada_mod(c, w_ada, b_ada):
    L, D, D6 = w_ada.shape
    B = c.shape[0]
    tn = 1024
    return pl.pallas_call(
        _ada_kernel,
        grid=(L, D6 // tn),
        in_specs=[
            pl.BlockSpec((B, D), lambda l, j: (0, 0)),
            pl.BlockSpec((1, D, tn), lambda l, j: (l, 0, j)),
            pl.BlockSpec((1, 1, tn), lambda l, j: (l, 0, j)),
        ],
        out_specs=pl.BlockSpec((1, B, tn), lambda l, j: (l, 0, j)),
        out_shape=jax.ShapeDtypeStruct((L, B, D6), F32),
        compiler_params=pltpu.CompilerParams(vmem_limit_bytes=VMEM_LIMIT),
        name="ada_mod",
    )(c, w_ada, b_ada.reshape(L, 1, D6))


def _rms_blocks(x, bd, g):
    hi, lo = _split(x * x)
    ms = _dot(hi, bd) + _dot(lo, bd)
    return x * lax.rsqrt(ms + EPS) * g


def _proj_kernel(x_ref, sc_ref, sh_ref, ng_ref, w_ref, gr_ref, bd64_ref, bd32_ref, lgq_ref, lgkv_ref,
                 wuq_ref, wukv_ref, cos_ref, sin_ref,
                 qa_ref, kcr_ref, vcr_ref, ks_ref, vs_ref, kw_ref, vw_ref, gs_ref,
                 qb_ref, kb_ref, vb_ref, qc_ref, kc_ref, vc_ref, mc_ref, qd_ref, kd_ref, vd_ref,
                 kmean_s):
    i = pl.program_id(1)
    hd = HEAD_DIM
    x = x_ref[0]
    h = _rms(x, ng_ref[...]) * (1.0 + sc_ref[0]) + sh_ref[0]
    hb = h.astype(BF16)
    bd64 = bd64_ref[...]
    bd64h = bd64_ref[0:2 * hd, 0:2 * hd]
    bd32 = bd32_ref[...]

    def grow(r, w=4 * hd):
        return gr_ref[r:r + 1, 0:w]

    def proj(a, b):
        return _dot(hb, w_ref[:, a:b])

    def store_heads(ref, slab, n):
        for hh in range(n):
            ref[0, hh] = slab[:, hh * hd:(hh + 1) * hd].astype(BF16)

    def store_heads_t(ref, slab_t, n):
        for hh in range(n):
            ref[0, hh] = slab_t[hh * hd:(hh + 1) * hd].astype(BF16)

    store_heads(qa_ref, _rms_blocks(proj(0, 256), bd64, grow(GR_NSA_Q)), NSA_HEADS)
    p = proj(256, 640)
    kcr_ref[0] = p[:, 0:64]
    vcr_ref[0] = p[:, 64:128]
    kk = _rms_blocks(p[:, 128:256], bd64h, grow(GR_NSA_K, 2 * hd))
    ks_ref[0, 0] = kk[:, :hd].astype(BF16)
    kw_ref[0, 0] = kk[:, hd:].astype(BF16)
    vt = p[:, 256:384].T
    vs_ref[0, 0] = vt[:hd].astype(BF16)
    vw_ref[0, 0] = vt[hd:].astype(BF16)

    p = proj(640, 1152)
    store_heads(qb_ref, _rms_blocks(p[:, 0:256], bd64, grow(GR_SWA_Q)), SWA_HEADS)
    store_heads(kb_ref, _rms_blocks(p[:, 256:384], bd64h, grow(GR_SWA_K, 2 * hd)), SWA_KV_HEADS)
    store_heads_t(vb_ref, p[:, 384:512].T, SWA_KV_HEADS)

    @pl.when(i == 0)
    def _():
        kmean_s[...] = jnp.zeros(kmean_s.shape, F32)

    p = proj(1152, 1920)
    qn = _rms_blocks(p[:, 0:256], bd64, grow(GR_MOBA_Q))
    kn = _rms_blocks(p[:, 256:512], bd64, grow(GR_MOBA_K))
    store_heads(qc_ref, qn, MOBA_HEADS)
    store_heads(kc_ref, kn, MOBA_HEADS)
    store_heads_t(vc_ref, p[:, 512:768].T, MOBA_HEADS)
    nblk = kmean_s.shape[0]
    blk_iota = lax.broadcasted_iota(jnp.int32, (1, nblk), 1)
    head_of_lane = lax.broadcasted_iota(jnp.int32, (1, 4 * hd), 1) // hd
    kmeans = kmean_s[...]
    for hh in range(MOBA_HEADS):
        g = _dot3_nt(jnp.where(head_of_lane == hh, qn, 0.0), kmeans)
        g = jnp.where(blk_iota < i, g, NEG_INF)
        mc_ref[0, hh] = jnp.where(blk_iota == i, 1.0, _topk_mask(g, MOBA_TOPK))
    kmean_s[pl.ds(i, 1), :] = jnp.mean(kn, axis=0, keepdims=True)

    p = proj(1920, 2560)
    gs_ref[0] = _sigmoid(p[:, 512:640])
    cos = cos_ref[...]
    sin = sin_ref[...]
    half = MLA_ROPE // 2
    first_half = lax.broadcasted_iota(jnp.int32, (1, 4 * MLA_ROPE), 1) % MLA_ROPE < half

    def rope(v):
        swapped = jnp.where(first_half, pltpu.roll(v, 4 * MLA_ROPE - half, 1), pltpu.roll(v, half, 1))
        return v * cos + swapped * sin

    qlat = _dot(_rms(p[:, 0:384], lgq_ref[...]).astype(BF16), wuq_ref[...])
    kvlat = _dot(_rms(p[:, 384:512], lgkv_ref[...]).astype(BF16), wukv_ref[...])
    q_nope = _rms_blocks(qlat[:, 0:256], bd64, grow(GR_MLA_QN))
    q_rot = rope(_rms_blocks(qlat[:, 256:384], bd32, grow(GR_MLA_QR, 2 * hd)))
    k_nope = _rms_blocks(kvlat[:, 0:256], bd64, grow(GR_MLA_KN))
    k_rot = rope(_rms_blocks(p[:, 512:640], bd32, grow(GR_MLA_KR, 2 * hd)))[:, :MLA_ROPE]
    store_heads_t(vd_ref, kvlat[:, 256:512].T, MLA_HEADS)
    for hh in range(MLA_HEADS):
        qd_ref[0, hh] = jnp.concatenate(
            [q_nope[:, hh * hd:(hh + 1) * hd], q_rot[:, hh * MLA_ROPE:(hh + 1) * MLA_ROPE]], axis=1).astype(BF16)
        kd_ref[0, hh] = jnp.concatenate([k_nope[:, hh * hd:(hh + 1) * hd], k_rot], axis=1).astype(BF16)


def _proj_call(x, sc1, sh1, ng, w_attn, gain_rows, bd64, bd32, lgq, lgkv, wuq, wukv, cos, sin):
    B, S, D = x.shape
    tm = MOBA_BLOCK
    nblk = S // tm
    hd = HEAD_DIM

    def full(shape):
        return pl.BlockSpec(shape, lambda b, i: (0,) * len(shape))

    def heads(nh, d):
        return pl.BlockSpec((1, nh, tm, d), lambda b, i: (b, 0, i, 0))

    in_specs = [
        pl.BlockSpec((1, tm, D), lambda b, i: (b, i, 0)),
        pl.BlockSpec((1, 1, D), lambda b, i: (b, 0, 0)),
        pl.BlockSpec((1, 1, D), lambda b, i: (b, 0, 0)),
        full((1, D)),
        full((D, ATTN_COLS)),
        full(gain_rows.shape),
        full(bd64.shape),
        full(bd32.shape),
        full(lgq.shape),
        full(lgkv.shape),
        full(wuq.shape),
        full(wukv.shape),
        pl.BlockSpec((tm, 4 * MLA_ROPE), lambda b, i: (i, 0)),
        pl.BlockSpec((tm, 4 * MLA_ROPE), lambda b, i: (i, 0)),
    ]
    row64 = pl.BlockSpec((1, tm, hd), lambda b, i: (b, i, 0))
    def heads_t(nh, d):
        return pl.BlockSpec((1, nh, d, tm), lambda b, i: (b, 0, 0, i))

    out_specs = [
        heads(4, hd), row64, row64, heads(1, hd), heads_t(1, hd), heads(1, hd), heads_t(1, hd),
        pl.BlockSpec((1, tm, 128), lambda b, i: (b, i, 0)),
        heads(4, hd), heads(2, hd), heads_t(2, hd),
        heads(4, hd), heads(4, hd), heads_t(4, hd), heads(4, nblk),
        heads(4, MLA_NOPE + MLA_ROPE), heads(4, MLA_NOPE + MLA_ROPE), heads_t(4, MLA_V),
    ]

    def sd(shape, dt):
        return jax.ShapeDtypeStruct(shape, dt)

    out_shape = [
        sd((B, 4, S, hd), BF16), sd((B, S, hd), F32), sd((B, S, hd), F32),
        sd((B, 1, S, hd), BF16), sd((B, 1, hd, S), BF16), sd((B, 1, S, hd), BF16), sd((B, 1, hd, S), BF16),
        sd((B, S, 128), F32),
        sd((B, 4, S, hd), BF16), sd((B, 2, S, hd), BF16), sd((B, 2, hd, S), BF16),
        sd((B, 4, S, hd), BF16), sd((B, 4, S, hd), BF16), sd((B, 4, hd, S), BF16), sd((B, 4, S, nblk), F32),
        sd((B, 4, S, 96), BF16), sd((B, 4, S, 96), BF16), sd((B, 4, MLA_V, S), BF16),
    ]
    return pl.pallas_call(
        _proj_kernel,
        grid=(B, nblk),
        in_specs=in_specs,
        out_specs=out_specs,
        out_shape=out_shape,
        scratch_shapes=[pltpu.VMEM((nblk, MOBA_HEADS * hd), F32)],
        compiler_params=pltpu.CompilerParams(
            dimension_semantics=("arbitrary", "arbitrary"), vmem_limit_bytes=VMEM_LIMIT),
        name="proj_prep",
    )(x, sc1, sh1, ng, w_attn, gain_rows, bd64, bd32, lgq, lgkv, wuq, wukv, cos, sin)


def _compress_kernel(gk_ref, gv_ref, pe_ref, w1_ref, w2_ref, gkc_ref, kc_ref, vc_ref):
    half = w1_ref.shape[1] // 2
    outs = []
    for j, g_ref in enumerate((gk_ref, gv_ref)):
        g = g_ref[0].astype(BF16)
        top = _dot(g, w1_ref[j, :half].astype(BF16))
        bot = _dot(g, w1_ref[j, half:].astype(BF16))
        bot = jnp.concatenate([bot[1:], bot[:1]], axis=0)
        pe = jnp.broadcast_to(pe_ref[j], (8, pe_ref.shape[2]))
        bias = _dot3(pe, w1_ref[j])[0:1]
        hid = top + bot + bias
        hid = hid * _sigmoid(hid)
        outs.append(_dot(hid.astype(BF16), w2_ref[j].astype(BF16)))
    kc_ref[0] = _rms(outs[0], gkc_ref[...]).astype(BF16)
    dk = outs[1].shape[1]
    vc_ref[0] = jnp.concatenate([outs[1], outs[1]], axis=1).T[:dk].astype(BF16)


def _compress_call(kc_raw, vc_raw, pe, w1, w2, g_kc):
    B, S, dk = kc_raw.shape
    n_grp = S // CMP_STRIDE
    gk = kc_raw.reshape(B, n_grp, CMP_STRIDE * dk)
    gv = vc_raw.reshape(B, n_grp, CMP_STRIDE * dk)
    pe_flat = pe.reshape(2, 1, CMP_BLOCK * dk)
    grp_spec = pl.BlockSpec((1, n_grp, CMP_STRIDE * dk), lambda b: (b, 0, 0))
    out_spec = pl.BlockSpec((1, n_grp, dk), lambda b: (b, 0, 0))
    return pl.pallas_call(
        _compress_kernel,
        grid=(B,),
        in_specs=[
            grp_spec, grp_spec,
            pl.BlockSpec(pe_flat.shape, lambda b: (0, 0, 0)),
            pl.BlockSpec(w1.shape, lambda b: (0, 0, 0)),
            pl.BlockSpec(w2.shape, lambda b: (0, 0, 0)),
            pl.BlockSpec((1, dk), lambda b: (0, 0)),
        ],
        out_specs=[out_spec, pl.BlockSpec((1, dk, n_grp), lambda b: (b, 0, 0))],
        out_shape=[jax.ShapeDtypeStruct((B, n_grp, dk), BF16), jax.ShapeDtypeStruct((B, dk, n_grp), BF16)],
        compiler_params=pltpu.CompilerParams(vmem_limit_bytes=VMEM_LIMIT),
        name="nsa_compress",
    )(gk, gv, pe_flat, w1, w2, g_kc)


def _cmp_attn_kernel(q_ref, kc_ref, vct_ref, cover_ref, o_ref, sel_ref, *, slopes, tq, n_cmp):
    i = pl.program_id(1)
    kc = kc_ref[0]
    vct = vct_ref[0]
    ncp = kc.shape[0]
    t_full = i * tq + lax.broadcasted_iota(jnp.int32, (ncp, tq), 1)
    n_iota = lax.broadcasted_iota(jnp.int32, (ncp, tq), 0)
    dist_i = t_full - (n_iota * CMP_STRIDE + (CMP_BLOCK - 1))
    vis = (n_iota < n_cmp) & (dist_i >= 0)
    dist = dist_i.astype(F32)
    visf = vis.astype(F32)
    psum = jnp.zeros((ncp, tq), F32)
    outs = []
    for hh in range(NSA_HEADS):
        s = _dot_nt(kc, q_ref[0, hh]) - (slopes[hh] * LOG2E) * dist
        s = jnp.where(vis, s, NEG_INF)
        e = jnp.exp2(s - jnp.max(s, axis=0, keepdims=True)) * visf
        p = e / jnp.maximum(jnp.sum(e, axis=0, keepdims=True), 1e-30)
        outs.append(_dot(vct, p.astype(BF16)))
        psum = psum + p
    o_ref[0] = jnp.concatenate(outs, axis=0).T
    ph, plo = _split(psum)
    cover = cover_ref[...]
    p_slc = _dot(cover, ph) + _dot(cover, plo)
    n_sel = cover.shape[0]
    cur = (i * tq + lax.broadcasted_iota(jnp.int32, (1, tq), 1)) // SEL_BLOCK
    j = lax.broadcasted_iota(jnp.int32, (n_sel, 1), 0)
    forced = jnp.where(j == 0, 1.0, jnp.where(j == cur, 1.0, jnp.where(j == cur - 1, 1.0, 0.0)))
    score = jnp.where(j <= cur, p_slc + FORCE_BONUS * forced, NEG_INF)
    sel_ref[0, 0] = _topk_mask(score, min(SEL_TOPK, n_sel), axis=0)


def _cmp_attn_call(qa, kc, vc, slopes):
    B, H, S, dk = qa.shape
    ncp = kc.shape[1]
    n_cmp = (S - CMP_BLOCK) // CMP_STRIDE + 1
    n_sel = S // SEL_BLOCK
    tq = 256
    starts = np.arange(ncp) * CMP_STRIDE
    jb = np.arange(n_sel) * SEL_BLOCK
    cover = ((starts[:, None] < jb[None, :] + SEL_BLOCK) & (starts[:, None] + CMP_BLOCK > jb[None, :])
             & (np.arange(ncp)[:, None] < n_cmp))
    cover = jnp.asarray(cover.T.astype(np.float32), dtype=BF16)
    return pl.pallas_call(
        functools.partial(_cmp_attn_kernel, slopes=tuple(float(s) for s in slopes), tq=tq, n_cmp=n_cmp),
        grid=(B, S // tq),
        in_specs=[
            pl.BlockSpec((1, H, tq, dk), lambda b, i: (b, 0, i, 0)),
            pl.BlockSpec((1, ncp, dk), lambda b, i: (b, 0, 0)),
            pl.BlockSpec((1, dk, ncp), lambda b, i: (b, 0, 0)),
            pl.BlockSpec((n_sel, ncp), lambda b, i: (0, 0)),
        ],
        out_specs=[
            pl.BlockSpec((1, tq, H * dk), lambda b, i: (b, i, 0)),
            pl.BlockSpec((1, 1, n_sel, tq), lambda b, i: (b, 0, 0, i)),
        ],
        out_shape=[jax.ShapeDtypeStruct((B, S, H * dk), F32),
                   jax.ShapeDtypeStruct((B, 1, n_sel, S), F32)],
        compiler_params=pltpu.CompilerParams(vmem_limit_bytes=VMEM_LIMIT),
        name="nsa_cmp_attn",
    )(qa, kc, vc, cover)


def _flash_kernel(*refs, H, G, hpc, tq, tk, slopes, window, mask_block, mask_per_head, mask_t, has_sink):
    refs = list(refs)
    q_ref, k_ref, v_ref = refs[:3]
    pos = 3
    mask_ref = sink_ref = None
    if mask_block:
        mask_ref = refs[pos]
        pos += 1
    if has_sink:
        sink_ref = refs[pos]
        pos += 1
    o_ref, m_s, l_s, acc_s = refs[pos:pos + 4]
    need_s = refs[pos + 4] if mask_block else None
    use_aug = slopes is not None or bool(mask_block)
    R = H // G
    C = H // hpc
    M = hpc * tq
    dv = v_ref.shape[-2]
    i = pl.program_id(1)
    q0 = i * tq
    hi = q0 // tk
    lo = jnp.maximum(q0 - (window - 1), 0) // tk if window else 0

    def rel_pos():
        r_row = jnp.concatenate([lax.broadcasted_iota(jnp.int32, (1, tq), 1)] * hpc, axis=1)
        return (r_row - lax.broadcasted_iota(jnp.int32, (tk, M), 0)).astype(F32)

    dq = q_ref.shape[-1]
    n_mask = mask_ref.shape[-2 if mask_t else -1] if mask_block else 0
    aug_w = LANE if dq + AUG_MASK0 + n_mask <= LANE else 2 * LANE
    aug0 = dq
    lane_a = lax.broadcasted_iota(jnp.int32, (1, aug_w), 1)
    feature_lane = lane_a < dq

    def slope_lanes(hh):
        row = jnp.zeros((1, aug_w), F32)
        if slopes is not None:
            for n, piece in enumerate(_bf16_pieces(slopes[hh])):
                row = jnp.where((lane_a == aug0 + n) | (lane_a == aug0 + n + 3), piece, row)
        return row

    def widen(a):
        return jnp.concatenate([a, jnp.zeros((a.shape[0], aug_w - dq), a.dtype)], axis=1)

    qs, slope_rows, bms = [], [], []
    for c in range(C):
        heads = [c * hpc + r for r in range(hpc)]
        if hpc == 1:
            qs.append(q_ref[0, heads[0]])
        else:
            qs.append(jnp.concatenate([q_ref[0, hh] for hh in heads], axis=0))
        if slopes is not None:
            slope_rows.append(jnp.concatenate([jnp.full((1, tq), slopes[hh], F32) for hh in heads], axis=1))
        if mask_block:
            parts = [mask_ref[0, hh if mask_per_head else 0] for hh in heads]
            bms.append(jnp.concatenate(parts, axis=1) if mask_t else jnp.concatenate(parts, axis=0).T)
    m_s[...] = jnp.full(m_s.shape, NEG_INF, F32)
    l_s[...] = jnp.zeros(l_s.shape, F32)
    acc_s[...] = jnp.zeros(acc_s.shape, F32)

    if mask_block:
        nblk = bms[0].shape[0]
        col = bms[0]
        for bm in bms[1:]:
            col = jnp.maximum(col, bm)
        col = jnp.max(col, axis=1, keepdims=True)
        bpt = tk // mask_block
        for jt in range(nblk // bpt):
            need_s[jt] = jnp.max(col[jt * bpt:(jt + 1) * bpt]).astype(jnp.int32)

    if use_aug:
        for c in range(C):
            qa = jnp.concatenate(
                [jnp.broadcast_to(slope_lanes(c * hpc + r), (tq, aug_w)) for r in range(hpc)], axis=0)
            if mask_block:
                place = jnp.where(
                    lax.broadcasted_iota(jnp.int32, (nblk, aug_w), 0) + (aug0 + AUG_MASK0) == lane_a,
                    1.0, 0.0).astype(BF16)
                off_sel = ((bms[c] - 1.0) * -NEG_INF).astype(BF16)
                qa = qa + lax.dot_general(off_sel, place, (((0,), (0,)), ((), ())), preferred_element_type=F32)
            qs[c] = jnp.where(feature_lane, widen(qs[c]), qa.astype(BF16))

    def tile(j, edge):
        k0 = pl.multiple_of(j * tk, tk)
        valid = None
        if edge:
            dist = rel_pos() + (q0 - k0).astype(F32)
            valid = dist >= 0.0
            if window:
                valid = valid & (dist < float(window))
        if use_aug:
            t_key = lax.broadcasted_iota(jnp.int32, (tk, aug_w), 0) + k0
            k_aug = jnp.zeros((tk, aug_w), F32)
            if mask_block:
                key_blk = lax.shift_right_logical(t_key, int(math.log2(mask_block)))
                k_aug = jnp.where(key_blk + (aug0 + AUG_MASK0) == lane_a, 1.0, 0.0)
            if slopes is not None:
                t_hi = (lax.shift_right_logical(t_key, 8) * 256).astype(F32)
                t_lo = (t_key & 255).astype(F32)
                k_aug = jnp.where(lane_a < aug0 + 3, t_hi, jnp.where(lane_a < aug0 + 6, t_lo, k_aug))
            k_aug = k_aug.astype(BF16)
        k_cats = {}
        scores = []
        for c in range(C):
            g = (c * hpc) // R
            if g not in k_cats:
                k = k_ref[0, g, pl.ds(k0, tk), :]
                k_cats[g] = jnp.where(feature_lane, widen(k), k_aug) if use_aug else k
            scores.append(_dot_nt(k_cats[g], qs[c]))
        probs, alphas = [], []
        for g in range(C):
            s = scores[g]
            if valid is not None:
                s = jnp.where(valid, s, NEG_INF)
            m_prev = m_s[g]
            m_new = jnp.maximum(m_prev, jnp.max(s, axis=0, keepdims=True))
            alpha = jnp.exp2(m_prev - m_new)
            p = jnp.exp2(s - m_new)
            l_s[g] = alpha * l_s[g] + jnp.sum(p, axis=0, keepdims=True)
            m_s[g] = m_new
            probs.append(p.astype(BF16))
            alphas.append(alpha)
        for c in range(C):
            v = v_ref[0, (c * hpc) // R, :, pl.ds(k0, tk)]
            acc_s[c] = alphas[c] * acc_s[c] + _dot(v, probs[c])

    def body(step, carry):
        j = hi - step
        is_edge = step == 0
        if window:
            is_edge = is_edge | (q0 - j * tk + (tq - 1) >= window)
        run = (step == 0) | (need_s[j] > 0) if mask_block else None

        def when(c):
            return pl.when(c if run is None else c & run)

        @when(is_edge)
        def _():
            tile(j, True)

        @when(jnp.logical_not(is_edge))
        def _():
            tile(j, False)

        return carry

    lax.fori_loop(0, hi - lo + 1, body, 0)

    for c in range(C):
        m = m_s[c]
        l = l_s[c]
        acc = acc_s[c]
        if has_sink:
            sk = LOG2E * jnp.concatenate(
                [jnp.broadcast_to(sink_ref[:, c * hpc + r:c * hpc + r + 1], (1, tq)) for r in range(hpc)], axis=1)
            if slopes is not None:
                t_q = jnp.concatenate([lax.broadcasted_iota(jnp.int32, (1, tq), 1) + q0] * hpc, axis=1)
                m = m - slope_rows[c] * t_q.astype(F32)
            m_f = jnp.maximum(m, sk)
            a = jnp.exp2(m - m_f)
            l = l * a + jnp.exp2(sk - m_f)
            acc = acc * a
        out = acc / l
        for r in range(hpc):
            hh = c * hpc + r
            o_ref[0, :, hh * dv:(hh + 1) * dv] = out[:, r * tq:(r + 1) * tq].T.astype(o_ref.dtype)


def _flash_call(q, k, v, *, slopes=None, window=0, mask=None, mask_block=0, mask_t=False, sinks=None,
                tq=128, tk=256, hpc=None, out_dtype=BF16, name="flash"):
    B, H, S, dq = q.shape
    G = k.shape[1]
    dv = v.shape[-2]
    tk = min(tk, S)
    assert tk % tq == 0 and S % tk == 0
    R = H // G
    hpc = R if hpc is None else hpc
    assert R % hpc == 0
    C = H // hpc
    in_specs = [
        pl.BlockSpec((1, H, tq, dq), lambda b, i: (b, 0, i, 0)),
        pl.BlockSpec((1, G, S, dq), lambda b, i: (b, 0, 0, 0)),
        pl.BlockSpec((1, G, dv, S), lambda b, i: (b, 0, 0, 0)),
    ]
    args = [q, k, v]
    mask_per_head = False
    if mask is not None:
        hm = mask.shape[1]
        mask_per_head = hm > 1
        if mask_t:
            in_specs.append(pl.BlockSpec((1, hm, mask.shape[2], tq), lambda b, i: (b, 0, 0, i)))
        else:
            in_specs.append(pl.BlockSpec((1, hm, tq, mask.shape[3]), lambda b, i: (b, 0, i, 0)))
        args.append(mask)
    if sinks is not None:
        in_specs.append(pl.BlockSpec(sinks.shape, lambda b, i: (0, 0)))
        args.append(sinks)
    kern = functools.partial(
        _flash_kernel, H=H, G=G, hpc=hpc, tq=tq, tk=tk,
        slopes=None if slopes is None else tuple(float(s) * LOG2E for s in slopes),
        window=window, mask_block=mask_block if mask is not None else 0,
        mask_per_head=mask_per_head, mask_t=mask_t, has_sink=sinks is not None)
    scratch =[pltpu.VMEM((C, 1, hpc * tq), F32), pltpu.VMEM((C, 1, hpc * tq), F32),
               pltpu.VMEM((C, dv, hpc * tq), F32)]
    if mask is not None:
        assert tk % mask_block == 0
        scratch.append(pltpu.SMEM((S // tk,), jnp.int32))
    return pl.pallas_call(
        kern,
        grid=(B, S // tq),
        in_specs=in_specs,
        out_specs=pl.BlockSpec((1, tq, H * dv), lambda b, i: (b, i, 0)),
        out_shape=jax.ShapeDtypeStruct((B, S, H * dv), out_dtype),
        scratch_shapes=scratch,
        compiler_params=pltpu.CompilerParams(vmem_limit_bytes=VMEM_LIMIT),
        name=name,
    )(*args)


def _merge_kernel(x_ref, sc1_ref, sh1_ref, g1_ref, sc2_ref, sh2_ref, ng_ref, wg_ref, wb_ref, wo_ref,
                  ocmp_ref, oslc_ref, owin_ref, gs_ref, ob_ref, oc_ref, od_ref, wr_ref, br_ref,
                  xo_ref, h2_ref, rt_ref):
    hd = HEAD_DIM
    x = x_ref[0]
    ng = ng_ref[...]
    h = _rms(x, ng[0:1]) * (1.0 + sc1_ref[0]) + sh1_ref[0]
    hb = h.astype(BF16)
    gs = gs_ref[0]
    ocmp = ocmp_ref[0]
    oslc = oslc_ref[0]
    owin = owin_ref[0]
    parts = []
    for hh in range(NSA_HEADS):
        c0 = GATE_LANE0 + 3 * hh
        sl = slice(hh * hd, (hh + 1) * hd)
        parts.append(gs[:, c0:c0 + 1] * ocmp[:, sl] + gs[:, c0 + 1:c0 + 2] * oslc[:, sl]
                     + gs[:, c0 + 2:c0 + 3] * owin[:, sl])
    o_a = jnp.concatenate(parts, axis=1).astype(BF16)
    branches = (o_a, ob_ref[0], oc_ref[0], od_ref[0])
    D = x.shape[1]
    mixed = None
    for n in range(N_BRANCH):
        gate = _sigmoid(_dot(hb, wg_ref[:, n * D:(n + 1) * D]))
        term = gate * _dot(branches[n], wb_ref[n])
        mixed = term if mixed is None else mixed + term
    xn = x + g1_ref[0] * _dot(mixed.astype(BF16), wo_ref[...])
    xo_ref[0] = xn
    h2 = _rms(xn, ng[1:2]) * (1.0 + sc2_ref[0]) + sh2_ref[0]
    for c in range(SUB):
        h2_ref[pl.ds(c, h2.shape[0], stride=SUB), :] = h2[:, c * LANE:(c + 1) * LANE]

    logits = _dot3(h2, wr_ref[...]) + br_ref[...]
    lane = lax.broadcasted_iota(jnp.int32, logits.shape, 1)
    lanef = lane.astype(F32)
    is_c = lane < N_GROUPS
    lc = jnp.where(is_c, logits, NEG_INF)
    mc = jnp.max(lc, axis=-1, keepdims=True)
    grp = jnp.min(jnp.where(lc == mc, lanef, 1e9), axis=-1, keepdims=True)
    p_grp = 1.0 / jnp.sum(jnp.where(is_c, jnp.exp(lc - mc), 0.0), axis=-1, keepdims=True)
    e_lane = lanef - float(N_GROUPS)
    in_grp = (lane >= N_GROUPS) & (lane < N_GROUPS + N_EXPERTS) & (
        jnp.floor(e_lane / EXPERTS_PER_GROUP) == grp)
    lf = jnp.where(in_grp, logits, NEG_INF)
    m1 = jnp.max(lf, axis=-1, keepdims=True)
    i1 = jnp.min(jnp.where(lf == m1, e_lane, 1e9), axis=-1, keepdims=True)
    lf2 = jnp.where(e_lane == i1, NEG_INF, lf)
    m2 = jnp.max(lf2, axis=-1, keepdims=True)
    i2 = jnp.min(jnp.where(lf2 == m2, e_lane, 1e9), axis=-1, keepdims=True)
    e2 = jnp.exp(m2 - m1)
    w1 = p_grp / (1.0 + e2)
    w2 = p_grp * e2 / (1.0 + e2)
    rt = jnp.where(lane == 0, i1, jnp.where(lane == 1, i2, jnp.where(lane == 2, w1, jnp.where(lane == 3, w2, 0.0))))
    rt_ref[0] = rt


def _merge_call(x, mods, ng, wg, wb, wo, ocmp, oslc, owin, gs, ob, oc, od, wr, br):
    B, S, D = x.shape
    tm = 256
    sc1, sh1, g1, sc2, sh2 = mods

    def full(a):
        return pl.BlockSpec(a.shape, lambda b, i: (0,) * a.ndim)

    modspec = pl.BlockSpec((1, 1, D), lambda b, i: (b, 0, 0))
    row = lambda w: pl.BlockSpec((1, tm, w), lambda b, i: (b, i, 0))
    in_specs = [row(D), modspec, modspec, modspec, modspec, modspec, full(ng), full(wg), full(wb), full(wo),
                row(256), row(256), row(256), row(128), row(256), row(256), row(256), full(wr), full(br)]
    return pl.pallas_call(
        _merge_kernel,
        grid=(B, S // tm),
        in_specs=in_specs,
        out_specs=[row(D), pl.BlockSpec((tm * SUB, LANE), lambda b, i: (b * (S // tm) + i, 0)), row(128)],
        out_shape=[jax.ShapeDtypeStruct((B, S, D), F32), jax.ShapeDtypeStruct((B * S * SUB, LANE), F32),
                   jax.ShapeDtypeStruct((B, S, 128), F32)],
        compiler_params=pltpu.CompilerParams(vmem_limit_bytes=VMEM_LIMIT),
        name="merge_router",
    )(x, sc1, sh1, g1, sc2, sh2, ng, wg, wb, wo, ocmp, oslc, owin, gs, ob, oc, od, wr, br)


def _expert_kernel(blk_e_ref, nreal_ref, tok_ref, dst_ref, h2_hbm, w13_ref, w2_ref, y2_hbm, xbuf, ybuf, gsem, ssem):
    del blk_e_ref
    i = pl.program_id(0)
    nb = pl.num_programs(0)
    slot = i % 2
    rows = EXPERT_ROWS
    grp = 8

    def tile_rows(t):
        return pl.ds(pl.multiple_of(t * SUB, SUB), SUB)

    def gather_copy(blk, s, r):
        tok = tok_ref[blk * rows + r]
        return pltpu.make_async_copy(h2_hbm.at[tile_rows(tok), :], xbuf.at[s, tile_rows(r), :], gsem.at[s])

    def scatter_copy(blk, s, r):
        d = dst_ref[blk * rows + r]
        return pltpu.make_async_copy(ybuf.at[s, tile_rows(r), :], y2_hbm.at[tile_rows(d), :], ssem.at[s])

    def for_real_rows(blk, per_row, per_group):
        n = nreal_ref[blk]
        nfull = n // grp

        def body(g, carry):
            per_group(g)
            return carry
        lax.fori_loop(0, nfull, body, 0)
        for u in range(grp - 1):
            r = nfull * grp + u

            @pl.when(r < n)
            def _():
                per_row(r)

    def start_gather(blk, s):
        def group(g):
            for u in range(grp):
                gather_copy(blk, s, g * grp + u).start()
        for_real_rows(blk, lambda r: gather_copy(blk, s, r).start(), group)

    def start_scatter(blk, s):
        def group(g):
            for u in range(grp):
                scatter_copy(blk, s, g * grp + u).start()
        for_real_rows(blk, lambda r: scatter_copy(blk, s, r).start(), group)

    def wait_gather(blk, s):
        def tiles(k):
            return pltpu.make_async_copy(h2_hbm.at[pl.ds(0, k * SUB), :], xbuf.at[s, pl.ds(0, k * SUB), :], gsem.at[s])
        for_real_rows(blk, lambda r: tiles(1).wait(), lambda g: tiles(grp).wait())

    def wait_scatter(blk, s):
        def tiles(k):
            return pltpu.make_async_copy(ybuf.at[s, pl.ds(0, k * SUB), :], y2_hbm.at[pl.ds(0, k * SUB), :], ssem.at[s])
        for_real_rows(blk, lambda r: tiles(1).wait(), lambda g: tiles(grp).wait())

    @pl.when(i == 0)
    def _():
        xbuf[...] = jnp.zeros(xbuf.shape, F32)
        start_gather(0, 0)

    @pl.when(i + 1 < nb)
    def _():
        start_gather(i + 1, 1 - slot)

    wait_gather(i, slot)

    @pl.when(i >= 2)
    def _():
        wait_scatter(i - 2, slot)

    ab = None
    for c in range(0, SUB, 2):
        xc = jnp.concatenate([xbuf[slot, pl.ds(c, rows, stride=SUB), :],
                              xbuf[slot, pl.ds(c + 1, rows, stride=SUB), :]], axis=1).astype(BF16)
        part = _dot(xc, w13_ref[0, 0, c * LANE:(c + 2) * LANE, :].astype(BF16))
        ab = part if ab is None else ab + part
    de = ab.shape[1] // 2
    a = ab[:, :de]
    b = ab[:, de:]
    act = ((a * _sigmoid(a)) * b).astype(BF16)
    for c in range(0, SUB, 2):
        y = _dot(act, w2_ref[0, 0, :, c * LANE:(c + 2) * LANE].astype(BF16))
        ybuf[slot, pl.ds(c, rows, stride=SUB), :] = y[:, :LANE]
        ybuf[slot, pl.ds(c + 1, rows, stride=SUB), :] = y[:, LANE:]
    start_scatter(i, slot)

    @pl.when(i == nb - 1)
    def _():
        @pl.when(nb >= 2)
        def _():
            wait_scatter(i - 1, 1 - slot)
        wait_scatter(i, slot)


def _expert_call(h2, blk_e, n_real, row_tok, row_dst, w13, w2, layer):
    D = SUB * LANE
    assert h2.shape[1] == LANE and w13.shape[2] == D
    n_blocks = blk_e.shape[0]
    de2 = w13.shape[-1]
    grid_spec = pltpu.PrefetchScalarGridSpec(
        num_scalar_prefetch=4,
        grid=(n_blocks,),
        in_specs=[
            pl.BlockSpec(memory_space=pl.ANY),
            pl.BlockSpec((1, 1, D, de2), lambda i, be, nr, rt, rd: (layer, be[i], 0, 0)),
            pl.BlockSpec((1, 1, de2 // 2, D), lambda i, be, nr, rt, rd: (layer, be[i], 0, 0)),
        ],
        out_specs=pl.BlockSpec(memory_space=pl.ANY),
        scratch_shapes=[pltpu.VMEM((2, EXPERT_ROWS * SUB, LANE), F32), pltpu.VMEM((2, EXPERT_ROWS * SUB, LANE), F32),
                        pltpu.SemaphoreType.DMA((2,)), pltpu.SemaphoreType.DMA((2,))],
    )
    return pl.pallas_call(
        _expert_kernel,
        grid_spec=grid_spec,
        out_shape=jax.ShapeDtypeStruct((2 * h2.shape[0], LANE), F32),
        compiler_params=pltpu.CompilerParams(
            dimension_semantics=("arbitrary",), vmem_limit_bytes=VMEM_LIMIT),
        name="experts",
    )(blk_e, n_real, row_tok, row_dst, h2, w13, w2)


def _combine_kernel(xn_ref, g2_ref, rt_ref, y0_ref, y1_ref, o_ref):
    rt = rt_ref[...]
    tm = xn_ref.shape[0]
    w0 = rt[:, 2:3]
    w1 = rt[:, 3:4]
    g2 = g2_ref[0]
    for c in range(SUB):
        cols = slice(c * LANE, (c + 1) * LANE)
        y = w0 * y0_ref[pl.ds(c, tm, stride=SUB), :] + w1 * y1_ref[pl.ds(c, tm, stride=SUB), :]
        o_ref[:, cols] = xn_ref[:, cols] + g2[:, cols] * y


def _combine_call(xn, g2, route, y2):
    B, S, D = xn.shape
    tm = 512
    spb = S // tm
    nt = B * spb
    return pl.pallas_call(
        _combine_kernel,
        grid=(B, spb),
        in_specs=[
            pl.BlockSpec((tm, D), lambda b, i: (b * spb + i, 0)),
            pl.BlockSpec((1, 1, D), lambda b, i: (b, 0, 0)),
            pl.BlockSpec((tm, 128), lambda b, i: (b * spb + i, 0)),
            pl.BlockSpec((tm * SUB, LANE), lambda b, i: (b * spb + i, 0)),
            pl.BlockSpec((tm * SUB, LANE), lambda b, i: (nt + b * spb + i, 0)),
        ],
        out_specs=pl.BlockSpec((tm, D), lambda b, i: (b * spb + i, 0)),
        out_shape=jax.ShapeDtypeStruct((B * S, D), F32),
        compiler_params=pltpu.CompilerParams(vmem_limit_bytes=VMEM_LIMIT),
        name="moe_combine",
    )(xn.reshape(B * S, D), g2, route, y2, y2).reshape(B, S, D)


def _moe(h2, route, w13, w2, layer):
    N = h2.shape[0] // SUB
    K = 2
    E = N_EXPERTS
    rows = EXPERT_ROWS
    flat_e = route[:, 0:2].astype(jnp.int32).reshape(-1)
    order = jnp.argsort(flat_e).astype(jnp.int32)
    counts = jnp.sum((flat_e[:, None] == jnp.arange(E)[None, :]).astype(jnp.int32), axis=0)
    padded = (counts + rows - 1) // rows * rows
    pend = jnp.cumsum(padded)
    pstart = pend - padded
    start = jnp.cumsum(counts) - counts
    n_blocks = (N * K) // rows + E
    R = n_blocks * rows
    blk_e = jnp.minimum(
        jnp.sum((jnp.arange(n_blocks)[:, None] * rows >= pend[None, :]).astype(jnp.int32), axis=1), E - 1)
    pos = jnp.arange(R, dtype=jnp.int32)
    e_pos = jnp.repeat(blk_e, rows)
    local = pos - pstart[e_pos]
    is_real = (pos < pend[E - 1]) & (local < counts[e_pos])
    rank = jnp.clip(start[e_pos] + local, 0, N * K - 1)
    pair = order[rank]
    row_tok = jnp.where(is_real, pair // K, 0).astype(jnp.int32)
    row_dst = jnp.where(is_real, (pair % K) * N + pair // K, 0).astype(jnp.int32)
    n_real = jnp.sum(is_real.reshape(n_blocks, rows).astype(jnp.int32), axis=1)
    return _expert_call(h2, blk_e.astype(jnp.int32), n_real, row_tok, row_dst, w13, w2, layer)


def kernel(x, c, w_ada, b_ada, norm_gain, w_in, qk_gain, cmp_pe, cmp_w1, cmp_w2, swa_sinks,
           lat_gain_q, lat_gain_kv, rope_gain, w_uq, w_ukv, w_branch, w_out,
           w_coarse, b_coarse, w_fine, b_fine, w13, w2):
    B, S, D = x.shape
    L = w_in.shape[0]
    assert S % MOBA_BLOCK == 0 and D == 1024
    slopes = _alibi_slopes()

    half = MLA_ROPE // 2
    inv = ROPE_THETA ** (-jnp.arange(half, dtype=F32) / half)
    ang = jnp.arange(S).astype(F32)[:, None] * inv[None, :]
    cos = jnp.tile(jnp.cos(ang), (1, 8))
    sin = jnp.tile(jnp.concatenate([-jnp.sin(ang), jnp.sin(ang)], axis=1), (1, 4))

    mod = _ada_mod(c, w_ada, b_ada)

    w_attn = jnp.concatenate(
        [w_in[:, :, :448], w_in[:, :, 512:576], w_in[:, :, 448:512], w_in[:, :, 576:640],
         w_in[:, :, 652:ATTN_OLD], w_in[:, :, 640:652],
         jnp.zeros((L, D, ATTN_COLS - ATTN_OLD), F32)], axis=2).astype(BF16)
    w_gate = w_in[:, :, ATTN_OLD:].astype(BF16)
    dq, dkv = MLA_NOPE + MLA_ROPE, MLA_NOPE + MLA_V
    uq = w_uq.reshape(L, MLA_Q_RANK, MLA_HEADS, dq)
    w_uq_b = jnp.concatenate([uq[..., :MLA_NOPE].reshape(L, MLA_Q_RANK, -1),
                              uq[..., MLA_NOPE:].reshape(L, MLA_Q_RANK, -1)], axis=2).astype(BF16)
    ukv = w_ukv.reshape(L, MLA_KV_RANK, MLA_HEADS, dkv)
    w_ukv_b = jnp.concatenate([ukv[..., :MLA_NOPE].reshape(L, MLA_KV_RANK, -1),
                               ukv[..., MLA_NOPE:].reshape(L, MLA_KV_RANK, -1)], axis=2).astype(BF16)

    qs = HEAD_DIM ** -0.5 * LOG2E
    ms = dq ** -0.5 * LOG2E

    def lanes(v, reps, scale=1.0):
        r = jnp.tile(v, (1, reps)) * scale
        return jnp.pad(r, ((0, 0), (0, 4 * HEAD_DIM - r.shape[1])))

    gq = qk_gain
    gain_rows = jnp.stack([
        lanes(gq[:, QK_NSA_Q], 4, qs),
        lanes(jnp.concatenate([gq[:, QK_NSA_KS], gq[:, QK_NSA_KW]], axis=1), 1),
        lanes(gq[:, QK_SWA_Q], 4, qs),
        lanes(gq[:, QK_SWA_K], 2),
        lanes(gq[:, QK_MOBA_Q], 4, qs),
        lanes(gq[:, QK_MOBA_K], 4),
        lanes(gq[:, QK_MLA_Q], 4, ms),
        lanes(rope_gain[:, 0], 4, ms),
        lanes(gq[:, QK_MLA_K], 4),
        lanes(rope_gain[:, 1], 4),
    ], axis=1)
    bd64 = jnp.asarray(np.kron(np.eye(4), np.ones((HEAD_DIM, HEAD_DIM))) / HEAD_DIM, dtype=BF16)
    bd32 = jnp.asarray(np.kron(np.eye(4), np.ones((MLA_ROPE, MLA_ROPE))) / MLA_ROPE, dtype=BF16)
    w_branch_b = w_branch.astype(BF16)
    w_out_b = w_out.astype(BF16)
    w_router = jnp.concatenate(
        [w_coarse, w_fine, jnp.zeros((L, D, 128 - N_GROUPS - N_EXPERTS), F32)], axis=2)
    b_router = jnp.concatenate(
        [b_coarse, b_fine, jnp.zeros((L, 128 - N_GROUPS - N_EXPERTS), F32)], axis=1)
    sinks_pad = jnp.concatenate([swa_sinks, jnp.zeros((L, 128 - SWA_HEADS), F32)], axis=1)

    for l in range(L):
        m6 = mod[l].reshape(B, 6, 1, D)
        sh1, sc1, g1, sh2, sc2, g2 = (m6[:, j] for j in range(6))
        (qa, kcr, vcr, ks, vs, kw, vw, gs, qb, kb, vb, qc, kc, vc, mc, qd, kd, vd) = _proj_call(
            x, sc1, sh1, norm_gain[l, 0:1], w_attn[l], gain_rows[l], bd64, bd32, lat_gain_q[l][None],
            lat_gain_kv[l][None], w_uq_b[l], w_ukv_b[l], cos, sin)
        kcmp, vcmp = _compress_call(kcr, vcr, cmp_pe[l], cmp_w1[l], cmp_w2[l], qk_gain[l, QK_NSA_KC][None])
        o_cmp, sel = _cmp_attn_call(qa, kcmp, vcmp, slopes[0])
        o_slc = _flash_call(qa, ks, vs, slopes=slopes[0], mask=sel, mask_block=SEL_BLOCK, mask_t=True,
                            tq=256, tk=256, hpc=2, out_dtype=F32, name="nsa_slc")
        o_win = _flash_call(qa, kw, vw, slopes=slopes[0], window=NSA_WINDOW,
                            tq=256, tk=256, hpc=2, out_dtype=F32, name="nsa_win")
        o_b = _flash_call(qb, kb, vb, slopes=slopes[1], window=SWA_WINDOW, sinks=sinks_pad[l][None],
                          tq=256, tk=256, name="swa")
        o_c = _flash_call(qc, kc, vc, slopes=slopes[2], mask=mc, mask_block=MOBA_BLOCK,
                          tq=256, tk=512, name="moba")
        o_d = _flash_call(qd, kd, vd, tq=256, tk=512, name="mla")
        xn, h2, route = _merge_call(
            x, (sc1, sh1, g1, sc2, sh2), norm_gain[l], w_gate[l], w_branch_b[l], w_out_b[l],
            o_cmp, o_slc, o_win, gs, o_b, o_c, o_d, w_router[l], b_router[l][None])
        route = route.reshape(B * S, 128)
        y2 = _moe(h2, route, w13, w2, l)
        x = _combine_call(xn, g2, route, y2)
    return x
```

```python
import functools
import math

import numpy as np
import jax
import jax.numpy as jnp
from jax import lax
from jax.experimental import pallas as pl
from jax.experimental.pallas import tpu as pltpu

F32 = jnp.float32
BF16 = jnp.bfloat16

HEAD_DIM = 64
NEG_INF = -1e30
EPS = 1e-6
NSA_HEADS = 4
CMP_BLOCK = 32
CMP_STRIDE = 16
CMP_HIDDEN = 256
SEL_BLOCK = 64
SEL_TOPK = 8
NSA_WINDOW = 512
FORCE_BONUS = 1e4
SWA_HEADS = 4
SWA_KV_HEADS = 2
SWA_WINDOW = 128
MOBA_HEADS = 4
MOBA_BLOCK = 256
MOBA_TOPK = 3
MLA_HEADS = 4
MLA_Q_RANK = 384
MLA_KV_RANK = 128
MLA_NOPE = 64
MLA_ROPE = 32
MLA_V = 64
ROPE_THETA = 10000.0
N_BRANCH = 4
BRANCH_WIDTH = 256
N_GROUPS = 4
EXPERTS_PER_GROUP = 8
N_EXPERTS = N_GROUPS * EXPERTS_PER_GROUP
D_EXPERT = 256

(GR_NSA_Q, GR_NSA_K, GR_SWA_Q, GR_SWA_K, GR_MOBA_Q, GR_MOBA_K,
 GR_MLA_QN, GR_MLA_QR, GR_MLA_KN, GR_MLA_KR) = range(10)
QK_NSA_Q, QK_NSA_KC, QK_NSA_KS, QK_NSA_KW = 0, 1, 2, 3
QK_SWA_Q, QK_SWA_K, QK_MOBA_Q, QK_MOBA_K, QK_MLA_Q, QK_MLA_K = 4, 5, 6, 7, 8, 9

ATTN_OLD = 2476
ATTN_COLS = 2560
GATE_LANE0 = 32
LOG2E = math.log2(math.e)
SUB, LANE = 8, 128
EXPERT_ROWS = 256
VMEM_LIMIT = 56 * 1024 * 1024


def _alibi_slopes():
    n = NSA_HEADS + SWA_HEADS + MOBA_HEADS

    def pow2(m):
        start = 2.0 ** (-8.0 / m)
        return [start ** (i + 1) for i in range(m)]

    c = 2 ** int(math.floor(math.log2(n)))
    s = pow2(c) + (pow2(2 * c)[0::2][: n - c] if c < n else [])
    s = -np.sort(-np.asarray(s, np.float32))
    return s.reshape(NSA_HEADS, 3).T


def _dot(a, b):
    return jnp.dot(a, b, preferred_element_type=F32)


def _dot_nt(a, b):
    return lax.dot_general(a, b, (((1,), (1,)), ((), ())), preferred_element_type=F32)


def _split(a):
    hi = a.astype(BF16)
    lo = (a - hi.astype(F32)).astype(BF16)
    return hi, lo


def _dot3(a, b):
    ah, al = _split(a)
    bh, bl = _split(b)
    return _dot(ah, bh) + (_dot(ah, bl) + _dot(al, bh))


def _dot3_nt(a, b):
    ah, al = _split(a)
    bh, bl = _split(b)
    return _dot_nt(ah, bh) + (_dot_nt(ah, bl) + _dot_nt(al, bh))


def _rms(x, g):
    return x * lax.rsqrt(jnp.mean(x * x, axis=-1, keepdims=True) + EPS) * g


def _sigmoid(x):
    return 1.0 / (1.0 + jnp.exp(-x))


AUG_MASK0 = 6


def _bf16_pieces(x, n=3):
    out, r = [], float(x)
    for _ in range(n):
        piece = float(np.asarray(r, np.float32).astype(jnp.bfloat16).astype(np.float32))
        out.append(piece)
        r -= piece
    return out


def _topk_mask(score, k, axis=1):
    n = score.shape[axis]
    iota = lax.broadcasted_iota(jnp.int32, score.shape, axis).astype(F32)
    sel = jnp.zeros(score.shape, F32)
    for _ in range(k):
        m = jnp.max(score, axis=axis, keepdims=True)
        idx = jnp.min(jnp.where(score == m, iota, float(n)), axis=axis, keepdims=True)
        hit = iota == idx
        sel = jnp.where(hit, jnp.where(m > 0.5 * NEG_INF, 1.0, 0.0), sel)
        score = jnp.where(hit, -3e38, score)
    return sel


def _ada_kernel(c_ref, w_ref, b_ref, o_ref):
    c = c_ref[...]
    a = c * _sigmoid(c)
    o_ref[0] = _dot(a, w_ref[0]) + b_ref[0]


def _ada_mod(c, w_ada, b_ada):
    L, D, D6 = w_ada.shape
    B = c.shape[0]
    tn = 1024
    return pl.pallas_call(
        _ada_kernel,
        grid=(L, D6 // tn),
        in_specs=[
            pl.BlockSpec((B, D), lambda l, j: (0, 0)),
            pl.BlockSpec((1, D, tn), lambda l, j: (l, 0, j)),
            pl.BlockSpec((1, 1, tn), lambda l, j: (l, 0, j)),
        ],
        out_specs=pl.BlockSpec((1, B, tn), lambda l, j: (l, 0, j)),
        out_shape=jax.ShapeDtypeStruct((L, B, D6), F32),
        compiler_params=pltpu.CompilerParams(vmem_limit_bytes=VMEM_LIMIT),
        name="ada_mod",
    )(c, w_ada, b_ada.reshape(L, 1, D6))


def _rms_blocks(x, bd, g):
    hi, lo = _split(x * x)
    ms = _dot(hi, bd) + _dot(lo, bd)
    return x * lax.rsqrt(ms + EPS) * g


def _proj_kernel(x_ref, sc_ref, sh_ref, ng_ref, w_ref, gr_ref, bd64_ref, bd32_ref, lgq_ref, lgkv_ref,
                 wuq_ref, wukv_ref, cos_ref, sin_ref,
                 qa_ref, kcr_ref, vcr_ref, ks_ref, vs_ref, kw_ref, vw_ref, gs_ref,
                 qb_ref, kb_ref, vb_ref, qc_ref, kc_ref, vc_ref, mc_ref, qd_ref, kd_ref, vd_ref,
                 kmean_s):
    i = pl.program_id(1)
    hd = HEAD_DIM
    x = x_ref[0]
    h = _rms(x, ng_ref[...]) * (1.0 + sc_ref[0]) + sh_ref[0]
    hb = h.astype(BF16)
    bd64 = bd64_ref[...]
    bd64h = bd64_ref[0:2 * hd, 0:2 * hd]
    bd32 = bd32_ref[...]

    def grow(r, w=4 * hd):
        return gr_ref[r:r + 1, 0:w]

    def proj(a, b):
        return _dot(hb, w_ref[:, a:b])

    def store_heads(ref, slab, n):
        for hh in range(n):
            ref[0, hh] = slab[:, hh * hd:(hh + 1) * hd].astype(BF16)

    def store_heads_t(ref, slab_t, n):
        for hh in range(n):
            ref[0, hh] = slab_t[hh * hd:(hh + 1) * hd].astype(BF16)

    store_heads(qa_ref, _rms_blocks(proj(0, 256), bd64, grow(GR_NSA_Q)), NSA_HEADS)
    p = proj(256, 640)
    kcr_ref[0] = p[:, 0:64]
    vcr_ref[0] = p[:, 64:128]
    kk = _rms_blocks(p[:, 128:256], bd64h, grow(GR_NSA_K, 2 * hd))
    ks_ref[0, 0] = kk[:, :hd].astype(BF16)
    kw_ref[0, 0] = kk[:, hd:].astype(BF16)
    vt = p[:, 256:384].T
    vs_ref[0, 0] = vt[:hd].astype(BF16)
    vw_ref[0, 0] = vt[hd:].astype(BF16)

    p = proj(640, 1152)
    store_heads(qb_ref, _rms_blocks(p[:, 0:256], bd64, grow(GR_SWA_Q)), SWA_HEADS)
    store_heads(kb_ref, _rms_blocks(p[:, 256:384], bd64h, grow(GR_SWA_K, 2 * hd)), SWA_KV_HEADS)
    store_heads_t(vb_ref, p[:, 384:512].T, SWA_KV_HEADS)

    @pl.when(i == 0)
    def _():
        kmean_s[...] = jnp.zeros(kmean_s.shape, F32)

    p = proj(1152, 1920)
    qn = _rms_blocks(p[:, 0:256], bd64, grow(GR_MOBA_Q))
    kn = _rms_blocks(p[:, 256:512], bd64, grow(GR_MOBA_K))
    store_heads(qc_ref, qn, MOBA_HEADS)
    store_heads(kc_ref, kn, MOBA_HEADS)
    store_heads_t(vc_ref, p[:, 512:768].T, MOBA_HEADS)
    nblk = kmean_s.shape[0]
    blk_iota = lax.broadcasted_iota(jnp.int32, (1, nblk), 1)
    head_of_lane = lax.broadcasted_iota(jnp.int32, (1, 4 * hd), 1) // hd
    kmeans = kmean_s[...]
    for hh in range(MOBA_HEADS):
        g = _dot3_nt(jnp.where(head_of_lane == hh, qn, 0.0), kmeans)
        g = jnp.where(blk_iota < i, g, NEG_INF)
        mc_ref[0, hh] = jnp.where(blk_iota == i, 1.0, _topk_mask(g, MOBA_TOPK))
    kmean_s[pl.ds(i, 1), :] = jnp.mean(kn, axis=0, keepdims=True)

    p = proj(1920, 2560)
    gs_ref[0] = _sigmoid(p[:, 512:640])
    cos = cos_ref[...]
    sin = sin_ref[...]
    half = MLA_ROPE // 2
    first_half = lax.broadcasted_iota(jnp.int32, (1, 4 * MLA_ROPE), 1) % MLA_ROPE < half

    def rope(v):
        swapped = jnp.where(first_half, pltpu.roll(v, 4 * MLA_ROPE - half, 1), pltpu.roll(v, half, 1))
        return v * cos + swapped * sin

    qlat = _dot(_rms(p[:, 0:384], lgq_ref[...]).astype(BF16), wuq_ref[...])
    kvlat = _dot(_rms(p[:, 384:512], lgkv_ref[...]).astype(BF16), wukv_ref[...])
    q_nope = _rms_blocks(qlat[:, 0:256], bd64, grow(GR_MLA_QN))
    q_rot = rope(_rms_blocks(qlat[:, 256:384], bd32, grow(GR_MLA_QR, 2 * hd)))
    k_nope = _rms_blocks(kvlat[:, 0:256], bd64, grow(GR_MLA_KN))
    k_rot = rope(_rms_blocks(p[:, 512:640], bd32, grow(GR_MLA_KR, 2 * hd)))[:, :MLA_ROPE]
    store_heads_t(vd_ref, kvlat[:, 256:512].T, MLA_HEADS)
    for hh in range(MLA_HEADS):
        qd_ref[0, hh] = jnp.concatenate(
            [q_nope[:, hh * hd:(hh + 1) * hd], q_rot[:, hh * MLA_ROPE:(hh + 1) * MLA_ROPE]], axis=1).astype(BF16)
        kd_ref[0, hh] = jnp.concatenate([k_nope[:, hh * hd:(hh + 1) * hd], k_rot], axis=1).astype(BF16)


def _proj_call(x, sc1, sh1, ng, w_attn, gain_rows, bd64, bd32, lgq, lgkv, wuq, wukv, cos, sin):
    B, S, D = x.shape
    tm = MOBA_BLOCK
    nblk = S // tm
    hd = HEAD_DIM

    def full(shape):
        return pl.BlockSpec(shape, lambda b, i: (0,) * len(shape))

    def heads(nh, d):
        return pl.BlockSpec((1, nh, tm, d), lambda b, i: (b, 0, i, 0))

    in_specs = [
        pl.BlockSpec((1, tm, D), lambda b, i: (b, i, 0)),
        pl.BlockSpec((1, 1, D), lambda b, i: (b, 0, 0)),
        pl.BlockSpec((1, 1, D), lambda b, i: (b, 0, 0)),
        full((1, D)),
        full((D, ATTN_COLS)),
        full(gain_rows.shape),
        full(bd64.shape),
        full(bd32.shape),
        full(lgq.shape),
        full(lgkv.shape),
        full(wuq.shape),
        full(wukv.shape),
        pl.BlockSpec((tm, 4 * MLA_ROPE), lambda b, i: (i, 0)),
        pl.BlockSpec((tm, 4 * MLA_ROPE), lambda b, i: (i, 0)),
    ]
    row64 = pl.BlockSpec((1, tm, hd), lambda b, i: (b, i, 0))
    def heads_t(nh, d):
        return pl.BlockSpec((1, nh, d, tm), lambda b, i: (b, 0, 0, i))

    out_specs = [
        heads(4, hd), row64, row64, heads(1, hd), heads_t(1, hd), heads(1, hd), heads_t(1, hd),
        pl.BlockSpec((1, tm, 128), lambda b, i: (b, i, 0)),
        heads(4, hd), heads(2, hd), heads_t(2, hd),
        heads(4, hd), heads(4, hd), heads_t(4, hd), heads(4, nblk),
        heads(4, MLA_NOPE + MLA_ROPE), heads(4, MLA_NOPE + MLA_ROPE), heads_t(4, MLA_V),
    ]

    def sd(shape, dt):
        return jax.ShapeDtypeStruct(shape, dt)

    out_shape = [
        sd((B, 4, S, hd), BF16), sd((B, S, hd), F32), sd((B, S, hd), F32),
        sd((B, 1, S, hd), BF16), sd((B, 1, hd, S), BF16), sd((B, 1, S, hd), BF16), sd((B, 1, hd, S), BF16),
        sd((B, S, 128), F32),
        sd((B, 4, S, hd), BF16), sd((B, 2, S, hd), BF16), sd((B, 2, hd, S), BF16),
        sd((B, 4, S, hd), BF16), sd((B, 4, S, hd), BF16), sd((B, 4, hd, S), BF16), sd((B, 4, S, nblk), F32),
        sd((B, 4, S, 96), BF16), sd((B, 4, S, 96), BF16), sd((B, 4, MLA_V, S), BF16),
    ]
    return pl.pallas_call(
        _proj_kernel,
        grid=(B, nblk),
        in_specs=in_specs,
        out_specs=out_specs,
        out_shape=out_shape,
        scratch_shapes=[pltpu.VMEM((nblk, MOBA_HEADS * hd), F32)],
        compiler_params=pltpu.CompilerParams(
            dimension_semantics=("arbitrary", "arbitrary"), vmem_limit_bytes=VMEM_LIMIT),
        name="proj_prep",
    )(x, sc1, sh1, ng, w_attn, gain_rows, bd64, bd32, lgq, lgkv, wuq, wukv, cos, sin)


def _compress_kernel(gk_ref, gv_ref, pe_ref, w1_ref, w2_ref, gkc_ref, kc_ref, vc_ref):
    half = w1_ref.shape[1] // 2
    outs = []
    for j, g_ref in enumerate((gk_ref, gv_ref)):
        g = g_ref[0].astype(BF16)
        top = _dot(g, w1_ref[j, :half].astype(BF16))
        bot = _dot(g, w1_ref[j, half:].astype(BF16))
        bot = jnp.concatenate([bot[1:], bot[:1]], axis=0)
        pe = jnp.broadcast_to(pe_ref[j], (8, pe_ref.shape[2]))
        bias = _dot3(pe, w1_ref[j])[0:1]
        hid = top + bot + bias
        hid = hid * _sigmoid(hid)
        outs.append(_dot(hid.astype(BF16), w2_ref[j].astype(BF16)))
    kc_ref[0] = _rms(outs[0], gkc_ref[...]).astype(BF16)
    dk = outs[1].shape[1]
    vc_ref[0] = jnp.concatenate([outs[1], outs[1]], axis=1).T[:dk].astype(BF16)


def _compress_call(kc_raw, vc_raw, pe, w1, w2, g_kc):
    B, S, dk = kc_raw.shape
    n_grp = S // CMP_STRIDE
    gk = kc_raw.reshape(B, n_grp, CMP_STRIDE * dk)
    gv = vc_raw.reshape(B, n_grp, CMP_STRIDE * dk)
    pe_flat = pe.reshape(2, 1, CMP_BLOCK * dk)
    grp_spec = pl.BlockSpec((1, n_grp, CMP_STRIDE * dk), lambda b: (b, 0, 0))
    out_spec = pl.BlockSpec((1, n_grp, dk), lambda b: (b, 0, 0))
    return pl.pallas_call(
        _compress_kernel,
        grid=(B,),
        in_specs=[
            grp_spec, grp_spec,
            pl.BlockSpec(pe_flat.shape, lambda b: (0, 0, 0)),
            pl.BlockSpec(w1.shape, lambda b: (0, 0, 0)),
            pl.BlockSpec(w2.shape, lambda b: (0, 0, 0)),
            pl.BlockSpec((1, dk), lambda b: (0, 0)),
        ],
        out_specs=[out_spec, pl.BlockSpec((1, dk, n_grp), lambda b: (b, 0, 0))],
        out_shape=[jax.ShapeDtypeStruct((B, n_grp, dk), BF16), jax.ShapeDtypeStruct((B, dk, n_grp), BF16)],
        compiler_params=pltpu.CompilerParams(vmem_limit_bytes=VMEM_LIMIT),
        name="nsa_compress",
    )(gk, gv, pe_flat, w1, w2, g_kc)


def _cmp_attn_kernel(q_ref, kc_ref, vct_ref, cover_ref, o_ref, sel_ref, *, slopes, tq, n_cmp):
    i = pl.program_id(1)
    kc = kc_ref[0]
    vct = vct_ref[0]
    ncp = kc.shape[0]
    t_full = i * tq + lax.broadcasted_iota(jnp.int32, (ncp, tq), 1)
    n_iota = lax.broadcasted_iota(jnp.int32, (ncp, tq), 0)
    dist_i = t_full - (n_iota * CMP_STRIDE + (CMP_BLOCK - 1))
    vis = (n_iota < n_cmp) & (dist_i >= 0)
    dist = dist_i.astype(F32)
    visf = vis.astype(F32)
    psum = jnp.zeros((ncp, tq), F32)
    outs = []
    for hh in range(NSA_HEADS):
        s = _dot_nt(kc, q_ref[0, hh]) - (slopes[hh] * LOG2E) * dist
        s = jnp.where(vis, s, NEG_INF)
        e = jnp.exp2(s - jnp.max(s, axis=0, keepdims=True)) * visf
        p = e / jnp.maximum(jnp.sum(e, axis=0, keepdims=True), 1e-30)
        outs.append(_dot(vct, p.astype(BF16)))
        psum = psum + p
    o_ref[0] = jnp.concatenate(outs, axis=0).T
    ph, plo = _split(psum)
    cover = cover_ref[...]
    p_slc = _dot(cover, ph) + _dot(cover, plo)
    n_sel = cover.shape[0]
    cur = (i * tq + lax.broadcasted_iota(jnp.int32, (1, tq), 1)) // SEL_BLOCK
    j = lax.broadcasted_iota(jnp.int32, (n_sel, 1), 0)
    forced = jnp.where(j == 0, 1.0, jnp.where(j == cur, 1.0, jnp.where(j == cur - 1, 1.0, 0.0)))
    score = jnp.where(j <= cur, p_slc + FORCE_BONUS * forced, NEG_INF)
    sel_ref[0, 0] = _topk_mask(score, min(SEL_TOPK, n_sel), axis=0)


def _cmp_attn_call(qa, kc, vc, slopes):
    B, H, S, dk = qa.shape
    ncp = kc.shape[1]
    n_cmp = (S - CMP_BLOCK) // CMP_STRIDE + 1
    n_sel = S // SEL_BLOCK
    tq = 256
    starts = np.arange(ncp) * CMP_STRIDE
    jb = np.arange(n_sel) * SEL_BLOCK
    cover = ((starts[:, None] < jb[None, :] + SEL_BLOCK) & (starts[:, None] + CMP_BLOCK > jb[None, :])
             & (np.arange(ncp)[:, None] < n_cmp))
    cover = jnp.asarray(cover.T.astype(np.float32), dtype=BF16)
    return pl.pallas_call(
        functools.partial(_cmp_attn_kernel, slopes=tuple(float(s) for s in slopes), tq=tq, n_cmp=n_cmp),
        grid=(B, S // tq),
        in_specs=[
            pl.BlockSpec((1, H, tq, dk), lambda b, i: (b, 0, i, 0)),
            pl.BlockSpec((1, ncp, dk), lambda b, i: (b, 0, 0)),
            pl.BlockSpec((1, dk, ncp), lambda b, i: (b, 0, 0)),
            pl.BlockSpec((n_sel, ncp), lambda b, i: (0, 0)),
        ],
        out_specs=[
            pl.BlockSpec((1, tq, H * dk), lambda b, i: (b, i, 0)),
            pl.BlockSpec((1, 1, n_sel, tq), lambda b, i: (b, 0, 0, i)),
        ],
        out_shape=[jax.ShapeDtypeStruct((B, S, H * dk), F32),
                   jax.ShapeDtypeStruct((B, 1, n_sel, S), F32)],
        compiler_params=pltpu.CompilerParams(vmem_limit_bytes=VMEM_LIMIT),
        name="nsa_cmp_attn",
    )(qa, kc, vc, cover)


def _flash_kernel(*refs, H, G, hpc, tq, tk, slopes, window, mask_block, mask_per_head, mask_t, has_sink, skip):
    refs = list(refs)
    q_ref, k_ref, v_ref = refs[:3]
    pos = 3
    mask_ref = sink_ref = None
    if mask_block:
        mask_ref = refs[pos]
        pos += 1
    if has_sink:
        sink_ref = refs[pos]
        pos += 1
    o_ref, m_s, acc_s = refs[pos:pos + 3]
    need_s = refs[pos + 3] if skip else None
    use_aug = slopes is not None or bool(mask_block)
    R = H // G
    C = H // hpc
    M = hpc * tq
    dv = v_ref.shape[-2]
    i = pl.program_id(1)
    q0 = i * tq
    hi = q0 // tk
    lo = jnp.maximum(q0 - (window - 1), 0) // tk if window else 0

    def rel_pos():
        r_row = jnp.concatenate([lax.broadcasted_iota(jnp.int32, (1, tq), 1)] * hpc, axis=1)
        return (r_row - lax.broadcasted_iota(jnp.int32, (tk, M), 0)).astype(F32)

    dq = q_ref.shape[-1]
    n_mask = mask_ref.shape[-2 if mask_t else -1] if mask_block else 0
    aug_w = LANE if dq + AUG_MASK0 + n_mask <= LANE else 2 * LANE
    aug0 = dq
    lane_a = lax.broadcasted_iota(jnp.int32, (1, aug_w), 1)
    feature_lane = lane_a < dq

    def slope_lanes(hh):
        row = jnp.zeros((1, aug_w), F32)
        if slopes is not None:
            for n, piece in enumerate(_bf16_pieces(slopes[hh])):
                row = jnp.where((lane_a == aug0 + n) | (lane_a == aug0 + n + 3), piece, row)
        return row

    def widen(a):
        return jnp.concatenate([a, jnp.zeros((a.shape[0], aug_w - dq), a.dtype)], axis=1)

    qs, slope_rows, bms = [], [], []
    for c in range(C):
        heads = [c * hpc + r for r in range(hpc)]
        if hpc == 1:
            qs.append(q_ref[0, heads[0]])
        else:
            qs.append(jnp.concatenate([q_ref[0, hh] for hh in heads], axis=0))
        if slopes is not None:
            slope_rows.append(jnp.concatenate([jnp.full((1, tq), slopes[hh], F32) for hh in heads], axis=1))
        if mask_block:
            parts = [mask_ref[0, hh if mask_per_head else 0] for hh in heads]
            bms.append(jnp.concatenate(parts, axis=1) if mask_t else jnp.concatenate(parts, axis=0).T)
    m_s[...] = jnp.full(m_s.shape, NEG_INF, F32)
    acc_s[...] = jnp.zeros(acc_s.shape, F32)
    ones_rows = jnp.ones((acc_s.shape[1] - dv, tk), BF16)

    if mask_block:
        nblk = bms[0].shape[0]
    if need_s is not None:
        col = bms[0]
        for bm in bms[1:]:
            col = jnp.maximum(col, bm)
        col = jnp.max(col, axis=1, keepdims=True)
        bpt = tk // mask_block
        for jt in range(nblk // bpt):
            need_s[jt] = jnp.max(col[jt * bpt:(jt + 1) * bpt]).astype(jnp.int32)

    if use_aug:
        for c in range(C):
            qa = jnp.concatenate(
                [jnp.broadcast_to(slope_lanes(c * hpc + r), (tq, aug_w)) for r in range(hpc)], axis=0)
            if mask_block:
                place = jnp.where(
                    lax.broadcasted_iota(jnp.int32, (nblk, aug_w), 0) + (aug0 + AUG_MASK0) == lane_a,
                    1.0, 0.0).astype(BF16)
                off_sel = ((bms[c] - 1.0) * -NEG_INF).astype(BF16)
                qa = qa + lax.dot_general(off_sel, place, (((0,), (0,)), ((), ())), preferred_element_type=F32)
            qs[c] = jnp.where(feature_lane, widen(qs[c]), qa.astype(BF16))

    def tile_scores(j):
        k0 = pl.multiple_of(j * tk, tk)
        if use_aug:
            t_key = lax.broadcasted_iota(jnp.int32, (tk, aug_w), 0) + k0
            k_aug = jnp.zeros((tk, aug_w), F32)
            if mask_block:
                key_blk = lax.shift_right_logical(t_key, int(math.log2(mask_block)))
                k_aug = jnp.where(key_blk + (aug0 + AUG_MASK0) == lane_a, 1.0, 0.0)
            if slopes is not None:
                t_hi = (lax.shift_right_logical(t_key, 8) * 256).astype(F32)
                t_lo = (t_key & 255).astype(F32)
                k_aug = jnp.where(lane_a < aug0 + 3, t_hi, jnp.where(lane_a < aug0 + 6, t_lo, k_aug))
            k_aug = k_aug.astype(BF16)
        k_cats = {}
        scores = []
        for c in range(C):
            g = (c * hpc) // R
            if g not in k_cats:
                k = k_ref[0, g, pl.ds(k0, tk), :]
                k_cats[g] = jnp.where(feature_lane, widen(k), k_aug) if use_aug else k
            scores.append(_dot_nt(k_cats[g], qs[c]))
        return scores

    def tile_update(j, edge, scores):
        k0 = pl.multiple_of(j * tk, tk)
        valid = None
        if edge:
            dist = rel_pos() + (q0 - k0).astype(F32)
            valid = dist >= 0.0
            if window:
                valid = valid & (dist < float(window))
        probs, alphas = [], []
        for g in range(C):
            s = scores[g]
            if valid is not None:
                s = jnp.where(valid, s, NEG_INF)
            m_prev = m_s[g]
            m_new = jnp.maximum(m_prev, jnp.max(s, axis=0, keepdims=True))
            alpha = jnp.exp2(m_prev - m_new)
            p = jnp.exp2(s - m_new)
            m_s[g] = m_new
            probs.append(p.astype(BF16))
            alphas.append(alpha)
        for c in range(C):
            v = jnp.concatenate([v_ref[0, (c * hpc) // R, :, pl.ds(k0, tk)], ones_rows], axis=0)
            acc_s[c] = alphas[c] * acc_s[c] + _dot(v, probs[c])

    def tile(j, edge):
        tile_update(j, edge, tile_scores(j))

    def body(step, carry):
        j = hi - step
        is_edge = step == 0
        if window:
            is_edge = is_edge | (q0 - j * tk + (tq - 1) >= window)
        run = (step == 0) | (need_s[j] > 0) if need_s is not None else None

        def when(c):
            return pl.when(c if run is None else c & run)

        @when(is_edge)
        def _():
            tile(j, True)

        @when(jnp.logical_not(is_edge))
        def _():
            tile(j, False)

        return carry

    lax.fori_loop(0, hi - lo + 1, body, 0)

    for c in range(C):
        m = m_s[c]
        l = acc_s[c, dv:dv + 1]
        acc = acc_s[c, 0:dv]
        if has_sink:
            sk = LOG2E * jnp.concatenate(
                [jnp.broadcast_to(sink_ref[:, c * hpc + r:c * hpc + r + 1], (1, tq)) for r in range(hpc)], axis=1)
            if slopes is not None:
                t_q = jnp.concatenate([lax.broadcasted_iota(jnp.int32, (1, tq), 1) + q0] * hpc, axis=1)
                m = m - slope_rows[c] * t_q.astype(F32)
            m_f = jnp.maximum(m, sk)
            a = jnp.exp2(m - m_f)
            l = l * a + jnp.exp2(sk - m_f)
            acc = acc * a
        out = acc / l
        for r in range(hpc):
            hh = c * hpc + r
            o_ref[0, :, hh * dv:(hh + 1) * dv] = out[:, r * tq:(r + 1) * tq].T.astype(o_ref.dtype)


def _flash_call(q, k, v, *, slopes=None, window=0, mask=None, mask_block=0, mask_t=False, sinks=None,
                tq=128, tk=256, hpc=None, skip_tiles=False, out_dtype=BF16, name="flash"):
    B, H, S, dq = q.shape
    G = k.shape[1]
    dv = v.shape[-2]
    tk = min(tk, S)
    assert tk % tq == 0 and S % tk == 0
    R = H // G
    hpc = R if hpc is None else hpc
    assert R % hpc == 0
    C = H // hpc
    in_specs = [
        pl.BlockSpec((1, H, tq, dq), lambda b, i: (b, 0, i, 0)),
        pl.BlockSpec((1, G, S, dq), lambda b, i: (b, 0, 0, 0)),
        pl.BlockSpec((1, G, dv, S), lambda b, i: (b, 0, 0, 0)),
    ]
    args = [q, k, v]
    mask_per_head = False
    if mask is not None:
        hm = mask.shape[1]
        mask_per_head = hm > 1
        if mask_t:
            in_specs.append(pl.BlockSpec((1, hm, mask.shape[2], tq), lambda b, i: (b, 0, 0, i)))
        else:
            in_specs.append(pl.BlockSpec((1, hm, tq, mask.shape[3]), lambda b, i: (b, 0, i, 0)))
        args.append(mask)
    if sinks is not None:
        in_specs.append(pl.BlockSpec(sinks.shape, lambda b, i: (0, 0)))
        args.append(sinks)
    kern = functools.partial(
        _flash_kernel, H=H, G=G, hpc=hpc, tq=tq, tk=tk,
        slopes=None if slopes is None else tuple(float(s) * LOG2E for s in slopes),
        window=window, mask_block=mask_block if mask is not None else 0,
        mask_per_head=mask_per_head, mask_t=mask_t, has_sink=sinks is not None,
        skip=skip_tiles)
    ones_rows = 16
    scratch = [pltpu.VMEM((C, 1, hpc * tq), F32), pltpu.VMEM((C, dv + ones_rows, hpc * tq), F32)]
    if mask is not None:
        assert tk % mask_block == 0
    if skip_tiles:
        assert mask is not None
        scratch.append(pltpu.SMEM((S // tk,), jnp.int32))
    return pl.pallas_call(
        kern,
        grid=(B, S // tq),
        in_specs=in_specs,
        out_specs=pl.BlockSpec((1, tq, H * dv), lambda b, i: (b, i, 0)),
        out_shape=jax.ShapeDtypeStruct((B, S, H * dv), out_dtype),
        scratch_shapes=scratch,
        compiler_params=pltpu.CompilerParams(vmem_limit_bytes=VMEM_LIMIT),
        name=name,
    )(*args)


def _merge_kernel(x_ref, sc1_ref, sh1_ref, g1_ref, sc2_ref, sh2_ref, ng_ref, wg_ref, wb_ref, wo_ref,
                  ocmp_ref, oslc_ref, owin_ref, gs_ref, ob_ref, oc_ref, od_ref, wr_ref, br_ref,
                  xo_ref, h2_ref, rt_ref):
    hd = HEAD_DIM
    x = x_ref[0]
    ng = ng_ref[...]
    h = _rms(x, ng[0:1]) * (1.0 + sc1_ref[0]) + sh1_ref[0]
    hb = h.astype(BF16)
    gs = gs_ref[0]
    ocmp = ocmp_ref[0]
    oslc = oslc_ref[0]
    owin = owin_ref[0]
    parts = []
    for hh in range(NSA_HEADS):
        c0 = GATE_LANE0 + 3 * hh
        sl = slice(hh * hd, (hh + 1) * hd)
        parts.append(gs[:, c0:c0 + 1] * ocmp[:, sl] + gs[:, c0 + 1:c0 + 2] * oslc[:, sl]
                     + gs[:, c0 + 2:c0 + 3] * owin[:, sl])
    o_a = jnp.concatenate(parts, axis=1).astype(BF16)
    branches = (o_a, ob_ref[0], oc_ref[0], od_ref[0])
    D = x.shape[1]
    mixed = None
    for n in range(N_BRANCH):
        gate = _sigmoid(_dot(hb, wg_ref[:, n * D:(n + 1) * D]))
        term = gate * _dot(branches[n], wb_ref[n])
        mixed = term if mixed is None else mixed + term
    xn = x + g1_ref[0] * _dot(mixed.astype(BF16), wo_ref[...])
    xo_ref[0] = xn
    h2 = _rms(xn, ng[1:2]) * (1.0 + sc2_ref[0]) + sh2_ref[0]
    for c in range(SUB):
        h2_ref[pl.ds(c, h2.shape[0], stride=SUB), :] = h2[:, c * LANE:(c + 1) * LANE]

    logits = _dot3(h2, wr_ref[...]) + br_ref[...]
    lane = lax.broadcasted_iota(jnp.int32, logits.shape, 1)
    lanef = lane.astype(F32)
    is_c = lane < N_GROUPS
    lc = jnp.where(is_c, logits, NEG_INF)
    mc = jnp.max(lc, axis=-1, keepdims=True)
    grp = jnp.min(jnp.where(lc == mc, lanef, 1e9), axis=-1, keepdims=True)
    p_grp = 1.0 / jnp.sum(jnp.where(is_c, jnp.exp(lc - mc), 0.0), axis=-1, keepdims=True)
    e_lane = lanef - float(N_GROUPS)
    in_grp = (lane >= N_GROUPS) & (lane < N_GROUPS + N_EXPERTS) & (
        jnp.floor(e_lane / EXPERTS_PER_GROUP) == grp)
    lf = jnp.where(in_grp, logits, NEG_INF)
    m1 = jnp.max(lf, axis=-1, keepdims=True)
    i1 = jnp.min(jnp.where(lf == m1, e_lane, 1e9), axis=-1, keepdims=True)
    lf2 = jnp.where(e_lane == i1, NEG_INF, lf)
    m2 = jnp.max(lf2, axis=-1, keepdims=True)
    i2 = jnp.min(jnp.where(lf2 == m2, e_lane, 1e9), axis=-1, keepdims=True)
    e2 = jnp.exp(m2 - m1)
    w1 = p_grp / (1.0 + e2)
    w2 = p_grp * e2 / (1.0 + e2)
    rt = jnp.where(lane == 0, i1, jnp.where(lane == 1, i2, jnp.where(lane == 2, w1, jnp.where(lane == 3, w2, 0.0))))
    rt_ref[0] = rt


def _merge_call(x, mods, ng, wg, wb, wo, ocmp, oslc, owin, gs, ob, oc, od, wr, br):
    B, S, D = x.shape
    tm = 256
    sc1, sh1, g1, sc2, sh2 = mods

    def full(a):
        return pl.BlockSpec(a.shape, lambda b, i: (0,) * a.ndim)

    modspec = pl.BlockSpec((1, 1, D), lambda b, i: (b, 0, 0))
    row = lambda w: pl.BlockSpec((1, tm, w), lambda b, i: (b, i, 0))
    in_specs = [row(D), modspec, modspec, modspec, modspec, modspec, full(ng), full(wg), full(wb), full(wo),
                row(256), row(256), row(256), row(128), row(256), row(256), row(256), full(wr), full(br)]
    return pl.pallas_call(
        _merge_kernel,
        grid=(B, S // tm),
        in_specs=in_specs,
        out_specs=[row(D), pl.BlockSpec((tm * SUB, LANE), lambda b, i: (b * (S // tm) + i, 0)), row(128)],
        out_shape=[jax.ShapeDtypeStruct((B, S, D), F32), jax.ShapeDtypeStruct((B * S * SUB, LANE), F32),
                   jax.ShapeDtypeStruct((B, S, 128), F32)],
        compiler_params=pltpu.CompilerParams(vmem_limit_bytes=VMEM_LIMIT),
        name="merge_router",
    )(x, sc1, sh1, g1, sc2, sh2, ng, wg, wb, wo, ocmp, oslc, owin, gs, ob, oc, od, wr, br)


def _expert_kernel(blk_e_ref, nreal_ref, tok_ref, dst_ref, h2_hbm, w13_ref, w2_ref, y2_hbm, xbuf, ybuf, gsem, ssem):
    del blk_e_ref
    i = pl.program_id(0)
    nb = pl.num_programs(0)
    slot = i % 2
    rows = EXPERT_ROWS
    grp = 8

    def tile_rows(t):
        return pl.ds(pl.multiple_of(t * SUB, SUB), SUB)

    def gather_copy(blk, s, r):
        tok = tok_ref[blk * rows + r]
        return pltpu.make_async_copy(h2_hbm.at[tile_rows(tok), :], xbuf.at[s, tile_rows(r), :], gsem.at[s])

    def scatter_copy(blk, s, r):
        d = dst_ref[blk * rows + r]
        return pltpu.make_async_copy(ybuf.at[s, tile_rows(r), :], y2_hbm.at[tile_rows(d), :], ssem.at[s])

    def for_real_rows(blk, per_row, per_group):
        n = nreal_ref[blk]
        nfull = n // grp

        def body(g, carry):
            per_group(g)
            return carry
        lax.fori_loop(0, nfull, body, 0)
        for u in range(grp - 1):
            r = nfull * grp + u

            @pl.when(r < n)
            def _():
                per_row(r)

    def start_gather(blk, s):
        def group(g):
            for u in range(grp):
                gather_copy(blk, s, g * grp + u).start()
        for_real_rows(blk, lambda r: gather_copy(blk, s, r).start(), group)

    def start_scatter(blk, s):
        def group(g):
            for u in range(grp):
                scatter_copy(blk, s, g * grp + u).start(priority=1)
        for_real_rows(blk, lambda r: scatter_copy(blk, s, r).start(priority=1), group)

    def wait_gather(blk, s):
        def tiles(k):
            return pltpu.make_async_copy(h2_hbm.at[pl.ds(0, k * SUB), :], xbuf.at[s, pl.ds(0, k * SUB), :], gsem.at[s])
        for_real_rows(blk, lambda r: tiles(1).wait(), lambda g: tiles(grp).wait())

    def wait_scatter(blk, s):
        def tiles(k):
            return pltpu.make_async_copy(ybuf.at[s, pl.ds(0, k * SUB), :], y2_hbm.at[pl.ds(0, k * SUB), :], ssem.at[s])
        for_real_rows(blk, lambda r: tiles(1).wait(), lambda g: tiles(grp).wait())

    @pl.when(i == 0)
    def _():
        xbuf[...] = jnp.zeros(xbuf.shape, F32)
        start_gather(0, 0)

    @pl.when(i + 1 < nb)
    def _():
        start_gather(i + 1, 1 - slot)

    wait_gather(i, slot)

    @pl.when(i >= 2)
    def _():
        wait_scatter(i - 2, slot)

    ab = None
    for c in range(0, SUB, 2):
        xc = jnp.concatenate([xbuf[slot, pl.ds(c, rows, stride=SUB), :],
                              xbuf[slot, pl.ds(c + 1, rows, stride=SUB), :]], axis=1).astype(BF16)
        part = _dot(xc, w13_ref[0, 0, c * LANE:(c + 2) * LANE, :].astype(BF16))
        ab = part if ab is None else ab + part
    de = ab.shape[1] // 2
    a = ab[:, :de]
    b = ab[:, de:]
    act = ((a * _sigmoid(a)) * b).astype(BF16)
    for c in range(0, SUB, 2):
        y = _dot(act, w2_ref[0, 0, :, c * LANE:(c + 2) * LANE].astype(BF16))
        ybuf[slot, pl.ds(c, rows, stride=SUB), :] = y[:, :LANE]
        ybuf[slot, pl.ds(c + 1, rows, stride=SUB), :] = y[:, LANE:]
    start_scatter(i, slot)

    @pl.when(i == nb - 1)
    def _():
        @pl.when(nb >= 2)
        def _():
            wait_scatter(i - 1, 1 - slot)
        wait_scatter(i, slot)


def _expert_call(h2, blk_e, n_real, row_tok, row_dst, w13, w2, layer):
    D = SUB * LANE
    assert h2.shape[1] == LANE and w13.shape[2] == D
    n_blocks = blk_e.shape[0]
    de2 = w13.shape[-1]
    grid_spec = pltpu.PrefetchScalarGridSpec(
        num_scalar_prefetch=4,
        grid=(n_blocks,),
        in_specs=[
            pl.BlockSpec(memory_space=pl.ANY),
            pl.BlockSpec((1, 1, D, de2), lambda i, be, nr, rt, rd: (layer, be[i], 0, 0)),
            pl.BlockSpec((1, 1, de2 // 2, D), lambda i, be, nr, rt, rd: (layer, be[i], 0, 0)),
        ],
        out_specs=pl.BlockSpec(memory_space=pl.ANY),
        scratch_shapes=[pltpu.VMEM((2, EXPERT_ROWS * SUB, LANE), F32), pltpu.VMEM((2, EXPERT_ROWS * SUB, LANE), F32),
                        pltpu.SemaphoreType.DMA((2,)), pltpu.SemaphoreType.DMA((2,))],
    )
    return pl.pallas_call(
        _expert_kernel,
        grid_spec=grid_spec,
        out_shape=jax.ShapeDtypeStruct((2 * h2.shape[0], LANE), F32),
        compiler_params=pltpu.CompilerParams(
            dimension_semantics=("arbitrary",), vmem_limit_bytes=VMEM_LIMIT),
        name="experts",
    )(blk_e, n_real, row_tok, row_dst, h2, w13, w2)


def _combine_kernel(xn_ref, g2_ref, rt_ref, y0_ref, y1_ref, o_ref):
    rt = rt_ref[...]
    tm = xn_ref.shape[0]
    w0 = rt[:, 2:3]
    w1 = rt[:, 3:4]
    g2 = g2_ref[0]
    for c in range(SUB):
        cols = slice(c * LANE, (c + 1) * LANE)
        y = w0 * y0_ref[pl.ds(c, tm, stride=SUB), :] + w1 * y1_ref[pl.ds(c, tm, stride=SUB), :]
        o_ref[:, cols] = xn_ref[:, cols] + g2[:, cols] * y


def _combine_call(xn, g2, route, y2):
    B, S, D = xn.shape
    tm = 512
    spb = S // tm
    nt = B * spb
    return pl.pallas_call(
        _combine_kernel,
        grid=(B, spb),
        in_specs=[
            pl.BlockSpec((tm, D), lambda b, i: (b * spb + i, 0)),
            pl.BlockSpec((1, 1, D), lambda b, i: (b, 0, 0)),
            pl.BlockSpec((tm, 128), lambda b, i: (b * spb + i, 0)),
            pl.BlockSpec((tm * SUB, LANE), lambda b, i: (b * spb + i, 0)),
            pl.BlockSpec((tm * SUB, LANE), lambda b, i: (nt + b * spb + i, 0)),
        ],
        out_specs=pl.BlockSpec((tm, D), lambda b, i: (b * spb + i, 0)),
        out_shape=jax.ShapeDtypeStruct((B * S, D), F32),
        compiler_params=pltpu.CompilerParams(vmem_limit_bytes=VMEM_LIMIT),
        name="moe_combine",
    )(xn.reshape(B * S, D), g2, route, y2, y2).reshape(B, S, D)


def _moe(h2, route, w13, w2, layer):
    N = h2.shape[0] // SUB
    K = 2
    E = N_EXPERTS
    rows = EXPERT_ROWS
    flat_e = route[:, 0:2].astype(jnp.int32).reshape(-1)
    order = jnp.argsort(flat_e).astype(jnp.int32)
    counts = jnp.sum((flat_e[:, None] == jnp.arange(E)[None, :]).astype(jnp.int32), axis=0)
    padded = (counts + rows - 1) // rows * rows
    pend = jnp.cumsum(padded)
    pstart = pend - padded
    start = jnp.cumsum(counts) - counts
    n_blocks = (N * K) // rows + E
    R = n_blocks * rows
    blk_e = jnp.minimum(
        jnp.sum((jnp.arange(n_blocks)[:, None] * rows >= pend[None, :]).astype(jnp.int32), axis=1), E - 1)
    pos = jnp.arange(R, dtype=jnp.int32).reshape(n_blocks, rows)
    local = pos - pstart[blk_e][:, None]
    is_real = ((pos < pend[E - 1]) & (local < counts[blk_e][:, None])).reshape(R)
    rank = jnp.clip(start[blk_e][:, None] + local, 0, N * K - 1).reshape(R)
    pair = order[rank]
    row_tok = jnp.where(is_real, pair // K, 0).astype(jnp.int32)
    row_dst = jnp.where(is_real, (pair % K) * N + pair // K, 0).astype(jnp.int32)
    n_real = jnp.sum(is_real.reshape(n_blocks, rows).astype(jnp.int32), axis=1)
    return _expert_call(h2, blk_e.astype(jnp.int32), n_real, row_tok, row_dst, w13, w2, layer)


def kernel(x, c, w_ada, b_ada, norm_gain, w_in, qk_gain, cmp_pe, cmp_w1, cmp_w2, swa_sinks,
           lat_gain_q, lat_gain_kv, rope_gain, w_uq, w_ukv, w_branch, w_out,
           w_coarse, b_coarse, w_fine, b_fine, w13, w2):
    B, S, D = x.shape
    L = w_in.shape[0]
    assert S % MOBA_BLOCK == 0 and D == 1024
    slopes = _alibi_slopes()

    half = MLA_ROPE // 2
    inv = ROPE_THETA ** (-jnp.arange(half, dtype=F32) / half)
    ang = jnp.arange(S).astype(F32)[:, None] * inv[None, :]
    cos = jnp.tile(jnp.cos(ang), (1, 8))
    sin = jnp.tile(jnp.concatenate([-jnp.sin(ang), jnp.sin(ang)], axis=1), (1, 4))

    mod = _ada_mod(c, w_ada, b_ada)

    w_attn = jnp.concatenate(
        [w_in[:, :, :448], w_in[:, :, 512:576], w_in[:, :, 448:512], w_in[:, :, 576:640],
         w_in[:, :, 652:ATTN_OLD], w_in[:, :, 640:652],
         jnp.zeros((L, D, ATTN_COLS - ATTN_OLD), F32)], axis=2).astype(BF16)
    w_gate = w_in[:, :, ATTN_OLD:].astype(BF16)
    dq, dkv = MLA_NOPE + MLA_ROPE, MLA_NOPE + MLA_V
    uq = w_uq.reshape(L, MLA_Q_RANK, MLA_HEADS, dq)
    w_uq_b = jnp.concatenate([uq[..., :MLA_NOPE].reshape(L, MLA_Q_RANK, -1),
                              uq[..., MLA_NOPE:].reshape(L, MLA_Q_RANK, -1)], axis=2).astype(BF16)
    ukv = w_ukv.reshape(L, MLA_KV_RANK, MLA_HEADS, dkv)
    w_ukv_b = jnp.concatenate([ukv[..., :MLA_NOPE].reshape(L, MLA_KV_RANK, -1),
                               ukv[..., MLA_NOPE:].reshape(L, MLA_KV_RANK, -1)], axis=2).astype(BF16)

    qs = HEAD_DIM ** -0.5 * LOG2E
    ms = dq ** -0.5 * LOG2E

    def lanes(v, reps, scale=1.0):
        r = jnp.tile(v, (1, reps)) * scale
        return jnp.pad(r, ((0, 0), (0, 4 * HEAD_DIM - r.shape[1])))

    gq = qk_gain
    gain_rows = jnp.stack([
        lanes(gq[:, QK_NSA_Q], 4, qs),
        lanes(jnp.concatenate([gq[:, QK_NSA_KS], gq[:, QK_NSA_KW]], axis=1), 1),
        lanes(gq[:, QK_SWA_Q], 4, qs),
        lanes(gq[:, QK_SWA_K], 2),
        lanes(gq[:, QK_MOBA_Q], 4, qs),
        lanes(gq[:, QK_MOBA_K], 4),
        lanes(gq[:, QK_MLA_Q], 4, ms),
        lanes(rope_gain[:, 0], 4, ms),
        lanes(gq[:, QK_MLA_K], 4),
        lanes(rope_gain[:, 1], 4),
    ], axis=1)
    bd64 = jnp.asarray(np.kron(np.eye(4), np.ones((HEAD_DIM, HEAD_DIM))) / HEAD_DIM, dtype=BF16)
    bd32 = jnp.asarray(np.kron(np.eye(4), np.ones((MLA_ROPE, MLA_ROPE))) / MLA_ROPE, dtype=BF16)
    w_branch_b = w_branch.astype(BF16)
    w_out_b = w_out.astype(BF16)
    w_router = jnp.concatenate(
        [w_coarse, w_fine, jnp.zeros((L, D, 128 - N_GROUPS - N_EXPERTS), F32)], axis=2)
    b_router = jnp.concatenate(
        [b_coarse, b_fine, jnp.zeros((L, 128 - N_GROUPS - N_EXPERTS), F32)], axis=1)
    sinks_pad = jnp.concatenate([swa_sinks, jnp.zeros((L, 128 - SWA_HEADS), F32)], axis=1)

    for l in range(L):
        m6 = mod[l].reshape(B, 6, 1, D)
        sh1, sc1, g1, sh2, sc2, g2 = (m6[:, j] for j in range(6))
        (qa, kcr, vcr, ks, vs, kw, vw, gs, qb, kb, vb, qc, kc, vc, mc, qd, kd, vd) = _proj_call(
            x, sc1, sh1, norm_gain[l, 0:1], w_attn[l], gain_rows[l], bd64, bd32, lat_gain_q[l][None],
            lat_gain_kv[l][None], w_uq_b[l], w_ukv_b[l], cos, sin)
        kcmp, vcmp = _compress_call(kcr, vcr, cmp_pe[l], cmp_w1[l], cmp_w2[l], qk_gain[l, QK_NSA_KC][None])
        o_cmp, sel = _cmp_attn_call(qa, kcmp, vcmp, slopes[0])
        o_slc = _flash_call(qa, ks, vs, slopes=slopes[0], mask=sel, mask_block=SEL_BLOCK, mask_t=True,
                            tq=256, tk=256, hpc=2, skip_tiles=True, out_dtype=F32, name="nsa_slc")
        o_win = _flash_call(qa, kw, vw, slopes=slopes[0], window=NSA_WINDOW,
                            tq=256, tk=256, hpc=2, out_dtype=F32, name="nsa_win")
        o_b = _flash_call(qb, kb, vb, slopes=slopes[1], window=SWA_WINDOW, sinks=sinks_pad[l][None],
                          tq=256, tk=256, name="swa")
        o_c = _flash_call(qc, kc, vc, slopes=slopes[2], mask=mc, mask_block=MOBA_BLOCK,
                          tq=256, tk=512, name="moba")
        o_d = _flash_call(qd, kd, vd, tq=256, tk=512, name="mla")
        xn, h2, route = _merge_call(
            x, (sc1, sh1, g1, sc2, sh2), norm_gain[l], w_gate[l], w_branch_b[l], w_out_b[l],
            o_cmp, o_slc, o_win, gs, o_b, o_c, o_d, w_router[l], b_router[l][None])
        route = route.reshape(B * S, 128)
        y2 = _moe(h2, route, w13, w2, l)
        x = _combine_call(xn, g2, route, y2)
    return x
```

```python
import functools
import math

import numpy as np
import jax
import jax.numpy as jnp
from jax import lax
from jax.experimental import pallas as pl
from jax.experimental.pallas import tpu as pltpu

F32 = jnp.float32
BF16 = jnp.bfloat16

HEAD_DIM = 64
NEG_INF = -1e30
EPS = 1e-6
NSA_HEADS = 4
CMP_BLOCK = 32
CMP_STRIDE = 16
CMP_HIDDEN = 256
SEL_BLOCK = 64
SEL_TOPK = 8
NSA_WINDOW = 512
FORCE_BONUS = 1e4
SWA_HEADS = 4
SWA_KV_HEADS = 2
SWA_WINDOW = 128
MOBA_HEADS = 4
MOBA_BLOCK = 256
MOBA_TOPK = 3
MLA_HEADS = 4
MLA_Q_RANK = 384
MLA_KV_RANK = 128
MLA_NOPE = 64
MLA_ROPE = 32
MLA_V = 64
ROPE_THETA = 10000.0
N_BRANCH = 4
BRANCH_WIDTH = 256
N_GROUPS = 4
EXPERTS_PER_GROUP = 8
N_EXPERTS = N_GROUPS * EXPERTS_PER_GROUP
D_EXPERT = 256

(GR_NSA_Q, GR_NSA_K, GR_SWA_Q, GR_SWA_K, GR_MOBA_Q, GR_MOBA_K,
 GR_MLA_QN, GR_MLA_QR, GR_MLA_KN, GR_MLA_KR) = range(10)
QK_NSA_Q, QK_NSA_KC, QK_NSA_KS, QK_NSA_KW = 0, 1, 2, 3
QK_SWA_Q, QK_SWA_K, QK_MOBA_Q, QK_MOBA_K, QK_MLA_Q, QK_MLA_K = 4, 5, 6, 7, 8, 9

ATTN_OLD = 2476
ATTN_COLS = 2560
GATE_LANE0 = 32
LOG2E = math.log2(math.e)
SUB, LANE = 8, 128
EXPERT_ROWS = 256
VMEM_LIMIT = 56 * 1024 * 1024


def _alibi_slopes():
    n = NSA_HEADS + SWA_HEADS + MOBA_HEADS

    def pow2(m):
        start = 2.0 ** (-8.0 / m)
        return [start ** (i + 1) for i in range(m)]

    c = 2 ** int(math.floor(math.log2(n)))
    s = pow2(c) + (pow2(2 * c)[0::2][: n - c] if c < n else [])
    s = -np.sort(-np.asarray(s, np.float32))
    return s.reshape(NSA_HEADS, 3).T


def _dot(a, b):
    return jnp.dot(a, b, preferred_element_type=F32)


def _dot_nt(a, b):
    return lax.dot_general(a, b, (((1,), (1,)), ((), ())), preferred_element_type=F32)


def _split(a):
    hi = a.astype(BF16)
    lo = (a - hi.astype(F32)).astype(BF16)
    return hi, lo


def _dot3(a, b):
    ah, al = _split(a)
    bh, bl = _split(b)
    return _dot(ah, bh) + (_dot(ah, bl) + _dot(al, bh))


def _dot3_nt(a, b):
    ah, al = _split(a)
    bh, bl = _split(b)
    return _dot_nt(ah, bh) + (_dot_nt(ah, bl) + _dot_nt(al, bh))


def _rms(x, g):
    return x * lax.rsqrt(jnp.mean(x * x, axis=-1, keepdims=True) + EPS) * g


def _sigmoid(x):
    return 1.0 / (1.0 + jnp.exp(-x))


AUG_MASK0 = 6


def _bf16_pieces(x, n=3):
    out, r = [], float(x)
    for _ in range(n):
        piece = float(np.asarray(r, np.float32).astype(jnp.bfloat16).astype(np.float32))
        out.append(piece)
        r -= piece
    return out


def _topk_mask(score, k, axis=1):
    n = score.shape[axis]
    iota = lax.broadcasted_iota(jnp.int32, score.shape, axis).astype(F32)
    sel = jnp.zeros(score.shape, F32)
    for _ in range(k):
        m = jnp.max(score, axis=axis, keepdims=True)
        idx = jnp.min(jnp.where(score == m, iota, float(n)), axis=axis, keepdims=True)
        hit = iota == idx
        sel = jnp.where(hit, jnp.where(m > 0.5 * NEG_INF, 1.0, 0.0), sel)
        score = jnp.where(hit, -3e38, score)
    return sel


def _ada_kernel(c_ref, w_ref, b_ref, o_ref):
    c = c_ref[...]
    a = c * _sigmoid(c)
    o_ref[0] = _dot(a, w_ref[0]) + b_ref[0]


def _ada_mod(c, w_ada, b_ada):
    L, D, D6 = w_ada.shape
    B = c.shape[0]
    tn = 1024
    return pl.pallas_call(
        _ada_kernel,
        grid=(L, D6 // tn),
        in_specs=[
            pl.BlockSpec((B, D), lambda l, j: (0, 0)),
            pl.BlockSpec((1, D, tn), lambda l, j: (l, 0, j)),
            pl.BlockSpec((1, 1, tn), lambda l, j: (l, 0, j)),
        ],
        out_specs=pl.BlockSpec((1, B, tn), lambda l, j: (l, 0, j)),
        out_shape=jax.ShapeDtypeStruct((L, B, D6), F32),
        compiler_params=pltpu.CompilerParams(vmem_limit_bytes=VMEM_LIMIT),
        name="ada_mod",
    )(c, w_ada, b_ada.reshape(L, 1, D6))


def _rms_blocks(x, bd, g):
    hi, lo = _split(x * x)
    ms = _dot(hi, bd) + _dot(lo, bd)
    return x * lax.rsqrt(ms + EPS) * g


def _proj_kernel(x_ref, sc_ref, sh_ref, ng_ref, w_ref, gr_ref, bd64_ref, bd32_ref, lgq_ref, lgkv_ref,
                 wuq_ref, wukv_ref, cos_ref, sin_ref,
                 qa_ref, kcr_ref, vcr_ref, ks_ref, vs_ref, kw_ref, vw_ref, gs_ref,
                 qb_ref, kb_ref, vb_ref, qc_ref, kc_ref, vc_ref, mc_ref, qd_ref, kd_ref, vd_ref,
                 kmean_s):
    i = pl.program_id(1)
    hd = HEAD_DIM
    x = x_ref[0]
    h = _rms(x, ng_ref[...]) * (1.0 + sc_ref[0]) + sh_ref[0]
    hb = h.astype(BF16)
    bd64 = bd64_ref[...]
    bd64h = bd64_ref[0:2 * hd, 0:2 * hd]
    bd32 = bd32_ref[...]

    def grow(r, w=4 * hd):
        return gr_ref[r:r + 1, 0:w]

    def proj(a, b):
        return _dot(hb, w_ref[:, a:b])

    def store_heads(ref, slab, n):
        for hh in range(n):
            ref[0, hh] = slab[:, hh * hd:(hh + 1) * hd].astype(BF16)

    def store_heads_t(ref, slab_t, n):
        for hh in range(n):
            ref[0, hh] = slab_t[hh * hd:(hh + 1) * hd].astype(BF16)

    store_heads(qa_ref, _rms_blocks(proj(0, 256), bd64, grow(GR_NSA_Q)), NSA_HEADS)
    p = proj(256, 640)
    kcr_ref[0] = p[:, 0:64]
    vcr_ref[0] = p[:, 64:128]
    kk = _rms_blocks(p[:, 128:256], bd64h, grow(GR_NSA_K, 2 * hd))
    ks_ref[0, 0] = kk[:, :hd].astype(BF16)
    kw_ref[0, 0] = kk[:, hd:].astype(BF16)
    vt = p[:, 256:384].T
    vs_ref[0, 0] = vt[:hd].astype(BF16)
    vw_ref[0, 0] = vt[hd:].astype(BF16)

    p = proj(640, 1152)
    store_heads(qb_ref, _rms_blocks(p[:, 0:256], bd64, grow(GR_SWA_Q)), SWA_HEADS)
    store_heads(kb_ref, _rms_blocks(p[:, 256:384], bd64h, grow(GR_SWA_K, 2 * hd)), SWA_KV_HEADS)
    store_heads_t(vb_ref, p[:, 384:512].T, SWA_KV_HEADS)

    @pl.when(i == 0)
    def _():
        kmean_s[...] = jnp.zeros(kmean_s.shape, F32)

    p = proj(1152, 1920)
    qn = _rms_blocks(p[:, 0:256], bd64, grow(GR_MOBA_Q))
    kn = _rms_blocks(p[:, 256:512], bd64, grow(GR_MOBA_K))
    store_heads(qc_ref, qn, MOBA_HEADS)
    store_heads(kc_ref, kn, MOBA_HEADS)
    store_heads_t(vc_ref, p[:, 512:768].T, MOBA_HEADS)
    nblk = kmean_s.shape[0]
    blk_iota = lax.broadcasted_iota(jnp.int32, (1, nblk), 1)
    head_of_lane = lax.broadcasted_iota(jnp.int32, (1, 4 * hd), 1) // hd
    kmeans = kmean_s[...]
    for hh in range(MOBA_HEADS):
        g = _dot3_nt(jnp.where(head_of_lane == hh, qn, 0.0), kmeans)
        g = jnp.where(blk_iota < i, g, NEG_INF)
        mc_ref[0, hh] = jnp.where(blk_iota == i, 1.0, _topk_mask(g, MOBA_TOPK))
    kmean_s[pl.ds(i, 1), :] = jnp.mean(kn, axis=0, keepdims=True)

    p = proj(1920, 2560)
    gs_ref[0] = _sigmoid(p[:, 512:640])
    cos = cos_ref[...]
    sin = sin_ref[...]
    half = MLA_ROPE // 2
    first_half = lax.broadcasted_iota(jnp.int32, (1, 4 * MLA_ROPE), 1) % MLA_ROPE < half

    def rope(v):
        swapped = jnp.where(first_half, pltpu.roll(v, 4 * MLA_ROPE - half, 1), pltpu.roll(v, half, 1))
        return v * cos + swapped * sin

    qlat = _dot(_rms(p[:, 0:384], lgq_ref[...]).astype(BF16), wuq_ref[...])
    kvlat = _dot(_rms(p[:, 384:512], lgkv_ref[...]).astype(BF16), wukv_ref[...])
    q_nope = _rms_blocks(qlat[:, 0:256], bd64, grow(GR_MLA_QN))
    q_rot = rope(_rms_blocks(qlat[:, 256:384], bd32, grow(GR_MLA_QR, 2 * hd)))
    k_nope = _rms_blocks(kvlat[:, 0:256], bd64, grow(GR_MLA_KN))
    k_rot = rope(_rms_blocks(p[:, 512:640], bd32, grow(GR_MLA_KR, 2 * hd)))[:, :MLA_ROPE]
    store_heads_t(vd_ref, kvlat[:, 256:512].T, MLA_HEADS)
    for hh in range(MLA_HEADS):
        qd_ref[0, hh] = jnp.concatenate(
            [q_nope[:, hh * hd:(hh + 1) * hd], q_rot[:, hh * MLA_ROPE:(hh + 1) * MLA_ROPE]], axis=1).astype(BF16)
        kd_ref[0, hh] = jnp.concatenate([k_nope[:, hh * hd:(hh + 1) * hd], k_rot], axis=1).astype(BF16)


def _proj_call(x, sc1, sh1, ng, w_attn, gain_rows, bd64, bd32, lgq, lgkv, wuq, wukv, cos, sin):
    B, S, D = x.shape
    tm = MOBA_BLOCK
    nblk = S // tm
    hd = HEAD_DIM

    def full(shape):
        return pl.BlockSpec(shape, lambda b, i: (0,) * len(shape))

    def heads(nh, d):
        return pl.BlockSpec((1, nh, tm, d), lambda b, i: (b, 0, i, 0))

    in_specs = [
        pl.BlockSpec((1, tm, D), lambda b, i: (b, i, 0)),
        pl.BlockSpec((1, 1, D), lambda b, i: (b, 0, 0)),
        pl.BlockSpec((1, 1, D), lambda b, i: (b, 0, 0)),
        full((1, D)),
        full((D, ATTN_COLS)),
        full(gain_rows.shape),
        full(bd64.shape),
        full(bd32.shape),
        full(lgq.shape),
        full(lgkv.shape),
        full(wuq.shape),
        full(wukv.shape),
        pl.BlockSpec((tm, 4 * MLA_ROPE), lambda b, i: (i, 0)),
        pl.BlockSpec((tm, 4 * MLA_ROPE), lambda b, i: (i, 0)),
    ]
    row64 = pl.BlockSpec((1, tm, hd), lambda b, i: (b, i, 0))
    def heads_t(nh, d):
        return pl.BlockSpec((1, nh, d, tm), lambda b, i: (b, 0, 0, i))

    out_specs = [
        heads(4, hd), row64, row64, heads(1, hd), heads_t(1, hd), heads(1, hd), heads_t(1, hd),
        pl.BlockSpec((1, tm, 128), lambda b, i: (b, i, 0)),
        heads(4, hd), heads(2, hd), heads_t(2, hd),
        heads(4, hd), heads(4, hd), heads_t(4, hd), heads(4, nblk),
        heads(4, MLA_NOPE + MLA_ROPE), heads(4, MLA_NOPE + MLA_ROPE), heads_t(4, MLA_V),
    ]

    def sd(shape, dt):
        return jax.ShapeDtypeStruct(shape, dt)

    out_shape = [
        sd((B, 4, S, hd), BF16), sd((B, S, hd), F32), sd((B, S, hd), F32),
        sd((B, 1, S, hd), BF16), sd((B, 1, hd, S), BF16), sd((B, 1, S, hd), BF16), sd((B, 1, hd, S), BF16),
        sd((B, S, 128), F32),
        sd((B, 4, S, hd), BF16), sd((B, 2, S, hd), BF16), sd((B, 2, hd, S), BF16),
        sd((B, 4, S, hd), BF16), sd((B, 4, S, hd), BF16), sd((B, 4, hd, S), BF16), sd((B, 4, S, nblk), F32),
        sd((B, 4, S, 96), BF16), sd((B, 4, S, 96), BF16), sd((B, 4, MLA_V, S), BF16),
    ]
    return pl.pallas_call(
        _proj_kernel,
        grid=(B, nblk),
        in_specs=in_specs,
        out_specs=out_specs,
        out_shape=out_shape,
        scratch_shapes=[pltpu.VMEM((nblk, MOBA_HEADS * hd), F32)],
        compiler_params=pltpu.CompilerParams(
            dimension_semantics=("arbitrary", "arbitrary"), vmem_limit_bytes=VMEM_LIMIT),
        name="proj_prep",
    )(x, sc1, sh1, ng, w_attn, gain_rows, bd64, bd32, lgq, lgkv, wuq, wukv, cos, sin)


def _compress_kernel(gk_ref, gv_ref, pe_ref, w1_ref, w2_ref, gkc_ref, kc_ref, vc_ref):
    half = w1_ref.shape[1] // 2
    outs = []
    for j, g_ref in enumerate((gk_ref, gv_ref)):
        g = g_ref[0].astype(BF16)
        top = _dot(g, w1_ref[j, :half].astype(BF16))
        bot = _dot(g, w1_ref[j, half:].astype(BF16))
        bot = jnp.concatenate([bot[1:], bot[:1]], axis=0)
        pe = jnp.broadcast_to(pe_ref[j], (8, pe_ref.shape[2]))
        bias = _dot3(pe, w1_ref[j])[0:1]
        hid = top + bot + bias
        hid = hid * _sigmoid(hid)
        outs.append(_dot(hid.astype(BF16), w2_ref[j].astype(BF16)))
    kc_ref[0] = _rms(outs[0], gkc_ref[...]).astype(BF16)
    dk = outs[1].shape[1]
    vc_ref[0] = jnp.concatenate([outs[1], outs[1]], axis=1).T[:dk].astype(BF16)


def _compress_call(kc_raw, vc_raw, pe, w1, w2, g_kc):
    B, S, dk = kc_raw.shape
    n_grp = S // CMP_STRIDE
    gk = kc_raw.reshape(B, n_grp, CMP_STRIDE * dk)
    gv = vc_raw.reshape(B, n_grp, CMP_STRIDE * dk)
    pe_flat = pe.reshape(2, 1, CMP_BLOCK * dk)
    grp_spec = pl.BlockSpec((1, n_grp, CMP_STRIDE * dk), lambda b: (b, 0, 0))
    out_spec = pl.BlockSpec((1, n_grp, dk), lambda b: (b, 0, 0))
    return pl.pallas_call(
        _compress_kernel,
        grid=(B,),
        in_specs=[
            grp_spec, grp_spec,
            pl.BlockSpec(pe_flat.shape, lambda b: (0, 0, 0)),
            pl.BlockSpec(w1.shape, lambda b: (0, 0, 0)),
            pl.BlockSpec(w2.shape, lambda b: (0, 0, 0)),
            pl.BlockSpec((1, dk), lambda b: (0, 0)),
        ],
        out_specs=[out_spec, pl.BlockSpec((1, dk, n_grp), lambda b: (b, 0, 0))],
        out_shape=[jax.ShapeDtypeStruct((B, n_grp, dk), BF16), jax.ShapeDtypeStruct((B, dk, n_grp), BF16)],
        compiler_params=pltpu.CompilerParams(vmem_limit_bytes=VMEM_LIMIT),
        name="nsa_compress",
    )(gk, gv, pe_flat, w1, w2, g_kc)


def _cmp_attn_kernel(q_ref, kc_ref, vct_ref, cover_ref, o_ref, sel_ref, *, slopes, tq, n_cmp):
    i = pl.program_id(1)
    kc = kc_ref[0]
    vct = vct_ref[0]
    ncp = kc.shape[0]
    t_full = i * tq + lax.broadcasted_iota(jnp.int32, (ncp, tq), 1)
    n_iota = lax.broadcasted_iota(jnp.int32, (ncp, tq), 0)
    dist_i = t_full - (n_iota * CMP_STRIDE + (CMP_BLOCK - 1))
    vis = (n_iota < n_cmp) & (dist_i >= 0)
    dist = dist_i.astype(F32)
    visf = vis.astype(F32)
    psum = jnp.zeros((ncp, tq), F32)
    outs = []
    for hh in range(NSA_HEADS):
        s = _dot_nt(kc, q_ref[0, hh]) - (slopes[hh] * LOG2E) * dist
        s = jnp.where(vis, s, NEG_INF)
        e = jnp.exp2(s - jnp.max(s, axis=0, keepdims=True)) * visf
        p = e / jnp.maximum(jnp.sum(e, axis=0, keepdims=True), 1e-30)
        outs.append(_dot(vct, p.astype(BF16)))
        psum = psum + p
    o_ref[0] = jnp.concatenate(outs, axis=0).T
    ph, plo = _split(psum)
    cover = cover_ref[...]
    p_slc = _dot(cover, ph) + _dot(cover, plo)
    n_sel = cover.shape[0]
    cur = (i * tq + lax.broadcasted_iota(jnp.int32, (1, tq), 1)) // SEL_BLOCK
    j = lax.broadcasted_iota(jnp.int32, (n_sel, 1), 0)
    forced = jnp.where(j == 0, 1.0, jnp.where(j == cur, 1.0, jnp.where(j == cur - 1, 1.0, 0.0)))
    score = jnp.where(j <= cur, p_slc + FORCE_BONUS * forced, NEG_INF)
    sel_ref[0, 0] = _topk_mask(score, min(SEL_TOPK, n_sel), axis=0)


def _cmp_attn_call(qa, kc, vc, slopes):
    B, H, S, dk = qa.shape
    ncp = kc.shape[1]
    n_cmp = (S - CMP_BLOCK) // CMP_STRIDE + 1
    n_sel = S // SEL_BLOCK
    tq = 256
    starts = np.arange(ncp) * CMP_STRIDE
    jb = np.arange(n_sel) * SEL_BLOCK
    cover = ((starts[:, None] < jb[None, :] + SEL_BLOCK) & (starts[:, None] + CMP_BLOCK > jb[None, :])
             & (np.arange(ncp)[:, None] < n_cmp))
    cover = jnp.asarray(cover.T.astype(np.float32), dtype=BF16)
    return pl.pallas_call(
        functools.partial(_cmp_attn_kernel, slopes=tuple(float(s) for s in slopes), tq=tq, n_cmp=n_cmp),
        grid=(B, S // tq),
        in_specs=[
            pl.BlockSpec((1, H, tq, dk), lambda b, i: (b, 0, i, 0)),
            pl.BlockSpec((1, ncp, dk), lambda b, i: (b, 0, 0)),
            pl.BlockSpec((1, dk, ncp), lambda b, i: (b, 0, 0)),
            pl.BlockSpec((n_sel, ncp), lambda b, i: (0, 0)),
        ],
        out_specs=[
            pl.BlockSpec((1, tq, H * dk), lambda b, i: (b, i, 0)),
            pl.BlockSpec((1, 1, n_sel, tq), lambda b, i: (b, 0, 0, i)),
        ],
        out_shape=[jax.ShapeDtypeStruct((B, S, H * dk), F32),
                   jax.ShapeDtypeStruct((B, 1, n_sel, S), F32)],
        compiler_params=pltpu.CompilerParams(vmem_limit_bytes=VMEM_LIMIT),
        name="nsa_cmp_attn",
    )(qa, kc, vc, cover)


def _flash_kernel(*refs, H, G, hpc, tq, tk, slopes, window, mask_block, mask_per_head, mask_t, has_sink, skip):
    refs = list(refs)
    q_ref, k_ref, v_ref = refs[:3]
    pos = 3
    mask_ref = sink_ref = None
    if mask_block:
        mask_ref = refs[pos]
        pos += 1
    if has_sink:
        sink_ref = refs[pos]
        pos += 1
    o_ref, m_s, acc_s = refs[pos:pos + 3]
    need_s = refs[pos + 3] if skip else None
    use_aug = slopes is not None or bool(mask_block)
    R = H // G
    C = H // hpc
    M = hpc * tq
    dv = v_ref.shape[-2]
    i = pl.program_id(1)
    q0 = i * tq
    hi = q0 // tk
    lo = jnp.maximum(q0 - (window - 1), 0) // tk if window else 0

    def rel_pos():
        r_row = jnp.concatenate([lax.broadcasted_iota(jnp.int32, (1, tq), 1)] * hpc, axis=1)
        return (r_row - lax.broadcasted_iota(jnp.int32, (tk, M), 0)).astype(F32)

    dq = q_ref.shape[-1]
    n_mask = mask_ref.shape[-2 if mask_t else -1] if mask_block else 0
    aug_w = LANE if dq + AUG_MASK0 + n_mask <= LANE else 2 * LANE
    aug0 = dq
    lane_a = lax.broadcasted_iota(jnp.int32, (1, aug_w), 1)
    feature_lane = lane_a < dq

    def slope_lanes(hh):
        row = jnp.zeros((1, aug_w), F32)
        if slopes is not None:
            for n, piece in enumerate(_bf16_pieces(slopes[hh])):
                row = jnp.where((lane_a == aug0 + n) | (lane_a == aug0 + n + 3), piece, row)
        return row

    def widen(a):
        return jnp.concatenate([a, jnp.zeros((a.shape[0], aug_w - dq), a.dtype)], axis=1)

    qs, slope_rows, bms = [], [], []
    for c in range(C):
        heads = [c * hpc + r for r in range(hpc)]
        if hpc == 1:
            qs.append(q_ref[0, heads[0]])
        else:
            qs.append(jnp.concatenate([q_ref[0, hh] for hh in heads], axis=0))
        if slopes is not None:
            slope_rows.append(jnp.concatenate([jnp.full((1, tq), slopes[hh], F32) for hh in heads], axis=1))
        if mask_block:
            parts = [mask_ref[0, hh if mask_per_head else 0] for hh in heads]
            bms.append(jnp.concatenate(parts, axis=1) if mask_t else jnp.concatenate(parts, axis=0).T)
    m_s[...] = jnp.full(m_s.shape, NEG_INF, F32)
    acc_s[...] = jnp.zeros(acc_s.shape, F32)
    ones_rows = jnp.ones((acc_s.shape[1] - dv, tk), BF16)

    if mask_block:
        nblk = bms[0].shape[0]
    if need_s is not None:
        col = bms[0]
        for bm in bms[1:]:
            col = jnp.maximum(col, bm)
        col = jnp.max(col, axis=1, keepdims=True)
        bpt = tk // mask_block
        for jt in range(nblk // bpt):
            need_s[jt] = jnp.max(col[jt * bpt:(jt + 1) * bpt]).astype(jnp.int32)

    if use_aug:
        for c in range(C):
            qa = jnp.concatenate(
                [jnp.broadcast_to(slope_lanes(c * hpc + r), (tq, aug_w)) for r in range(hpc)], axis=0)
            if mask_block:
                place = jnp.where(
                    lax.broadcasted_iota(jnp.int32, (nblk, aug_w), 0) + (aug0 + AUG_MASK0) == lane_a,
                    1.0, 0.0).astype(BF16)
                off_sel = ((bms[c] - 1.0) * -NEG_INF).astype(BF16)
                qa = qa + lax.dot_general(off_sel, place, (((0,), (0,)), ((), ())), preferred_element_type=F32)
            qs[c] = jnp.where(feature_lane, widen(qs[c]), qa.astype(BF16))

    def tile_scores(j):
        k0 = pl.multiple_of(j * tk, tk)
        if use_aug:
            t_key = lax.broadcasted_iota(jnp.int32, (tk, aug_w), 0) + k0
            k_aug = jnp.zeros((tk, aug_w), F32)
            if mask_block:
                key_blk = lax.shift_right_logical(t_key, int(math.log2(mask_block)))
                k_aug = jnp.where(key_blk + (aug0 + AUG_MASK0) == lane_a, 1.0, 0.0)
            if slopes is not None:
                t_hi = (lax.shift_right_logical(t_key, 8) * 256).astype(F32)
                t_lo = (t_key & 255).astype(F32)
                k_aug = jnp.where(lane_a < aug0 + 3, t_hi, jnp.where(lane_a < aug0 + 6, t_lo, k_aug))
            k_aug = k_aug.astype(BF16)
        k_cats = {}
        scores = []
        for c in range(C):
            g = (c * hpc) // R
            if g not in k_cats:
                k = k_ref[0, g, pl.ds(k0, tk), :]
                k_cats[g] = jnp.where(feature_lane, widen(k), k_aug) if use_aug else k
            scores.append(_dot_nt(k_cats[g], qs[c]))
        return scores

    def tile_update(j, edge, scores):
        k0 = pl.multiple_of(j * tk, tk)
        valid = None
        if edge:
            dist = rel_pos() + (q0 - k0).astype(F32)
            valid = dist >= 0.0
            if window:
                valid = valid & (dist < float(window))
        probs, alphas = [], []
        for g in range(C):
            s = scores[g]
            if valid is not None:
                s = jnp.where(valid, s, NEG_INF)
            m_prev = m_s[g]
            m_new = jnp.maximum(m_prev, jnp.max(s, axis=0, keepdims=True))
            alpha = jnp.exp2(m_prev - m_new)
            p = jnp.exp2(s - m_new)
            m_s[g] = m_new
            probs.append(p.astype(BF16))
            alphas.append(alpha)
        for c in range(C):
            v = jnp.concatenate([v_ref[0, (c * hpc) // R, :, pl.ds(k0, tk)], ones_rows], axis=0)
            acc_s[c] = alphas[c] * acc_s[c] + _dot(v, probs[c])

    def tile(j, edge):
        tile_update(j, edge, tile_scores(j))

    def body(step, carry):
        j = hi - step
        is_edge = step == 0
        if window:
            is_edge = is_edge | (q0 - j * tk + (tq - 1) >= window)
        run = (step == 0) | (need_s[j] > 0) if need_s is not None else None

        def when(c):
            return pl.when(c if run is None else c & run)

        @when(is_edge)
        def _():
            tile(j, True)

        @when(jnp.logical_not(is_edge))
        def _():
            tile(j, False)

        return carry

    lax.fori_loop(0, hi - lo + 1, body, 0)

    for c in range(C):
        m = m_s[c]
        l = acc_s[c, dv:dv + 1]
        acc = acc_s[c, 0:dv]
        if has_sink:
            sk = LOG2E * jnp.concatenate(
                [jnp.broadcast_to(sink_ref[:, c * hpc + r:c * hpc + r + 1], (1, tq)) for r in range(hpc)], axis=1)
            if slopes is not None:
                t_q = jnp.concatenate([lax.broadcasted_iota(jnp.int32, (1, tq), 1) + q0] * hpc, axis=1)
                m = m - slope_rows[c] * t_q.astype(F32)
            m_f = jnp.maximum(m, sk)
            a = jnp.exp2(m - m_f)
            l = l * a + jnp.exp2(sk - m_f)
            acc = acc * a
        out = acc / l
        for r in range(hpc):
            hh = c * hpc + r
            o_ref[0, :, hh * dv:(hh + 1) * dv] = out[:, r * tq:(r + 1) * tq].T.astype(o_ref.dtype)


def _flash_call(q, k, v, *, slopes=None, window=0, mask=None, mask_block=0, mask_t=False, sinks=None,
                tq=128, tk=256, hpc=None, skip_tiles=False, out_dtype=BF16, name="flash"):
    B, H, S, dq = q.shape
    G = k.shape[1]
    dv = v.shape[-2]
    tk = min(tk, S)
    assert tk % tq == 0 and S % tk == 0
    R = H // G
    hpc = R if hpc is None else hpc
    assert R % hpc == 0
    C = H // hpc
    in_specs = [
        pl.BlockSpec((1, H, tq, dq), lambda b, i: (b, 0, i, 0)),
        pl.BlockSpec((1, G, S, dq), lambda b, i: (b, 0, 0, 0)),
        pl.BlockSpec((1, G, dv, S), lambda b, i: (b, 0, 0, 0)),
    ]
    args = [q, k, v]
    mask_per_head = False
    if mask is not None:
        hm = mask.shape[1]
        mask_per_head = hm > 1
        if mask_t:
            in_specs.append(pl.BlockSpec((1, hm, mask.shape[2], tq), lambda b, i: (b, 0, 0, i)))
        else:
            in_specs.append(pl.BlockSpec((1, hm, tq, mask.shape[3]), lambda b, i: (b, 0, i, 0)))
        args.append(mask)
    if sinks is not None:
        in_specs.append(pl.BlockSpec(sinks.shape, lambda b, i: (0, 0)))
        args.append(sinks)
    kern = functools.partial(
        _flash_kernel, H=H, G=G, hpc=hpc, tq=tq, tk=tk,
        slopes=None if slopes is None else tuple(float(s) * LOG2E for s in slopes),
        window=window, mask_block=mask_block if mask is not None else 0,
        mask_per_head=mask_per_head, mask_t=mask_t, has_sink=sinks is not None,
        skip=skip_tiles)
    ones_rows = 16
    scratch = [pltpu.VMEM((C, 1, hpc * tq), F32), pltpu.VMEM((C, dv + ones_rows, hpc * tq), F32)]
    if mask is not None:
        assert tk % mask_block == 0
    if skip_tiles:
        assert mask is not None
        scratch.append(pltpu.SMEM((S // tk,), jnp.int32))
    return pl.pallas_call(
        kern,
        grid=(B, S // tq),
        in_specs=in_specs,
        out_specs=pl.BlockSpec((1, tq, H * dv), lambda b, i: (b, i, 0)),
        out_shape=jax.ShapeDtypeStruct((B, S, H * dv), out_dtype),
        scratch_shapes=scratch,
        compiler_params=pltpu.CompilerParams(vmem_limit_bytes=VMEM_LIMIT),
        name=name,
    )(*args)


def _merge_kernel(x_ref, sc1_ref, sh1_ref, g1_ref, sc2_ref, sh2_ref, ng_ref, wg_ref, wb_ref, wo_ref,
                  ocmp_ref, oslc_ref, owin_ref, gs_ref, ob_ref, oc_ref, od_ref, wr_ref, br_ref,
                  xo_ref, h2_ref, rt_ref):
    hd = HEAD_DIM
    x = x_ref[0]
    ng = ng_ref[...]
    h = _rms(x, ng[0:1]) * (1.0 + sc1_ref[0]) + sh1_ref[0]
    hb = h.astype(BF16)
    gs = gs_ref[0]
    ocmp = ocmp_ref[0]
    oslc = oslc_ref[0]
    owin = owin_ref[0]
    parts = []
    for hh in range(NSA_HEADS):
        c0 = GATE_LANE0 + 3 * hh
        sl = slice(hh * hd, (hh + 1) * hd)
        parts.append(gs[:, c0:c0 + 1] * ocmp[:, sl] + gs[:, c0 + 1:c0 + 2] * oslc[:, sl]
                     + gs[:, c0 + 2:c0 + 3] * owin[:, sl])
    o_a = jnp.concatenate(parts, axis=1).astype(BF16)
    branches = (o_a, ob_ref[0], oc_ref[0], od_ref[0])
    D = x.shape[1]
    mixed = None
    for n in range(N_BRANCH):
        gate = _sigmoid(_dot(hb, wg_ref[:, n * D:(n + 1) * D]))
        term = gate * _dot(branches[n], wb_ref[n])
        mixed = term if mixed is None else mixed + term
    xn = x + g1_ref[0] * _dot(mixed.astype(BF16), wo_ref[...])
    xo_ref[0] = xn
    h2 = _rms(xn, ng[1:2]) * (1.0 + sc2_ref[0]) + sh2_ref[0]
    for c in range(SUB):
        h2_ref[pl.ds(c, h2.shape[0], stride=SUB), :] = h2[:, c * LANE:(c + 1) * LANE]

    logits = _dot3(h2, wr_ref[...]) + br_ref[...]
    lane = lax.broadcasted_iota(jnp.int32, logits.shape, 1)
    lanef = lane.astype(F32)
    is_c = lane < N_GROUPS
    lc = jnp.where(is_c, logits, NEG_INF)
    mc = jnp.max(lc, axis=-1, keepdims=True)
    grp = jnp.min(jnp.where(lc == mc, lanef, 1e9), axis=-1, keepdims=True)
    p_grp = 1.0 / jnp.sum(jnp.where(is_c, jnp.exp(lc - mc), 0.0), axis=-1, keepdims=True)
    e_lane = lanef - float(N_GROUPS)
    in_grp = (lane >= N_GROUPS) & (lane < N_GROUPS + N_EXPERTS) & (
        jnp.floor(e_lane / EXPERTS_PER_GROUP) == grp)
    lf = jnp.where(in_grp, logits, NEG_INF)
    m1 = jnp.max(lf, axis=-1, keepdims=True)
    i1 = jnp.min(jnp.where(lf == m1, e_lane, 1e9), axis=-1, keepdims=True)
    lf2 = jnp.where(e_lane == i1, NEG_INF, lf)
    m2 = jnp.max(lf2, axis=-1, keepdims=True)
    i2 = jnp.min(jnp.where(lf2 == m2, e_lane, 1e9), axis=-1, keepdims=True)
    e2 = jnp.exp(m2 - m1)
    w1 = p_grp / (1.0 + e2)
    w2 = p_grp * e2 / (1.0 + e2)
    rt = jnp.where(lane == 0, i1, jnp.where(lane == 1, i2, jnp.where(lane == 2, w1, jnp.where(lane == 3, w2, 0.0))))
    rt_ref[0] = rt


def _merge_call(x, mods, ng, wg, wb, wo, ocmp, oslc, owin, gs, ob, oc, od, wr, br):
    B, S, D = x.shape
    tm = 512
    sc1, sh1, g1, sc2, sh2 = mods

    def full(a):
        return pl.BlockSpec(a.shape, lambda b, i: (0,) * a.ndim)

    modspec = pl.BlockSpec((1, 1, D), lambda b, i: (b, 0, 0))
    row = lambda w: pl.BlockSpec((1, tm, w), lambda b, i: (b, i, 0))
    in_specs = [row(D), modspec, modspec, modspec, modspec, modspec, full(ng), full(wg), full(wb), full(wo),
                row(256), row(256), row(256), row(128), row(256), row(256), row(256), full(wr), full(br)]
    return pl.pallas_call(
        _merge_kernel,
        grid=(B, S // tm),
        in_specs=in_specs,
        out_specs=[row(D), pl.BlockSpec((tm * SUB, LANE), lambda b, i: (b * (S // tm) + i, 0)), row(128)],
        out_shape=[jax.ShapeDtypeStruct((B, S, D), F32), jax.ShapeDtypeStruct((B * S * SUB, LANE), F32),
                   jax.ShapeDtypeStruct((B, S, 128), F32)],
        compiler_params=pltpu.CompilerParams(vmem_limit_bytes=VMEM_LIMIT),
        name="merge_router",
    )(x, sc1, sh1, g1, sc2, sh2, ng, wg, wb, wo, ocmp, oslc, owin, gs, ob, oc, od, wr, br)


def _expert_kernel(blk_e_ref, nreal_ref, tok_ref, dst_ref, h2_hbm, w13_ref, w2_ref, y2_hbm, xbuf, ybuf, gsem, ssem):
    del blk_e_ref
    i = pl.program_id(0)
    nb = pl.num_programs(0)
    slot = i % 2
    rows = EXPERT_ROWS
    grp = 8

    def tile_rows(t):
        return pl.ds(pl.multiple_of(t * SUB, SUB), SUB)

    def gather_copy(blk, s, r):
        tok = tok_ref[blk * rows + r]
        return pltpu.make_async_copy(h2_hbm.at[tile_rows(tok), :], xbuf.at[s, tile_rows(r), :], gsem.at[s])

    def scatter_copy(blk, s, r):
        d = dst_ref[blk * rows + r]
        return pltpu.make_async_copy(ybuf.at[s, tile_rows(r), :], y2_hbm.at[tile_rows(d), :], ssem.at[s])

    def for_real_rows(blk, per_row, per_group):
        n = nreal_ref[blk]
        nfull = n // grp

        def body(g, carry):
            per_group(g)
            return carry
        lax.fori_loop(0, nfull, body, 0)
        for u in range(grp - 1):
            r = nfull * grp + u

            @pl.when(r < n)
            def _():
                per_row(r)

    def start_gather(blk, s):
        def group(g):
            for u in range(grp):
                gather_copy(blk, s, g * grp + u).start()
        for_real_rows(blk, lambda r: gather_copy(blk, s, r).start(), group)

    def start_scatter(blk, s):
        def group(g):
            for u in range(grp):
                scatter_copy(blk, s, g * grp + u).start(priority=1)
        for_real_rows(blk, lambda r: scatter_copy(blk, s, r).start(priority=1), group)

    def wait_gather(blk, s):
        def tiles(k):
            return pltpu.make_async_copy(h2_hbm.at[pl.ds(0, k * SUB), :], xbuf.at[s, pl.ds(0, k * SUB), :], gsem.at[s])
        for_real_rows(blk, lambda r: tiles(1).wait(), lambda g: tiles(grp).wait())

    def wait_scatter(blk, s):
        def tiles(k):
            return pltpu.make_async_copy(ybuf.at[s, pl.ds(0, k * SUB), :], y2_hbm.at[pl.ds(0, k * SUB), :], ssem.at[s])
        for_real_rows(blk, lambda r: tiles(1).wait(), lambda g: tiles(grp).wait())

    @pl.when(i == 0)
    def _():
        xbuf[...] = jnp.zeros(xbuf.shape, F32)
        start_gather(0, 0)

    @pl.when(i + 1 < nb)
    def _():
        start_gather(i + 1, 1 - slot)

    wait_gather(i, slot)

    @pl.when(i >= 2)
    def _():
        wait_scatter(i - 2, slot)

    ab = None
    for c in range(0, SUB, 2):
        xc = jnp.concatenate([xbuf[slot, pl.ds(c, rows, stride=SUB), :],
                              xbuf[slot, pl.ds(c + 1, rows, stride=SUB), :]], axis=1).astype(BF16)
        part = _dot(xc, w13_ref[0, 0, c * LANE:(c + 2) * LANE, :].astype(BF16))
        ab = part if ab is None else ab + part
    de = ab.shape[1] // 2
    a = ab[:, :de]
    b = ab[:, de:]
    act = ((a * _sigmoid(a)) * b).astype(BF16)
    for c in range(0, SUB, 2):
        y = _dot(act, w2_ref[0, 0, :, c * LANE:(c + 2) * LANE].astype(BF16))
        ybuf[slot, pl.ds(c, rows, stride=SUB), :] = y[:, :LANE]
        ybuf[slot, pl.ds(c + 1, rows, stride=SUB), :] = y[:, LANE:]
    start_scatter(i, slot)

    @pl.when(i == nb - 1)
    def _():
        @pl.when(nb >= 2)
        def _():
            wait_scatter(i - 1, 1 - slot)
        wait_scatter(i, slot)


def _expert_call(h2, blk_e, n_real, row_tok, row_dst, w13, w2, layer):
    D = SUB * LANE
    assert h2.shape[1] == LANE and w13.shape[2] == D
    n_blocks = blk_e.shape[0]
    de2 = w13.shape[-1]
    grid_spec = pltpu.PrefetchScalarGridSpec(
        num_scalar_prefetch=4,
        grid=(n_blocks,),
        in_specs=[
            pl.BlockSpec(memory_space=pl.ANY),
            pl.BlockSpec((1, 1, D, de2), lambda i, be, nr, rt, rd: (layer, be[i], 0, 0)),
            pl.BlockSpec((1, 1, de2 // 2, D), lambda i, be, nr, rt, rd: (layer, be[i], 0, 0)),
        ],
        out_specs=pl.BlockSpec(memory_space=pl.ANY),
        scratch_shapes=[pltpu.VMEM((2, EXPERT_ROWS * SUB, LANE), F32), pltpu.VMEM((2, EXPERT_ROWS * SUB, LANE), F32),
                        pltpu.SemaphoreType.DMA((2,)), pltpu.SemaphoreType.DMA((2,))],
    )
    return pl.pallas_call(
        _expert_kernel,
        grid_spec=grid_spec,
        out_shape=jax.ShapeDtypeStruct((2 * h2.shape[0], LANE), F32),
        compiler_params=pltpu.CompilerParams(
            dimension_semantics=("arbitrary",), vmem_limit_bytes=VMEM_LIMIT),
        name="experts",
    )(blk_e, n_real, row_tok, row_dst, h2, w13, w2)


def _combine_kernel(xn_ref, g2_ref, rt_ref, y0_ref, y1_ref, o_ref):
    rt = rt_ref[...]
    tm = xn_ref.shape[0]
    w0 = rt[:, 2:3]
    w1 = rt[:, 3:4]
    g2 = g2_ref[0]
    for c in range(SUB):
        cols = slice(c * LANE, (c + 1) * LANE)
        y = w0 * y0_ref[pl.ds(c, tm, stride=SUB), :] + w1 * y1_ref[pl.ds(c, tm, stride=SUB), :]
        o_ref[:, cols] = xn_ref[:, cols] + g2[:, cols] * y


def _combine_call(xn, g2, route, y2):
    B, S, D = xn.shape
    tm = 512
    spb = S // tm
    nt = B * spb
    return pl.pallas_call(
        _combine_kernel,
        grid=(B, spb),
        in_specs=[
            pl.BlockSpec((tm, D), lambda b, i: (b * spb + i, 0)),
            pl.BlockSpec((1, 1, D), lambda b, i: (b, 0, 0)),
            pl.BlockSpec((tm, 128), lambda b, i: (b * spb + i, 0)),
            pl.BlockSpec((tm * SUB, LANE), lambda b, i: (b * spb + i, 0)),
            pl.BlockSpec((tm * SUB, LANE), lambda b, i: (nt + b * spb + i, 0)),
        ],
        out_specs=pl.BlockSpec((tm, D), lambda b, i: (b * spb + i, 0)),
        out_shape=jax.ShapeDtypeStruct((B * S, D), F32),
        compiler_params=pltpu.CompilerParams(vmem_limit_bytes=VMEM_LIMIT),
        name="moe_combine",
    )(xn.reshape(B * S, D), g2, route, y2, y2).reshape(B, S, D)


def _moe(h2, route, w13, w2, layer):
    N = h2.shape[0] // SUB
    K = 2
    E = N_EXPERTS
    rows = EXPERT_ROWS
    flat_e = route[:, 0:2].astype(jnp.int32).reshape(-1)
    order = jnp.argsort(flat_e).astype(jnp.int32)
    counts = jnp.sum((flat_e[:, None] == jnp.arange(E)[None, :]).astype(jnp.int32), axis=0)
    padded = (counts + rows - 1) // rows * rows
    pend = jnp.cumsum(padded)
    pstart = pend - padded
    start = jnp.cumsum(counts) - counts
    n_blocks = (N * K) // rows + E
    R = n_blocks * rows
    blk_e = jnp.minimum(
        jnp.sum((jnp.arange(n_blocks)[:, None] * rows >= pend[None, :]).astype(jnp.int32), axis=1), E - 1)
    pos = jnp.arange(R, dtype=jnp.int32).reshape(n_blocks, rows)
    local = pos - pstart[blk_e][:, None]
    is_real = ((pos < pend[E - 1]) & (local < counts[blk_e][:, None])).reshape(R)
    rank = jnp.clip(start[blk_e][:, None] + local, 0, N * K - 1).reshape(R)
    pair = order[rank]
    row_tok = jnp.where(is_real, pair // K, 0).astype(jnp.int32)
    row_dst = jnp.where(is_real, (pair % K) * N + pair // K, 0).astype(jnp.int32)
    n_real = jnp.sum(is_real.reshape(n_blocks, rows).astype(jnp.int32), axis=1)
    return _expert_call(h2, blk_e.astype(jnp.int32), n_real, row_tok, row_dst, w13, w2, layer)


def kernel(x, c, w_ada, b_ada, norm_gain, w_in, qk_gain, cmp_pe, cmp_w1, cmp_w2, swa_sinks,
           lat_gain_q, lat_gain_kv, rope_gain, w_uq, w_ukv, w_branch, w_out,
           w_coarse, b_coarse, w_fine, b_fine, w13, w2):
    B, S, D = x.shape
    L = w_in.shape[0]
    assert S % MOBA_BLOCK == 0 and D == 1024
    slopes = _alibi_slopes()

    half = MLA_ROPE // 2
    inv = ROPE_THETA ** (-jnp.arange(half, dtype=F32) / half)
    ang = jnp.arange(S).astype(F32)[:, None] * inv[None, :]
    cos = jnp.tile(jnp.cos(ang), (1, 8))
    sin = jnp.tile(jnp.concatenate([-jnp.sin(ang), jnp.sin(ang)], axis=1), (1, 4))

    mod = _ada_mod(c, w_ada, b_ada)

    w_attn = jnp.concatenate(
        [w_in[:, :, :448], w_in[:, :, 512:576], w_in[:, :, 448:512], w_in[:, :, 576:640],
         w_in[:, :, 652:ATTN_OLD], w_in[:, :, 640:652],
         jnp.zeros((L, D, ATTN_COLS - ATTN_OLD), F32)], axis=2).astype(BF16)
    w_gate = w_in[:, :, ATTN_OLD:].astype(BF16)
    dq, dkv = MLA_NOPE + MLA_ROPE, MLA_NOPE + MLA_V
    uq = w_uq.reshape(L, MLA_Q_RANK, MLA_HEADS, dq)
    w_uq_b = jnp.concatenate([uq[..., :MLA_NOPE].reshape(L, MLA_Q_RANK, -1),
                              uq[..., MLA_NOPE:].reshape(L, MLA_Q_RANK, -1)], axis=2).astype(BF16)
    ukv = w_ukv.reshape(L, MLA_KV_RANK, MLA_HEADS, dkv)
    w_ukv_b = jnp.concatenate([ukv[..., :MLA_NOPE].reshape(L, MLA_KV_RANK, -1),
                               ukv[..., MLA_NOPE:].reshape(L, MLA_KV_RANK, -1)], axis=2).astype(BF16)

    qs = HEAD_DIM ** -0.5 * LOG2E
    ms = dq ** -0.5 * LOG2E

    def lanes(v, reps, scale=1.0):
        r = jnp.tile(v, (1, reps)) * scale
        return jnp.pad(r, ((0, 0), (0, 4 * HEAD_DIM - r.shape[1])))

    gq = qk_gain
    gain_rows = jnp.stack([
        lanes(gq[:, QK_NSA_Q], 4, qs),
        lanes(jnp.concatenate([gq[:, QK_NSA_KS], gq[:, QK_NSA_KW]], axis=1), 1),
        lanes(gq[:, QK_SWA_Q], 4, qs),
        lanes(gq[:, QK_SWA_K], 2),
        lanes(gq[:, QK_MOBA_Q], 4, qs),
        lanes(gq[:, QK_MOBA_K], 4),
        lanes(gq[:, QK_MLA_Q], 4, ms),
        lanes(rope_gain[:, 0], 4, ms),
        lanes(gq[:, QK_MLA_K], 4),
        lanes(rope_gain[:, 1], 4),
    ], axis=1)
    bd64 = jnp.asarray(np.kron(np.eye(4), np.ones((HEAD_DIM, HEAD_DIM))) / HEAD_DIM, dtype=BF16)
    bd32 = jnp.asarray(np.kron(np.eye(4), np.ones((MLA_ROPE, MLA_ROPE))) / MLA_ROPE, dtype=BF16)
    w_branch_b = w_branch.astype(BF16)
    w_out_b = w_out.astype(BF16)
    w_router = jnp.concatenate(
        [w_coarse, w_fine, jnp.zeros((L, D, 128 - N_GROUPS - N_EXPERTS), F32)], axis=2)
    b_router = jnp.concatenate(
        [b_coarse, b_fine, jnp.zeros((L, 128 - N_GROUPS - N_EXPERTS), F32)], axis=1)
    sinks_pad = jnp.concatenate([swa_sinks, jnp.zeros((L, 128 - SWA_HEADS), F32)], axis=1)

    for l in range(L):
        m6 = mod[l].reshape(B, 6, 1, D)
        sh1, sc1, g1, sh2, sc2, g2 = (m6[:, j] for j in range(6))
        (qa, kcr, vcr, ks, vs, kw, vw, gs, qb, kb, vb, qc, kc, vc, mc, qd, kd, vd) = _proj_call(
            x, sc1, sh1, norm_gain[l, 0:1], w_attn[l], gain_rows[l], bd64, bd32, lat_gain_q[l][None],
            lat_gain_kv[l][None], w_uq_b[l], w_ukv_b[l], cos, sin)
        kcmp, vcmp = _compress_call(kcr, vcr, cmp_pe[l], cmp_w1[l], cmp_w2[l], qk_gain[l, QK_NSA_KC][None])
        o_cmp, sel = _cmp_attn_call(qa, kcmp, vcmp, slopes[0])
        o_slc = _flash_call(qa, ks, vs, slopes=slopes[0], mask=sel, mask_block=SEL_BLOCK, mask_t=True,
                            tq=256, tk=256, hpc=2, skip_tiles=True, out_dtype=F32, name="nsa_slc")
        o_win = _flash_call(qa, kw, vw, slopes=slopes[0], window=NSA_WINDOW,
                            tq=256, tk=256, hpc=2, out_dtype=F32, name="nsa_win")
        o_b = _flash_call(qb, kb, vb, slopes=slopes[1], window=SWA_WINDOW, sinks=sinks_pad[l][None],
                          tq=256, tk=256, name="swa")
        o_c = _flash_call(qc, kc, vc, slopes=slopes[2], mask=mc, mask_block=MOBA_BLOCK,
                          tq=512, tk=512, name="moba")
        o_d = _flash_call(qd, kd, vd, tq=512, tk=512, name="mla")
        xn, h2, route = _merge_call(
            x, (sc1, sh1, g1, sc2, sh2), norm_gain[l], w_gate[l], w_branch_b[l], w_out_b[l],
            o_cmp, o_slc, o_win, gs, o_b, o_c, o_d, w_router[l], b_router[l][None])
        route = route.reshape(B * S, 128)
        y2 = _moe(h2, route, w13, w2, l)
        x = _combine_call(xn, g2, route, y2)
    return x
```

```python
import functools
import math

import numpy as np
import jax
import jax.numpy as jnp
from jax import lax
from jax.experimental import pallas as pl
from jax.experimental.pallas import tpu as pltpu

F32 = jnp.float32
BF16 = jnp.bfloat16

HEAD_DIM = 64
NEG_INF = -1e30
EPS = 1e-6
NSA_HEADS = 4
CMP_BLOCK = 32
CMP_STRIDE = 16
CMP_HIDDEN = 256
SEL_BLOCK = 64
SEL_TOPK = 8
NSA_WINDOW = 512
FORCE_BONUS = 1e4
SWA_HEADS = 4
SWA_KV_HEADS = 2
SWA_WINDOW = 128
MOBA_HEADS = 4
MOBA_BLOCK = 256
MOBA_TOPK = 3
MLA_HEADS = 4
MLA_Q_RANK = 384
MLA_KV_RANK = 128
MLA_NOPE = 64
MLA_ROPE = 32
MLA_V = 64
ROPE_THETA = 10000.0
N_BRANCH = 4
BRANCH_WIDTH = 256
N_GROUPS = 4
EXPERTS_PER_GROUP = 8
N_EXPERTS = N_GROUPS * EXPERTS_PER_GROUP
D_EXPERT = 256

(GR_NSA_Q, GR_NSA_K, GR_SWA_Q, GR_SWA_K, GR_MOBA_Q, GR_MOBA_K,
 GR_MLA_QN, GR_MLA_QR, GR_MLA_KN, GR_MLA_KR) = range(10)
QK_NSA_Q, QK_NSA_KC, QK_NSA_KS, QK_NSA_KW = 0, 1, 2, 3
QK_SWA_Q, QK_SWA_K, QK_MOBA_Q, QK_MOBA_K, QK_MLA_Q, QK_MLA_K = 4, 5, 6, 7, 8, 9

ATTN_OLD = 2476
ATTN_COLS = 2560
GATE_LANE0 = 32
LOG2E = math.log2(math.e)
SUB, LANE = 8, 128
EXPERT_ROWS = 256
VMEM_LIMIT = 56 * 1024 * 1024


def _alibi_slopes():
    n = NSA_HEADS + SWA_HEADS + MOBA_HEADS

    def pow2(m):
        start = 2.0 ** (-8.0 / m)
        return [start ** (i + 1) for i in range(m)]

    c = 2 ** int(math.floor(math.log2(n)))
    s = pow2(c) + (pow2(2 * c)[0::2][: n - c] if c < n else [])
    s = -np.sort(-np.asarray(s, np.float32))
    return s.reshape(NSA_HEADS, 3).T


def _dot(a, b):
    return jnp.dot(a, b, preferred_element_type=F32)


def _dot_nt(a, b):
    return lax.dot_general(a, b, (((1,), (1,)), ((), ())), preferred_element_type=F32)


def _split(a):
    hi = a.astype(BF16)
    lo = (a - hi.astype(F32)).astype(BF16)
    return hi, lo


def _dot3(a, b):
    ah, al = _split(a)
    bh, bl = _split(b)
    return _dot(ah, bh) + (_dot(ah, bl) + _dot(al, bh))


def _dot3_nt(a, b):
    ah, al = _split(a)
    bh, bl = _split(b)
    return _dot_nt(ah, bh) + (_dot_nt(ah, bl) + _dot_nt(al, bh))


def _rms(x, g):
    return x * lax.rsqrt(jnp.mean(x * x, axis=-1, keepdims=True) + EPS) * g


def _sigmoid(x):
    return 1.0 / (1.0 + jnp.exp(-x))


AUG_MASK0 = 6


def _bf16_pieces(x, n=3):
    out, r = [], float(x)
    for _ in range(n):
        piece = float(np.asarray(r, np.float32).astype(jnp.bfloat16).astype(np.float32))
        out.append(piece)
        r -= piece
    return out


def _topk_mask(score, k, axis=1):
    n = score.shape[axis]
    iota = lax.broadcasted_iota(jnp.int32, score.shape, axis).astype(F32)
    sel = jnp.zeros(score.shape, F32)
    for _ in range(k):
        m = jnp.max(score, axis=axis, keepdims=True)
        idx = jnp.min(jnp.where(score == m, iota, float(n)), axis=axis, keepdims=True)
        hit = iota == idx
        sel = jnp.where(hit, jnp.where(m > 0.5 * NEG_INF, 1.0, 0.0), sel)
        score = jnp.where(hit, -3e38, score)
    return sel


def _ada_kernel(c_ref, w_ref, b_ref, o_ref):
    c = c_ref[...]
    a = c * _sigmoid(c)
    o_ref[0] = _dot(a, w_ref[0]) + b_ref[0]


def _ada_mod(c, w_ada, b_ada):
    L, D, D6 = w_ada.shape
    B = c.shape[0]
    tn = 1024
    return pl.pallas_call(
        _ada_kernel,
        grid=(L, D6 // tn),
        in_specs=[
            pl.BlockSpec((B, D), lambda l, j: (0, 0)),
            pl.BlockSpec((1, D, tn), lambda l, j: (l, 0, j)),
            pl.BlockSpec((1, 1, tn), lambda l, j: (l, 0, j)),
        ],
        out_specs=pl.BlockSpec((1, B, tn), lambda l, j: (l, 0, j)),
        out_shape=jax.ShapeDtypeStruct((L, B, D6), F32),
        compiler_params=pltpu.CompilerParams(vmem_limit_bytes=VMEM_LIMIT),
        name="ada_mod",
    )(c, w_ada, b_ada.reshape(L, 1, D6))


def _rms_blocks(x, bd, g):
    hi, lo = _split(x * x)
    ms = _dot(hi, bd) + _dot(lo, bd)
    return x * lax.rsqrt(ms + EPS) * g


def _proj_kernel(x_ref, sc_ref, sh_ref, ng_ref, w_ref, gr_ref, bd64_ref, bd32_ref, lgq_ref, lgkv_ref,
                 wuq_ref, wukv_ref, cos_ref, sin_ref,
                 qa_ref, kcr_ref, vcr_ref, ks_ref, vs_ref, kw_ref, vw_ref, gs_ref,
                 qb_ref, kb_ref, vb_ref, qc_ref, kc_ref, vc_ref, mc_ref, qd_ref, kd_ref, vd_ref,
                 kmean_s):
    i = pl.program_id(1)
    hd = HEAD_DIM
    x = x_ref[0]
    h = _rms(x, ng_ref[...]) * (1.0 + sc_ref[0]) + sh_ref[0]
    hb = h.astype(BF16)
    bd64 = bd64_ref[...]
    bd64h = bd64_ref[0:2 * hd, 0:2 * hd]
    bd32 = bd32_ref[...]

    def grow(r, w=4 * hd):
        return gr_ref[r:r + 1, 0:w]

    def proj(a, b):
        return _dot(hb, w_ref[:, a:b])

    def store_heads(ref, slab, n):
        for hh in range(n):
            ref[0, hh] = slab[:, hh * hd:(hh + 1) * hd].astype(BF16)

    def store_heads_t(ref, slab_t, n):
        for hh in range(n):
            ref[0, hh] = slab_t[hh * hd:(hh + 1) * hd].astype(BF16)

    store_heads(qa_ref, _rms_blocks(proj(0, 256), bd64, grow(GR_NSA_Q)), NSA_HEADS)
    p = proj(256, 640)
    kcr_ref[0] = p[:, 0:64]
    vcr_ref[0] = p[:, 64:128]
    kk = _rms_blocks(p[:, 128:256], bd64h, grow(GR_NSA_K, 2 * hd))
    ks_ref[0, 0] = kk[:, :hd].astype(BF16)
    kw_ref[0, 0] = kk[:, hd:].astype(BF16)
    vt = p[:, 256:384].T
    vs_ref[0, 0] = vt[:hd].astype(BF16)
    vw_ref[0, 0] = vt[hd:].astype(BF16)

    p = proj(640, 1152)
    store_heads(qb_ref, _rms_blocks(p[:, 0:256], bd64, grow(GR_SWA_Q)), SWA_HEADS)
    store_heads(kb_ref, _rms_blocks(p[:, 256:384], bd64h, grow(GR_SWA_K, 2 * hd)), SWA_KV_HEADS)
    store_heads_t(vb_ref, p[:, 384:512].T, SWA_KV_HEADS)

    @pl.when(i == 0)
    def _():
        kmean_s[...] = jnp.zeros(kmean_s.shape, F32)

    p = proj(1152, 1920)
    qn = _rms_blocks(p[:, 0:256], bd64, grow(GR_MOBA_Q))
    kn = _rms_blocks(p[:, 256:512], bd64, grow(GR_MOBA_K))
    store_heads(qc_ref, qn, MOBA_HEADS)
    store_heads(kc_ref, kn, MOBA_HEADS)
    store_heads_t(vc_ref, p[:, 512:768].T, MOBA_HEADS)
    nblk = kmean_s.shape[0]
    blk_iota = lax.broadcasted_iota(jnp.int32, (1, nblk), 1)
    head_of_lane = lax.broadcasted_iota(jnp.int32, (1, 4 * hd), 1) // hd
    kmeans = kmean_s[...]
    for hh in range(MOBA_HEADS):
        g = _dot3_nt(jnp.where(head_of_lane == hh, qn, 0.0), kmeans)
        g = jnp.where(blk_iota < i, g, NEG_INF)
        mc_ref[0, hh] = jnp.where(blk_iota == i, 1.0, _topk_mask(g, MOBA_TOPK))
    kmean_s[pl.ds(i, 1), :] = jnp.mean(kn, axis=0, keepdims=True)

    p = proj(1920, 2560)
    gs_ref[0] = _sigmoid(p[:, 512:640])
    cos = cos_ref[...]
    sin = sin_ref[...]
    half = MLA_ROPE // 2
    first_half = lax.broadcasted_iota(jnp.int32, (1, 4 * MLA_ROPE), 1) % MLA_ROPE < half

    def rope(v):
        swapped = jnp.where(first_half, pltpu.roll(v, 4 * MLA_ROPE - half, 1), pltpu.roll(v, half, 1))
        return v * cos + swapped * sin

    qlat = _dot(_rms(p[:, 0:384], lgq_ref[...]).astype(BF16), wuq_ref[...])
    kvlat = _dot(_rms(p[:, 384:512], lgkv_ref[...]).astype(BF16), wukv_ref[...])
    q_nope = _rms_blocks(qlat[:, 0:256], bd64, grow(GR_MLA_QN))
    q_rot = rope(_rms_blocks(qlat[:, 256:384], bd32, grow(GR_MLA_QR, 2 * hd)))
    k_nope = _rms_blocks(kvlat[:, 0:256], bd64, grow(GR_MLA_KN))
    k_rot = rope(_rms_blocks(p[:, 512:640], bd32, grow(GR_MLA_KR, 2 * hd)))[:, :MLA_ROPE]
    store_heads_t(vd_ref, kvlat[:, 256:512].T, MLA_HEADS)
    for hh in range(MLA_HEADS):
        qd_ref[0, hh] = jnp.concatenate(
            [q_nope[:, hh * hd:(hh + 1) * hd], q_rot[:, hh * MLA_ROPE:(hh + 1) * MLA_ROPE]], axis=1).astype(BF16)
        kd_ref[0, hh] = jnp.concatenate([k_nope[:, hh * hd:(hh + 1) * hd], k_rot], axis=1).astype(BF16)


def _proj_call(x, sc1, sh1, ng, w_attn, gain_rows, bd64, bd32, lgq, lgkv, wuq, wukv, cos, sin):
    B, S, D = x.shape
    tm = MOBA_BLOCK
    nblk = S // tm
    hd = HEAD_DIM

    def full(shape):
        return pl.BlockSpec(shape, lambda b, i: (0,) * len(shape))

    def heads(nh, d):
        return pl.BlockSpec((1, nh, tm, d), lambda b, i: (b, 0, i, 0))

    in_specs = [
        pl.BlockSpec((1, tm, D), lambda b, i: (b, i, 0)),
        pl.BlockSpec((1, 1, D), lambda b, i: (b, 0, 0)),
        pl.BlockSpec((1, 1, D), lambda b, i: (b, 0, 0)),
        full((1, D)),
        full((D, ATTN_COLS)),
        full(gain_rows.shape),
        full(bd64.shape),
        full(bd32.shape),
        full(lgq.shape),
        full(lgkv.shape),
        full(wuq.shape),
        full(wukv.shape),
        pl.BlockSpec((tm, 4 * MLA_ROPE), lambda b, i: (i, 0)),
        pl.BlockSpec((tm, 4 * MLA_ROPE), lambda b, i: (i, 0)),
    ]
    row64 = pl.BlockSpec((1, tm, hd), lambda b, i: (b, i, 0))
    def heads_t(nh, d):
        return pl.BlockSpec((1, nh, d, tm), lambda b, i: (b, 0, 0, i))

    out_specs = [
        heads(4, hd), row64, row64, heads(1, hd), heads_t(1, hd), heads(1, hd), heads_t(1, hd),
        pl.BlockSpec((1, tm, 128), lambda b, i: (b, i, 0)),
        heads(4, hd), heads(2, hd), heads_t(2, hd),
        heads(4, hd), heads(4, hd), heads_t(4, hd), heads(4, nblk),
        heads(4, MLA_NOPE + MLA_ROPE), heads(4, MLA_NOPE + MLA_ROPE), heads_t(4, MLA_V),
    ]

    def sd(shape, dt):
        return jax.ShapeDtypeStruct(shape, dt)

    out_shape = [
        sd((B, 4, S, hd), BF16), sd((B, S, hd), F32), sd((B, S, hd), F32),
        sd((B, 1, S, hd), BF16), sd((B, 1, hd, S), BF16), sd((B, 1, S, hd), BF16), sd((B, 1, hd, S), BF16),
        sd((B, S, 128), F32),
        sd((B, 4, S, hd), BF16), sd((B, 2, S, hd), BF16), sd((B, 2, hd, S), BF16),
        sd((B, 4, S, hd), BF16), sd((B, 4, S, hd), BF16), sd((B, 4, hd, S), BF16), sd((B, 4, S, nblk), F32),
        sd((B, 4, S, 96), BF16), sd((B, 4, S, 96), BF16), sd((B, 4, MLA_V, S), BF16),
    ]
    return pl.pallas_call(
        _proj_kernel,
        grid=(B, nblk),
        in_specs=in_specs,
        out_specs=out_specs,
        out_shape=out_shape,
        scratch_shapes=[pltpu.VMEM((nblk, MOBA_HEADS * hd), F32)],
        compiler_params=pltpu.CompilerParams(
            dimension_semantics=("arbitrary", "arbitrary"), vmem_limit_bytes=VMEM_LIMIT),
        name="proj_prep",
    )(x, sc1, sh1, ng, w_attn, gain_rows, bd64, bd32, lgq, lgkv, wuq, wukv, cos, sin)


def _compress_kernel(gk_ref, gv_ref, pe_ref, w1_ref, w2_ref, gkc_ref, kc_ref, vc_ref):
    half = w1_ref.shape[1] // 2
    outs = []
    for j, g_ref in enumerate((gk_ref, gv_ref)):
        g = g_ref[0].astype(BF16)
        top = _dot(g, w1_ref[j, :half].astype(BF16))
        bot = _dot(g, w1_ref[j, half:].astype(BF16))
        bot = jnp.concatenate([bot[1:], bot[:1]], axis=0)
        pe = jnp.broadcast_to(pe_ref[j], (8, pe_ref.shape[2]))
        bias = _dot3(pe, w1_ref[j])[0:1]
        hid = top + bot + bias
        hid = hid * _sigmoid(hid)
        outs.append(_dot(hid.astype(BF16), w2_ref[j].astype(BF16)))
    kc_ref[0] = _rms(outs[0], gkc_ref[...]).astype(BF16)
    dk = outs[1].shape[1]
    vc_ref[0] = jnp.concatenate([outs[1], outs[1]], axis=1).T[:dk].astype(BF16)


def _compress_call(kc_raw, vc_raw, pe, w1, w2, g_kc):
    B, S, dk = kc_raw.shape
    n_grp = S // CMP_STRIDE
    gk = kc_raw.reshape(B, n_grp, CMP_STRIDE * dk)
    gv = vc_raw.reshape(B, n_grp, CMP_STRIDE * dk)
    pe_flat = pe.reshape(2, 1, CMP_BLOCK * dk)
    grp_spec = pl.BlockSpec((1, n_grp, CMP_STRIDE * dk), lambda b: (b, 0, 0))
    out_spec = pl.BlockSpec((1, n_grp, dk), lambda b: (b, 0, 0))
    return pl.pallas_call(
        _compress_kernel,
        grid=(B,),
        in_specs=[
            grp_spec, grp_spec,
            pl.BlockSpec(pe_flat.shape, lambda b: (0, 0, 0)),
            pl.BlockSpec(w1.shape, lambda b: (0, 0, 0)),
            pl.BlockSpec(w2.shape, lambda b: (0, 0, 0)),
            pl.BlockSpec((1, dk), lambda b: (0, 0)),
        ],
        out_specs=[out_spec, pl.BlockSpec((1, dk, n_grp), lambda b: (b, 0, 0))],
        out_shape=[jax.ShapeDtypeStruct((B, n_grp, dk), BF16), jax.ShapeDtypeStruct((B, dk, n_grp), BF16)],
        compiler_params=pltpu.CompilerParams(vmem_limit_bytes=VMEM_LIMIT),
        name="nsa_compress",
    )(gk, gv, pe_flat, w1, w2, g_kc)


def _cmp_attn_kernel(q_ref, kc_ref, vct_ref, cover_ref, o_ref, sel_ref, *, slopes, tq, n_cmp):
    i = pl.program_id(1)
    kc = kc_ref[0]
    vct = vct_ref[0]
    ncp = kc.shape[0]
    t_full = i * tq + lax.broadcasted_iota(jnp.int32, (ncp, tq), 1)
    n_iota = lax.broadcasted_iota(jnp.int32, (ncp, tq), 0)
    dist_i = t_full - (n_iota * CMP_STRIDE + (CMP_BLOCK - 1))
    vis = (n_iota < n_cmp) & (dist_i >= 0)
    dist = dist_i.astype(F32)
    visf = vis.astype(F32)
    psum = jnp.zeros((ncp, tq), F32)
    outs = []
    for hh in range(NSA_HEADS):
        s = _dot_nt(kc, q_ref[0, hh]) - (slopes[hh] * LOG2E) * dist
        s = jnp.where(vis, s, NEG_INF)
        e = jnp.exp2(s - jnp.max(s, axis=0, keepdims=True)) * visf
        p = e / jnp.maximum(jnp.sum(e, axis=0, keepdims=True), 1e-30)
        outs.append(_dot(vct, p.astype(BF16)))
        psum = psum + p
    o_ref[0] = jnp.concatenate(outs, axis=0).T
    ph, plo = _split(psum)
    cover = cover_ref[...]
    p_slc = _dot(cover, ph) + _dot(cover, plo)
    n_sel = cover.shape[0]
    cur = (i * tq + lax.broadcasted_iota(jnp.int32, (1, tq), 1)) // SEL_BLOCK
    j = lax.broadcasted_iota(jnp.int32, (n_sel, 1), 0)
    forced = jnp.where(j == 0, 1.0, jnp.where(j == cur, 1.0, jnp.where(j == cur - 1, 1.0, 0.0)))
    score = jnp.where(j <= cur, p_slc + FORCE_BONUS * forced, NEG_INF)
    sel_ref[0, 0] = _topk_mask(score, min(SEL_TOPK, n_sel), axis=0)


def _cmp_attn_call(qa, kc, vc, slopes):
    B, H, S, dk = qa.shape
    ncp = kc.shape[1]
    n_cmp = (S - CMP_BLOCK) // CMP_STRIDE + 1
    n_sel = S // SEL_BLOCK
    tq = 256
    starts = np.arange(ncp) * CMP_STRIDE
    jb = np.arange(n_sel) * SEL_BLOCK
    cover = ((starts[:, None] < jb[None, :] + SEL_BLOCK) & (starts[:, None] + CMP_BLOCK > jb[None, :])
             & (np.arange(ncp)[:, None] < n_cmp))
    cover = jnp.asarray(cover.T.astype(np.float32), dtype=BF16)
    return pl.pallas_call(
        functools.partial(_cmp_attn_kernel, slopes=tuple(float(s) for s in slopes), tq=tq, n_cmp=n_cmp),
        grid=(B, S // tq),
        in_specs=[
            pl.BlockSpec((1, H, tq, dk), lambda b, i: (b, 0, i, 0)),
            pl.BlockSpec((1, ncp, dk), lambda b, i: (b, 0, 0)),
            pl.BlockSpec((1, dk, ncp), lambda b, i: (b, 0, 0)),
            pl.BlockSpec((n_sel, ncp), lambda b, i: (0, 0)),
        ],
        out_specs=[
            pl.BlockSpec((1, tq, H * dk), lambda b, i: (b, i, 0)),
            pl.BlockSpec((1, 1, n_sel, tq), lambda b, i: (b, 0, 0, i)),
        ],
        out_shape=[jax.ShapeDtypeStruct((B, S, H * dk), F32),
                   jax.ShapeDtypeStruct((B, 1, n_sel, S), F32)],
        compiler_params=pltpu.CompilerParams(vmem_limit_bytes=VMEM_LIMIT),
        name="nsa_cmp_attn",
    )(qa, kc, vc, cover)


def _flash_kernel(*refs, H, G, hpc, tq, tk, slopes, window, mask_block, mask_per_head, mask_t, has_sink, skip):
    refs = list(refs)
    q_ref, k_ref, v_ref = refs[:3]
    pos = 3
    mask_ref = sink_ref = None
    if mask_block:
        mask_ref = refs[pos]
        pos += 1
    if has_sink:
        sink_ref = refs[pos]
        pos += 1
    o_ref, m_s, acc_s = refs[pos:pos + 3]
    need_s = refs[pos + 3] if skip else None
    use_aug = slopes is not None or bool(mask_block)
    R = H // G
    C = H // hpc
    M = hpc * tq
    dv = v_ref.shape[-2]
    i = pl.program_id(1)
    q0 = i * tq
    hi = q0 // tk
    lo = jnp.maximum(q0 - (window - 1), 0) // tk if window else 0

    def rel_pos():
        r_row = jnp.concatenate([lax.broadcasted_iota(jnp.int32, (1, tq), 1)] * hpc, axis=1)
        return (r_row - lax.broadcasted_iota(jnp.int32, (tk, M), 0)).astype(F32)

    dq = q_ref.shape[-1]
    n_mask = mask_ref.shape[-2 if mask_t else -1] if mask_block else 0
    aug_w = LANE if dq + AUG_MASK0 + n_mask <= LANE else 2 * LANE
    aug0 = dq
    lane_a = lax.broadcasted_iota(jnp.int32, (1, aug_w), 1)
    feature_lane = lane_a < dq

    def slope_lanes(hh):
        row = jnp.zeros((1, aug_w), F32)
        if slopes is not None:
            for n, piece in enumerate(_bf16_pieces(slopes[hh])):
                row = jnp.where((lane_a == aug0 + n) | (lane_a == aug0 + n + 3), piece, row)
        return row

    def widen(a):
        return jnp.concatenate([a, jnp.zeros((a.shape[0], aug_w - dq), a.dtype)], axis=1)

    qs, slope_rows, bms = [], [], []
    for c in range(C):
        heads = [c * hpc + r for r in range(hpc)]
        if hpc == 1:
            qs.append(q_ref[0, heads[0]])
        else:
            qs.append(jnp.concatenate([q_ref[0, hh] for hh in heads], axis=0))
        if slopes is not None:
            slope_rows.append(jnp.concatenate([jnp.full((1, tq), slopes[hh], F32) for hh in heads], axis=1))
        if mask_block:
            parts = [mask_ref[0, hh if mask_per_head else 0] for hh in heads]
            bms.append(jnp.concatenate(parts, axis=1) if mask_t else jnp.concatenate(parts, axis=0).T)
    m_s[...] = jnp.full(m_s.shape, NEG_INF, F32)
    acc_s[...] = jnp.zeros(acc_s.shape, F32)
    ones_rows = jnp.ones((acc_s.shape[1] - dv, tk), BF16)

    if mask_block:
        nblk = bms[0].shape[0]
    if need_s is not None:
        col = bms[0]
        for bm in bms[1:]:
            col = jnp.maximum(col, bm)
        col = jnp.max(col, axis=1, keepdims=True)
        bpt = tk // mask_block
        for jt in range(nblk // bpt):
            need_s[jt] = jnp.max(col[jt * bpt:(jt + 1) * bpt]).astype(jnp.int32)

    if use_aug:
        for c in range(C):
            qa = jnp.concatenate(
                [jnp.broadcast_to(slope_lanes(c * hpc + r), (tq, aug_w)) for r in range(hpc)], axis=0)
            if mask_block:
                place = jnp.where(
                    lax.broadcasted_iota(jnp.int32, (nblk, aug_w), 0) + (aug0 + AUG_MASK0) == lane_a,
                    1.0, 0.0).astype(BF16)
                off_sel = ((bms[c] - 1.0) * -NEG_INF).astype(BF16)
                qa = qa + lax.dot_general(off_sel, place, (((0,), (0,)), ((), ())), preferred_element_type=F32)
            qs[c] = jnp.where(feature_lane, widen(qs[c]), qa.astype(BF16))

    def tile_scores(j):
        k0 = pl.multiple_of(j * tk, tk)
        if use_aug:
            t_key = lax.broadcasted_iota(jnp.int32, (tk, aug_w), 0) + k0
            k_aug = jnp.zeros((tk, aug_w), F32)
            if mask_block:
                key_blk = lax.shift_right_logical(t_key, int(math.log2(mask_block)))
                k_aug = jnp.where(key_blk + (aug0 + AUG_MASK0) == lane_a, 1.0, 0.0)
            if slopes is not None:
                t_hi = (lax.shift_right_logical(t_key, 8) * 256).astype(F32)
                t_lo = (t_key & 255).astype(F32)
                k_aug = jnp.where(lane_a < aug0 + 3, t_hi, jnp.where(lane_a < aug0 + 6, t_lo, k_aug))
            k_aug = k_aug.astype(BF16)
        k_cats = {}
        scores = []
        for c in range(C):
            g = (c * hpc) // R
            if g not in k_cats:
                k = k_ref[0, g, pl.ds(k0, tk), :]
                k_cats[g] = jnp.where(feature_lane, widen(k), k_aug) if use_aug else k
            scores.append(_dot_nt(k_cats[g], qs[c]))
        return scores

    def tile_update(j, edge, scores):
        k0 = pl.multiple_of(j * tk, tk)
        valid = None
        if edge:
            dist = rel_pos() + (q0 - k0).astype(F32)
            valid = dist >= 0.0
            if window:
                valid = valid & (dist < float(window))
        probs, alphas = [], []
        for g in range(C):
            s = scores[g]
            if valid is not None:
                s = jnp.where(valid, s, NEG_INF)
            m_prev = m_s[g]
            m_new = jnp.maximum(m_prev, jnp.max(s, axis=0, keepdims=True))
            alpha = jnp.exp2(m_prev - m_new)
            p = jnp.exp2(s - m_new)
            m_s[g] = m_new
            probs.append(p.astype(BF16))
            alphas.append(alpha)
        for c in range(C):
            v = jnp.concatenate([v_ref[0, (c * hpc) // R, :, pl.ds(k0, tk)], ones_rows], axis=0)
            acc_s[c] = alphas[c] * acc_s[c] + _dot(v, probs[c])

    def tile(j, edge):
        tile_update(j, edge, tile_scores(j))

    def body(step, carry):
        j = hi - step
        is_edge = step == 0
        if window:
            is_edge = is_edge | (q0 - j * tk + (tq - 1) >= window)
        run = (step == 0) | (need_s[j] > 0) if need_s is not None else None

        def when(c):
            return pl.when(c if run is None else c & run)

        @when(is_edge)
        def _():
            tile(j, True)

        @when(jnp.logical_not(is_edge))
        def _():
            tile(j, False)

        return carry

    lax.fori_loop(0, hi - lo + 1, body, 0)

    for c in range(C):
        m = m_s[c]
        l = acc_s[c, dv:dv + 1]
        acc = acc_s[c, 0:dv]
        if has_sink:
            sk = LOG2E * jnp.concatenate(
                [jnp.broadcast_to(sink_ref[:, c * hpc + r:c * hpc + r + 1], (1, tq)) for r in range(hpc)], axis=1)
            if slopes is not None:
                t_q = jnp.concatenate([lax.broadcasted_iota(jnp.int32, (1, tq), 1) + q0] * hpc, axis=1)
                m = m - slope_rows[c] * t_q.astype(F32)
            m_f = jnp.maximum(m, sk)
            a = jnp.exp2(m - m_f)
            l = l * a + jnp.exp2(sk - m_f)
            acc = acc * a
        out = acc / l
        for r in range(hpc):
            hh = c * hpc + r
            o_ref[0, :, hh * dv:(hh + 1) * dv] = out[:, r * tq:(r + 1) * tq].T.astype(o_ref.dtype)


def _flash_call(q, k, v, *, slopes=None, window=0, mask=None, mask_block=0, mask_t=False, sinks=None,
                tq=128, tk=256, hpc=None, skip_tiles=False, out_dtype=BF16, name="flash"):
    B, H, S, dq = q.shape
    G = k.shape[1]
    dv = v.shape[-2]
    tk = min(tk, S)
    assert tk % tq == 0 and S % tk == 0
    R = H // G
    hpc = R if hpc is None else hpc
    assert R % hpc == 0
    C = H // hpc
    in_specs = [
        pl.BlockSpec((1, H, tq, dq), lambda b, i: (b, 0, i, 0)),
        pl.BlockSpec((1, G, S, dq), lambda b, i: (b, 0, 0, 0)),
        pl.BlockSpec((1, G, dv, S), lambda b, i: (b, 0, 0, 0)),
    ]
    args = [q, k, v]
    mask_per_head = False
    if mask is not None:
        hm = mask.shape[1]
        mask_per_head = hm > 1
        if mask_t:
            in_specs.append(pl.BlockSpec((1, hm, mask.shape[2], tq), lambda b, i: (b, 0, 0, i)))
        else:
            in_specs.append(pl.BlockSpec((1, hm, tq, mask.shape[3]), lambda b, i: (b, 0, i, 0)))
        args.append(mask)
    if sinks is not None:
        in_specs.append(pl.BlockSpec(sinks.shape, lambda b, i: (0, 0)))
        args.append(sinks)
    kern = functools.partial(
        _flash_kernel, H=H, G=G, hpc=hpc, tq=tq, tk=tk,
        slopes=None if slopes is None else tuple(float(s) * LOG2E for s in slopes),
        window=window, mask_block=mask_block if mask is not None else 0,
        mask_per_head=mask_per_head, mask_t=mask_t, has_sink=sinks is not None,
        skip=skip_tiles)
    ones_rows = 16
    scratch = [pltpu.VMEM((C, 1, hpc * tq), F32), pltpu.VMEM((C, dv + ones_rows, hpc * tq), F32)]
    if mask is not None:
        assert tk % mask_block == 0
    if skip_tiles:
        assert mask is not None
        scratch.append(pltpu.SMEM((S // tk,), jnp.int32))
    return pl.pallas_call(
        kern,
        grid=(B, S // tq),
        in_specs=in_specs,
        out_specs=pl.BlockSpec((1, tq, H * dv), lambda b, i: (b, i, 0)),
        out_shape=jax.ShapeDtypeStruct((B, S, H * dv), out_dtype),
        scratch_shapes=scratch,
        compiler_params=pltpu.CompilerParams(vmem_limit_bytes=VMEM_LIMIT),
        name=name,
    )(*args)


def _merge_kernel(x_ref, sc1_ref, sh1_ref, g1_ref, sc2_ref, sh2_ref, ng_ref, wg_ref, wb_ref, wo_ref,
                  ocmp_ref, oslc_ref, owin_ref, gs_ref, ob_ref, oc_ref, od_ref, wr_ref, br_ref,
                  xo_ref, h2_ref, rt_ref):
    hd = HEAD_DIM
    x = x_ref[0]
    ng = ng_ref[...]
    h = _rms(x, ng[0:1]) * (1.0 + sc1_ref[0]) + sh1_ref[0]
    hb = h.astype(BF16)
    gs = gs_ref[0]
    ocmp = ocmp_ref[0]
    oslc = oslc_ref[0]
    owin = owin_ref[0]
    parts = []
    for hh in range(NSA_HEADS):
        c0 = GATE_LANE0 + 3 * hh
        sl = slice(hh * hd, (hh + 1) * hd)
        parts.append(gs[:, c0:c0 + 1] * ocmp[:, sl] + gs[:, c0 + 1:c0 + 2] * oslc[:, sl]
                     + gs[:, c0 + 2:c0 + 3] * owin[:, sl])
    o_a = jnp.concatenate(parts, axis=1).astype(BF16)
    branches = (o_a, ob_ref[0], oc_ref[0], od_ref[0])
    D = x.shape[1]
    mixed = None
    for n in range(N_BRANCH):
        gate = _sigmoid(_dot(hb, wg_ref[:, n * D:(n + 1) * D]))
        term = gate * _dot(branches[n], wb_ref[n])
        mixed = term if mixed is None else mixed + term
    xn = x + g1_ref[0] * _dot(mixed.astype(BF16), wo_ref[...])
    xo_ref[0] = xn
    h2 = _rms(xn, ng[1:2]) * (1.0 + sc2_ref[0]) + sh2_ref[0]
    for c in range(SUB):
        h2_ref[pl.ds(c, h2.shape[0], stride=SUB), :] = h2[:, c * LANE:(c + 1) * LANE]

    logits = _dot3(h2, wr_ref[...]) + br_ref[...]
    lane = lax.broadcasted_iota(jnp.int32, logits.shape, 1)
    lanef = lane.astype(F32)
    is_c = lane < N_GROUPS
    lc = jnp.where(is_c, logits, NEG_INF)
    mc = jnp.max(lc, axis=-1, keepdims=True)
    grp = jnp.min(jnp.where(lc == mc, lanef, 1e9), axis=-1, keepdims=True)
    p_grp = 1.0 / jnp.sum(jnp.where(is_c, jnp.exp(lc - mc), 0.0), axis=-1, keepdims=True)
    e_lane = lanef - float(N_GROUPS)
    in_grp = (lane >= N_GROUPS) & (lane < N_GROUPS + N_EXPERTS) & (
        jnp.floor(e_lane / EXPERTS_PER_GROUP) == grp)
    lf = jnp.where(in_grp, logits, NEG_INF)
    m1 = jnp.max(lf, axis=-1, keepdims=True)
    i1 = jnp.min(jnp.where(lf == m1, e_lane, 1e9), axis=-1, keepdims=True)
    lf2 = jnp.where(e_lane == i1, NEG_INF, lf)
    m2 = jnp.max(lf2, axis=-1, keepdims=True)
    i2 = jnp.min(jnp.where(lf2 == m2, e_lane, 1e9), axis=-1, keepdims=True)
    e2 = jnp.exp(m2 - m1)
    w1 = p_grp / (1.0 + e2)
    w2 = p_grp * e2 / (1.0 + e2)
    rt = jnp.where(lane == 0, i1, jnp.where(lane == 1, i2, jnp.where(lane == 2, w1, jnp.where(lane == 3, w2, 0.0))))
    rt_ref[0] = rt


def _merge_call(x, mods, ng, wg, wb, wo, ocmp, oslc, owin, gs, ob, oc, od, wr, br):
    B, S, D = x.shape
    tm = 512
    sc1, sh1, g1, sc2, sh2 = mods

    def full(a):
        return pl.BlockSpec(a.shape, lambda b, i: (0,) * a.ndim)

    modspec = pl.BlockSpec((1, 1, D), lambda b, i: (b, 0, 0))
    row = lambda w: pl.BlockSpec((1, tm, w), lambda b, i: (b, i, 0))
    in_specs = [row(D), modspec, modspec, modspec, modspec, modspec, full(ng), full(wg), full(wb), full(wo),
                row(256), row(256), row(256), row(128), row(256), row(256), row(256), full(wr), full(br)]
    return pl.pallas_call(
        _merge_kernel,
        grid=(B, S // tm),
        in_specs=in_specs,
        out_specs=[row(D), pl.BlockSpec((tm * SUB, LANE), lambda b, i: (b * (S // tm) + i, 0)), row(128)],
        out_shape=[jax.ShapeDtypeStruct((B, S, D), F32), jax.ShapeDtypeStruct((B * S * SUB, LANE), F32),
                   jax.ShapeDtypeStruct((B, S, 128), F32)],
        compiler_params=pltpu.CompilerParams(vmem_limit_bytes=VMEM_LIMIT),
        name="merge_router",
    )(x, sc1, sh1, g1, sc2, sh2, ng, wg, wb, wo, ocmp, oslc, owin, gs, ob, oc, od, wr, br)


def _expert_kernel(blk_e_ref, tok_ref, dst_ref, h2_hbm, w13_ref, w2_ref, y2_hbm, xbuf, ybuf, gsem, ssem):
    del blk_e_ref
    i = pl.program_id(0)
    nb = pl.num_programs(0)
    slot = i % 2
    rows = EXPERT_ROWS

    def tile_rows(t):
        return pl.ds(pl.multiple_of(t * SUB, SUB), SUB)

    def gather_copy(blk, s, r):
        tok = tok_ref[blk * rows + r]
        return pltpu.make_async_copy(h2_hbm.at[tile_rows(tok), :], xbuf.at[s, tile_rows(r), :], gsem.at[s])

    def scatter_copy(blk, s, r):
        d = dst_ref[blk * rows + r]
        return pltpu.make_async_copy(ybuf.at[s, tile_rows(r), :], y2_hbm.at[tile_rows(d), :], ssem.at[s])

    def loop_rows(fn):
        def body(r, carry):
            fn(r)
            return carry
        lax.fori_loop(0, rows, body, 0, unroll=8)

    def wait_gather(s):
        pltpu.make_async_copy(h2_hbm.at[pl.ds(0, rows * SUB), :], xbuf.at[s], gsem.at[s]).wait()

    def wait_scatter(s):
        pltpu.make_async_copy(ybuf.at[s], y2_hbm.at[pl.ds(0, rows * SUB), :], ssem.at[s]).wait()

    def compute():
        ab = None
        for c in range(0, SUB, 2):
            xc = jnp.concatenate([xbuf[slot, pl.ds(c, rows, stride=SUB), :],
                                  xbuf[slot, pl.ds(c + 1, rows, stride=SUB), :]], axis=1).astype(BF16)
            part = _dot(xc, w13_ref[0, 0, c * LANE:(c + 2) * LANE, :].astype(BF16))
            ab = part if ab is None else ab + part
        de = ab.shape[1] // 2
        a = ab[:, :de]
        b = ab[:, de:]
        act = ((a * _sigmoid(a)) * b).astype(BF16)
        for c in range(0, SUB, 2):
            y = _dot(act, w2_ref[0, 0, :, c * LANE:(c + 2) * LANE].astype(BF16))
            ybuf[slot, pl.ds(c, rows, stride=SUB), :] = y[:, :LANE]
            ybuf[slot, pl.ds(c + 1, rows, stride=SUB), :] = y[:, LANE:]

    def step(scatter_prev):
        nxt = jnp.minimum(i + 1, nb - 1)
        for r in range(rows):
            if scatter_prev:
                scatter_copy(i - 1, 1 - slot, r).start()
            gather_copy(nxt, 1 - slot, r).start()
        compute()

    @pl.when(i == 0)
    def _():
        loop_rows(lambda r: gather_copy(0, 0, r).start())

    wait_gather(slot)

    @pl.when(i >= 2)
    def _():
        wait_scatter(slot)

    @pl.when(i == 0)
    def _():
        step(False)

    @pl.when(i > 0)
    def _():
        step(True)

    @pl.when(i == nb - 1)
    def _():
        loop_rows(lambda r: scatter_copy(i, slot, r).start())
        wait_gather(1 - slot)

        @pl.when(nb >= 2)
        def _():
            wait_scatter(1 - slot)
        wait_scatter(slot)


def _expert_call(h2, blk_e, row_tok, row_dst, w13, w2, layer):
    D = SUB * LANE
    assert h2.shape[1] == LANE and w13.shape[2] == D
    n_blocks = blk_e.shape[0]
    de2 = w13.shape[-1]
    grid_spec = pltpu.PrefetchScalarGridSpec(
        num_scalar_prefetch=3,
        grid=(n_blocks,),
        in_specs=[
            pl.BlockSpec(memory_space=pl.ANY),
            pl.BlockSpec((1, 1, D, de2), lambda i, be, rt, rd: (layer, be[i], 0, 0)),
            pl.BlockSpec((1, 1, de2 // 2, D), lambda i, be, rt, rd: (layer, be[i], 0, 0)),
        ],
        out_specs=pl.BlockSpec(memory_space=pl.ANY),
        scratch_shapes=[pltpu.VMEM((2, EXPERT_ROWS * SUB, LANE), F32), pltpu.VMEM((2, EXPERT_ROWS * SUB, LANE), F32),
                        pltpu.SemaphoreType.DMA((2,)), pltpu.SemaphoreType.DMA((2,))],
    )
    return pl.pallas_call(
        _expert_kernel,
        grid_spec=grid_spec,
        out_shape=jax.ShapeDtypeStruct((row_tok.shape[0] * SUB, LANE), F32),
        compiler_params=pltpu.CompilerParams(
            dimension_semantics=("arbitrary",), vmem_limit_bytes=VMEM_LIMIT),
        name="experts",
    )(blk_e, row_tok, row_dst, h2, w13, w2)


def _combine_kernel(xn_ref, g2_ref, rt_ref, y0_ref, y1_ref, o_ref):
    rt = rt_ref[...]
    tm = xn_ref.shape[0]
    w0 = rt[:, 2:3]
    w1 = rt[:, 3:4]
    g2 = g2_ref[0]
    for c in range(SUB):
        cols = slice(c * LANE, (c + 1) * LANE)
        y = w0 * y0_ref[pl.ds(c, tm, stride=SUB), :] + w1 * y1_ref[pl.ds(c, tm, stride=SUB), :]
        o_ref[:, cols] = xn_ref[:, cols] + g2[:, cols] * y


def _combine_call(xn, g2, route, y2):
    B, S, D = xn.shape
    tm = 512
    spb = S // tm
    nt = B * spb
    return pl.pallas_call(
        _combine_kernel,
        grid=(B, spb),
        in_specs=[
            pl.BlockSpec((tm, D), lambda b, i: (b * spb + i, 0)),
            pl.BlockSpec((1, 1, D), lambda b, i: (b, 0, 0)),
            pl.BlockSpec((tm, 128), lambda b, i: (b * spb + i, 0)),
            pl.BlockSpec((tm * SUB, LANE), lambda b, i: (b * spb + i, 0)),
            pl.BlockSpec((tm * SUB, LANE), lambda b, i: (nt + b * spb + i, 0)),
        ],
        out_specs=pl.BlockSpec((tm, D), lambda b, i: (b * spb + i, 0)),
        out_shape=jax.ShapeDtypeStruct((B * S, D), F32),
        compiler_params=pltpu.CompilerParams(vmem_limit_bytes=VMEM_LIMIT),
        name="moe_combine",
    )(xn.reshape(B * S, D), g2, route, y2, y2).reshape(B, S, D)


def _moe(h2, route, w13, w2, layer):
    N = h2.shape[0] // SUB
    K = 2
    E = N_EXPERTS
    rows = EXPERT_ROWS
    flat_e = route[:, 0:2].astype(jnp.int32).reshape(-1)
    order = jnp.argsort(flat_e).astype(jnp.int32)
    counts = jnp.sum((flat_e[:, None] == jnp.arange(E)[None, :]).astype(jnp.int32), axis=0)
    padded = (counts + rows - 1) // rows * rows
    pend = jnp.cumsum(padded)
    pstart = pend - padded
    start = jnp.cumsum(counts) - counts
    n_blocks = (N * K) // rows + E
    R = n_blocks * rows
    blk_e = jnp.minimum(
        jnp.sum((jnp.arange(n_blocks)[:, None] * rows >= pend[None, :]).astype(jnp.int32), axis=1), E - 1)
    pos = jnp.arange(R, dtype=jnp.int32).reshape(n_blocks, rows)
    local = pos - pstart[blk_e][:, None]
    is_real = ((pos < pend[E - 1]) & (local < counts[blk_e][:, None])).reshape(R)
    rank = jnp.clip(start[blk_e][:, None] + local, 0, N * K - 1).reshape(R)
    pair = order[rank]
    flat = pos.reshape(R)
    real_before = jnp.cumsum(is_real.astype(jnp.int32)) - is_real.astype(jnp.int32)
    row_tok = jnp.where(is_real, pair // K, flat % N).astype(jnp.int32)
    row_dst = jnp.where(is_real, (pair % K) * N + pair // K, N * K + flat - real_before).astype(jnp.int32)
    return _expert_call(h2, blk_e.astype(jnp.int32), row_tok, row_dst, w13, w2, layer)


def kernel(x, c, w_ada, b_ada, norm_gain, w_in, qk_gain, cmp_pe, cmp_w1, cmp_w2, swa_sinks,
           lat_gain_q, lat_gain_kv, rope_gain, w_uq, w_ukv, w_branch, w_out,
           w_coarse, b_coarse, w_fine, b_fine, w13, w2):
    B, S, D = x.shape
    L = w_in.shape[0]
    assert S % MOBA_BLOCK == 0 and D == 1024
    slopes = _alibi_slopes()

    half = MLA_ROPE // 2
    inv = ROPE_THETA ** (-jnp.arange(half, dtype=F32) / half)
    ang = jnp.arange(S).astype(F32)[:, None] * inv[None, :]
    cos = jnp.tile(jnp.cos(ang), (1, 8))
    sin = jnp.tile(jnp.concatenate([-jnp.sin(ang), jnp.sin(ang)], axis=1), (1, 4))

    mod = _ada_mod(c, w_ada, b_ada)

    w_attn = jnp.concatenate(
        [w_in[:, :, :448], w_in[:, :, 512:576], w_in[:, :, 448:512], w_in[:, :, 576:640],
         w_in[:, :, 652:ATTN_OLD], w_in[:, :, 640:652],
         jnp.zeros((L, D, ATTN_COLS - ATTN_OLD), F32)], axis=2).astype(BF16)
    w_gate = w_in[:, :, ATTN_OLD:].astype(BF16)
    dq, dkv = MLA_NOPE + MLA_ROPE, MLA_NOPE + MLA_V
    uq = w_uq.reshape(L, MLA_Q_RANK, MLA_HEADS, dq)
    w_uq_b = jnp.concatenate([uq[..., :MLA_NOPE].reshape(L, MLA_Q_RANK, -1),
                              uq[..., MLA_NOPE:].reshape(L, MLA_Q_RANK, -1)], axis=2).astype(BF16)
    ukv = w_ukv.reshape(L, MLA_KV_RANK, MLA_HEADS, dkv)
    w_ukv_b = jnp.concatenate([ukv[..., :MLA_NOPE].reshape(L, MLA_KV_RANK, -1),
                               ukv[..., MLA_NOPE:].reshape(L, MLA_KV_RANK, -1)], axis=2).astype(BF16)

    qs = HEAD_DIM ** -0.5 * LOG2E
    ms = dq ** -0.5 * LOG2E

    def lanes(v, reps, scale=1.0):
        r = jnp.tile(v, (1, reps)) * scale
        return jnp.pad(r, ((0, 0), (0, 4 * HEAD_DIM - r.shape[1])))

    gq = qk_gain
    gain_rows = jnp.stack([
        lanes(gq[:, QK_NSA_Q], 4, qs),
        lanes(jnp.concatenate([gq[:, QK_NSA_KS], gq[:, QK_NSA_KW]], axis=1), 1),
        lanes(gq[:, QK_SWA_Q], 4, qs),
        lanes(gq[:, QK_SWA_K], 2),
        lanes(gq[:, QK_MOBA_Q], 4, qs),
        lanes(gq[:, QK_MOBA_K], 4),
        lanes(gq[:, QK_MLA_Q], 4, ms),
        lanes(rope_gain[:, 0], 4, ms),
        lanes(gq[:, QK_MLA_K], 4),
        lanes(rope_gain[:, 1], 4),
    ], axis=1)
    bd64 = jnp.asarray(np.kron(np.eye(4), np.ones((HEAD_DIM, HEAD_DIM))) / HEAD_DIM, dtype=BF16)
    bd32 = jnp.asarray(np.kron(np.eye(4), np.ones((MLA_ROPE, MLA_ROPE))) / MLA_ROPE, dtype=BF16)
    w_branch_b = w_branch.astype(BF16)
    w_out_b = w_out.astype(BF16)
    w_router = jnp.concatenate(
        [w_coarse, w_fine, jnp.zeros((L, D, 128 - N_GROUPS - N_EXPERTS), F32)], axis=2)
    b_router = jnp.concatenate(
        [b_coarse, b_fine, jnp.zeros((L, 128 - N_GROUPS - N_EXPERTS), F32)], axis=1)
    sinks_pad = jnp.concatenate([swa_sinks, jnp.zeros((L, 128 - SWA_HEADS), F32)], axis=1)

    for l in range(L):
        m6 = mod[l].reshape(B, 6, 1, D)
        sh1, sc1, g1, sh2, sc2, g2 = (m6[:, j] for j in range(6))
        (qa, kcr, vcr, ks, vs, kw, vw, gs, qb, kb, vb, qc, kc, vc, mc, qd, kd, vd) = _proj_call(
            x, sc1, sh1, norm_gain[l, 0:1], w_attn[l], gain_rows[l], bd64, bd32, lat_gain_q[l][None],
            lat_gain_kv[l][None], w_uq_b[l], w_ukv_b[l], cos, sin)
        kcmp, vcmp = _compress_call(kcr, vcr, cmp_pe[l], cmp_w1[l], cmp_w2[l], qk_gain[l, QK_NSA_KC][None])
        o_cmp, sel = _cmp_attn_call(qa, kcmp, vcmp, slopes[0])
        o_slc = _flash_call(qa, ks, vs, slopes=slopes[0], mask=sel, mask_block=SEL_BLOCK, mask_t=True,
                            tq=256, tk=256, hpc=2, skip_tiles=True, out_dtype=F32, name="nsa_slc")
        o_win = _flash_call(qa, kw, vw, slopes=slopes[0], window=NSA_WINDOW,
                            tq=256, tk=256, hpc=2, out_dtype=F32, name="nsa_win")
        o_b = _flash_call(qb, kb, vb, slopes=slopes[1], window=SWA_WINDOW, sinks=sinks_pad[l][None],
                          tq=256, tk=256, name="swa")
        o_c = _flash_call(qc, kc, vc, slopes=slopes[2], mask=mc, mask_block=MOBA_BLOCK,
                          tq=512, tk=512, name="moba")
        o_d = _flash_call(qd, kd, vd, tq=512, tk=512, name="mla")
        xn, h2, route = _merge_call(
            x, (sc1, sh1, g1, sc2, sh2), norm_gain[l], w_gate[l], w_branch_b[l], w_out_b[l],
            o_cmp, o_slc, o_win, gs, o_b, o_c, o_d, w_router[l], b_router[l][None])
        route = route.reshape(B * S, 128)
        y2 = _moe(h2, route, w13, w2, l)
        x = _combine_call(xn, g2, route, y2)
    return x
```

```python
import functools
import math

import numpy as np
import jax
import jax.numpy as jnp
from jax import lax
from jax.experimental import pallas as pl
from jax.experimental.pallas import tpu as pltpu

F32 = jnp.float32
BF16 = jnp.bfloat16

HEAD_DIM = 64
NEG_INF = -1e30
EPS = 1e-6
NSA_HEADS = 4
CMP_BLOCK = 32
CMP_STRIDE = 16
CMP_HIDDEN = 256
SEL_BLOCK = 64
SEL_TOPK = 8
NSA_WINDOW = 512
FORCE_BONUS = 1e4
SWA_HEADS = 4
SWA_KV_HEADS = 2
SWA_WINDOW = 128
MOBA_HEADS = 4
MOBA_BLOCK = 256
MOBA_TOPK = 3
MLA_HEADS = 4
MLA_Q_RANK = 384
MLA_KV_RANK = 128
MLA_NOPE = 64
MLA_ROPE = 32
MLA_V = 64
ROPE_THETA = 10000.0
N_BRANCH = 4
BRANCH_WIDTH = 256
N_GROUPS = 4
EXPERTS_PER_GROUP = 8
N_EXPERTS = N_GROUPS * EXPERTS_PER_GROUP
D_EXPERT = 256

(GR_NSA_Q, GR_NSA_K, GR_SWA_Q, GR_SWA_K, GR_MOBA_Q, GR_MOBA_K,
 GR_MLA_QN, GR_MLA_QR, GR_MLA_KN, GR_MLA_KR) = range(10)
QK_NSA_Q, QK_NSA_KC, QK_NSA_KS, QK_NSA_KW = 0, 1, 2, 3
QK_SWA_Q, QK_SWA_K, QK_MOBA_Q, QK_MOBA_K, QK_MLA_Q, QK_MLA_K = 4, 5, 6, 7, 8, 9

ATTN_OLD = 2476
ATTN_COLS = 2560
GATE_LANE0 = 32
LOG2E = math.log2(math.e)
SUB, LANE = 8, 128
EXPERT_ROWS = 128
VMEM_LIMIT = 56 * 1024 * 1024


def _alibi_slopes():
    n = NSA_HEADS + SWA_HEADS + MOBA_HEADS

    def pow2(m):
        start = 2.0 ** (-8.0 / m)
        return [start ** (i + 1) for i in range(m)]

    c = 2 ** int(math.floor(math.log2(n)))
    s = pow2(c) + (pow2(2 * c)[0::2][: n - c] if c < n else [])
    s = -np.sort(-np.asarray(s, np.float32))
    return s.reshape(NSA_HEADS, 3).T


def _dot(a, b):
    return jnp.dot(a, b, preferred_element_type=F32)


def _dot_nt(a, b):
    return lax.dot_general(a, b, (((1,), (1,)), ((), ())), preferred_element_type=F32)


def _split(a):
    hi = a.astype(BF16)
    lo = (a - hi.astype(F32)).astype(BF16)
    return hi, lo


def _dot3(a, b):
    ah, al = _split(a)
    bh, bl = _split(b)
    return _dot(ah, bh) + (_dot(ah, bl) + _dot(al, bh))


def _dot3_nt(a, b):
    ah, al = _split(a)
    bh, bl = _split(b)
    return _dot_nt(ah, bh) + (_dot_nt(ah, bl) + _dot_nt(al, bh))


def _rms(x, g):
    return x * lax.rsqrt(jnp.mean(x * x, axis=-1, keepdims=True) + EPS) * g


def _sigmoid(x):
    return 1.0 / (1.0 + jnp.exp(-x))


AUG_MASK0 = 6


def _bf16_pieces(x, n=3):
    out, r = [], float(x)
    for _ in range(n):
        piece = float(np.asarray(r, np.float32).astype(jnp.bfloat16).astype(np.float32))
        out.append(piece)
        r -= piece
    return out


def _topk_mask(score, k, axis=1):
    n = score.shape[axis]
    iota = lax.broadcasted_iota(jnp.int32, score.shape, axis).astype(F32)
    sel = jnp.zeros(score.shape, F32)
    for _ in range(k):
        m = jnp.max(score, axis=axis, keepdims=True)
        idx = jnp.min(jnp.where(score == m, iota, float(n)), axis=axis, keepdims=True)
        hit = iota == idx
        sel = jnp.where(hit, jnp.where(m > 0.5 * NEG_INF, 1.0, 0.0), sel)
        score = jnp.where(hit, -3e38, score)
    return sel


def _ada_kernel(c_ref, w_ref, b_ref, o_ref):
    c = c_ref[...]
    a = c * _sigmoid(c)
    o_ref[0] = _dot(a, w_ref[0]) + b_ref[0]


def _ada_mod(c, w_ada, b_ada):
    L, D, D6 = w_ada.shape
    B = c.shape[0]
    tn = 1024
    return pl.pallas_call(
        _ada_kernel,
        grid=(L, D6 // tn),
        in_specs=[
            pl.BlockSpec((B, D), lambda l, j: (0, 0)),
            pl.BlockSpec((1, D, tn), lambda l, j: (l, 0, j)),
            pl.BlockSpec((1, 1, tn), lambda l, j: (l, 0, j)),
        ],
        out_specs=pl.BlockSpec((1, B, tn), lambda l, j: (l, 0, j)),
        out_shape=jax.ShapeDtypeStruct((L, B, D6), F32),
        compiler_params=pltpu.CompilerParams(vmem_limit_bytes=VMEM_LIMIT),
        name="ada_mod",
    )(c, w_ada, b_ada.reshape(L, 1, D6))


def _rms_blocks(x, bd, g):
    hi, lo = _split(x * x)
    ms = _dot(hi, bd) + _dot(lo, bd)
    return x * lax.rsqrt(ms + EPS) * g


def _proj_kernel(x_ref, sc_ref, sh_ref, ng_ref, w_ref, gr_ref, bd64_ref, bd32_ref, lgq_ref, lgkv_ref,
                 wuq_ref, wukv_ref, cos_ref, sin_ref,
                 qa_ref, kcr_ref, vcr_ref, ks_ref, vs_ref, kw_ref, vw_ref, gs_ref,
                 qb_ref, kb_ref, vb_ref, qc_ref, kc_ref, vc_ref, mc_ref, qd_ref, kd_ref, vd_ref,
                 kmean_s):
    i = pl.program_id(1)
    hd = HEAD_DIM
    x = x_ref[0]
    h = _rms(x, ng_ref[...]) * (1.0 + sc_ref[0]) + sh_ref[0]
    hb = h.astype(BF16)
    bd64 = bd64_ref[...]
    bd64h = bd64_ref[0:2 * hd, 0:2 * hd]
    bd32 = bd32_ref[...]

    def grow(r, w=4 * hd):
        return gr_ref[r:r + 1, 0:w]

    def proj(a, b):
        return _dot(hb, w_ref[:, a:b])

    def store_heads(ref, slab, n):
        for hh in range(n):
            ref[0, hh] = slab[:, hh * hd:(hh + 1) * hd].astype(BF16)

    def store_heads_t(ref, slab_t, n):
        for hh in range(n):
            ref[0, hh] = slab_t[hh * hd:(hh + 1) * hd].astype(BF16)

    store_heads(qa_ref, _rms_blocks(proj(0, 256), bd64, grow(GR_NSA_Q)), NSA_HEADS)
    p = proj(256, 640)
    kcr_ref[0] = p[:, 0:64]
    vcr_ref[0] = p[:, 64:128]
    kk = _rms_blocks(p[:, 128:256], bd64h, grow(GR_NSA_K, 2 * hd))
    ks_ref[0, 0] = kk[:, :hd].astype(BF16)
    kw_ref[0, 0] = kk[:, hd:].astype(BF16)
    vt = p[:, 256:384].T
    vs_ref[0, 0] = vt[:hd].astype(BF16)
    vw_ref[0, 0] = vt[hd:].astype(BF16)

    p = proj(640, 1152)
    store_heads(qb_ref, _rms_blocks(p[:, 0:256], bd64, grow(GR_SWA_Q)), SWA_HEADS)
    store_heads(kb_ref, _rms_blocks(p[:, 256:384], bd64h, grow(GR_SWA_K, 2 * hd)), SWA_KV_HEADS)
    store_heads_t(vb_ref, p[:, 384:512].T, SWA_KV_HEADS)

    @pl.when(i == 0)
    def _():
        kmean_s[...] = jnp.zeros(kmean_s.shape, F32)

    p = proj(1152, 1920)
    qn = _rms_blocks(p[:, 0:256], bd64, grow(GR_MOBA_Q))
    kn = _rms_blocks(p[:, 256:512], bd64, grow(GR_MOBA_K))
    store_heads(qc_ref, qn, MOBA_HEADS)
    store_heads(kc_ref, kn, MOBA_HEADS)
    store_heads_t(vc_ref, p[:, 512:768].T, MOBA_HEADS)
    nblk = kmean_s.shape[0]
    blk_iota = lax.broadcasted_iota(jnp.int32, (1, nblk), 1)
    head_of_lane = lax.broadcasted_iota(jnp.int32, (1, 4 * hd), 1) // hd
    kmeans = kmean_s[...]
    for hh in range(MOBA_HEADS):
        g = _dot3_nt(jnp.where(head_of_lane == hh, qn, 0.0), kmeans)
        g = jnp.where(blk_iota < i, g, NEG_INF)
        mc_ref[0, hh] = jnp.where(blk_iota == i, 1.0, _topk_mask(g, MOBA_TOPK))
    kmean_s[pl.ds(i, 1), :] = jnp.mean(kn, axis=0, keepdims=True)

    p = proj(1920, 2560)
    gs_ref[0] = _sigmoid(p[:, 512:640])
    cos = cos_ref[...]
    sin = sin_ref[...]
    half = MLA_ROPE // 2
    first_half = lax.broadcasted_iota(jnp.int32, (1, 4 * MLA_ROPE), 1) % MLA_ROPE < half

    def rope(v):
        swapped = jnp.where(first_half, pltpu.roll(v, 4 * MLA_ROPE - half, 1), pltpu.roll(v, half, 1))
        return v * cos + swapped * sin

    qlat = _dot(_rms(p[:, 0:384], lgq_ref[...]).astype(BF16), wuq_ref[...])
    kvlat = _dot(_rms(p[:, 384:512], lgkv_ref[...]).astype(BF16), wukv_ref[...])
    q_nope = _rms_blocks(qlat[:, 0:256], bd64, grow(GR_MLA_QN))
    q_rot = rope(_rms_blocks(qlat[:, 256:384], bd32, grow(GR_MLA_QR, 2 * hd)))
    k_nope = _rms_blocks(kvlat[:, 0:256], bd64, grow(GR_MLA_KN))
    k_rot = rope(_rms_blocks(p[:, 512:640], bd32, grow(GR_MLA_KR, 2 * hd)))[:, :MLA_ROPE]
    store_heads_t(vd_ref, kvlat[:, 256:512].T, MLA_HEADS)
    for hh in range(MLA_HEADS):
        qd_ref[0, hh] = jnp.concatenate(
            [q_nope[:, hh * hd:(hh + 1) * hd], q_rot[:, hh * MLA_ROPE:(hh + 1) * MLA_ROPE]], axis=1).astype(BF16)
        kd_ref[0, hh] = jnp.concatenate([k_nope[:, hh * hd:(hh + 1) * hd], k_rot], axis=1).astype(BF16)


def _proj_call(x, sc1, sh1, ng, w_attn, gain_rows, bd64, bd32, lgq, lgkv, wuq, wukv, cos, sin):
    B, S, D = x.shape
    tm = MOBA_BLOCK
    nblk = S // tm
    hd = HEAD_DIM

    def full(shape):
        return pl.BlockSpec(shape, lambda b, i: (0,) * len(shape))

    def heads(nh, d):
        return pl.BlockSpec((1, nh, tm, d), lambda b, i: (b, 0, i, 0))

    in_specs = [
        pl.BlockSpec((1, tm, D), lambda b, i: (b, i, 0)),
        pl.BlockSpec((1, 1, D), lambda b, i: (b, 0, 0)),
        pl.BlockSpec((1, 1, D), lambda b, i: (b, 0, 0)),
        full((1, D)),
        full((D, ATTN_COLS)),
        full(gain_rows.shape),
        full(bd64.shape),
        full(bd32.shape),
        full(lgq.shape),
        full(lgkv.shape),
        full(wuq.shape),
        full(wukv.shape),
        pl.BlockSpec((tm, 4 * MLA_ROPE), lambda b, i: (i, 0)),
        pl.BlockSpec((tm, 4 * MLA_ROPE), lambda b, i: (i, 0)),
    ]
    row64 = pl.BlockSpec((1, tm, hd), lambda b, i: (b, i, 0))
    def heads_t(nh, d):
        return pl.BlockSpec((1, nh, d, tm), lambda b, i: (b, 0, 0, i))

    out_specs = [
        heads(4, hd), row64, row64, heads(1, hd), heads_t(1, hd), heads(1, hd), heads_t(1, hd),
        pl.BlockSpec((1, tm, 128), lambda b, i: (b, i, 0)),
        heads(4, hd), heads(2, hd), heads_t(2, hd),
        heads(4, hd), heads(4, hd), heads_t(4, hd), heads(4, nblk),
        heads(4, MLA_NOPE + MLA_ROPE), heads(4, MLA_NOPE + MLA_ROPE), heads_t(4, MLA_V),
    ]

    def sd(shape, dt):
        return jax.ShapeDtypeStruct(shape, dt)

    out_shape = [
        sd((B, 4, S, hd), BF16), sd((B, S, hd), F32), sd((B, S, hd), F32),
        sd((B, 1, S, hd), BF16), sd((B, 1, hd, S), BF16), sd((B, 1, S, hd), BF16), sd((B, 1, hd, S), BF16),
        sd((B, S, 128), F32),
        sd((B, 4, S, hd), BF16), sd((B, 2, S, hd), BF16), sd((B, 2, hd, S), BF16),
        sd((B, 4, S, hd), BF16), sd((B, 4, S, hd), BF16), sd((B, 4, hd, S), BF16), sd((B, 4, S, nblk), F32),
        sd((B, 4, S, 96), BF16), sd((B, 4, S, 96), BF16), sd((B, 4, MLA_V, S), BF16),
    ]
    return pl.pallas_call(
        _proj_kernel,
        grid=(B, nblk),
        in_specs=in_specs,
        out_specs=out_specs,
        out_shape=out_shape,
        scratch_shapes=[pltpu.VMEM((nblk, MOBA_HEADS * hd), F32)],
        compiler_params=pltpu.CompilerParams(
            dimension_semantics=("arbitrary", "arbitrary"), vmem_limit_bytes=VMEM_LIMIT),
        name="proj_prep",
    )(x, sc1, sh1, ng, w_attn, gain_rows, bd64, bd32, lgq, lgkv, wuq, wukv, cos, sin)


def _compress_kernel(gk_ref, gv_ref, pe_ref, w1_ref, w2_ref, gkc_ref, kc_ref, vc_ref):
    half = w1_ref.shape[1] // 2
    outs = []
    for j, g_ref in enumerate((gk_ref, gv_ref)):
        g = g_ref[0].astype(BF16)
        top = _dot(g, w1_ref[j, :half].astype(BF16))
        bot = _dot(g, w1_ref[j, half:].astype(BF16))
        bot = jnp.concatenate([bot[1:], bot[:1]], axis=0)
        pe = jnp.broadcast_to(pe_ref[j], (8, pe_ref.shape[2]))
        bias = _dot3(pe, w1_ref[j])[0:1]
        hid = top + bot + bias
        hid = hid * _sigmoid(hid)
        outs.append(_dot(hid.astype(BF16), w2_ref[j].astype(BF16)))
    kc_ref[0] = _rms(outs[0], gkc_ref[...]).astype(BF16)
    dk = outs[1].shape[1]
    vc_ref[0] = jnp.concatenate([outs[1], outs[1]], axis=1).T[:dk].astype(BF16)


def _compress_call(kc_raw, vc_raw, pe, w1, w2, g_kc):
    B, S, dk = kc_raw.shape
    n_grp = S // CMP_STRIDE
    gk = kc_raw.reshape(B, n_grp, CMP_STRIDE * dk)
    gv = vc_raw.reshape(B, n_grp, CMP_STRIDE * dk)
    pe_flat = pe.reshape(2, 1, CMP_BLOCK * dk)
    grp_spec = pl.BlockSpec((1, n_grp, CMP_STRIDE * dk), lambda b: (b, 0, 0))
    out_spec = pl.BlockSpec((1, n_grp, dk), lambda b: (b, 0, 0))
    return pl.pallas_call(
        _compress_kernel,
        grid=(B,),
        in_specs=[
            grp_spec, grp_spec,
            pl.BlockSpec(pe_flat.shape, lambda b: (0, 0, 0)),
            pl.BlockSpec(w1.shape, lambda b: (0, 0, 0)),
            pl.BlockSpec(w2.shape, lambda b: (0, 0, 0)),
            pl.BlockSpec((1, dk), lambda b: (0, 0)),
        ],
        out_specs=[out_spec, pl.BlockSpec((1, dk, n_grp), lambda b: (b, 0, 0))],
        out_shape=[jax.ShapeDtypeStruct((B, n_grp, dk), BF16), jax.ShapeDtypeStruct((B, dk, n_grp), BF16)],
        compiler_params=pltpu.CompilerParams(vmem_limit_bytes=VMEM_LIMIT),
        name="nsa_compress",
    )(gk, gv, pe_flat, w1, w2, g_kc)


def _cmp_attn_kernel(q_ref, kc_ref, vct_ref, cover_ref, o_ref, sel_ref, *, slopes, tq, n_cmp):
    i = pl.program_id(1)
    kc = kc_ref[0]
    vct = vct_ref[0]
    ncp = kc.shape[0]
    t_full = i * tq + lax.broadcasted_iota(jnp.int32, (ncp, tq), 1)
    n_iota = lax.broadcasted_iota(jnp.int32, (ncp, tq), 0)
    dist_i = t_full - (n_iota * CMP_STRIDE + (CMP_BLOCK - 1))
    vis = (n_iota < n_cmp) & (dist_i >= 0)
    dist = dist_i.astype(F32)
    visf = vis.astype(F32)
    psum = jnp.zeros((ncp, tq), F32)
    outs = []
    for hh in range(NSA_HEADS):
        s = _dot_nt(kc, q_ref[0, hh]) - (slopes[hh] * LOG2E) * dist
        s = jnp.where(vis, s, NEG_INF)
        e = jnp.exp2(s - jnp.max(s, axis=0, keepdims=True)) * visf
        p = e / jnp.maximum(jnp.sum(e, axis=0, keepdims=True), 1e-30)
        outs.append(_dot(vct, p.astype(BF16)))
        psum = psum + p
    o_ref[0] = jnp.concatenate(outs, axis=0).T
    ph, plo = _split(psum)
    cover = cover_ref[...]
    p_slc = _dot(cover, ph) + _dot(cover, plo)
    n_sel = cover.shape[0]
    cur = (i * tq + lax.broadcasted_iota(jnp.int32, (1, tq), 1)) // SEL_BLOCK
    j = lax.broadcasted_iota(jnp.int32, (n_sel, 1), 0)
    forced = jnp.where(j == 0, 1.0, jnp.where(j == cur, 1.0, jnp.where(j == cur - 1, 1.0, 0.0)))
    score = jnp.where(j <= cur, p_slc + FORCE_BONUS * forced, NEG_INF)
    sel_ref[0, 0] = _topk_mask(score, min(SEL_TOPK, n_sel), axis=0)


def _cmp_attn_call(qa, kc, vc, slopes):
    B, H, S, dk = qa.shape
    ncp = kc.shape[1]
    n_cmp = (S - CMP_BLOCK) // CMP_STRIDE + 1
    n_sel = S // SEL_BLOCK
    tq = 256
    starts = np.arange(ncp) * CMP_STRIDE
    jb = np.arange(n_sel) * SEL_BLOCK
    cover = ((starts[:, None] < jb[None, :] + SEL_BLOCK) & (starts[:, None] + CMP_BLOCK > jb[None, :])
             & (np.arange(ncp)[:, None] < n_cmp))
    cover = jnp.asarray(cover.T.astype(np.float32), dtype=BF16)
    return pl.pallas_call(
        functools.partial(_cmp_attn_kernel, slopes=tuple(float(s) for s in slopes), tq=tq, n_cmp=n_cmp),
        grid=(B, S // tq),
        in_specs=[
            pl.BlockSpec((1, H, tq, dk), lambda b, i: (b, 0, i, 0)),
            pl.BlockSpec((1, ncp, dk), lambda b, i: (b, 0, 0)),
            pl.BlockSpec((1, dk, ncp), lambda b, i: (b, 0, 0)),
            pl.BlockSpec((n_sel, ncp), lambda b, i: (0, 0)),
        ],
        out_specs=[
            pl.BlockSpec((1, tq, H * dk), lambda b, i: (b, i, 0)),
            pl.BlockSpec((1, 1, n_sel, tq), lambda b, i: (b, 0, 0, i)),
        ],
        out_shape=[jax.ShapeDtypeStruct((B, S, H * dk), F32),
                   jax.ShapeDtypeStruct((B, 1, n_sel, S), F32)],
        compiler_params=pltpu.CompilerParams(vmem_limit_bytes=VMEM_LIMIT),
        name="nsa_cmp_attn",
    )(qa, kc, vc, cover)


def _flash_kernel(*refs, H, G, hpc, tq, tk, slopes, window, mask_block, mask_per_head, mask_t, has_sink, skip):
    refs = list(refs)
    q_ref, k_ref, v_ref = refs[:3]
    pos = 3
    mask_ref = sink_ref = None
    if mask_block:
        mask_ref = refs[pos]
        pos += 1
    if has_sink:
        sink_ref = refs[pos]
        pos += 1
    o_ref, m_s, acc_s = refs[pos:pos + 3]
    need_s = refs[pos + 3] if skip else None
    use_aug = slopes is not None or bool(mask_block)
    R = H // G
    C = H // hpc
    M = hpc * tq
    dv = v_ref.shape[-2]
    i = pl.program_id(1)
    q0 = i * tq
    hi = q0 // tk
    lo = jnp.maximum(q0 - (window - 1), 0) // tk if window else 0

    def rel_pos():
        r_row = jnp.concatenate([lax.broadcasted_iota(jnp.int32, (1, tq), 1)] * hpc, axis=1)
        return (r_row - lax.broadcasted_iota(jnp.int32, (tk, M), 0)).astype(F32)

    dq = q_ref.shape[-1]
    n_mask = mask_ref.shape[-2 if mask_t else -1] if mask_block else 0
    aug_w = LANE if dq + AUG_MASK0 + n_mask <= LANE else 2 * LANE
    aug0 = dq
    lane_a = lax.broadcasted_iota(jnp.int32, (1, aug_w), 1)
    feature_lane = lane_a < dq

    def slope_lanes(hh):
        row = jnp.zeros((1, aug_w), F32)
        if slopes is not None:
            for n, piece in enumerate(_bf16_pieces(slopes[hh])):
                row = jnp.where((lane_a == aug0 + n) | (lane_a == aug0 + n + 3), piece, row)
        return row

    def widen(a):
        return jnp.concatenate([a, jnp.zeros((a.shape[0], aug_w - dq), a.dtype)], axis=1)

    qs, slope_rows, bms = [], [], []
    for c in range(C):
        heads = [c * hpc + r for r in range(hpc)]
        if hpc == 1:
            qs.append(q_ref[0, heads[0]])
        else:
            qs.append(jnp.concatenate([q_ref[0, hh] for hh in heads], axis=0))
        if slopes is not None:
            slope_rows.append(jnp.concatenate([jnp.full((1, tq), slopes[hh], F32) for hh in heads], axis=1))
        if mask_block:
            parts = [mask_ref[0, hh if mask_per_head else 0] for hh in heads]
            bms.append(jnp.concatenate(parts, axis=1) if mask_t else jnp.concatenate(parts, axis=0).T)
    m_s[...] = jnp.full(m_s.shape, NEG_INF, F32)
    acc_s[...] = jnp.zeros(acc_s.shape, F32)
    ones_rows = jnp.ones((acc_s.shape[1] - dv, tk), BF16)

    if mask_block:
        nblk = bms[0].shape[0]
    if need_s is not None:
        col = bms[0]
        for bm in bms[1:]:
            col = jnp.maximum(col, bm)
        col = jnp.max(col, axis=1, keepdims=True)
        bpt = tk // mask_block
        for jt in range(nblk // bpt):
            need_s[jt] = jnp.max(col[jt * bpt:(jt + 1) * bpt]).astype(jnp.int32)

    if use_aug:
        for c in range(C):
            qa = jnp.concatenate(
                [jnp.broadcast_to(slope_lanes(c * hpc + r), (tq, aug_w)) for r in range(hpc)], axis=0)
            if mask_block:
                place = jnp.where(
                    lax.broadcasted_iota(jnp.int32, (nblk, aug_w), 0) + (aug0 + AUG_MASK0) == lane_a,
                    1.0, 0.0).astype(BF16)
                off_sel = ((bms[c] - 1.0) * -NEG_INF).astype(BF16)
                qa = qa + lax.dot_general(off_sel, place, (((0,), (0,)), ((), ())), preferred_element_type=F32)
            qs[c] = jnp.where(feature_lane, widen(qs[c]), qa.astype(BF16))

    def tile_scores(j):
        k0 = pl.multiple_of(j * tk, tk)
        if use_aug:
            t_key = lax.broadcasted_iota(jnp.int32, (tk, aug_w), 0) + k0
            k_aug = jnp.zeros((tk, aug_w), F32)
            if mask_block:
                key_blk = lax.shift_right_logical(t_key, int(math.log2(mask_block)))
                k_aug = jnp.where(key_blk + (aug0 + AUG_MASK0) == lane_a, 1.0, 0.0)
            if slopes is not None:
                t_hi = (lax.shift_right_logical(t_key, 8) * 256).astype(F32)
                t_lo = (t_key & 255).astype(F32)
                k_aug = jnp.where(lane_a < aug0 + 3, t_hi, jnp.where(lane_a < aug0 + 6, t_lo, k_aug))
            k_aug = k_aug.astype(BF16)
        k_cats = {}
        scores = []
        for c in range(C):
            g = (c * hpc) // R
            if g not in k_cats:
                k = k_ref[0, g, pl.ds(k0, tk), :]
                k_cats[g] = jnp.where(feature_lane, widen(k), k_aug) if use_aug else k
            scores.append(_dot_nt(k_cats[g], qs[c]))
        return scores

    def tile_update(j, edge, scores):
        k0 = pl.multiple_of(j * tk, tk)
        valid = None
        if edge:
            dist = rel_pos() + (q0 - k0).astype(F32)
            valid = dist >= 0.0
            if window:
                valid = valid & (dist < float(window))
        probs, alphas = [], []
        for g in range(C):
            s = scores[g]
            if valid is not None:
                s = jnp.where(valid, s, NEG_INF)
            m_prev = m_s[g]
            m_new = jnp.maximum(m_prev, jnp.max(s, axis=0, keepdims=True))
            alpha = jnp.exp2(m_prev - m_new)
            p = jnp.exp2(s - m_new)
            m_s[g] = m_new
            probs.append(p.astype(BF16))
            alphas.append(alpha)
        for c in range(C):
            v = jnp.concatenate([v_ref[0, (c * hpc) // R, :, pl.ds(k0, tk)], ones_rows], axis=0)
            acc_s[c] = alphas[c] * acc_s[c] + _dot(v, probs[c])

    def tile(j, edge):
        tile_update(j, edge, tile_scores(j))

    def body(step, carry):
        j = hi - step
        is_edge = step == 0
        if window:
            is_edge = is_edge | (q0 - j * tk + (tq - 1) >= window)
        run = (step == 0) | (need_s[j] > 0) if need_s is not None else None

        def when(c):
            return pl.when(c if run is None else c & run)

        @when(is_edge)
        def _():
            tile(j, True)

        @when(jnp.logical_not(is_edge))
        def _():
            tile(j, False)

        return carry

    lax.fori_loop(0, hi - lo + 1, body, 0)

    for c in range(C):
        m = m_s[c]
        l = acc_s[c, dv:dv + 1]
        acc = acc_s[c, 0:dv]
        if has_sink:
            sk = LOG2E * jnp.concatenate(
                [jnp.broadcast_to(sink_ref[:, c * hpc + r:c * hpc + r + 1], (1, tq)) for r in range(hpc)], axis=1)
            if slopes is not None:
                t_q = jnp.concatenate([lax.broadcasted_iota(jnp.int32, (1, tq), 1) + q0] * hpc, axis=1)
                m = m - slope_rows[c] * t_q.astype(F32)
            m_f = jnp.maximum(m, sk)
            a = jnp.exp2(m - m_f)
            l = l * a + jnp.exp2(sk - m_f)
            acc = acc * a
        out = acc / l
        for r in range(hpc):
            hh = c * hpc + r
            o_ref[0, :, hh * dv:(hh + 1) * dv] = out[:, r * tq:(r + 1) * tq].T.astype(o_ref.dtype)


def _flash_call(q, k, v, *, slopes=None, window=0, mask=None, mask_block=0, mask_t=False, sinks=None,
                tq=128, tk=256, hpc=None, skip_tiles=False, out_dtype=BF16, name="flash"):
    B, H, S, dq = q.shape
    G = k.shape[1]
    dv = v.shape[-2]
    tk = min(tk, S)
    assert tk % tq == 0 and S % tk == 0
    R = H // G
    hpc = R if hpc is None else hpc
    assert R % hpc == 0
    C = H // hpc
    in_specs = [
        pl.BlockSpec((1, H, tq, dq), lambda b, i: (b, 0, i, 0)),
        pl.BlockSpec((1, G, S, dq), lambda b, i: (b, 0, 0, 0)),
        pl.BlockSpec((1, G, dv, S), lambda b, i: (b, 0, 0, 0)),
    ]
    args = [q, k, v]
    mask_per_head = False
    if mask is not None:
        hm = mask.shape[1]
        mask_per_head = hm > 1
        if mask_t:
            in_specs.append(pl.BlockSpec((1, hm, mask.shape[2], tq), lambda b, i: (b, 0, 0, i)))
        else:
            in_specs.append(pl.BlockSpec((1, hm, tq, mask.shape[3]), lambda b, i: (b, 0, i, 0)))
        args.append(mask)
    if sinks is not None:
        in_specs.append(pl.BlockSpec(sinks.shape, lambda b, i: (0, 0)))
        args.append(sinks)
    kern = functools.partial(
        _flash_kernel, H=H, G=G, hpc=hpc, tq=tq, tk=tk,
        slopes=None if slopes is None else tuple(float(s) * LOG2E for s in slopes),
        window=window, mask_block=mask_block if mask is not None else 0,
        mask_per_head=mask_per_head, mask_t=mask_t, has_sink=sinks is not None,
        skip=skip_tiles)
    ones_rows = 16
    scratch = [pltpu.VMEM((C, 1, hpc * tq), F32), pltpu.VMEM((C, dv + ones_rows, hpc * tq), F32)]
    if mask is not None:
        assert tk % mask_block == 0
    if skip_tiles:
        assert mask is not None
        scratch.append(pltpu.SMEM((S // tk,), jnp.int32))
    return pl.pallas_call(
        kern,
        grid=(B, S // tq),
        in_specs=in_specs,
        out_specs=pl.BlockSpec((1, tq, H * dv), lambda b, i: (b, i, 0)),
        out_shape=jax.ShapeDtypeStruct((B, S, H * dv), out_dtype),
        scratch_shapes=scratch,
        compiler_params=pltpu.CompilerParams(vmem_limit_bytes=VMEM_LIMIT),
        name=name,
    )(*args)


def _merge_kernel(x_ref, sc1_ref, sh1_ref, g1_ref, sc2_ref, sh2_ref, ng_ref, wg_ref, wb_ref, wo_ref,
                  ocmp_ref, oslc_ref, owin_ref, gs_ref, ob_ref, oc_ref, od_ref, wr_ref, br_ref,
                  xo_ref, h2_ref, rt_ref):
    hd = HEAD_DIM
    x = x_ref[0]
    ng = ng_ref[...]
    h = _rms(x, ng[0:1]) * (1.0 + sc1_ref[0]) + sh1_ref[0]
    hb = h.astype(BF16)
    gs = gs_ref[0]
    ocmp = ocmp_ref[0]
    oslc = oslc_ref[0]
    owin = owin_ref[0]
    parts = []
    for hh in range(NSA_HEADS):
        c0 = GATE_LANE0 + 3 * hh
        sl = slice(hh * hd, (hh + 1) * hd)
        parts.append(gs[:, c0:c0 + 1] * ocmp[:, sl] + gs[:, c0 + 1:c0 + 2] * oslc[:, sl]
                     + gs[:, c0 + 2:c0 + 3] * owin[:, sl])
    o_a = jnp.concatenate(parts, axis=1).astype(BF16)
    branches = (o_a, ob_ref[0], oc_ref[0], od_ref[0])
    D = x.shape[1]
    mixed = None
    for n in range(N_BRANCH):
        gate = _sigmoid(_dot(hb, wg_ref[:, n * D:(n + 1) * D]))
        term = gate * _dot(branches[n], wb_ref[n])
        mixed = term if mixed is None else mixed + term
    xn = x + g1_ref[0] * _dot(mixed.astype(BF16), wo_ref[...])
    xo_ref[0] = xn
    h2 = _rms(xn, ng[1:2]) * (1.0 + sc2_ref[0]) + sh2_ref[0]
    for c in range(SUB):
        h2_ref[pl.ds(c, h2.shape[0], stride=SUB), :] = h2[:, c * LANE:(c + 1) * LANE]

    logits = _dot3(h2, wr_ref[...]) + br_ref[...]
    lane = lax.broadcasted_iota(jnp.int32, logits.shape, 1)
    lanef = lane.astype(F32)
    is_c = lane < N_GROUPS
    lc = jnp.where(is_c, logits, NEG_INF)
    mc = jnp.max(lc, axis=-1, keepdims=True)
    grp = jnp.min(jnp.where(lc == mc, lanef, 1e9), axis=-1, keepdims=True)
    p_grp = 1.0 / jnp.sum(jnp.where(is_c, jnp.exp(lc - mc), 0.0), axis=-1, keepdims=True)
    e_lane = lanef - float(N_GROUPS)
    in_grp = (lane >= N_GROUPS) & (lane < N_GROUPS + N_EXPERTS) & (
        jnp.floor(e_lane / EXPERTS_PER_GROUP) == grp)
    lf = jnp.where(in_grp, logits, NEG_INF)
    m1 = jnp.max(lf, axis=-1, keepdims=True)
    i1 = jnp.min(jnp.where(lf == m1, e_lane, 1e9), axis=-1, keepdims=True)
    lf2 = jnp.where(e_lane == i1, NEG_INF, lf)
    m2 = jnp.max(lf2, axis=-1, keepdims=True)
    i2 = jnp.min(jnp.where(lf2 == m2, e_lane, 1e9), axis=-1, keepdims=True)
    e2 = jnp.exp(m2 - m1)
    w1 = p_grp / (1.0 + e2)
    w2 = p_grp * e2 / (1.0 + e2)
    rt = jnp.where(lane == 0, i1, jnp.where(lane == 1, i2, jnp.where(lane == 2, w1, jnp.where(lane == 3, w2, 0.0))))
    rt_ref[0] = rt


def _merge_call(x, mods, ng, wg, wb, wo, ocmp, oslc, owin, gs, ob, oc, od, wr, br):
    B, S, D = x.shape
    tm = 512
    sc1, sh1, g1, sc2, sh2 = mods

    def full(a):
        return pl.BlockSpec(a.shape, lambda b, i: (0,) * a.ndim)

    modspec = pl.BlockSpec((1, 1, D), lambda b, i: (b, 0, 0))
    row = lambda w: pl.BlockSpec((1, tm, w), lambda b, i: (b, i, 0))
    in_specs = [row(D), modspec, modspec, modspec, modspec, modspec, full(ng), full(wg), full(wb), full(wo),
                row(256), row(256), row(256), row(128), row(256), row(256), row(256), full(wr), full(br)]
    return pl.pallas_call(
        _merge_kernel,
        grid=(B, S // tm),
        in_specs=in_specs,
        out_specs=[row(D), pl.BlockSpec((tm * SUB, LANE), lambda b, i: (b * (S // tm) + i, 0)), row(128)],
        out_shape=[jax.ShapeDtypeStruct((B, S, D), F32), jax.ShapeDtypeStruct((B * S * SUB, LANE), F32),
                   jax.ShapeDtypeStruct((B, S, 128), F32)],
        compiler_params=pltpu.CompilerParams(vmem_limit_bytes=VMEM_LIMIT),
        name="merge_router",
    )(x, sc1, sh1, g1, sc2, sh2, ng, wg, wb, wo, ocmp, oslc, owin, gs, ob, oc, od, wr, br)


def _expert_kernel(blk_e_ref, tok_ref, dst_ref, h2_hbm, w13_ref, w2_ref, y2_hbm, xbuf, ybuf, gsem, ssem):
    del blk_e_ref
    i = pl.program_id(0)
    nb = pl.num_programs(0)
    slot = i % 2
    rows = EXPERT_ROWS

    def tile_rows(t):
        return pl.ds(pl.multiple_of(t * SUB, SUB), SUB)

    def gather_copy(blk, s, r):
        tok = tok_ref[blk * rows + r]
        return pltpu.make_async_copy(h2_hbm.at[tile_rows(tok), :], xbuf.at[s, tile_rows(r), :], gsem.at[s])

    def scatter_copy(blk, s, r):
        d = dst_ref[blk * rows + r]
        return pltpu.make_async_copy(ybuf.at[s, tile_rows(r), :], y2_hbm.at[tile_rows(d), :], ssem.at[s])

    def loop_rows(fn):
        def body(r, carry):
            fn(r)
            return carry
        lax.fori_loop(0, rows, body, 0, unroll=8)

    def wait_gather(s):
        pltpu.make_async_copy(h2_hbm.at[pl.ds(0, rows * SUB), :], xbuf.at[s], gsem.at[s]).wait()

    def wait_scatter(s):
        pltpu.make_async_copy(ybuf.at[s], y2_hbm.at[pl.ds(0, rows * SUB), :], ssem.at[s]).wait()

    def compute():
        ab = None
        for c in range(0, SUB, 2):
            xc = jnp.concatenate([xbuf[slot, pl.ds(c, rows, stride=SUB), :],
                                  xbuf[slot, pl.ds(c + 1, rows, stride=SUB), :]], axis=1).astype(BF16)
            part = _dot(xc, w13_ref[0, 0, c * LANE:(c + 2) * LANE, :].astype(BF16))
            ab = part if ab is None else ab + part
        de = ab.shape[1] // 2
        a = ab[:, :de]
        b = ab[:, de:]
        act = ((a * _sigmoid(a)) * b).astype(BF16)
        for c in range(0, SUB, 2):
            y = _dot(act, w2_ref[0, 0, :, c * LANE:(c + 2) * LANE].astype(BF16))
            ybuf[slot, pl.ds(c, rows, stride=SUB), :] = y[:, :LANE]
            ybuf[slot, pl.ds(c + 1, rows, stride=SUB), :] = y[:, LANE:]

    def step(scatter_prev):
        nxt = jnp.minimum(i + 1, nb - 1)
        for r in range(rows):
            if scatter_prev:
                scatter_copy(i - 1, 1 - slot, r).start()
            gather_copy(nxt, 1 - slot, r).start()
        compute()

    @pl.when(i == 0)
    def _():
        loop_rows(lambda r: gather_copy(0, 0, r).start())

    wait_gather(slot)

    @pl.when(i >= 2)
    def _():
        wait_scatter(slot)

    @pl.when(i == 0)
    def _():
        step(False)

    @pl.when(i > 0)
    def _():
        step(True)

    @pl.when(i == nb - 1)
    def _():
        loop_rows(lambda r: scatter_copy(i, slot, r).start())
        wait_gather(1 - slot)

        @pl.when(nb >= 2)
        def _():
            wait_scatter(1 - slot)
        wait_scatter(slot)


def _expert_call(h2, blk_e, row_tok, row_dst, w13, w2, layer):
    D = SUB * LANE
    assert h2.shape[1] == LANE and w13.shape[2] == D
    n_blocks = blk_e.shape[0]
    de2 = w13.shape[-1]
    grid_spec = pltpu.PrefetchScalarGridSpec(
        num_scalar_prefetch=3,
        grid=(n_blocks,),
        in_specs=[
            pl.BlockSpec(memory_space=pl.ANY),
            pl.BlockSpec((1, 1, D, de2), lambda i, be, rt, rd: (layer, be[i], 0, 0)),
            pl.BlockSpec((1, 1, de2 // 2, D), lambda i, be, rt, rd: (layer, be[i], 0, 0)),
        ],
        out_specs=pl.BlockSpec(memory_space=pl.ANY),
        scratch_shapes=[pltpu.VMEM((2, EXPERT_ROWS * SUB, LANE), F32), pltpu.VMEM((2, EXPERT_ROWS * SUB, LANE), F32),
                        pltpu.SemaphoreType.DMA((2,)), pltpu.SemaphoreType.DMA((2,))],
    )
    return pl.pallas_call(
        _expert_kernel,
        grid_spec=grid_spec,
        out_shape=jax.ShapeDtypeStruct((row_tok.shape[0] * SUB, LANE), F32),
        compiler_params=pltpu.CompilerParams(
            dimension_semantics=("arbitrary",), vmem_limit_bytes=VMEM_LIMIT),
        name="experts",
    )(blk_e, row_tok, row_dst, h2, w13, w2)


def _combine_kernel(xn_ref, g2_ref, rt_ref, y0_ref, y1_ref, o_ref):
    rt = rt_ref[...]
    tm = xn_ref.shape[0]
    w0 = rt[:, 2:3]
    w1 = rt[:, 3:4]
    g2 = g2_ref[0]
    for c in range(SUB):
        cols = slice(c * LANE, (c + 1) * LANE)
        y = w0 * y0_ref[pl.ds(c, tm, stride=SUB), :] + w1 * y1_ref[pl.ds(c, tm, stride=SUB), :]
        o_ref[:, cols] = xn_ref[:, cols] + g2[:, cols] * y


def _combine_call(xn, g2, route, y2):
    B, S, D = xn.shape
    tm = 512
    spb = S // tm
    nt = B * spb
    return pl.pallas_call(
        _combine_kernel,
        grid=(B, spb),
        in_specs=[
            pl.BlockSpec((tm, D), lambda b, i: (b * spb + i, 0)),
            pl.BlockSpec((1, 1, D), lambda b, i: (b, 0, 0)),
            pl.BlockSpec((tm, 128), lambda b, i: (b * spb + i, 0)),
            pl.BlockSpec((tm * SUB, LANE), lambda b, i: (b * spb + i, 0)),
            pl.BlockSpec((tm * SUB, LANE), lambda b, i: (nt + b * spb + i, 0)),
        ],
        out_specs=pl.BlockSpec((tm, D), lambda b, i: (b * spb + i, 0)),
        out_shape=jax.ShapeDtypeStruct((B * S, D), F32),
        compiler_params=pltpu.CompilerParams(vmem_limit_bytes=VMEM_LIMIT),
        name="moe_combine",
    )(xn.reshape(B * S, D), g2, route, y2, y2).reshape(B, S, D)


def _moe(h2, route, w13, w2, layer):
    N = h2.shape[0] // SUB
    K = 2
    E = N_EXPERTS
    rows = EXPERT_ROWS
    flat_e = route[:, 0:2].astype(jnp.int32).reshape(-1)
    order = jnp.argsort(flat_e).astype(jnp.int32)
    counts = jnp.sum((flat_e[:, None] == jnp.arange(E)[None, :]).astype(jnp.int32), axis=0)
    padded = (counts + rows - 1) // rows * rows
    pend = jnp.cumsum(padded)
    pstart = pend - padded
    start = jnp.cumsum(counts) - counts
    n_blocks = (N * K) // rows + E
    R = n_blocks * rows
    blk_e = jnp.minimum(
        jnp.sum((jnp.arange(n_blocks)[:, None] * rows >= pend[None, :]).astype(jnp.int32), axis=1), E - 1)
    pos = jnp.arange(R, dtype=jnp.int32).reshape(n_blocks, rows)
    local = pos - pstart[blk_e][:, None]
    is_real = ((pos < pend[E - 1]) & (local < counts[blk_e][:, None])).reshape(R)
    rank = jnp.clip(start[blk_e][:, None] + local, 0, N * K - 1).reshape(R)
    pair = order[rank]
    flat = pos.reshape(R)
    real_before = jnp.cumsum(is_real.astype(jnp.int32)) - is_real.astype(jnp.int32)
    row_tok = jnp.where(is_real, pair // K, flat % N).astype(jnp.int32)
    row_dst = jnp.where(is_real, (pair % K) * N + pair // K, N * K + flat - real_before).astype(jnp.int32)
    return _expert_call(h2, blk_e.astype(jnp.int32), row_tok, row_dst, w13, w2, layer)


def kernel(x, c, w_ada, b_ada, norm_gain, w_in, qk_gain, cmp_pe, cmp_w1, cmp_w2, swa_sinks,
           lat_gain_q, lat_gain_kv, rope_gain, w_uq, w_ukv, w_branch, w_out,
           w_coarse, b_coarse, w_fine, b_fine, w13, w2):
    B, S, D = x.shape
    L = w_in.shape[0]
    assert S % MOBA_BLOCK == 0 and D == 1024
    slopes = _alibi_slopes()

    half = MLA_ROPE // 2
    inv = ROPE_THETA ** (-jnp.arange(half, dtype=F32) / half)
    ang = jnp.arange(S).astype(F32)[:, None] * inv[None, :]
    cos = jnp.tile(jnp.cos(ang), (1, 8))
    sin = jnp.tile(jnp.concatenate([-jnp.sin(ang), jnp.sin(ang)], axis=1), (1, 4))

    mod = _ada_mod(c, w_ada, b_ada)

    w_attn = jnp.concatenate(
        [w_in[:, :, :448], w_in[:, :, 512:576], w_in[:, :, 448:512], w_in[:, :, 576:640],
         w_in[:, :, 652:ATTN_OLD], w_in[:, :, 640:652],
         jnp.zeros((L, D, ATTN_COLS - ATTN_OLD), F32)], axis=2).astype(BF16)
    w_gate = w_in[:, :, ATTN_OLD:].astype(BF16)
    dq, dkv = MLA_NOPE + MLA_ROPE, MLA_NOPE + MLA_V
    uq = w_uq.reshape(L, MLA_Q_RANK, MLA_HEADS, dq)
    w_uq_b = jnp.concatenate([uq[..., :MLA_NOPE].reshape(L, MLA_Q_RANK, -1),
                              uq[..., MLA_NOPE:].reshape(L, MLA_Q_RANK, -1)], axis=2).astype(BF16)
    ukv = w_ukv.reshape(L, MLA_KV_RANK, MLA_HEADS, dkv)
    w_ukv_b = jnp.concatenate([ukv[..., :MLA_NOPE].reshape(L, MLA_KV_RANK, -1),
                               ukv[..., MLA_NOPE:].reshape(L, MLA_KV_RANK, -1)], axis=2).astype(BF16)

    qs = HEAD_DIM ** -0.5 * LOG2E
    ms = dq ** -0.5 * LOG2E

    def lanes(v, reps, scale=1.0):
        r = jnp.tile(v, (1, reps)) * scale
        return jnp.pad(r, ((0, 0), (0, 4 * HEAD_DIM - r.shape[1])))

    gq = qk_gain
    gain_rows = jnp.stack([
        lanes(gq[:, QK_NSA_Q], 4, qs),
        lanes(jnp.concatenate([gq[:, QK_NSA_KS], gq[:, QK_NSA_KW]], axis=1), 1),
        lanes(gq[:, QK_SWA_Q], 4, qs),
        lanes(gq[:, QK_SWA_K], 2),
        lanes(gq[:, QK_MOBA_Q], 4, qs),
        lanes(gq[:, QK_MOBA_K], 4),
        lanes(gq[:, QK_MLA_Q], 4, ms),
        lanes(rope_gain[:, 0], 4, ms),
        lanes(gq[:, QK_MLA_K], 4),
        lanes(rope_gain[:, 1], 4),
    ], axis=1)
    bd64 = jnp.asarray(np.kron(np.eye(4), np.ones((HEAD_DIM, HEAD_DIM))) / HEAD_DIM, dtype=BF16)
    bd32 = jnp.asarray(np.kron(np.eye(4), np.ones((MLA_ROPE, MLA_ROPE))) / MLA_ROPE, dtype=BF16)
    w_branch_b = w_branch.astype(BF16)
    w_out_b = w_out.astype(BF16)
    w_router = jnp.concatenate(
        [w_coarse, w_fine, jnp.zeros((L, D, 128 - N_GROUPS - N_EXPERTS), F32)], axis=2)
    b_router = jnp.concatenate(
        [b_coarse, b_fine, jnp.zeros((L, 128 - N_GROUPS - N_EXPERTS), F32)], axis=1)
    sinks_pad = jnp.concatenate([swa_sinks, jnp.zeros((L, 128 - SWA_HEADS), F32)], axis=1)

    for l in range(L):
        m6 = mod[l].reshape(B, 6, 1, D)
        sh1, sc1, g1, sh2, sc2, g2 = (m6[:, j] for j in range(6))
        (qa, kcr, vcr, ks, vs, kw, vw, gs, qb, kb, vb, qc, kc, vc, mc, qd, kd, vd) = _proj_call(
            x, sc1, sh1, norm_gain[l, 0:1], w_attn[l], gain_rows[l], bd64, bd32, lat_gain_q[l][None],
            lat_gain_kv[l][None], w_uq_b[l], w_ukv_b[l], cos, sin)
        kcmp, vcmp = _compress_call(kcr, vcr, cmp_pe[l], cmp_w1[l], cmp_w2[l], qk_gain[l, QK_NSA_KC][None])
        o_cmp, sel = _cmp_attn_call(qa, kcmp, vcmp, slopes[0])
        o_slc = _flash_call(qa, ks, vs, slopes=slopes[0], mask=sel, mask_block=SEL_BLOCK, mask_t=True,
                            tq=256, tk=256, hpc=2, skip_tiles=True, out_dtype=F32, name="nsa_slc")
        o_win = _flash_call(qa, kw, vw, slopes=slopes[0], window=NSA_WINDOW,
                            tq=256, tk=256, hpc=2, out_dtype=F32, name="nsa_win")
        o_b = _flash_call(qb, kb, vb, slopes=slopes[1], window=SWA_WINDOW, sinks=sinks_pad[l][None],
                          tq=256, tk=256, name="swa")
        o_c = _flash_call(qc, kc, vc, slopes=slopes[2], mask=mc, mask_block=MOBA_BLOCK,
                          tq=512, tk=512, name="moba")
        o_d = _flash_call(qd, kd, vd, tq=512, tk=512, name="mla")
        xn, h2, route = _merge_call(
            x, (sc1, sh1, g1, sc2, sh2), norm_gain[l], w_gate[l], w_branch_b[l], w_out_b[l],
            o_cmp, o_slc, o_win, gs, o_b, o_c, o_d, w_router[l], b_router[l][None])
        route = route.reshape(B * S, 128)
        y2 = _moe(h2, route, w13, w2, l)
        x = _combine_call(xn, g2, route, y2)
    return x
```

```python
import functools
import math

import numpy as np
import jax
import jax.numpy as jnp
from jax import lax
from jax.experimental import pallas as pl
from jax.experimental.pallas import tpu as pltpu

F32 = jnp.float32
BF16 = jnp.bfloat16

HEAD_DIM = 64
NEG_INF = -1e30
EPS = 1e-6
NSA_HEADS = 4
CMP_BLOCK = 32
CMP_STRIDE = 16
CMP_HIDDEN = 256
SEL_BLOCK = 64
SEL_TOPK = 8
NSA_WINDOW = 512
FORCE_BONUS = 1e4
SWA_HEADS = 4
SWA_KV_HEADS = 2
SWA_WINDOW = 128
MOBA_HEADS = 4
MOBA_BLOCK = 256
MOBA_TOPK = 3
MLA_HEADS = 4
MLA_Q_RANK = 384
MLA_KV_RANK = 128
MLA_NOPE = 64
MLA_ROPE = 32
MLA_V = 64
ROPE_THETA = 10000.0
N_BRANCH = 4
BRANCH_WIDTH = 256
N_GROUPS = 4
EXPERTS_PER_GROUP = 8
N_EXPERTS = N_GROUPS * EXPERTS_PER_GROUP
D_EXPERT = 256

(GR_NSA_Q, GR_NSA_K, GR_SWA_Q, GR_SWA_K, GR_MOBA_Q, GR_MOBA_K,
 GR_MLA_QN, GR_MLA_QR, GR_MLA_KN, GR_MLA_KR) = range(10)
QK_NSA_Q, QK_NSA_KC, QK_NSA_KS, QK_NSA_KW = 0, 1, 2, 3
QK_SWA_Q, QK_SWA_K, QK_MOBA_Q, QK_MOBA_K, QK_MLA_Q, QK_MLA_K = 4, 5, 6, 7, 8, 9

ATTN_OLD = 2476
ATTN_COLS = 2560
GATE_LANE0 = 32
LOG2E = math.log2(math.e)
SUB, LANE = 8, 128
EXPERT_ROWS = 256
VMEM_LIMIT = 56 * 1024 * 1024


def _alibi_slopes():
    n = NSA_HEADS + SWA_HEADS + MOBA_HEADS

    def pow2(m):
        start = 2.0 ** (-8.0 / m)
        return [start ** (i + 1) for i in range(m)]

    c = 2 ** int(math.floor(math.log2(n)))
    s = pow2(c) + (pow2(2 * c)[0::2][: n - c] if c < n else [])
    s = -np.sort(-np.asarray(s, np.float32))
    return s.reshape(NSA_HEADS, 3).T


def _dot(a, b):
    return jnp.dot(a, b, preferred_element_type=F32)


def _dot_nt(a, b):
    return lax.dot_general(a, b, (((1,), (1,)), ((), ())), preferred_element_type=F32)


def _split(a):
    hi = a.astype(BF16)
    lo = (a - hi.astype(F32)).astype(BF16)
    return hi, lo


def _dot3(a, b):
    ah, al = _split(a)
    bh, bl = _split(b)
    return _dot(ah, bh) + (_dot(ah, bl) + _dot(al, bh))


def _dot3_nt(a, b):
    ah, al = _split(a)
    bh, bl = _split(b)
    return _dot_nt(ah, bh) + (_dot_nt(ah, bl) + _dot_nt(al, bh))


def _rms(x, g):
    return x * lax.rsqrt(jnp.mean(x * x, axis=-1, keepdims=True) + EPS) * g


def _sigmoid(x):
    return 1.0 / (1.0 + jnp.exp(-x))


AUG_MASK0 = 6


def _bf16_pieces(x, n=3):
    out, r = [], float(x)
    for _ in range(n):
        piece = float(np.asarray(r, np.float32).astype(jnp.bfloat16).astype(np.float32))
        out.append(piece)
        r -= piece
    return out


def _topk_mask(score, k, axis=1):
    n = score.shape[axis]
    iota = lax.broadcasted_iota(jnp.int32, score.shape, axis).astype(F32)
    sel = jnp.zeros(score.shape, F32)
    for _ in range(k):
        m = jnp.max(score, axis=axis, keepdims=True)
        idx = jnp.min(jnp.where(score == m, iota, float(n)), axis=axis, keepdims=True)
        hit = iota == idx
        sel = jnp.where(hit, jnp.where(m > 0.5 * NEG_INF, 1.0, 0.0), sel)
        score = jnp.where(hit, -3e38, score)
    return sel


def _ada_kernel(c_ref, w_ref, b_ref, o_ref):
    c = c_ref[...]
    a = c * _sigmoid(c)
    o_ref[0] = _dot(a, w_ref[0]) + b_ref[0]


def _ada_mod(c, w_ada, b_ada):
    L, D, D6 = w_ada.shape
    B = c.shape[0]
    tn = 1024
    return pl.pallas_call(
        _ada_kernel,
        grid=(L, D6 // tn),
        in_specs=[
            pl.BlockSpec((B, D), lambda l, j: (0, 0)),
            pl.BlockSpec((1, D, tn), lambda l, j: (l, 0, j)),
            pl.BlockSpec((1, 1, tn), lambda l, j: (l, 0, j)),
        ],
        out_specs=pl.BlockSpec((1, B, tn), lambda l, j: (l, 0, j)),
        out_shape=jax.ShapeDtypeStruct((L, B, D6), F32),
        compiler_params=pltpu.CompilerParams(vmem_limit_bytes=VMEM_LIMIT),
        name="ada_mod",
    )(c, w_ada, b_ada.reshape(L, 1, D6))


def _rms_blocks(x, bd, g):
    hi, lo = _split(x * x)
    ms = _dot(hi, bd) + _dot(lo, bd)
    return x * lax.rsqrt(ms + EPS) * g


def _proj_kernel(x_ref, sc_ref, sh_ref, ng_ref, w_ref, gr_ref, bd64_ref, bd32_ref, lgq_ref, lgkv_ref,
                 wuq_ref, wukv_ref, cos_ref, sin_ref,
                 qa_ref, kcr_ref, vcr_ref, ks_ref, vs_ref, kw_ref, vw_ref, gs_ref,
                 qb_ref, kb_ref, vb_ref, qc_ref, kc_ref, vc_ref, mc_ref, qd_ref, kd_ref, vd_ref,
                 kmean_s):
    i = pl.program_id(1)
    hd = HEAD_DIM
    x = x_ref[0]
    h = _rms(x, ng_ref[...]) * (1.0 + sc_ref[0]) + sh_ref[0]
    hb = h.astype(BF16)
    bd64 = bd64_ref[...]
    bd64h = bd64_ref[0:2 * hd, 0:2 * hd]
    bd32 = bd32_ref[...]

    def grow(r, w=4 * hd):
        return gr_ref[r:r + 1, 0:w]

    def proj(a, b):
        return _dot(hb, w_ref[:, a:b])

    def store_heads(ref, slab, n):
        for hh in range(n):
            ref[0, hh] = slab[:, hh * hd:(hh + 1) * hd].astype(BF16)

    def store_heads_t(ref, slab_t, n):
        for hh in range(n):
            ref[0, hh] = slab_t[hh * hd:(hh + 1) * hd].astype(BF16)

    store_heads(qa_ref, _rms_blocks(proj(0, 256), bd64, grow(GR_NSA_Q)), NSA_HEADS)
    p = proj(256, 640)
    kcr_ref[0] = p[:, 0:64]
    vcr_ref[0] = p[:, 64:128]
    kk = _rms_blocks(p[:, 128:256], bd64h, grow(GR_NSA_K, 2 * hd))
    ks_ref[0, 0] = kk[:, :hd].astype(BF16)
    kw_ref[0, 0] = kk[:, hd:].astype(BF16)
    vt = p[:, 256:384].T
    vs_ref[0, 0] = vt[:hd].astype(BF16)
    vw_ref[0, 0] = vt[hd:].astype(BF16)

    p = proj(640, 1152)
    store_heads(qb_ref, _rms_blocks(p[:, 0:256], bd64, grow(GR_SWA_Q)), SWA_HEADS)
    store_heads(kb_ref, _rms_blocks(p[:, 256:384], bd64h, grow(GR_SWA_K, 2 * hd)), SWA_KV_HEADS)
    store_heads_t(vb_ref, p[:, 384:512].T, SWA_KV_HEADS)

    @pl.when(i == 0)
    def _():
        kmean_s[...] = jnp.zeros(kmean_s.shape, F32)

    p = proj(1152, 1920)
    qn = _rms_blocks(p[:, 0:256], bd64, grow(GR_MOBA_Q))
    kn = _rms_blocks(p[:, 256:512], bd64, grow(GR_MOBA_K))
    store_heads(qc_ref, qn, MOBA_HEADS)
    store_heads(kc_ref, kn, MOBA_HEADS)
    store_heads_t(vc_ref, p[:, 512:768].T, MOBA_HEADS)
    nblk = kmean_s.shape[0]
    blk_iota = lax.broadcasted_iota(jnp.int32, (1, nblk), 1)
    head_of_lane = lax.broadcasted_iota(jnp.int32, (1, 4 * hd), 1) // hd
    kmeans = kmean_s[...]
    for hh in range(MOBA_HEADS):
        g = _dot3_nt(jnp.where(head_of_lane == hh, qn, 0.0), kmeans)
        g = jnp.where(blk_iota < i, g, NEG_INF)
        mc_ref[0, hh] = jnp.where(blk_iota == i, 1.0, _topk_mask(g, MOBA_TOPK))
    kmean_s[pl.ds(i, 1), :] = jnp.mean(kn, axis=0, keepdims=True)

    p = proj(1920, 2560)
    gs_ref[0] = _sigmoid(p[:, 512:640])
    cos = cos_ref[...]
    sin = sin_ref[...]
    half = MLA_ROPE // 2
    first_half = lax.broadcasted_iota(jnp.int32, (1, 4 * MLA_ROPE), 1) % MLA_ROPE < half

    def rope(v):
        swapped = jnp.where(first_half, pltpu.roll(v, 4 * MLA_ROPE - half, 1), pltpu.roll(v, half, 1))
        return v * cos + swapped * sin

    qlat = _dot(_rms(p[:, 0:384], lgq_ref[...]).astype(BF16), wuq_ref[...])
    kvlat = _dot(_rms(p[:, 384:512], lgkv_ref[...]).astype(BF16), wukv_ref[...])
    q_nope = _rms_blocks(qlat[:, 0:256], bd64, grow(GR_MLA_QN))
    q_rot = rope(_rms_blocks(qlat[:, 256:384], bd32, grow(GR_MLA_QR, 2 * hd)))
    k_nope = _rms_blocks(kvlat[:, 0:256], bd64, grow(GR_MLA_KN))
    k_rot = rope(_rms_blocks(p[:, 512:640], bd32, grow(GR_MLA_KR, 2 * hd)))[:, :MLA_ROPE]
    store_heads_t(vd_ref, kvlat[:, 256:512].T, MLA_HEADS)
    for hh in range(MLA_HEADS):
        qd_ref[0, hh] = jnp.concatenate(
            [q_nope[:, hh * hd:(hh + 1) * hd], q_rot[:, hh * MLA_ROPE:(hh + 1) * MLA_ROPE]], axis=1).astype(BF16)
        kd_ref[0, hh] = jnp.concatenate([k_nope[:, hh * hd:(hh + 1) * hd], k_rot], axis=1).astype(BF16)


def _proj_call(x, sc1, sh1, ng, w_attn, gain_rows, bd64, bd32, lgq, lgkv, wuq, wukv, cos, sin):
    B, S, D = x.shape
    tm = MOBA_BLOCK
    nblk = S // tm
    hd = HEAD_DIM

    def full(shape):
        return pl.BlockSpec(shape, lambda b, i: (0,) * len(shape))

    def heads(nh, d):
        return pl.BlockSpec((1, nh, tm, d), lambda b, i: (b, 0, i, 0))

    in_specs = [
        pl.BlockSpec((1, tm, D), lambda b, i: (b, i, 0)),
        pl.BlockSpec((1, 1, D), lambda b, i: (b, 0, 0)),
        pl.BlockSpec((1, 1, D), lambda b, i: (b, 0, 0)),
        full((1, D)),
        full((D, ATTN_COLS)),
        full(gain_rows.shape),
        full(bd64.shape),
        full(bd32.shape),
        full(lgq.shape),
        full(lgkv.shape),
        full(wuq.shape),
        full(wukv.shape),
        pl.BlockSpec((tm, 4 * MLA_ROPE), lambda b, i: (i, 0)),
        pl.BlockSpec((tm, 4 * MLA_ROPE), lambda b, i: (i, 0)),
    ]
    row64 = pl.BlockSpec((1, tm, hd), lambda b, i: (b, i, 0))
    def heads_t(nh, d):
        return pl.BlockSpec((1, nh, d, tm), lambda b, i: (b, 0, 0, i))

    out_specs = [
        heads(4, hd), row64, row64, heads(1, hd), heads_t(1, hd), heads(1, hd), heads_t(1, hd),
        pl.BlockSpec((1, tm, 128), lambda b, i: (b, i, 0)),
        heads(4, hd), heads(2, hd), heads_t(2, hd),
        heads(4, hd), heads(4, hd), heads_t(4, hd), heads(4, nblk),
        heads(4, MLA_NOPE + MLA_ROPE), heads(4, MLA_NOPE + MLA_ROPE), heads_t(4, MLA_V),
    ]

    def sd(shape, dt):
        return jax.ShapeDtypeStruct(shape, dt)

    out_shape = [
        sd((B, 4, S, hd), BF16), sd((B, S, hd), F32), sd((B, S, hd), F32),
        sd((B, 1, S, hd), BF16), sd((B, 1, hd, S), BF16), sd((B, 1, S, hd), BF16), sd((B, 1, hd, S), BF16),
        sd((B, S, 128), F32),
        sd((B, 4, S, hd), BF16), sd((B, 2, S, hd), BF16), sd((B, 2, hd, S), BF16),
        sd((B, 4, S, hd), BF16), sd((B, 4, S, hd), BF16), sd((B, 4, hd, S), BF16), sd((B, 4, S, nblk), F32),
        sd((B, 4, S, 96), BF16), sd((B, 4, S, 96), BF16), sd((B, 4, MLA_V, S), BF16),
    ]
    return pl.pallas_call(
        _proj_kernel,
        grid=(B, nblk),
        in_specs=in_specs,
        out_specs=out_specs,
        out_shape=out_shape,
        scratch_shapes=[pltpu.VMEM((nblk, MOBA_HEADS * hd), F32)],
        compiler_params=pltpu.CompilerParams(
            dimension_semantics=("arbitrary", "arbitrary"), vmem_limit_bytes=VMEM_LIMIT),
        name="proj_prep",
    )(x, sc1, sh1, ng, w_attn, gain_rows, bd64, bd32, lgq, lgkv, wuq, wukv, cos, sin)


def _compress_kernel(gk_ref, gv_ref, pe_ref, w1_ref, w2_ref, gkc_ref, kc_ref, vc_ref):
    half = w1_ref.shape[1] // 2
    outs = []
    for j, g_ref in enumerate((gk_ref, gv_ref)):
        g = g_ref[0].astype(BF16)
        top = _dot(g, w1_ref[j, :half].astype(BF16))
        bot = _dot(g, w1_ref[j, half:].astype(BF16))
        bot = jnp.concatenate([bot[1:], bot[:1]], axis=0)
        pe = jnp.broadcast_to(pe_ref[j], (8, pe_ref.shape[2]))
        bias = _dot3(pe, w1_ref[j])[0:1]
        hid = top + bot + bias
        hid = hid * _sigmoid(hid)
        outs.append(_dot(hid.astype(BF16), w2_ref[j].astype(BF16)))
    kc_ref[0] = _rms(outs[0], gkc_ref[...]).astype(BF16)
    dk = outs[1].shape[1]
    vc_ref[0] = jnp.concatenate([outs[1], outs[1]], axis=1).T[:dk].astype(BF16)


def _compress_call(kc_raw, vc_raw, pe, w1, w2, g_kc):
    B, S, dk = kc_raw.shape
    n_grp = S // CMP_STRIDE
    gk = kc_raw.reshape(B, n_grp, CMP_STRIDE * dk)
    gv = vc_raw.reshape(B, n_grp, CMP_STRIDE * dk)
    pe_flat = pe.reshape(2, 1, CMP_BLOCK * dk)
    grp_spec = pl.BlockSpec((1, n_grp, CMP_STRIDE * dk), lambda b: (b, 0, 0))
    out_spec = pl.BlockSpec((1, n_grp, dk), lambda b: (b, 0, 0))
    return pl.pallas_call(
        _compress_kernel,
        grid=(B,),
        in_specs=[
            grp_spec, grp_spec,
            pl.BlockSpec(pe_flat.shape, lambda b: (0, 0, 0)),
            pl.BlockSpec(w1.shape, lambda b: (0, 0, 0)),
            pl.BlockSpec(w2.shape, lambda b: (0, 0, 0)),
            pl.BlockSpec((1, dk), lambda b: (0, 0)),
        ],
        out_specs=[out_spec, pl.BlockSpec((1, dk, n_grp), lambda b: (b, 0, 0))],
        out_shape=[jax.ShapeDtypeStruct((B, n_grp, dk), BF16), jax.ShapeDtypeStruct((B, dk, n_grp), BF16)],
        compiler_params=pltpu.CompilerParams(vmem_limit_bytes=VMEM_LIMIT),
        name="nsa_compress",
    )(gk, gv, pe_flat, w1, w2, g_kc)


def _cmp_attn_kernel(q_ref, kc_ref, vct_ref, cover_ref, o_ref, sel_ref, *, slopes, tq, n_cmp):
    i = pl.program_id(1)
    kc = kc_ref[0]
    vct = vct_ref[0]
    ncp = kc.shape[0]
    t_full = i * tq + lax.broadcasted_iota(jnp.int32, (ncp, tq), 1)
    n_iota = lax.broadcasted_iota(jnp.int32, (ncp, tq), 0)
    dist_i = t_full - (n_iota * CMP_STRIDE + (CMP_BLOCK - 1))
    vis = (n_iota < n_cmp) & (dist_i >= 0)
    dist = dist_i.astype(F32)
    visf = vis.astype(F32)
    psum = jnp.zeros((ncp, tq), F32)
    outs = []
    for hh in range(NSA_HEADS):
        s = _dot_nt(kc, q_ref[0, hh]) - (slopes[hh] * LOG2E) * dist
        s = jnp.where(vis, s, NEG_INF)
        e = jnp.exp2(s - jnp.max(s, axis=0, keepdims=True)) * visf
        p = e / jnp.maximum(jnp.sum(e, axis=0, keepdims=True), 1e-30)
        outs.append(_dot(vct, p.astype(BF16)))
        psum = psum + p
    o_ref[0] = jnp.concatenate(outs, axis=0).T
    ph, plo = _split(psum)
    cover = cover_ref[...]
    p_slc = _dot(cover, ph) + _dot(cover, plo)
    n_sel = cover.shape[0]
    cur = (i * tq + lax.broadcasted_iota(jnp.int32, (1, tq), 1)) // SEL_BLOCK
    j = lax.broadcasted_iota(jnp.int32, (n_sel, 1), 0)
    forced = jnp.where(j == 0, 1.0, jnp.where(j == cur, 1.0, jnp.where(j == cur - 1, 1.0, 0.0)))
    score = jnp.where(j <= cur, p_slc + FORCE_BONUS * forced, NEG_INF)
    sel_ref[0, 0] = _topk_mask(score, min(SEL_TOPK, n_sel), axis=0)


def _cmp_attn_call(qa, kc, vc, slopes):
    B, H, S, dk = qa.shape
    ncp = kc.shape[1]
    n_cmp = (S - CMP_BLOCK) // CMP_STRIDE + 1
    n_sel = S // SEL_BLOCK
    tq = 256
    starts = np.arange(ncp) * CMP_STRIDE
    jb = np.arange(n_sel) * SEL_BLOCK
    cover = ((starts[:, None] < jb[None, :] + SEL_BLOCK) & (starts[:, None] + CMP_BLOCK > jb[None, :])
             & (np.arange(ncp)[:, None] < n_cmp))
    cover = jnp.asarray(cover.T.astype(np.float32), dtype=BF16)
    return pl.pallas_call(
        functools.partial(_cmp_attn_kernel, slopes=tuple(float(s) for s in slopes), tq=tq, n_cmp=n_cmp),
        grid=(B, S // tq),
        in_specs=[
            pl.BlockSpec((1, H, tq, dk), lambda b, i: (b, 0, i, 0)),
            pl.BlockSpec((1, ncp, dk), lambda b, i: (b, 0, 0)),
            pl.BlockSpec((1, dk, ncp), lambda b, i: (b, 0, 0)),
            pl.BlockSpec((n_sel, ncp), lambda b, i: (0, 0)),
        ],
        out_specs=[
            pl.BlockSpec((1, tq, H * dk), lambda b, i: (b, i, 0)),
            pl.BlockSpec((1, 1, n_sel, tq), lambda b, i: (b, 0, 0, i)),
        ],
        out_shape=[jax.ShapeDtypeStruct((B, S, H * dk), F32),
                   jax.ShapeDtypeStruct((B, 1, n_sel, S), F32)],
        compiler_params=pltpu.CompilerParams(vmem_limit_bytes=VMEM_LIMIT),
        name="nsa_cmp_attn",
    )(qa, kc, vc, cover)


def _flash_kernel(*refs, H, G, hpc, tq, tk, slopes, window, mask_block, mask_per_head, mask_t, has_sink, skip):
    refs = list(refs)
    q_ref, k_ref, v_ref = refs[:3]
    pos = 3
    mask_ref = sink_ref = None
    if mask_block:
        mask_ref = refs[pos]
        pos += 1
    if has_sink:
        sink_ref = refs[pos]
        pos += 1
    o_ref, m_s, acc_s = refs[pos:pos + 3]
    need_s = refs[pos + 3] if skip else None
    use_aug = slopes is not None or bool(mask_block)
    R = H // G
    C = H // hpc
    M = hpc * tq
    dv = v_ref.shape[-2]
    i = pl.program_id(1)
    q0 = i * tq
    hi = q0 // tk
    lo = jnp.maximum(q0 - (window - 1), 0) // tk if window else 0

    def rel_pos():
        r_row = jnp.concatenate([lax.broadcasted_iota(jnp.int32, (1, tq), 1)] * hpc, axis=1)
        return (r_row - lax.broadcasted_iota(jnp.int32, (tk, M), 0)).astype(F32)

    dq = q_ref.shape[-1]
    n_mask = mask_ref.shape[-2 if mask_t else -1] if mask_block else 0
    aug_w = LANE if dq + AUG_MASK0 + n_mask <= LANE else 2 * LANE
    aug0 = dq
    lane_a = lax.broadcasted_iota(jnp.int32, (1, aug_w), 1)
    feature_lane = lane_a < dq

    def slope_lanes(hh):
        row = jnp.zeros((1, aug_w), F32)
        if slopes is not None:
            for n, piece in enumerate(_bf16_pieces(slopes[hh])):
                row = jnp.where((lane_a == aug0 + n) | (lane_a == aug0 + n + 3), piece, row)
        return row

    def widen(a):
        return jnp.concatenate([a, jnp.zeros((a.shape[0], aug_w - dq), a.dtype)], axis=1)

    qs, slope_rows, bms = [], [], []
    for c in range(C):
        heads = [c * hpc + r for r in range(hpc)]
        if hpc == 1:
            qs.append(q_ref[0, heads[0]])
        else:
            qs.append(jnp.concatenate([q_ref[0, hh] for hh in heads], axis=0))
        if slopes is not None:
            slope_rows.append(jnp.concatenate([jnp.full((1, tq), slopes[hh], F32) for hh in heads], axis=1))
        if mask_block:
            parts = [mask_ref[0, hh if mask_per_head else 0] for hh in heads]
            bms.append(jnp.concatenate(parts, axis=1) if mask_t else jnp.concatenate(parts, axis=0).T)
    m_s[...] = jnp.full(m_s.shape, NEG_INF, F32)
    acc_s[...] = jnp.zeros(acc_s.shape, F32)
    ones_rows = jnp.ones((acc_s.shape[1] - dv, tk), BF16)

    if mask_block:
        nblk = bms[0].shape[0]
    if need_s is not None:
        col = bms[0]
        for bm in bms[1:]:
            col = jnp.maximum(col, bm)
        col = jnp.max(col, axis=1, keepdims=True)
        bpt = tk // mask_block
        for jt in range(nblk // bpt):
            need_s[jt] = jnp.max(col[jt * bpt:(jt + 1) * bpt]).astype(jnp.int32)

    if use_aug:
        for c in range(C):
            qa = jnp.concatenate(
                [jnp.broadcast_to(slope_lanes(c * hpc + r), (tq, aug_w)) for r in range(hpc)], axis=0)
            if mask_block:
                place = jnp.where(
                    lax.broadcasted_iota(jnp.int32, (nblk, aug_w), 0) + (aug0 + AUG_MASK0) == lane_a,
                    1.0, 0.0).astype(BF16)
                off_sel = ((bms[c] - 1.0) * -NEG_INF).astype(BF16)
                qa = qa + lax.dot_general(off_sel, place, (((0,), (0,)), ((), ())), preferred_element_type=F32)
            qs[c] = jnp.where(feature_lane, widen(qs[c]), qa.astype(BF16))

    def tile_scores(j):
        k0 = pl.multiple_of(j * tk, tk)
        if use_aug:
            t_key = lax.broadcasted_iota(jnp.int32, (tk, aug_w), 0) + k0
            k_aug = jnp.zeros((tk, aug_w), F32)
            if mask_block:
                key_blk = lax.shift_right_logical(t_key, int(math.log2(mask_block)))
                k_aug = jnp.where(key_blk + (aug0 + AUG_MASK0) == lane_a, 1.0, 0.0)
            if slopes is not None:
                t_hi = (lax.shift_right_logical(t_key, 8) * 256).astype(F32)
                t_lo = (t_key & 255).astype(F32)
                k_aug = jnp.where(lane_a < aug0 + 3, t_hi, jnp.where(lane_a < aug0 + 6, t_lo, k_aug))
            k_aug = k_aug.astype(BF16)
        k_cats = {}
        scores = []
        for c in range(C):
            g = (c * hpc) // R
            if g not in k_cats:
                k = k_ref[0, g, pl.ds(k0, tk), :]
                k_cats[g] = jnp.where(feature_lane, widen(k), k_aug) if use_aug else k
            scores.append(_dot_nt(k_cats[g], qs[c]))
        return scores

    def tile_update(j, edge, scores):
        k0 = pl.multiple_of(j * tk, tk)
        valid = None
        if edge:
            dist = rel_pos() + (q0 - k0).astype(F32)
            valid = dist >= 0.0
            if window:
                valid = valid & (dist < float(window))
        probs, alphas = [], []
        for g in range(C):
            s = scores[g]
            if valid is not None:
                s = jnp.where(valid, s, NEG_INF)
            m_prev = m_s[g]
            m_new = jnp.maximum(m_prev, jnp.max(s, axis=0, keepdims=True))
            alpha = jnp.exp2(m_prev - m_new)
            p = jnp.exp2(s - m_new)
            m_s[g] = m_new
            probs.append(p.astype(BF16))
            alphas.append(alpha)
        for c in range(C):
            v = jnp.concatenate([v_ref[0, (c * hpc) // R, :, pl.ds(k0, tk)], ones_rows], axis=0)
            acc_s[c] = alphas[c] * acc_s[c] + _dot(v, probs[c])

    def tile(j, edge):
        tile_update(j, edge, tile_scores(j))

    def body(step, carry):
        j = hi - step
        is_edge = step == 0
        if window:
            is_edge = is_edge | (q0 - j * tk + (tq - 1) >= window)
        run = (step == 0) | (need_s[j] > 0) if need_s is not None else None

        def when(c):
            return pl.when(c if run is None else c & run)

        @when(is_edge)
        def _():
            tile(j, True)

        @when(jnp.logical_not(is_edge))
        def _():
            tile(j, False)

        return carry

    lax.fori_loop(0, hi - lo + 1, body, 0)

    for c in range(C):
        m = m_s[c]
        l = acc_s[c, dv:dv + 1]
        acc = acc_s[c, 0:dv]
        if has_sink:
            sk = LOG2E * jnp.concatenate(
                [jnp.broadcast_to(sink_ref[:, c * hpc + r:c * hpc + r + 1], (1, tq)) for r in range(hpc)], axis=1)
            if slopes is not None:
                t_q = jnp.concatenate([lax.broadcasted_iota(jnp.int32, (1, tq), 1) + q0] * hpc, axis=1)
                m = m - slope_rows[c] * t_q.astype(F32)
            m_f = jnp.maximum(m, sk)
            a = jnp.exp2(m - m_f)
            l = l * a + jnp.exp2(sk - m_f)
            acc = acc * a
        out = acc / l
        for r in range(hpc):
            hh = c * hpc + r
            o_ref[0, :, hh * dv:(hh + 1) * dv] = out[:, r * tq:(r + 1) * tq].T.astype(o_ref.dtype)


def _flash_call(q, k, v, *, slopes=None, window=0, mask=None, mask_block=0, mask_t=False, sinks=None,
                tq=128, tk=256, hpc=None, skip_tiles=False, out_dtype=BF16, name="flash"):
    B, H, S, dq = q.shape
    G = k.shape[1]
    dv = v.shape[-2]
    tk = min(tk, S)
    assert tk % tq == 0 and S % tk == 0
    R = H // G
    hpc = R if hpc is None else hpc
    assert R % hpc == 0
    C = H // hpc
    in_specs = [
        pl.BlockSpec((1, H, tq, dq), lambda b, i: (b, 0, i, 0)),
        pl.BlockSpec((1, G, S, dq), lambda b, i: (b, 0, 0, 0)),
        pl.BlockSpec((1, G, dv, S), lambda b, i: (b, 0, 0, 0)),
    ]
    args = [q, k, v]
    mask_per_head = False
    if mask is not None:
        hm = mask.shape[1]
        mask_per_head = hm > 1
        if mask_t:
            in_specs.append(pl.BlockSpec((1, hm, mask.shape[2], tq), lambda b, i: (b, 0, 0, i)))
        else:
            in_specs.append(pl.BlockSpec((1, hm, tq, mask.shape[3]), lambda b, i: (b, 0, i, 0)))
        args.append(mask)
    if sinks is not None:
        in_specs.append(pl.BlockSpec(sinks.shape, lambda b, i: (0, 0)))
        args.append(sinks)
    kern = functools.partial(
        _flash_kernel, H=H, G=G, hpc=hpc, tq=tq, tk=tk,
        slopes=None if slopes is None else tuple(float(s) * LOG2E for s in slopes),
        window=window, mask_block=mask_block if mask is not None else 0,
        mask_per_head=mask_per_head, mask_t=mask_t, has_sink=sinks is not None,
        skip=skip_tiles)
    ones_rows = 16
    scratch = [pltpu.VMEM((C, 1, hpc * tq), F32), pltpu.VMEM((C, dv + ones_rows, hpc * tq), F32)]
    if mask is not None:
        assert tk % mask_block == 0
    if skip_tiles:
        assert mask is not None
        scratch.append(pltpu.SMEM((S // tk,), jnp.int32))
    return pl.pallas_call(
        kern,
        grid=(B, S // tq),
        in_specs=in_specs,
        out_specs=pl.BlockSpec((1, tq, H * dv), lambda b, i: (b, i, 0)),
        out_shape=jax.ShapeDtypeStruct((B, S, H * dv), out_dtype),
        scratch_shapes=scratch,
        compiler_params=pltpu.CompilerParams(vmem_limit_bytes=VMEM_LIMIT),
        name=name,
    )(*args)


def _merge_kernel(x_ref, sc1_ref, sh1_ref, g1_ref, sc2_ref, sh2_ref, ng_ref, wg_ref, wb_ref, wo_ref,
                  ocmp_ref, oslc_ref, owin_ref, gs_ref, ob_ref, oc_ref, od_ref, wr_ref, br_ref,
                  xo_ref, h2_ref, rt_ref):
    hd = HEAD_DIM
    x = x_ref[0]
    ng = ng_ref[...]
    h = _rms(x, ng[0:1]) * (1.0 + sc1_ref[0]) + sh1_ref[0]
    hb = h.astype(BF16)
    gs = gs_ref[0]
    ocmp = ocmp_ref[0]
    oslc = oslc_ref[0]
    owin = owin_ref[0]
    parts = []
    for hh in range(NSA_HEADS):
        c0 = GATE_LANE0 + 3 * hh
        sl = slice(hh * hd, (hh + 1) * hd)
        parts.append(gs[:, c0:c0 + 1] * ocmp[:, sl] + gs[:, c0 + 1:c0 + 2] * oslc[:, sl]
                     + gs[:, c0 + 2:c0 + 3] * owin[:, sl])
    o_a = jnp.concatenate(parts, axis=1).astype(BF16)
    branches = (o_a, ob_ref[0], oc_ref[0], od_ref[0])
    D = x.shape[1]
    mixed = None
    for n in range(N_BRANCH):
        gate = _sigmoid(_dot(hb, wg_ref[:, n * D:(n + 1) * D]))
        term = gate * _dot(branches[n], wb_ref[n])
        mixed = term if mixed is None else mixed + term
    xn = x + g1_ref[0] * _dot(mixed.astype(BF16), wo_ref[...])
    xo_ref[0] = xn
    h2 = _rms(xn, ng[1:2]) * (1.0 + sc2_ref[0]) + sh2_ref[0]
    for c in range(SUB):
        h2_ref[pl.ds(c, h2.shape[0], stride=SUB), :] = h2[:, c * LANE:(c + 1) * LANE]

    logits = _dot3(h2, wr_ref[...]) + br_ref[...]
    lane = lax.broadcasted_iota(jnp.int32, logits.shape, 1)
    lanef = lane.astype(F32)
    is_c = lane < N_GROUPS
    lc = jnp.where(is_c, logits, NEG_INF)
    mc = jnp.max(lc, axis=-1, keepdims=True)
    grp = jnp.min(jnp.where(lc == mc, lanef, 1e9), axis=-1, keepdims=True)
    p_grp = 1.0 / jnp.sum(jnp.where(is_c, jnp.exp(lc - mc), 0.0), axis=-1, keepdims=True)
    e_lane = lanef - float(N_GROUPS)
    in_grp = (lane >= N_GROUPS) & (lane < N_GROUPS + N_EXPERTS) & (
        jnp.floor(e_lane / EXPERTS_PER_GROUP) == grp)
    lf = jnp.where(in_grp, logits, NEG_INF)
    m1 = jnp.max(lf, axis=-1, keepdims=True)
    i1 = jnp.min(jnp.where(lf == m1, e_lane, 1e9), axis=-1, keepdims=True)
    lf2 = jnp.where(e_lane == i1, NEG_INF, lf)
    m2 = jnp.max(lf2, axis=-1, keepdims=True)
    i2 = jnp.min(jnp.where(lf2 == m2, e_lane, 1e9), axis=-1, keepdims=True)
    e2 = jnp.exp(m2 - m1)
    w1 = p_grp / (1.0 + e2)
    w2 = p_grp * e2 / (1.0 + e2)
    rt = jnp.where(lane == 0, i1, jnp.where(lane == 1, i2, jnp.where(lane == 2, w1, jnp.where(lane == 3, w2, 0.0))))
    rt_ref[0] = rt


def _merge_call(x, mods, ng, wg, wb, wo, ocmp, oslc, owin, gs, ob, oc, od, wr, br):
    B, S, D = x.shape
    tm = 512
    sc1, sh1, g1, sc2, sh2 = mods

    def full(a):
        return pl.BlockSpec(a.shape, lambda b, i: (0,) * a.ndim)

    modspec = pl.BlockSpec((1, 1, D), lambda b, i: (b, 0, 0))
    row = lambda w: pl.BlockSpec((1, tm, w), lambda b, i: (b, i, 0))
    in_specs = [row(D), modspec, modspec, modspec, modspec, modspec, full(ng), full(wg), full(wb), full(wo),
                row(256), row(256), row(256), row(128), row(256), row(256), row(256), full(wr), full(br)]
    return pl.pallas_call(
        _merge_kernel,
        grid=(B, S // tm),
        in_specs=in_specs,
        out_specs=[row(D), pl.BlockSpec((tm * SUB, LANE), lambda b, i: (b * (S // tm) + i, 0)), row(128)],
        out_shape=[jax.ShapeDtypeStruct((B, S, D), F32), jax.ShapeDtypeStruct((B * S * SUB, LANE), F32),
                   jax.ShapeDtypeStruct((B, S, 128), F32)],
        compiler_params=pltpu.CompilerParams(vmem_limit_bytes=VMEM_LIMIT),
        name="merge_router",
    )(x, sc1, sh1, g1, sc2, sh2, ng, wg, wb, wo, ocmp, oslc, owin, gs, ob, oc, od, wr, br)


def _expert_kernel(blk_e_ref, nact_ref, tok_ref, dst_ref, h2_hbm, w13_ref, w2_ref, y2_hbm, xbuf, ybuf, gsem, ssem):
    del blk_e_ref
    i = pl.program_id(0)
    nb = nact_ref[0]
    slot = i % 2
    rows = EXPERT_ROWS

    def tile_rows(t):
        return pl.ds(pl.multiple_of(t * SUB, SUB), SUB)

    def gather_copy(blk, s, r):
        tok = tok_ref[blk * rows + r]
        return pltpu.make_async_copy(h2_hbm.at[tile_rows(tok), :], xbuf.at[s, tile_rows(r), :], gsem.at[s])

    def scatter_copy(blk, s, r):
        d = dst_ref[blk * rows + r]
        return pltpu.make_async_copy(ybuf.at[s, tile_rows(r), :], y2_hbm.at[tile_rows(d), :], ssem.at[s])

    def loop_rows(fn):
        def body(r, carry):
            fn(r)
            return carry
        lax.fori_loop(0, rows, body, 0, unroll=8)

    def wait_gather(s):
        pltpu.make_async_copy(h2_hbm.at[pl.ds(0, rows * SUB), :], xbuf.at[s], gsem.at[s]).wait()

    def wait_scatter(s):
        pltpu.make_async_copy(ybuf.at[s], y2_hbm.at[pl.ds(0, rows * SUB), :], ssem.at[s]).wait()

    def compute():
        ab = None
        for c in range(0, SUB, 2):
            xc = jnp.concatenate([xbuf[slot, pl.ds(c, rows, stride=SUB), :],
                                  xbuf[slot, pl.ds(c + 1, rows, stride=SUB), :]], axis=1).astype(BF16)
            part = _dot(xc, w13_ref[0, 0, c * LANE:(c + 2) * LANE, :].astype(BF16))
            ab = part if ab is None else ab + part
        de = ab.shape[1] // 2
        a = ab[:, :de]
        b = ab[:, de:]
        act = ((a * _sigmoid(a)) * b).astype(BF16)
        for c in range(0, SUB, 2):
            y = _dot(act, w2_ref[0, 0, :, c * LANE:(c + 2) * LANE].astype(BF16))
            ybuf[slot, pl.ds(c, rows, stride=SUB), :] = y[:, :LANE]
            ybuf[slot, pl.ds(c + 1, rows, stride=SUB), :] = y[:, LANE:]

    def step(scatter_prev):
        nxt = jnp.minimum(i + 1, nb - 1)
        for r in range(rows):
            if scatter_prev:
                scatter_copy(i - 1, 1 - slot, r).start()
            gather_copy(nxt, 1 - slot, r).start()
        compute()

    @pl.when(i == 0)
    def _():
        loop_rows(lambda r: gather_copy(0, 0, r).start())

    @pl.when(i < nb)
    def _():
        wait_gather(slot)

    @pl.when((i >= 2) & (i < nb))
    def _():
        wait_scatter(slot)

    @pl.when(i == 0)
    def _():
        step(False)

    @pl.when((i > 0) & (i < nb))
    def _():
        step(True)

    @pl.when(i == nb - 1)
    def _():
        loop_rows(lambda r: scatter_copy(i, slot, r).start())
        wait_gather(1 - slot)

        @pl.when(nb >= 2)
        def _():
            wait_scatter(1 - slot)
        wait_scatter(slot)

    @pl.when(i >= nb)
    def _():
        d0 = pl.multiple_of(dst_ref[i * rows] * SUB, SUB)
        fill = pltpu.make_async_copy(ybuf.at[slot], y2_hbm.at[pl.ds(d0, rows * SUB), :], ssem.at[slot])
        fill.start()
        fill.wait()


def _expert_call(h2, blk_e, n_active, row_tok, row_dst, w13, w2, layer):
    D = SUB * LANE
    assert h2.shape[1] == LANE and w13.shape[2] == D
    n_blocks = blk_e.shape[0]
    de2 = w13.shape[-1]
    grid_spec = pltpu.PrefetchScalarGridSpec(
        num_scalar_prefetch=4,
        grid=(n_blocks,),
        in_specs=[
            pl.BlockSpec(memory_space=pl.ANY),
            pl.BlockSpec((1, 1, D, de2), lambda i, be, na, rt, rd: (layer, be[i], 0, 0)),
            pl.BlockSpec((1, 1, de2 // 2, D), lambda i, be, na, rt, rd: (layer, be[i], 0, 0)),
        ],
        out_specs=pl.BlockSpec(memory_space=pl.ANY),
        scratch_shapes=[pltpu.VMEM((2, EXPERT_ROWS * SUB, LANE), F32), pltpu.VMEM((2, EXPERT_ROWS * SUB, LANE), F32),
                        pltpu.SemaphoreType.DMA((2,)), pltpu.SemaphoreType.DMA((2,))],
    )
    return pl.pallas_call(
        _expert_kernel,
        grid_spec=grid_spec,
        out_shape=jax.ShapeDtypeStruct((row_tok.shape[0] * SUB, LANE), F32),
        compiler_params=pltpu.CompilerParams(
            dimension_semantics=("arbitrary",), vmem_limit_bytes=VMEM_LIMIT),
        name="experts",
    )(blk_e, n_active, row_tok, row_dst, h2, w13, w2)


def _combine_kernel(xn_ref, g2_ref, rt_ref, y0_ref, y1_ref, o_ref):
    rt = rt_ref[...]
    tm = xn_ref.shape[0]
    w0 = rt[:, 2:3]
    w1 = rt[:, 3:4]
    g2 = g2_ref[0]
    for c in range(SUB):
        cols = slice(c * LANE, (c + 1) * LANE)
        y = w0 * y0_ref[pl.ds(c, tm, stride=SUB), :] + w1 * y1_ref[pl.ds(c, tm, stride=SUB), :]
        o_ref[:, cols] = xn_ref[:, cols] + g2[:, cols] * y


def _combine_call(xn, g2, route, y2):
    B, S, D = xn.shape
    tm = 512
    spb = S // tm
    nt = B * spb
    return pl.pallas_call(
        _combine_kernel,
        grid=(B, spb),
        in_specs=[
            pl.BlockSpec((tm, D), lambda b, i: (b * spb + i, 0)),
            pl.BlockSpec((1, 1, D), lambda b, i: (b, 0, 0)),
            pl.BlockSpec((tm, 128), lambda b, i: (b * spb + i, 0)),
            pl.BlockSpec((tm * SUB, LANE), lambda b, i: (b * spb + i, 0)),
            pl.BlockSpec((tm * SUB, LANE), lambda b, i: (nt + b * spb + i, 0)),
        ],
        out_specs=pl.BlockSpec((tm, D), lambda b, i: (b * spb + i, 0)),
        out_shape=jax.ShapeDtypeStruct((B * S, D), F32),
        compiler_params=pltpu.CompilerParams(vmem_limit_bytes=VMEM_LIMIT),
        name="moe_combine",
    )(xn.reshape(B * S, D), g2, route, y2, y2).reshape(B, S, D)


def _moe(h2, route, w13, w2, layer):
    N = h2.shape[0] // SUB
    K = 2
    E = N_EXPERTS
    rows = EXPERT_ROWS
    flat_e = route[:, 0:2].astype(jnp.int32).reshape(-1)
    order = jnp.argsort(flat_e).astype(jnp.int32)
    counts = jnp.sum((flat_e[:, None] == jnp.arange(E)[None, :]).astype(jnp.int32), axis=0)
    padded = (counts + rows - 1) // rows * rows
    pend = jnp.cumsum(padded)
    pstart = pend - padded
    start = jnp.cumsum(counts) - counts
    n_blocks = (N * K) // rows + E
    R = n_blocks * rows
    blk_e = jnp.minimum(
        jnp.sum((jnp.arange(n_blocks)[:, None] * rows >= pend[None, :]).astype(jnp.int32), axis=1), E - 1)
    pos = jnp.arange(R, dtype=jnp.int32).reshape(n_blocks, rows)
    local = pos - pstart[blk_e][:, None]
    is_real = ((pos < pend[E - 1]) & (local < counts[blk_e][:, None])).reshape(R)
    rank = jnp.clip(start[blk_e][:, None] + local, 0, N * K - 1).reshape(R)
    pair = order[rank]
    flat = pos.reshape(R)
    real_before = jnp.cumsum(is_real.astype(jnp.int32)) - is_real.astype(jnp.int32)
    row_tok = jnp.where(is_real, pair // K, flat % N).astype(jnp.int32)
    row_dst = jnp.where(is_real, (pair % K) * N + pair // K, N * K + flat - real_before).astype(jnp.int32)
    n_active = (pend[E - 1:E] // rows).astype(jnp.int32)
    return _expert_call(h2, blk_e.astype(jnp.int32), n_active, row_tok, row_dst, w13, w2, layer)


def kernel(x, c, w_ada, b_ada, norm_gain, w_in, qk_gain, cmp_pe, cmp_w1, cmp_w2, swa_sinks,
           lat_gain_q, lat_gain_kv, rope_gain, w_uq, w_ukv, w_branch, w_out,
           w_coarse, b_coarse, w_fine, b_fine, w13, w2):
    B, S, D = x.shape
    L = w_in.shape[0]
    assert S % MOBA_BLOCK == 0 and D == 1024
    slopes = _alibi_slopes()

    half = MLA_ROPE // 2
    inv = ROPE_THETA ** (-jnp.arange(half, dtype=F32) / half)
    ang = jnp.arange(S).astype(F32)[:, None] * inv[None, :]
    cos = jnp.tile(jnp.cos(ang), (1, 8))
    sin = jnp.tile(jnp.concatenate([-jnp.sin(ang), jnp.sin(ang)], axis=1), (1, 4))

    mod = _ada_mod(c, w_ada, b_ada)

    w_attn = jnp.concatenate(
        [w_in[:, :, :448], w_in[:, :, 512:576], w_in[:, :, 448:512], w_in[:, :, 576:640],
         w_in[:, :, 652:ATTN_OLD], w_in[:, :, 640:652],
         jnp.zeros((L, D, ATTN_COLS - ATTN_OLD), F32)], axis=2).astype(BF16)
    w_gate = w_in[:, :, ATTN_OLD:].astype(BF16)
    dq, dkv = MLA_NOPE + MLA_ROPE, MLA_NOPE + MLA_V
    uq = w_uq.reshape(L, MLA_Q_RANK, MLA_HEADS, dq)
    w_uq_b = jnp.concatenate([uq[..., :MLA_NOPE].reshape(L, MLA_Q_RANK, -1),
                              uq[..., MLA_NOPE:].reshape(L, MLA_Q_RANK, -1)], axis=2).astype(BF16)
    ukv = w_ukv.reshape(L, MLA_KV_RANK, MLA_HEADS, dkv)
    w_ukv_b = jnp.concatenate([ukv[..., :MLA_NOPE].reshape(L, MLA_KV_RANK, -1),
                               ukv[..., MLA_NOPE:].reshape(L, MLA_KV_RANK, -1)], axis=2).astype(BF16)

    qs = HEAD_DIM ** -0.5 * LOG2E
    ms = dq ** -0.5 * LOG2E

    def lanes(v, reps, scale=1.0):
        r = jnp.tile(v, (1, reps)) * scale
        return jnp.pad(r, ((0, 0), (0, 4 * HEAD_DIM - r.shape[1])))

    gq = qk_gain
    gain_rows = jnp.stack([
        lanes(gq[:, QK_NSA_Q], 4, qs),
        lanes(jnp.concatenate([gq[:, QK_NSA_KS], gq[:, QK_NSA_KW]], axis=1), 1),
        lanes(gq[:, QK_SWA_Q], 4, qs),
        lanes(gq[:, QK_SWA_K], 2),
        lanes(gq[:, QK_MOBA_Q], 4, qs),
        lanes(gq[:, QK_MOBA_K], 4),
        lanes(gq[:, QK_MLA_Q], 4, ms),
        lanes(rope_gain[:, 0], 4, ms),
        lanes(gq[:, QK_MLA_K], 4),
        lanes(rope_gain[:, 1], 4),
    ], axis=1)
    bd64 = jnp.asarray(np.kron(np.eye(4), np.ones((HEAD_DIM, HEAD_DIM))) / HEAD_DIM, dtype=BF16)
    bd32 = jnp.asarray(np.kron(np.eye(4), np.ones((MLA_ROPE, MLA_ROPE))) / MLA_ROPE, dtype=BF16)
    w_branch_b = w_branch.astype(BF16)
    w_out_b = w_out.astype(BF16)
    w_router = jnp.concatenate(
        [w_coarse, w_fine, jnp.zeros((L, D, 128 - N_GROUPS - N_EXPERTS), F32)], axis=2)
    b_router = jnp.concatenate(
        [b_coarse, b_fine, jnp.zeros((L, 128 - N_GROUPS - N_EXPERTS), F32)], axis=1)
    sinks_pad = jnp.concatenate([swa_sinks, jnp.zeros((L, 128 - SWA_HEADS), F32)], axis=1)

    for l in range(L):
        m6 = mod[l].reshape(B, 6, 1, D)
        sh1, sc1, g1, sh2, sc2, g2 = (m6[:, j] for j in range(6))
        (qa, kcr, vcr, ks, vs, kw, vw, gs, qb, kb, vb, qc, kc, vc, mc, qd, kd, vd) = _proj_call(
            x, sc1, sh1, norm_gain[l, 0:1], w_attn[l], gain_rows[l], bd64, bd32, lat_gain_q[l][None],
            lat_gain_kv[l][None], w_uq_b[l], w_ukv_b[l], cos, sin)
        kcmp, vcmp = _compress_call(kcr, vcr, cmp_pe[l], cmp_w1[l], cmp_w2[l], qk_gain[l, QK_NSA_KC][None])
        o_cmp, sel = _cmp_attn_call(qa, kcmp, vcmp, slopes[0])
        o_slc = _flash_call(qa, ks, vs, slopes=slopes[0], mask=sel, mask_block=SEL_BLOCK, mask_t=True,
                            tq=256, tk=256, hpc=2, skip_tiles=True, out_dtype=F32, name="nsa_slc")
        o_win = _flash_call(qa, kw, vw, slopes=slopes[0], window=NSA_WINDOW,
                            tq=256, tk=256, hpc=2, out_dtype=F32, name="nsa_win")
        o_b = _flash_call(qb, kb, vb, slopes=slopes[1], window=SWA_WINDOW, sinks=sinks_pad[l][None],
                          tq=256, tk=256, name="swa")
        o_c = _flash_call(qc, kc, vc, slopes=slopes[2], mask=mc, mask_block=MOBA_BLOCK,
                          tq=512, tk=512, name="moba")
        o_d = _flash_call(qd, kd, vd, tq=512, tk=512, name="mla")
        xn, h2, route = _merge_call(
            x, (sc1, sh1, g1, sc2, sh2), norm_gain[l], w_gate[l], w_branch_b[l], w_out_b[l],
            o_cmp, o_slc, o_win, gs, o_b, o_c, o_d, w_router[l], b_router[l][None])
        route = route.reshape(B * S, 128)
        y2 = _moe(h2, route, w13, w2, l)
        x = _combine_call(xn, g2, route, y2)
    return x
```

```python
import functools
import math

import numpy as np
import jax
import jax.numpy as jnp
from jax import lax
from jax.experimental import pallas as pl
from jax.experimental.pallas import tpu as pltpu

F32 = jnp.float32
BF16 = jnp.bfloat16

HEAD_DIM = 64
NEG_INF = -1e30
EPS = 1e-6
NSA_HEADS = 4
CMP_BLOCK = 32
CMP_STRIDE = 16
CMP_HIDDEN = 256
SEL_BLOCK = 64
SEL_TOPK = 8
NSA_WINDOW = 512
FORCE_BONUS = 1e4
SWA_HEADS = 4
SWA_KV_HEADS = 2
SWA_WINDOW = 128
MOBA_HEADS = 4
MOBA_BLOCK = 256
MOBA_TOPK = 3
MLA_HEADS = 4
MLA_Q_RANK = 384
MLA_KV_RANK = 128
MLA_NOPE = 64
MLA_ROPE = 32
MLA_V = 64
ROPE_THETA = 10000.0
N_BRANCH = 4
BRANCH_WIDTH = 256
N_GROUPS = 4
EXPERTS_PER_GROUP = 8
N_EXPERTS = N_GROUPS * EXPERTS_PER_GROUP
D_EXPERT = 256

(GR_NSA_Q, GR_NSA_K, GR_SWA_Q, GR_SWA_K, GR_MOBA_Q, GR_MOBA_K,
 GR_MLA_QN, GR_MLA_QR, GR_MLA_KN, GR_MLA_KR) = range(10)
QK_NSA_Q, QK_NSA_KC, QK_NSA_KS, QK_NSA_KW = 0, 1, 2, 3
QK_SWA_Q, QK_SWA_K, QK_MOBA_Q, QK_MOBA_K, QK_MLA_Q, QK_MLA_K = 4, 5, 6, 7, 8, 9

ATTN_OLD = 2476
ATTN_COLS = 2560
GATE_LANE0 = 32
LOG2E = math.log2(math.e)
SUB, LANE = 8, 128
EXPERT_ROWS = 256
VMEM_LIMIT = 56 * 1024 * 1024


def _alibi_slopes():
    n = NSA_HEADS + SWA_HEADS + MOBA_HEADS

    def pow2(m):
        start = 2.0 ** (-8.0 / m)
        return [start ** (i + 1) for i in range(m)]

    c = 2 ** int(math.floor(math.log2(n)))
    s = pow2(c) + (pow2(2 * c)[0::2][: n - c] if c < n else [])
    s = -np.sort(-np.asarray(s, np.float32))
    return s.reshape(NSA_HEADS, 3).T


def _dot(a, b):
    return jnp.dot(a, b, preferred_element_type=F32)


def _dot_nt(a, b):
    return lax.dot_general(a, b, (((1,), (1,)), ((), ())), preferred_element_type=F32)


def _split(a):
    hi = a.astype(BF16)
    lo = (a - hi.astype(F32)).astype(BF16)
    return hi, lo


def _dot3(a, b):
    ah, al = _split(a)
    bh, bl = _split(b)
    return _dot(ah, bh) + (_dot(ah, bl) + _dot(al, bh))


def _dot3_nt(a, b):
    ah, al = _split(a)
    bh, bl = _split(b)
    return _dot_nt(ah, bh) + (_dot_nt(ah, bl) + _dot_nt(al, bh))


def _rms(x, g):
    return x * lax.rsqrt(jnp.mean(x * x, axis=-1, keepdims=True) + EPS) * g


def _sigmoid(x):
    return 1.0 / (1.0 + jnp.exp(-x))


AUG_MASK0 = 6


def _bf16_pieces(x, n=3):
    out, r = [], float(x)
    for _ in range(n):
        piece = float(np.asarray(r, np.float32).astype(jnp.bfloat16).astype(np.float32))
        out.append(piece)
        r -= piece
    return out


def _topk_mask(score, k, axis=1):
    n = score.shape[axis]
    iota = lax.broadcasted_iota(jnp.int32, score.shape, axis).astype(F32)
    sel = jnp.zeros(score.shape, F32)
    for _ in range(k):
        m = jnp.max(score, axis=axis, keepdims=True)
        idx = jnp.min(jnp.where(score == m, iota, float(n)), axis=axis, keepdims=True)
        hit = iota == idx
        sel = jnp.where(hit, jnp.where(m > 0.5 * NEG_INF, 1.0, 0.0), sel)
        score = jnp.where(hit, -3e38, score)
    return sel


def _ada_kernel(c_ref, w_ref, b_ref, o_ref):
    c = c_ref[...]
    a = c * _sigmoid(c)
    o_ref[0] = _dot(a, w_ref[0]) + b_ref[0]


def _ada_mod(c, w_ada, b_ada):
    L, D, D6 = w_ada.shape
    B = c.shape[0]
    tn = 1024
    return pl.pallas_call(
        _ada_kernel,
        grid=(L, D6 // tn),
        in_specs=[
            pl.BlockSpec((B, D), lambda l, j: (0, 0)),
            pl.BlockSpec((1, D, tn), lambda l, j: (l, 0, j)),
            pl.BlockSpec((1, 1, tn), lambda l, j: (l, 0, j)),
        ],
        out_specs=pl.BlockSpec((1, B, tn), lambda l, j: (l, 0, j)),
        out_shape=jax.ShapeDtypeStruct((L, B, D6), F32),
        compiler_params=pltpu.CompilerParams(vmem_limit_bytes=VMEM_LIMIT),
        name="ada_mod",
    )(c, w_ada, b_ada.reshape(L, 1, D6))


def _rms_blocks(x, bd, g):
    hi, lo = _split(x * x)
    ms = _dot(hi, bd) + _dot(lo, bd)
    return x * lax.rsqrt(ms + EPS) * g


def _proj_kernel(x_ref, sc_ref, sh_ref, ng_ref, w_ref, gr_ref, bd64_ref, bd32_ref, lgq_ref, lgkv_ref,
                 wuq_ref, wukv_ref, cos_ref, sin_ref,
                 qa_ref, kcr_ref, vcr_ref, ks_ref, vs_ref, kw_ref, vw_ref, gs_ref,
                 qb_ref, kb_ref, vb_ref, qc_ref, kc_ref, vc_ref, mc_ref, qd_ref, kd_ref, vd_ref,
                 kmean_s):
    i = pl.program_id(1)
    hd = HEAD_DIM
    x = x_ref[0]
    h = _rms(x, ng_ref[...]) * (1.0 + sc_ref[0]) + sh_ref[0]
    hb = h.astype(BF16)
    bd64 = bd64_ref[...]
    bd64h = bd64_ref[0:2 * hd, 0:2 * hd]
    bd32 = bd32_ref[...]

    def grow(r, w=4 * hd):
        return gr_ref[r:r + 1, 0:w]

    def proj(a, b):
        return _dot(hb, w_ref[:, a:b])

    def store_heads(ref, slab, n):
        for hh in range(n):
            ref[0, hh] = slab[:, hh * hd:(hh + 1) * hd].astype(BF16)

    def store_heads_t(ref, slab_t, n):
        for hh in range(n):
            ref[0, hh] = slab_t[hh * hd:(hh + 1) * hd].astype(BF16)

    store_heads(qa_ref, _rms_blocks(proj(0, 256), bd64, grow(GR_NSA_Q)), NSA_HEADS)
    p = proj(256, 640)
    kcr_ref[0] = p[:, 0:64]
    vcr_ref[0] = p[:, 64:128]
    kk = _rms_blocks(p[:, 128:256], bd64h, grow(GR_NSA_K, 2 * hd))
    ks_ref[0, 0] = kk[:, :hd].astype(BF16)
    kw_ref[0, 0] = kk[:, hd:].astype(BF16)
    vt = p[:, 256:384].T
    vs_ref[0, 0] = vt[:hd].astype(BF16)
    vw_ref[0, 0] = vt[hd:].astype(BF16)

    p = proj(640, 1152)
    store_heads(qb_ref, _rms_blocks(p[:, 0:256], bd64, grow(GR_SWA_Q)), SWA_HEADS)
    store_heads(kb_ref, _rms_blocks(p[:, 256:384], bd64h, grow(GR_SWA_K, 2 * hd)), SWA_KV_HEADS)
    store_heads_t(vb_ref, p[:, 384:512].T, SWA_KV_HEADS)

    @pl.when(i == 0)
    def _():
        kmean_s[...] = jnp.zeros(kmean_s.shape, F32)

    p = proj(1152, 1920)
    qn = _rms_blocks(p[:, 0:256], bd64, grow(GR_MOBA_Q))
    kn = _rms_blocks(p[:, 256:512], bd64, grow(GR_MOBA_K))
    store_heads(qc_ref, qn, MOBA_HEADS)
    store_heads(kc_ref, kn, MOBA_HEADS)
    store_heads_t(vc_ref, p[:, 512:768].T, MOBA_HEADS)
    nblk = kmean_s.shape[0]
    blk_iota = lax.broadcasted_iota(jnp.int32, (1, nblk), 1)
    head_of_lane = lax.broadcasted_iota(jnp.int32, (1, 4 * hd), 1) // hd
    kmeans = kmean_s[...]
    for hh in range(MOBA_HEADS):
        g = _dot3_nt(jnp.where(head_of_lane == hh, qn, 0.0), kmeans)
        g = jnp.where(blk_iota < i, g, NEG_INF)
        mc_ref[0, hh] = jnp.where(blk_iota == i, 1.0, _topk_mask(g, MOBA_TOPK))
    kmean_s[pl.ds(i, 1), :] = jnp.mean(kn, axis=0, keepdims=True)

    p = proj(1920, 2560)
    gs_ref[0] = _sigmoid(p[:, 512:640])
    cos = cos_ref[...]
    sin = sin_ref[...]
    half = MLA_ROPE // 2
    first_half = lax.broadcasted_iota(jnp.int32, (1, 4 * MLA_ROPE), 1) % MLA_ROPE < half

    def rope(v):
        swapped = jnp.where(first_half, pltpu.roll(v, 4 * MLA_ROPE - half, 1), pltpu.roll(v, half, 1))
        return v * cos + swapped * sin

    qlat = _dot(_rms(p[:, 0:384], lgq_ref[...]).astype(BF16), wuq_ref[...])
    kvlat = _dot(_rms(p[:, 384:512], lgkv_ref[...]).astype(BF16), wukv_ref[...])
    q_nope = _rms_blocks(qlat[:, 0:256], bd64, grow(GR_MLA_QN))
    q_rot = rope(_rms_blocks(qlat[:, 256:384], bd32, grow(GR_MLA_QR, 2 * hd)))
    k_nope = _rms_blocks(kvlat[:, 0:256], bd64, grow(GR_MLA_KN))
    k_rot = rope(_rms_blocks(p[:, 512:640], bd32, grow(GR_MLA_KR, 2 * hd)))[:, :MLA_ROPE]
    store_heads_t(vd_ref, kvlat[:, 256:512].T, MLA_HEADS)
    for hh in range(MLA_HEADS):
        qd_ref[0, hh] = jnp.concatenate(
            [q_nope[:, hh * hd:(hh + 1) * hd], q_rot[:, hh * MLA_ROPE:(hh + 1) * MLA_ROPE]], axis=1).astype(BF16)
        kd_ref[0, hh] = jnp.concatenate([k_nope[:, hh * hd:(hh + 1) * hd], k_rot], axis=1).astype(BF16)


def _proj_call(x, sc1, sh1, ng, w_attn, gain_rows, bd64, bd32, lgq, lgkv, wuq, wukv, cos, sin):
    B, S, D = x.shape
    tm = MOBA_BLOCK
    nblk = S // tm
    hd = HEAD_DIM

    def full(shape):
        return pl.BlockSpec(shape, lambda b, i: (0,) * len(shape))

    def heads(nh, d):
        return pl.BlockSpec((1, nh, tm, d), lambda b, i: (b, 0, i, 0))

    in_specs = [
        pl.BlockSpec((1, tm, D), lambda b, i: (b, i, 0)),
        pl.BlockSpec((1, 1, D), lambda b, i: (b, 0, 0)),
        pl.BlockSpec((1, 1, D), lambda b, i: (b, 0, 0)),
        full((1, D)),
        full((D, ATTN_COLS)),
        full(gain_rows.shape),
        full(bd64.shape),
        full(bd32.shape),
        full(lgq.shape),
        full(lgkv.shape),
        full(wuq.shape),
        full(wukv.shape),
        pl.BlockSpec((tm, 4 * MLA_ROPE), lambda b, i: (i, 0)),
        pl.BlockSpec((tm, 4 * MLA_ROPE), lambda b, i: (i, 0)),
    ]
    row64 = pl.BlockSpec((1, tm, hd), lambda b, i: (b, i, 0))
    def heads_t(nh, d):
        return pl.BlockSpec((1, nh, d, tm), lambda b, i: (b, 0, 0, i))

    out_specs = [
        heads(4, hd), row64, row64, heads(1, hd), heads_t(1, hd), heads(1, hd), heads_t(1, hd),
        pl.BlockSpec((1, tm, 128), lambda b, i: (b, i, 0)),
        heads(4, hd), heads(2, hd), heads_t(2, hd),
        heads(4, hd), heads(4, hd), heads_t(4, hd), heads(4, nblk),
        heads(4, MLA_NOPE + MLA_ROPE), heads(4, MLA_NOPE + MLA_ROPE), heads_t(4, MLA_V),
    ]

    def sd(shape, dt):
        return jax.ShapeDtypeStruct(shape, dt)

    out_shape = [
        sd((B, 4, S, hd), BF16), sd((B, S, hd), F32), sd((B, S, hd), F32),
        sd((B, 1, S, hd), BF16), sd((B, 1, hd, S), BF16), sd((B, 1, S, hd), BF16), sd((B, 1, hd, S), BF16),
        sd((B, S, 128), F32),
        sd((B, 4, S, hd), BF16), sd((B, 2, S, hd), BF16), sd((B, 2, hd, S), BF16),
        sd((B, 4, S, hd), BF16), sd((B, 4, S, hd), BF16), sd((B, 4, hd, S), BF16), sd((B, 4, S, nblk), F32),
        sd((B, 4, S, 96), BF16), sd((B, 4, S, 96), BF16), sd((B, 4, MLA_V, S), BF16),
    ]
    return pl.pallas_call(
        _proj_kernel,
        grid=(B, nblk),
        in_specs=in_specs,
        out_specs=out_specs,
        out_shape=out_shape,
        scratch_shapes=[pltpu.VMEM((nblk, MOBA_HEADS * hd), F32)],
        compiler_params=pltpu.CompilerParams(
            dimension_semantics=("arbitrary", "arbitrary"), vmem_limit_bytes=VMEM_LIMIT),
        name="proj_prep",
    )(x, sc1, sh1, ng, w_attn, gain_rows, bd64, bd32, lgq, lgkv, wuq, wukv, cos, sin)


def _compress_kernel(gk_ref, gv_ref, pe_ref, w1_ref, w2_ref, gkc_ref, kc_ref, vc_ref):
    half = w1_ref.shape[1] // 2
    outs = []
    for j, g_ref in enumerate((gk_ref, gv_ref)):
        g = g_ref[0].astype(BF16)
        top = _dot(g, w1_ref[j, :half].astype(BF16))
        bot = _dot(g, w1_ref[j, half:].astype(BF16))
        bot = jnp.concatenate([bot[1:], bot[:1]], axis=0)
        pe = jnp.broadcast_to(pe_ref[j], (8, pe_ref.shape[2]))
        bias = _dot3(pe, w1_ref[j])[0:1]
        hid = top + bot + bias
        hid = hid * _sigmoid(hid)
        outs.append(_dot(hid.astype(BF16), w2_ref[j].astype(BF16)))
    kc_ref[0] = _rms(outs[0], gkc_ref[...]).astype(BF16)
    dk = outs[1].shape[1]
    vc_ref[0] = jnp.concatenate([outs[1], outs[1]], axis=1).T[:dk].astype(BF16)


def _compress_call(kc_raw, vc_raw, pe, w1, w2, g_kc):
    B, S, dk = kc_raw.shape
    n_grp = S // CMP_STRIDE
    gk = kc_raw.reshape(B, n_grp, CMP_STRIDE * dk)
    gv = vc_raw.reshape(B, n_grp, CMP_STRIDE * dk)
    pe_flat = pe.reshape(2, 1, CMP_BLOCK * dk)
    grp_spec = pl.BlockSpec((1, n_grp, CMP_STRIDE * dk), lambda b: (b, 0, 0))
    out_spec = pl.BlockSpec((1, n_grp, dk), lambda b: (b, 0, 0))
    return pl.pallas_call(
        _compress_kernel,
        grid=(B,),
        in_specs=[
            grp_spec, grp_spec,
            pl.BlockSpec(pe_flat.shape, lambda b: (0, 0, 0)),
            pl.BlockSpec(w1.shape, lambda b: (0, 0, 0)),
            pl.BlockSpec(w2.shape, lambda b: (0, 0, 0)),
            pl.BlockSpec((1, dk), lambda b: (0, 0)),
        ],
        out_specs=[out_spec, pl.BlockSpec((1, dk, n_grp), lambda b: (b, 0, 0))],
        out_shape=[jax.ShapeDtypeStruct((B, n_grp, dk), BF16), jax.ShapeDtypeStruct((B, dk, n_grp), BF16)],
        compiler_params=pltpu.CompilerParams(vmem_limit_bytes=VMEM_LIMIT),
        name="nsa_compress",
    )(gk, gv, pe_flat, w1, w2, g_kc)


def _cmp_attn_kernel(q_ref, kc_ref, vct_ref, cover_ref, o_ref, sel_ref, *, slopes, tq, n_cmp):
    i = pl.program_id(1)
    kc = kc_ref[0]
    vct = vct_ref[0]
    ncp = kc.shape[0]
    t_full = i * tq + lax.broadcasted_iota(jnp.int32, (ncp, tq), 1)
    n_iota = lax.broadcasted_iota(jnp.int32, (ncp, tq), 0)
    dist_i = t_full - (n_iota * CMP_STRIDE + (CMP_BLOCK - 1))
    vis = (n_iota < n_cmp) & (dist_i >= 0)
    dist = dist_i.astype(F32)
    visf = vis.astype(F32)
    psum = jnp.zeros((ncp, tq), F32)
    raw = [_dot_nt(kc, q_ref[0, hh]) for hh in range(NSA_HEADS)]
    probs = []
    for hh in range(NSA_HEADS):
        s = raw[hh] - (slopes[hh] * LOG2E) * dist
        s = jnp.where(vis, s, NEG_INF)
        e = jnp.exp2(s - jnp.max(s, axis=0, keepdims=True)) * visf
        p = e / jnp.maximum(jnp.sum(e, axis=0, keepdims=True), 1e-30)
        probs.append(p.astype(BF16))
        psum = psum + p
    outs = [_dot(vct, pb) for pb in probs]
    o_ref[0] = jnp.concatenate(outs, axis=0).T
    ph, plo = _split(psum)
    cover = cover_ref[...]
    p_slc = _dot(cover, ph) + _dot(cover, plo)
    n_sel = cover.shape[0]
    cur = (i * tq + lax.broadcasted_iota(jnp.int32, (1, tq), 1)) // SEL_BLOCK
    j = lax.broadcasted_iota(jnp.int32, (n_sel, 1), 0)
    forced = jnp.where(j == 0, 1.0, jnp.where(j == cur, 1.0, jnp.where(j == cur - 1, 1.0, 0.0)))
    score = jnp.where(j <= cur, p_slc + FORCE_BONUS * forced, NEG_INF)
    sel_ref[0, 0] = _topk_mask(score, min(SEL_TOPK, n_sel), axis=0)


def _cmp_attn_call(qa, kc, vc, slopes):
    B, H, S, dk = qa.shape
    ncp = kc.shape[1]
    n_cmp = (S - CMP_BLOCK) // CMP_STRIDE + 1
    n_sel = S // SEL_BLOCK
    tq = 256
    starts = np.arange(ncp) * CMP_STRIDE
    jb = np.arange(n_sel) * SEL_BLOCK
    cover = ((starts[:, None] < jb[None, :] + SEL_BLOCK) & (starts[:, None] + CMP_BLOCK > jb[None, :])
             & (np.arange(ncp)[:, None] < n_cmp))
    cover = jnp.asarray(cover.T.astype(np.float32), dtype=BF16)
    return pl.pallas_call(
        functools.partial(_cmp_attn_kernel, slopes=tuple(float(s) for s in slopes), tq=tq, n_cmp=n_cmp),
        grid=(B, S // tq),
        in_specs=[
            pl.BlockSpec((1, H, tq, dk), lambda b, i: (b, 0, i, 0)),
            pl.BlockSpec((1, ncp, dk), lambda b, i: (b, 0, 0)),
            pl.BlockSpec((1, dk, ncp), lambda b, i: (b, 0, 0)),
            pl.BlockSpec((n_sel, ncp), lambda b, i: (0, 0)),
        ],
        out_specs=[
            pl.BlockSpec((1, tq, H * dk), lambda b, i: (b, i, 0)),
            pl.BlockSpec((1, 1, n_sel, tq), lambda b, i: (b, 0, 0, i)),
        ],
        out_shape=[jax.ShapeDtypeStruct((B, S, H * dk), F32),
                   jax.ShapeDtypeStruct((B, 1, n_sel, S), F32)],
        compiler_params=pltpu.CompilerParams(vmem_limit_bytes=VMEM_LIMIT),
        name="nsa_cmp_attn",
    )(qa, kc, vc, cover)


def _flash_kernel(*refs, H, G, hpc, tq, tk, slopes, window, mask_block, mask_per_head, mask_t, has_sink, skip):
    refs = list(refs)
    q_ref, k_ref, v_ref = refs[:3]
    pos = 3
    mask_ref = sink_ref = None
    if mask_block:
        mask_ref = refs[pos]
        pos += 1
    if has_sink:
        sink_ref = refs[pos]
        pos += 1
    o_ref, m_s, acc_s = refs[pos:pos + 3]
    need_s = refs[pos + 3] if skip else None
    use_aug = slopes is not None or bool(mask_block)
    R = H // G
    C = H // hpc
    M = hpc * tq
    dv = v_ref.shape[-2]
    i = pl.program_id(1)
    q0 = i * tq
    hi = q0 // tk
    lo = jnp.maximum(q0 - (window - 1), 0) // tk if window else 0

    def rel_pos():
        r_row = jnp.concatenate([lax.broadcasted_iota(jnp.int32, (1, tq), 1)] * hpc, axis=1)
        return (r_row - lax.broadcasted_iota(jnp.int32, (tk, M), 0)).astype(F32)

    dq = q_ref.shape[-1]
    n_mask = mask_ref.shape[-2 if mask_t else -1] if mask_block else 0
    aug_w = LANE if dq + AUG_MASK0 + n_mask <= LANE else 2 * LANE
    aug0 = dq
    lane_a = lax.broadcasted_iota(jnp.int32, (1, aug_w), 1)
    feature_lane = lane_a < dq

    def slope_lanes(hh):
        row = jnp.zeros((1, aug_w), F32)
        if slopes is not None:
            for n, piece in enumerate(_bf16_pieces(slopes[hh])):
                row = jnp.where((lane_a == aug0 + n) | (lane_a == aug0 + n + 3), piece, row)
        return row

    def widen(a):
        return jnp.concatenate([a, jnp.zeros((a.shape[0], aug_w - dq), a.dtype)], axis=1)

    qs, slope_rows, bms = [], [], []
    for c in range(C):
        heads = [c * hpc + r for r in range(hpc)]
        if hpc == 1:
            qs.append(q_ref[0, heads[0]])
        else:
            qs.append(jnp.concatenate([q_ref[0, hh] for hh in heads], axis=0))
        if slopes is not None:
            slope_rows.append(jnp.concatenate([jnp.full((1, tq), slopes[hh], F32) for hh in heads], axis=1))
        if mask_block:
            parts = [mask_ref[0, hh if mask_per_head else 0] for hh in heads]
            bms.append(jnp.concatenate(parts, axis=1) if mask_t else jnp.concatenate(parts, axis=0).T)
    m_s[...] = jnp.full(m_s.shape, NEG_INF, F32)
    acc_s[...] = jnp.zeros(acc_s.shape, F32)
    ones_rows = jnp.ones((acc_s.shape[1] - dv, tk), BF16)

    if mask_block:
        nblk = bms[0].shape[0]
    if need_s is not None:
        col = bms[0]
        for bm in bms[1:]:
            col = jnp.maximum(col, bm)
        col = jnp.max(col, axis=1, keepdims=True)
        bpt = tk // mask_block
        for jt in range(nblk // bpt):
            need_s[jt] = jnp.max(col[jt * bpt:(jt + 1) * bpt]).astype(jnp.int32)

    if use_aug:
        for c in range(C):
            qa = jnp.concatenate(
                [jnp.broadcast_to(slope_lanes(c * hpc + r), (tq, aug_w)) for r in range(hpc)], axis=0)
            if mask_block:
                place = jnp.where(
                    lax.broadcasted_iota(jnp.int32, (nblk, aug_w), 0) + (aug0 + AUG_MASK0) == lane_a,
                    1.0, 0.0).astype(BF16)
                off_sel = ((bms[c] - 1.0) * -NEG_INF).astype(BF16)
                qa = qa + lax.dot_general(off_sel, place, (((0,), (0,)), ((), ())), preferred_element_type=F32)
            qs[c] = jnp.where(feature_lane, widen(qs[c]), qa.astype(BF16))

    def tile_scores(j):
        k0 = pl.multiple_of(j * tk, tk)
        if use_aug:
            t_key = lax.broadcasted_iota(jnp.int32, (tk, aug_w), 0) + k0
            k_aug = jnp.zeros((tk, aug_w), F32)
            if mask_block:
                key_blk = lax.shift_right_logical(t_key, int(math.log2(mask_block)))
                k_aug = jnp.where(key_blk + (aug0 + AUG_MASK0) == lane_a, 1.0, 0.0)
            if slopes is not None:
                t_hi = (lax.shift_right_logical(t_key, 8) * 256).astype(F32)
                t_lo = (t_key & 255).astype(F32)
                k_aug = jnp.where(lane_a < aug0 + 3, t_hi, jnp.where(lane_a < aug0 + 6, t_lo, k_aug))
            k_aug = k_aug.astype(BF16)
        k_cats = {}
        scores = []
        for c in range(C):
            g = (c * hpc) // R
            if g not in k_cats:
                k = k_ref[0, g, pl.ds(k0, tk), :]
                k_cats[g] = jnp.where(feature_lane, widen(k), k_aug) if use_aug else k
            scores.append(_dot_nt(k_cats[g], qs[c]))
        return scores

    def tile_update(j, edge, scores):
        k0 = pl.multiple_of(j * tk, tk)
        valid = None
        if edge:
            dist = rel_pos() + (q0 - k0).astype(F32)
            valid = dist >= 0.0
            if window:
                valid = valid & (dist < float(window))
        probs, alphas = [], []
        for g in range(C):
            s = scores[g]
            if valid is not None:
                s = jnp.where(valid, s, NEG_INF)
            m_prev = m_s[g]
            m_new = jnp.maximum(m_prev, jnp.max(s, axis=0, keepdims=True))
            alpha = jnp.exp2(m_prev - m_new)
            p = jnp.exp2(s - m_new)
            m_s[g] = m_new
            probs.append(p.astype(BF16))
            alphas.append(alpha)
        for c in range(C):
            v = jnp.concatenate([v_ref[0, (c * hpc) // R, :, pl.ds(k0, tk)], ones_rows], axis=0)
            acc_s[c] = alphas[c] * acc_s[c] + _dot(v, probs[c])

    def tile(j, edge):
        tile_update(j, edge, tile_scores(j))

    def body(step, carry):
        j = hi - step
        is_edge = step == 0
        if window:
            is_edge = is_edge | (q0 - j * tk + (tq - 1) >= window)
        run = (step == 0) | (need_s[j] > 0) if need_s is not None else None

        def when(c):
            return pl.when(c if run is None else c & run)

        @when(is_edge)
        def _():
            tile(j, True)

        @when(jnp.logical_not(is_edge))
        def _():
            tile(j, False)

        return carry

    lax.fori_loop(0, hi - lo + 1, body, 0)

    for c in range(C):
        m = m_s[c]
        l = acc_s[c, dv:dv + 1]
        acc = acc_s[c, 0:dv]
        if has_sink:
            sk = LOG2E * jnp.concatenate(
                [jnp.broadcast_to(sink_ref[:, c * hpc + r:c * hpc + r + 1], (1, tq)) for r in range(hpc)], axis=1)
            if slopes is not None:
                t_q = jnp.concatenate([lax.broadcasted_iota(jnp.int32, (1, tq), 1) + q0] * hpc, axis=1)
                m = m - slope_rows[c] * t_q.astype(F32)
            m_f = jnp.maximum(m, sk)
            a = jnp.exp2(m - m_f)
            l = l * a + jnp.exp2(sk - m_f)
            acc = acc * a
        out = acc / l
        for r in range(hpc):
            hh = c * hpc + r
            o_ref[0, :, hh * dv:(hh + 1) * dv] = out[:, r * tq:(r + 1) * tq].T.astype(o_ref.dtype)


def _flash_call(q, k, v, *, slopes=None, window=0, mask=None, mask_block=0, mask_t=False, sinks=None,
                tq=128, tk=256, hpc=None, skip_tiles=False, out_dtype=BF16, name="flash"):
    B, H, S, dq = q.shape
    G = k.shape[1]
    dv = v.shape[-2]
    tk = min(tk, S)
    assert tk % tq == 0 and S % tk == 0
    R = H // G
    hpc = R if hpc is None else hpc
    assert R % hpc == 0
    C = H // hpc
    in_specs = [
        pl.BlockSpec((1, H, tq, dq), lambda b, i: (b, 0, i, 0)),
        pl.BlockSpec((1, G, S, dq), lambda b, i: (b, 0, 0, 0)),
        pl.BlockSpec((1, G, dv, S), lambda b, i: (b, 0, 0, 0)),
    ]
    args = [q, k, v]
    mask_per_head = False
    if mask is not None:
        hm = mask.shape[1]
        mask_per_head = hm > 1
        if mask_t:
            in_specs.append(pl.BlockSpec((1, hm, mask.shape[2], tq), lambda b, i: (b, 0, 0, i)))
        else:
            in_specs.append(pl.BlockSpec((1, hm, tq, mask.shape[3]), lambda b, i: (b, 0, i, 0)))
        args.append(mask)
    if sinks is not None:
        in_specs.append(pl.BlockSpec(sinks.shape, lambda b, i: (0, 0)))
        args.append(sinks)
    kern = functools.partial(
        _flash_kernel, H=H, G=G, hpc=hpc, tq=tq, tk=tk,
        slopes=None if slopes is None else tuple(float(s) * LOG2E for s in slopes),
        window=window, mask_block=mask_block if mask is not None else 0,
        mask_per_head=mask_per_head, mask_t=mask_t, has_sink=sinks is not None,
        skip=skip_tiles)
    ones_rows = 16
    scratch = [pltpu.VMEM((C, 1, hpc * tq), F32), pltpu.VMEM((C, dv + ones_rows, hpc * tq), F32)]
    if mask is not None:
        assert tk % mask_block == 0
    if skip_tiles:
        assert mask is not None
        scratch.append(pltpu.SMEM((S // tk,), jnp.int32))
    return pl.pallas_call(
        kern,
        grid=(B, S // tq),
        in_specs=in_specs,
        out_specs=pl.BlockSpec((1, tq, H * dv), lambda b, i: (b, i, 0)),
        out_shape=jax.ShapeDtypeStruct((B, S, H * dv), out_dtype),
        scratch_shapes=scratch,
        compiler_params=pltpu.CompilerParams(vmem_limit_bytes=VMEM_LIMIT),
        name=name,
    )(*args)


def _merge_kernel(x_ref, sc1_ref, sh1_ref, g1_ref, sc2_ref, sh2_ref, ng_ref, wg_ref, wb_ref, wo_ref,
                  ocmp_ref, oslc_ref, owin_ref, gs_ref, ob_ref, oc_ref, od_ref, wr_ref, br_ref,
                  xo_ref, h2_ref, rt_ref):
    hd = HEAD_DIM
    x = x_ref[0]
    ng = ng_ref[...]
    h = _rms(x, ng[0:1]) * (1.0 + sc1_ref[0]) + sh1_ref[0]
    hb = h.astype(BF16)
    gs = gs_ref[0]
    ocmp = ocmp_ref[0]
    oslc = oslc_ref[0]
    owin = owin_ref[0]
    parts = []
    for hh in range(NSA_HEADS):
        c0 = GATE_LANE0 + 3 * hh
        sl = slice(hh * hd, (hh + 1) * hd)
        parts.append(gs[:, c0:c0 + 1] * ocmp[:, sl] + gs[:, c0 + 1:c0 + 2] * oslc[:, sl]
                     + gs[:, c0 + 2:c0 + 3] * owin[:, sl])
    o_a = jnp.concatenate(parts, axis=1).astype(BF16)
    branches = (o_a, ob_ref[0], oc_ref[0], od_ref[0])
    D = x.shape[1]
    mixed = None
    for n in range(N_BRANCH):
        gate = _sigmoid(_dot(hb, wg_ref[:, n * D:(n + 1) * D]))
        term = gate * _dot(branches[n], wb_ref[n])
        mixed = term if mixed is None else mixed + term
    xn = x + g1_ref[0] * _dot(mixed.astype(BF16), wo_ref[...])
    xo_ref[0] = xn
    h2 = _rms(xn, ng[1:2]) * (1.0 + sc2_ref[0]) + sh2_ref[0]
    for c in range(SUB):
        h2_ref[pl.ds(c, h2.shape[0], stride=SUB), :] = h2[:, c * LANE:(c + 1) * LANE]

    logits = _dot3(h2, wr_ref[...]) + br_ref[...]
    lane = lax.broadcasted_iota(jnp.int32, logits.shape, 1)
    lanef = lane.astype(F32)
    is_c = lane < N_GROUPS
    lc = jnp.where(is_c, logits, NEG_INF)
    mc = jnp.max(lc, axis=-1, keepdims=True)
    grp = jnp.min(jnp.where(lc == mc, lanef, 1e9), axis=-1, keepdims=True)
    p_grp = 1.0 / jnp.sum(jnp.where(is_c, jnp.exp(lc - mc), 0.0), axis=-1, keepdims=True)
    e_lane = lanef - float(N_GROUPS)
    in_grp = (lane >= N_GROUPS) & (lane < N_GROUPS + N_EXPERTS) & (
        jnp.floor(e_lane / EXPERTS_PER_GROUP) == grp)
    lf = jnp.where(in_grp, logits, NEG_INF)
    m1 = jnp.max(lf, axis=-1, keepdims=True)
    i1 = jnp.min(jnp.where(lf == m1, e_lane, 1e9), axis=-1, keepdims=True)
    lf2 = jnp.where(e_lane == i1, NEG_INF, lf)
    m2 = jnp.max(lf2, axis=-1, keepdims=True)
    i2 = jnp.min(jnp.where(lf2 == m2, e_lane, 1e9), axis=-1, keepdims=True)
    e2 = jnp.exp(m2 - m1)
    w1 = p_grp / (1.0 + e2)
    w2 = p_grp * e2 / (1.0 + e2)
    rt = jnp.where(lane == 0, i1, jnp.where(lane == 1, i2, jnp.where(lane == 2, w1, jnp.where(lane == 3, w2, 0.0))))
    rt_ref[0] = rt


def _merge_call(x, mods, ng, wg, wb, wo, ocmp, oslc, owin, gs, ob, oc, od, wr, br):
    B, S, D = x.shape
    tm = 512
    sc1, sh1, g1, sc2, sh2 = mods

    def full(a):
        return pl.BlockSpec(a.shape, lambda b, i: (0,) * a.ndim)

    modspec = pl.BlockSpec((1, 1, D), lambda b, i: (b, 0, 0))
    row = lambda w: pl.BlockSpec((1, tm, w), lambda b, i: (b, i, 0))
    in_specs = [row(D), modspec, modspec, modspec, modspec, modspec, full(ng), full(wg), full(wb), full(wo),
                row(256), row(256), row(256), row(128), row(256), row(256), row(256), full(wr), full(br)]
    return pl.pallas_call(
        _merge_kernel,
        grid=(B, S // tm),
        in_specs=in_specs,
        out_specs=[row(D), pl.BlockSpec((tm * SUB, LANE), lambda b, i: (b * (S // tm) + i, 0)), row(128)],
        out_shape=[jax.ShapeDtypeStruct((B, S, D), F32), jax.ShapeDtypeStruct((B * S * SUB, LANE), F32),
                   jax.ShapeDtypeStruct((B, S, 128), F32)],
        compiler_params=pltpu.CompilerParams(vmem_limit_bytes=VMEM_LIMIT),
        name="merge_router",
    )(x, sc1, sh1, g1, sc2, sh2, ng, wg, wb, wo, ocmp, oslc, owin, gs, ob, oc, od, wr, br)


def _expert_kernel(blk_e_ref, nact_ref, tok_ref, dst_ref, h2_hbm, w13_ref, w2_ref, y2_hbm, xbuf, ybuf, gsem, ssem):
    del blk_e_ref
    i = pl.program_id(0)
    nb = nact_ref[0]
    slot = i % 2
    rows = EXPERT_ROWS

    def tile_rows(t):
        return pl.ds(pl.multiple_of(t * SUB, SUB), SUB)

    def gather_copy(blk, s, r):
        tok = tok_ref[blk * rows + r]
        return pltpu.make_async_copy(h2_hbm.at[tile_rows(tok), :], xbuf.at[s, tile_rows(r), :], gsem.at[s])

    def scatter_copy(blk, s, r):
        d = dst_ref[blk * rows + r]
        return pltpu.make_async_copy(ybuf.at[s, tile_rows(r), :], y2_hbm.at[tile_rows(d), :], ssem.at[s])

    def loop_rows(fn):
        def body(r, carry):
            fn(r)
            return carry
        lax.fori_loop(0, rows, body, 0, unroll=8)

    def wait_gather(s):
        pltpu.make_async_copy(h2_hbm.at[pl.ds(0, rows * SUB), :], xbuf.at[s], gsem.at[s]).wait()

    def wait_scatter(s):
        pltpu.make_async_copy(ybuf.at[s], y2_hbm.at[pl.ds(0, rows * SUB), :], ssem.at[s]).wait()

    def compute():
        ab = None
        for c in range(0, SUB, 2):
            xc = jnp.concatenate([xbuf[slot, pl.ds(c, rows, stride=SUB), :],
                                  xbuf[slot, pl.ds(c + 1, rows, stride=SUB), :]], axis=1).astype(BF16)
            part = _dot(xc, w13_ref[0, 0, c * LANE:(c + 2) * LANE, :].astype(BF16))
            ab = part if ab is None else ab + part
        de = ab.shape[1] // 2
        a = ab[:, :de]
        b = ab[:, de:]
        act = ((a * _sigmoid(a)) * b).astype(BF16)
        for c in range(0, SUB, 2):
            y = _dot(act, w2_ref[0, 0, :, c * LANE:(c + 2) * LANE].astype(BF16))
            ybuf[slot, pl.ds(c, rows, stride=SUB), :] = y[:, :LANE]
            ybuf[slot, pl.ds(c + 1, rows, stride=SUB), :] = y[:, LANE:]

    def step(scatter_prev):
        nxt = jnp.minimum(i + 1, nb - 1)
        for r in range(rows):
            if scatter_prev:
                scatter_copy(i - 1, 1 - slot, r).start()
            gather_copy(nxt, 1 - slot, r).start()
        compute()

    @pl.when(i == 0)
    def _():
        loop_rows(lambda r: gather_copy(0, 0, r).start())

    @pl.when(i < nb)
    def _():
        wait_gather(slot)

    @pl.when((i >= 2) & (i < nb))
    def _():
        wait_scatter(slot)

    @pl.when(i == 0)
    def _():
        step(False)

    @pl.when((i > 0) & (i < nb))
    def _():
        step(True)

    @pl.when(i == nb - 1)
    def _():
        loop_rows(lambda r: scatter_copy(i, slot, r).start())
        wait_gather(1 - slot)

        @pl.when(nb >= 2)
        def _():
            wait_scatter(1 - slot)
        wait_scatter(slot)

    @pl.when(i >= nb)
    def _():
        d0 = pl.multiple_of(dst_ref[i * rows] * SUB, SUB)
        fill = pltpu.make_async_copy(ybuf.at[slot], y2_hbm.at[pl.ds(d0, rows * SUB), :], ssem.at[slot])
        fill.start()
        fill.wait()


def _expert_call(h2, blk_e, n_active, row_tok, row_dst, w13, w2, layer):
    D = SUB * LANE
    assert h2.shape[1] == LANE and w13.shape[2] == D
    n_blocks = blk_e.shape[0]
    de2 = w13.shape[-1]
    grid_spec = pltpu.PrefetchScalarGridSpec(
        num_scalar_prefetch=4,
        grid=(n_blocks,),
        in_specs=[
            pl.BlockSpec(memory_space=pl.ANY),
            pl.BlockSpec((1, 1, D, de2), lambda i, be, na, rt, rd: (layer, be[i], 0, 0)),
            pl.BlockSpec((1, 1, de2 // 2, D), lambda i, be, na, rt, rd: (layer, be[i], 0, 0)),
        ],
        out_specs=pl.BlockSpec(memory_space=pl.ANY),
        scratch_shapes=[pltpu.VMEM((2, EXPERT_ROWS * SUB, LANE), F32), pltpu.VMEM((2, EXPERT_ROWS * SUB, LANE), F32),
                        pltpu.SemaphoreType.DMA((2,)), pltpu.SemaphoreType.DMA((2,))],
    )
    return pl.pallas_call(
        _expert_kernel,
        grid_spec=grid_spec,
        out_shape=jax.ShapeDtypeStruct((row_tok.shape[0] * SUB, LANE), F32),
        compiler_params=pltpu.CompilerParams(
            dimension_semantics=("arbitrary",), vmem_limit_bytes=VMEM_LIMIT),
        name="experts",
    )(blk_e, n_active, row_tok, row_dst, h2, w13, w2)


def _combine_kernel(xn_ref, g2_ref, rt_ref, y0_ref, y1_ref, o_ref):
    rt = rt_ref[...]
    tm = xn_ref.shape[0]
    w0 = rt[:, 2:3]
    w1 = rt[:, 3:4]
    g2 = g2_ref[0]
    for c in range(SUB):
        cols = slice(c * LANE, (c + 1) * LANE)
        y = w0 * y0_ref[pl.ds(c, tm, stride=SUB), :] + w1 * y1_ref[pl.ds(c, tm, stride=SUB), :]
        o_ref[:, cols] = xn_ref[:, cols] + g2[:, cols] * y


def _combine_call(xn, g2, route, y2):
    B, S, D = xn.shape
    tm = 512
    spb = S // tm
    nt = B * spb
    return pl.pallas_call(
        _combine_kernel,
        grid=(B, spb),
        in_specs=[
            pl.BlockSpec((tm, D), lambda b, i: (b * spb + i, 0)),
            pl.BlockSpec((1, 1, D), lambda b, i: (b, 0, 0)),
            pl.BlockSpec((tm, 128), lambda b, i: (b * spb + i, 0)),
            pl.BlockSpec((tm * SUB, LANE), lambda b, i: (b * spb + i, 0)),
            pl.BlockSpec((tm * SUB, LANE), lambda b, i: (nt + b * spb + i, 0)),
        ],
        out_specs=pl.BlockSpec((tm, D), lambda b, i: (b * spb + i, 0)),
        out_shape=jax.ShapeDtypeStruct((B * S, D), F32),
        compiler_params=pltpu.CompilerParams(vmem_limit_bytes=VMEM_LIMIT),
        name="moe_combine",
    )(xn.reshape(B * S, D), g2, route, y2, y2).reshape(B, S, D)


def _moe(h2, route, w13, w2, layer):
    N = h2.shape[0] // SUB
    K = 2
    E = N_EXPERTS
    rows = EXPERT_ROWS
    flat_e = route[:, 0:2].astype(jnp.int32).reshape(-1)
    order = jnp.argsort(flat_e).astype(jnp.int32)
    counts = jnp.sum((flat_e[:, None] == jnp.arange(E)[None, :]).astype(jnp.int32), axis=0)
    padded = (counts + rows - 1) // rows * rows
    pend = jnp.cumsum(padded)
    pstart = pend - padded
    start = jnp.cumsum(counts) - counts
    n_blocks = (N * K) // rows + E
    R = n_blocks * rows
    blk_e = jnp.minimum(
        jnp.sum((jnp.arange(n_blocks)[:, None] * rows >= pend[None, :]).astype(jnp.int32), axis=1), E - 1)
    pos = jnp.arange(R, dtype=jnp.int32).reshape(n_blocks, rows)
    local = pos - pstart[blk_e][:, None]
    is_real = ((pos < pend[E - 1]) & (local < counts[blk_e][:, None])).reshape(R)
    rank = jnp.clip(start[blk_e][:, None] + local, 0, N * K - 1).reshape(R)
    pair = order[rank]
    flat = pos.reshape(R)
    real_before = jnp.cumsum(is_real.astype(jnp.int32)) - is_real.astype(jnp.int32)
    row_tok = jnp.where(is_real, pair // K, flat % N).astype(jnp.int32)
    row_dst = jnp.where(is_real, (pair % K) * N + pair // K, N * K + flat - real_before).astype(jnp.int32)
    n_active = (pend[E - 1:E] // rows).astype(jnp.int32)
    return _expert_call(h2, blk_e.astype(jnp.int32), n_active, row_tok, row_dst, w13, w2, layer)


def kernel(x, c, w_ada, b_ada, norm_gain, w_in, qk_gain, cmp_pe, cmp_w1, cmp_w2, swa_sinks,
           lat_gain_q, lat_gain_kv, rope_gain, w_uq, w_ukv, w_branch, w_out,
           w_coarse, b_coarse, w_fine, b_fine, w13, w2):
    B, S, D = x.shape
    L = w_in.shape[0]
    assert S % MOBA_BLOCK == 0 and D == 1024
    slopes = _alibi_slopes()

    half = MLA_ROPE // 2
    inv = ROPE_THETA ** (-jnp.arange(half, dtype=F32) / half)
    ang = jnp.arange(S).astype(F32)[:, None] * inv[None, :]
    cos = jnp.tile(jnp.cos(ang), (1, 8))
    sin = jnp.tile(jnp.concatenate([-jnp.sin(ang), jnp.sin(ang)], axis=1), (1, 4))

    mod = _ada_mod(c, w_ada, b_ada)

    w_attn = jnp.concatenate(
        [w_in[:, :, :448], w_in[:, :, 512:576], w_in[:, :, 448:512], w_in[:, :, 576:640],
         w_in[:, :, 652:ATTN_OLD], w_in[:, :, 640:652],
         jnp.zeros((L, D, ATTN_COLS - ATTN_OLD), F32)], axis=2).astype(BF16)
    w_gate = w_in[:, :, ATTN_OLD:].astype(BF16)
    dq, dkv = MLA_NOPE + MLA_ROPE, MLA_NOPE + MLA_V
    uq = w_uq.reshape(L, MLA_Q_RANK, MLA_HEADS, dq)
    w_uq_b = jnp.concatenate([uq[..., :MLA_NOPE].reshape(L, MLA_Q_RANK, -1),
                              uq[..., MLA_NOPE:].reshape(L, MLA_Q_RANK, -1)], axis=2).astype(BF16)
    ukv = w_ukv.reshape(L, MLA_KV_RANK, MLA_HEADS, dkv)
    w_ukv_b = jnp.concatenate([ukv[..., :MLA_NOPE].reshape(L, MLA_KV_RANK, -1),
                               ukv[..., MLA_NOPE:].reshape(L, MLA_KV_RANK, -1)], axis=2).astype(BF16)

    qs = HEAD_DIM ** -0.5 * LOG2E
    ms = dq ** -0.5 * LOG2E

    def lanes(v, reps, scale=1.0):
        r = jnp.tile(v, (1, reps)) * scale
        return jnp.pad(r, ((0, 0), (0, 4 * HEAD_DIM - r.shape[1])))

    gq = qk_gain
    gain_rows = jnp.stack([
        lanes(gq[:, QK_NSA_Q], 4, qs),
        lanes(jnp.concatenate([gq[:, QK_NSA_KS], gq[:, QK_NSA_KW]], axis=1), 1),
        lanes(gq[:, QK_SWA_Q], 4, qs),
        lanes(gq[:, QK_SWA_K], 2),
        lanes(gq[:, QK_MOBA_Q], 4, qs),
        lanes(gq[:, QK_MOBA_K], 4),
        lanes(gq[:, QK_MLA_Q], 4, ms),
        lanes(rope_gain[:, 0], 4, ms),
        lanes(gq[:, QK_MLA_K], 4),
        lanes(rope_gain[:, 1], 4),
    ], axis=1)
    bd64 = jnp.asarray(np.kron(np.eye(4), np.ones((HEAD_DIM, HEAD_DIM))) / HEAD_DIM, dtype=BF16)
    bd32 = jnp.asarray(np.kron(np.eye(4), np.ones((MLA_ROPE, MLA_ROPE))) / MLA_ROPE, dtype=BF16)
    w_branch_b = w_branch.astype(BF16)
    w_out_b = w_out.astype(BF16)
    w_router = jnp.concatenate(
        [w_coarse, w_fine, jnp.zeros((L, D, 128 - N_GROUPS - N_EXPERTS), F32)], axis=2)
    b_router = jnp.concatenate(
        [b_coarse, b_fine, jnp.zeros((L, 128 - N_GROUPS - N_EXPERTS), F32)], axis=1)
    sinks_pad = jnp.concatenate([swa_sinks, jnp.zeros((L, 128 - SWA_HEADS), F32)], axis=1)

    for l in range(L):
        m6 = mod[l].reshape(B, 6, 1, D)
        sh1, sc1, g1, sh2, sc2, g2 = (m6[:, j] for j in range(6))
        (qa, kcr, vcr, ks, vs, kw, vw, gs, qb, kb, vb, qc, kc, vc, mc, qd, kd, vd) = _proj_call(
            x, sc1, sh1, norm_gain[l, 0:1], w_attn[l], gain_rows[l], bd64, bd32, lat_gain_q[l][None],
            lat_gain_kv[l][None], w_uq_b[l], w_ukv_b[l], cos, sin)
        kcmp, vcmp = _compress_call(kcr, vcr, cmp_pe[l], cmp_w1[l], cmp_w2[l], qk_gain[l, QK_NSA_KC][None])
        o_cmp, sel = _cmp_attn_call(qa, kcmp, vcmp, slopes[0])
        o_slc = _flash_call(qa, ks, vs, slopes=slopes[0], mask=sel, mask_block=SEL_BLOCK, mask_t=True,
                            tq=256, tk=256, hpc=2, skip_tiles=True, out_dtype=F32, name="nsa_slc")
        o_win = _flash_call(qa, kw, vw, slopes=slopes[0], window=NSA_WINDOW,
                            tq=256, tk=256, hpc=2, out_dtype=F32, name="nsa_win")
        o_b = _flash_call(qb, kb, vb, slopes=slopes[1], window=SWA_WINDOW, sinks=sinks_pad[l][None],
                          tq=256, tk=256, name="swa")
        o_c = _flash_call(qc, kc, vc, slopes=slopes[2], mask=mc, mask_block=MOBA_BLOCK,
                          tq=512, tk=512, name="moba")
        o_d = _flash_call(qd, kd, vd, tq=512, tk=512, name="mla")
        xn, h2, route = _merge_call(
            x, (sc1, sh1, g1, sc2, sh2), norm_gain[l], w_gate[l], w_branch_b[l], w_out_b[l],
            o_cmp, o_slc, o_win, gs, o_b, o_c, o_d, w_router[l], b_router[l][None])
        route = route.reshape(B * S, 128)
        y2 = _moe(h2, route, w13, w2, l)
        x = _combine_call(xn, g2, route, y2)
    return x
```

```python
import functools
import math

import numpy as np
import jax
import jax.numpy as jnp
from jax import lax
from jax.experimental import pallas as pl
from jax.experimental.pallas import tpu as pltpu

F32 = jnp.float32
BF16 = jnp.bfloat16

HEAD_DIM = 64
NEG_INF = -1e30
EPS = 1e-6
NSA_HEADS = 4
CMP_BLOCK = 32
CMP_STRIDE = 16
CMP_HIDDEN = 256
SEL_BLOCK = 64
SEL_TOPK = 8
NSA_WINDOW = 512
FORCE_BONUS = 1e4
SWA_HEADS = 4
SWA_KV_HEADS = 2
SWA_WINDOW = 128
MOBA_HEADS = 4
MOBA_BLOCK = 256
MOBA_TOPK = 3
MLA_HEADS = 4
MLA_Q_RANK = 384
MLA_KV_RANK = 128
MLA_NOPE = 64
MLA_ROPE = 32
MLA_V = 64
ROPE_THETA = 10000.0
N_BRANCH = 4
BRANCH_WIDTH = 256
N_GROUPS = 4
EXPERTS_PER_GROUP = 8
N_EXPERTS = N_GROUPS * EXPERTS_PER_GROUP
D_EXPERT = 256

(GR_NSA_Q, GR_NSA_K, GR_SWA_Q, GR_SWA_K, GR_MOBA_Q, GR_MOBA_K,
 GR_MLA_QN, GR_MLA_QR, GR_MLA_KN, GR_MLA_KR) = range(10)
QK_NSA_Q, QK_NSA_KC, QK_NSA_KS, QK_NSA_KW = 0, 1, 2, 3
QK_SWA_Q, QK_SWA_K, QK_MOBA_Q, QK_MOBA_K, QK_MLA_Q, QK_MLA_K = 4, 5, 6, 7, 8, 9

ATTN_OLD = 2476
ATTN_COLS = 2560
GATE_LANE0 = 32
LOG2E = math.log2(math.e)
SUB, LANE = 8, 128
EXPERT_ROWS = 256
VMEM_LIMIT = 56 * 1024 * 1024


def _alibi_slopes():
    n = NSA_HEADS + SWA_HEADS + MOBA_HEADS

    def pow2(m):
        start = 2.0 ** (-8.0 / m)
        return [start ** (i + 1) for i in range(m)]

    c = 2 ** int(math.floor(math.log2(n)))
    s = pow2(c) + (pow2(2 * c)[0::2][: n - c] if c < n else [])
    s = -np.sort(-np.asarray(s, np.float32))
    return s.reshape(NSA_HEADS, 3).T


def _dot(a, b):
    return jnp.dot(a, b, preferred_element_type=F32)


def _dot_nt(a, b):
    return lax.dot_general(a, b, (((1,), (1,)), ((), ())), preferred_element_type=F32)


def _split(a):
    hi = a.astype(BF16)
    lo = (a - hi.astype(F32)).astype(BF16)
    return hi, lo


def _dot3(a, b):
    ah, al = _split(a)
    bh, bl = _split(b)
    return _dot(ah, bh) + (_dot(ah, bl) + _dot(al, bh))


def _dot3_nt(a, b):
    ah, al = _split(a)
    bh, bl = _split(b)
    return _dot_nt(ah, bh) + (_dot_nt(ah, bl) + _dot_nt(al, bh))


def _rms(x, g):
    return x * lax.rsqrt(jnp.mean(x * x, axis=-1, keepdims=True) + EPS) * g


def _sigmoid(x):
    return 1.0 / (1.0 + jnp.exp(-x))


AUG_MASK0 = 6


def _bf16_pieces(x, n=3):
    out, r = [], float(x)
    for _ in range(n):
        piece = float(np.asarray(r, np.float32).astype(jnp.bfloat16).astype(np.float32))
        out.append(piece)
        r -= piece
    return out


def _topk_mask(score, k, axis=1):
    n = score.shape[axis]
    iota = lax.broadcasted_iota(jnp.int32, score.shape, axis).astype(F32)
    sel = jnp.zeros(score.shape, F32)
    for _ in range(k):
        m = jnp.max(score, axis=axis, keepdims=True)
        idx = jnp.min(jnp.where(score == m, iota, float(n)), axis=axis, keepdims=True)
        hit = iota == idx
        sel = jnp.where(hit, jnp.where(m > 0.5 * NEG_INF, 1.0, 0.0), sel)
        score = jnp.where(hit, -3e38, score)
    return sel


def _ada_kernel(c_ref, w_ref, b_ref, o_ref):
    c = c_ref[...]
    a = c * _sigmoid(c)
    o_ref[0] = _dot(a, w_ref[0]) + b_ref[0]


def _ada_mod(c, w_ada, b_ada):
    L, D, D6 = w_ada.shape
    B = c.shape[0]
    tn = 1024
    return pl.pallas_call(
        _ada_kernel,
        grid=(L, D6 // tn),
        in_specs=[
            pl.BlockSpec((B, D), lambda l, j: (0, 0)),
            pl.BlockSpec((1, D, tn), lambda l, j: (l, 0, j)),
            pl.BlockSpec((1, 1, tn), lambda l, j: (l, 0, j)),
        ],
        out_specs=pl.BlockSpec((1, B, tn), lambda l, j: (l, 0, j)),
        out_shape=jax.ShapeDtypeStruct((L, B, D6), F32),
        compiler_params=pltpu.CompilerParams(vmem_limit_bytes=VMEM_LIMIT),
        name="ada_mod",
    )(c, w_ada, b_ada.reshape(L, 1, D6))


def _rms_blocks(x, bd, g):
    hi, lo = _split(x * x)
    ms = _dot(hi, bd) + _dot(lo, bd)
    return x * lax.rsqrt(ms + EPS) * g


def _proj_kernel(x_ref, sc_ref, sh_ref, ng_ref, w_ref, gr_ref, bd64_ref, bd32_ref, lgq_ref, lgkv_ref,
                 wuq_ref, wukv_ref, cos_ref, sin_ref,
                 qa_ref, kcr_ref, vcr_ref, ks_ref, vs_ref, kw_ref, vw_ref, gs_ref,
                 qb_ref, kb_ref, vb_ref, qc_ref, kc_ref, vc_ref, mc_ref, qd_ref, kd_ref, vd_ref,
                 kmean_s):
    i = pl.program_id(1)
    hd = HEAD_DIM
    x = x_ref[0]
    h = _rms(x, ng_ref[...]) * (1.0 + sc_ref[0]) + sh_ref[0]
    hb = h.astype(BF16)
    bd64 = bd64_ref[...]
    bd64h = bd64_ref[0:2 * hd, 0:2 * hd]
    bd32 = bd32_ref[...]

    def grow(r, w=4 * hd):
        return gr_ref[r:r + 1, 0:w]

    def proj(a, b):
        return _dot(hb, w_ref[:, a:b])

    def store_heads(ref, slab, n):
        for hh in range(n):
            ref[0, hh] = slab[:, hh * hd:(hh + 1) * hd].astype(BF16)

    def store_heads_t(ref, slab_t, n):
        for hh in range(n):
            ref[0, hh] = slab_t[hh * hd:(hh + 1) * hd].astype(BF16)

    store_heads(qa_ref, _rms_blocks(proj(0, 256), bd64, grow(GR_NSA_Q)), NSA_HEADS)
    p = proj(256, 640)
    kcr_ref[0] = p[:, 0:64]
    vcr_ref[0] = p[:, 64:128]
    kk = _rms_blocks(p[:, 128:256], bd64h, grow(GR_NSA_K, 2 * hd))
    ks_ref[0, 0] = kk[:, :hd].astype(BF16)
    kw_ref[0, 0] = kk[:, hd:].astype(BF16)
    vt = p[:, 256:384].T
    vs_ref[0, 0] = vt[:hd].astype(BF16)
    vw_ref[0, 0] = vt[hd:].astype(BF16)

    p = proj(640, 1152)
    store_heads(qb_ref, _rms_blocks(p[:, 0:256], bd64, grow(GR_SWA_Q)), SWA_HEADS)
    store_heads(kb_ref, _rms_blocks(p[:, 256:384], bd64h, grow(GR_SWA_K, 2 * hd)), SWA_KV_HEADS)
    store_heads_t(vb_ref, p[:, 384:512].T, SWA_KV_HEADS)

    @pl.when(i == 0)
    def _():
        kmean_s[...] = jnp.zeros(kmean_s.shape, F32)

    p = proj(1152, 1920)
    qn = _rms_blocks(p[:, 0:256], bd64, grow(GR_MOBA_Q))
    kn = _rms_blocks(p[:, 256:512], bd64, grow(GR_MOBA_K))
    store_heads(qc_ref, qn, MOBA_HEADS)
    store_heads(kc_ref, kn, MOBA_HEADS)
    store_heads_t(vc_ref, p[:, 512:768].T, MOBA_HEADS)
    nblk = kmean_s.shape[0]
    tm = x.shape[0]
    bpt = tm // MOBA_BLOCK
    for u in range(bpt):
        kmean_s[pl.ds(i * bpt + u, 1), :] = jnp.mean(kn[u * MOBA_BLOCK:(u + 1) * MOBA_BLOCK], axis=0, keepdims=True)
    own = i * bpt + lax.broadcasted_iota(jnp.int32, (tm, 1), 0) // MOBA_BLOCK
    blk_iota = lax.broadcasted_iota(jnp.int32, (1, nblk), 1)
    head_of_lane = lax.broadcasted_iota(jnp.int32, (1, 4 * hd), 1) // hd
    kmeans = kmean_s[...]
    for hh in range(MOBA_HEADS):
        g = _dot3_nt(jnp.where(head_of_lane == hh, qn, 0.0), kmeans)
        g = jnp.where(blk_iota < own, g, NEG_INF)
        mc_ref[0, hh] = jnp.where(blk_iota == own, 1.0, _topk_mask(g, MOBA_TOPK))

    p = proj(1920, 2560)
    gs_ref[0] = _sigmoid(p[:, 512:640])
    cos = cos_ref[...]
    sin = sin_ref[...]
    half = MLA_ROPE // 2
    first_half = lax.broadcasted_iota(jnp.int32, (1, 4 * MLA_ROPE), 1) % MLA_ROPE < half

    def rope(v):
        swapped = jnp.where(first_half, pltpu.roll(v, 4 * MLA_ROPE - half, 1), pltpu.roll(v, half, 1))
        return v * cos + swapped * sin

    qlat = _dot(_rms(p[:, 0:384], lgq_ref[...]).astype(BF16), wuq_ref[...])
    kvlat = _dot(_rms(p[:, 384:512], lgkv_ref[...]).astype(BF16), wukv_ref[...])
    q_nope = _rms_blocks(qlat[:, 0:256], bd64, grow(GR_MLA_QN))
    q_rot = rope(_rms_blocks(qlat[:, 256:384], bd32, grow(GR_MLA_QR, 2 * hd)))
    k_nope = _rms_blocks(kvlat[:, 0:256], bd64, grow(GR_MLA_KN))
    k_rot = rope(_rms_blocks(p[:, 512:640], bd32, grow(GR_MLA_KR, 2 * hd)))[:, :MLA_ROPE]
    store_heads_t(vd_ref, kvlat[:, 256:512].T, MLA_HEADS)
    for hh in range(MLA_HEADS):
        qd_ref[0, hh] = jnp.concatenate(
            [q_nope[:, hh * hd:(hh + 1) * hd], q_rot[:, hh * MLA_ROPE:(hh + 1) * MLA_ROPE]], axis=1).astype(BF16)
        kd_ref[0, hh] = jnp.concatenate([k_nope[:, hh * hd:(hh + 1) * hd], k_rot], axis=1).astype(BF16)


def _proj_call(x, sc1, sh1, ng, w_attn, gain_rows, bd64, bd32, lgq, lgkv, wuq, wukv, cos, sin):
    B, S, D = x.shape
    tm = 2 * MOBA_BLOCK
    nblk = S // MOBA_BLOCK
    hd = HEAD_DIM

    def full(shape):
        return pl.BlockSpec(shape, lambda b, i: (0,) * len(shape))

    def heads(nh, d):
        return pl.BlockSpec((1, nh, tm, d), lambda b, i: (b, 0, i, 0))

    in_specs = [
        pl.BlockSpec((1, tm, D), lambda b, i: (b, i, 0)),
        pl.BlockSpec((1, 1, D), lambda b, i: (b, 0, 0)),
        pl.BlockSpec((1, 1, D), lambda b, i: (b, 0, 0)),
        full((1, D)),
        full((D, ATTN_COLS)),
        full(gain_rows.shape),
        full(bd64.shape),
        full(bd32.shape),
        full(lgq.shape),
        full(lgkv.shape),
        full(wuq.shape),
        full(wukv.shape),
        pl.BlockSpec((tm, 4 * MLA_ROPE), lambda b, i: (i, 0)),
        pl.BlockSpec((tm, 4 * MLA_ROPE), lambda b, i: (i, 0)),
    ]
    row64 = pl.BlockSpec((1, tm, hd), lambda b, i: (b, i, 0))
    def heads_t(nh, d):
        return pl.BlockSpec((1, nh, d, tm), lambda b, i: (b, 0, 0, i))

    out_specs = [
        heads(4, hd), row64, row64, heads(1, hd), heads_t(1, hd), heads(1, hd), heads_t(1, hd),
        pl.BlockSpec((1, tm, 128), lambda b, i: (b, i, 0)),
        heads(4, hd), heads(2, hd), heads_t(2, hd),
        heads(4, hd), heads(4, hd), heads_t(4, hd), heads(4, nblk),
        heads(4, MLA_NOPE + MLA_ROPE), heads(4, MLA_NOPE + MLA_ROPE), heads_t(4, MLA_V),
    ]

    def sd(shape, dt):
        return jax.ShapeDtypeStruct(shape, dt)

    out_shape = [
        sd((B, 4, S, hd), BF16), sd((B, S, hd), F32), sd((B, S, hd), F32),
        sd((B, 1, S, hd), BF16), sd((B, 1, hd, S), BF16), sd((B, 1, S, hd), BF16), sd((B, 1, hd, S), BF16),
        sd((B, S, 128), F32),
        sd((B, 4, S, hd), BF16), sd((B, 2, S, hd), BF16), sd((B, 2, hd, S), BF16),
        sd((B, 4, S, hd), BF16), sd((B, 4, S, hd), BF16), sd((B, 4, hd, S), BF16), sd((B, 4, S, nblk), F32),
        sd((B, 4, S, 96), BF16), sd((B, 4, S, 96), BF16), sd((B, 4, MLA_V, S), BF16),
    ]
    return pl.pallas_call(
        _proj_kernel,
        grid=(B, S // tm),
        in_specs=in_specs,
        out_specs=out_specs,
        out_shape=out_shape,
        scratch_shapes=[pltpu.VMEM((nblk, MOBA_HEADS * hd), F32)],
        compiler_params=pltpu.CompilerParams(
            dimension_semantics=("arbitrary", "arbitrary"), vmem_limit_bytes=VMEM_LIMIT),
        name="proj_prep",
    )(x, sc1, sh1, ng, w_attn, gain_rows, bd64, bd32, lgq, lgkv, wuq, wukv, cos, sin)


def _compress_kernel(gk_ref, gv_ref, pe_ref, w1_ref, w2_ref, gkc_ref, kc_ref, vc_ref):
    half = w1_ref.shape[1] // 2
    outs = []
    for j, g_ref in enumerate((gk_ref, gv_ref)):
        g = g_ref[0].astype(BF16)
        top = _dot(g, w1_ref[j, :half].astype(BF16))
        bot = _dot(g, w1_ref[j, half:].astype(BF16))
        bot = jnp.concatenate([bot[1:], bot[:1]], axis=0)
        pe = jnp.broadcast_to(pe_ref[j], (8, pe_ref.shape[2]))
        bias = _dot3(pe, w1_ref[j])[0:1]
        hid = top + bot + bias
        hid = hid * _sigmoid(hid)
        outs.append(_dot(hid.astype(BF16), w2_ref[j].astype(BF16)))
    kc_ref[0] = _rms(outs[0], gkc_ref[...]).astype(BF16)
    dk = outs[1].shape[1]
    vc_ref[0] = jnp.concatenate([outs[1], outs[1]], axis=1).T[:dk].astype(BF16)


def _compress_call(kc_raw, vc_raw, pe, w1, w2, g_kc):
    B, S, dk = kc_raw.shape
    n_grp = S // CMP_STRIDE
    gk = kc_raw.reshape(B, n_grp, CMP_STRIDE * dk)
    gv = vc_raw.reshape(B, n_grp, CMP_STRIDE * dk)
    pe_flat = pe.reshape(2, 1, CMP_BLOCK * dk)
    grp_spec = pl.BlockSpec((1, n_grp, CMP_STRIDE * dk), lambda b: (b, 0, 0))
    out_spec = pl.BlockSpec((1, n_grp, dk), lambda b: (b, 0, 0))
    return pl.pallas_call(
        _compress_kernel,
        grid=(B,),
        in_specs=[
            grp_spec, grp_spec,
            pl.BlockSpec(pe_flat.shape, lambda b: (0, 0, 0)),
            pl.BlockSpec(w1.shape, lambda b: (0, 0, 0)),
            pl.BlockSpec(w2.shape, lambda b: (0, 0, 0)),
            pl.BlockSpec((1, dk), lambda b: (0, 0)),
        ],
        out_specs=[out_spec, pl.BlockSpec((1, dk, n_grp), lambda b: (b, 0, 0))],
        out_shape=[jax.ShapeDtypeStruct((B, n_grp, dk), BF16), jax.ShapeDtypeStruct((B, dk, n_grp), BF16)],
        compiler_params=pltpu.CompilerParams(vmem_limit_bytes=VMEM_LIMIT),
        name="nsa_compress",
    )(gk, gv, pe_flat, w1, w2, g_kc)


def _cmp_attn_kernel(q_ref, kc_ref, vct_ref, cover_ref, o_ref, sel_ref, *, slopes, tq, n_cmp):
    i = pl.program_id(1)
    kc = kc_ref[0]
    vct = vct_ref[0]
    ncp = kc.shape[0]
    t_full = i * tq + lax.broadcasted_iota(jnp.int32, (ncp, tq), 1)
    n_iota = lax.broadcasted_iota(jnp.int32, (ncp, tq), 0)
    dist_i = t_full - (n_iota * CMP_STRIDE + (CMP_BLOCK - 1))
    vis = (n_iota < n_cmp) & (dist_i >= 0)
    dist = dist_i.astype(F32)
    visf = vis.astype(F32)
    psum = jnp.zeros((ncp, tq), F32)
    raw = [_dot_nt(kc, q_ref[0, hh]) for hh in range(NSA_HEADS)]
    probs = []
    for hh in range(NSA_HEADS):
        s = raw[hh] - (slopes[hh] * LOG2E) * dist
        s = jnp.where(vis, s, NEG_INF)
        e = jnp.exp2(s - jnp.max(s, axis=0, keepdims=True)) * visf
        p = e / jnp.maximum(jnp.sum(e, axis=0, keepdims=True), 1e-30)
        probs.append(p.astype(BF16))
        psum = psum + p
    outs = [_dot(vct, pb) for pb in probs]
    o_ref[0] = jnp.concatenate(outs, axis=0).T
    ph, plo = _split(psum)
    cover = cover_ref[...]
    p_slc = _dot(cover, ph) + _dot(cover, plo)
    n_sel = cover.shape[0]
    cur = (i * tq + lax.broadcasted_iota(jnp.int32, (1, tq), 1)) // SEL_BLOCK
    j = lax.broadcasted_iota(jnp.int32, (n_sel, 1), 0)
    forced = jnp.where(j == 0, 1.0, jnp.where(j == cur, 1.0, jnp.where(j == cur - 1, 1.0, 0.0)))
    score = jnp.where(j <= cur, p_slc + FORCE_BONUS * forced, NEG_INF)
    sel_ref[0, 0] = _topk_mask(score, min(SEL_TOPK, n_sel), axis=0)


def _cmp_attn_call(qa, kc, vc, slopes):
    B, H, S, dk = qa.shape
    ncp = kc.shape[1]
    n_cmp = (S - CMP_BLOCK) // CMP_STRIDE + 1
    n_sel = S // SEL_BLOCK
    tq = 256
    starts = np.arange(ncp) * CMP_STRIDE
    jb = np.arange(n_sel) * SEL_BLOCK
    cover = ((starts[:, None] < jb[None, :] + SEL_BLOCK) & (starts[:, None] + CMP_BLOCK > jb[None, :])
             & (np.arange(ncp)[:, None] < n_cmp))
    cover = jnp.asarray(cover.T.astype(np.float32), dtype=BF16)
    return pl.pallas_call(
        functools.partial(_cmp_attn_kernel, slopes=tuple(float(s) for s in slopes), tq=tq, n_cmp=n_cmp),
        grid=(B, S // tq),
        in_specs=[
            pl.BlockSpec((1, H, tq, dk), lambda b, i: (b, 0, i, 0)),
            pl.BlockSpec((1, ncp, dk), lambda b, i: (b, 0, 0)),
            pl.BlockSpec((1, dk, ncp), lambda b, i: (b, 0, 0)),
            pl.BlockSpec((n_sel, ncp), lambda b, i: (0, 0)),
        ],
        out_specs=[
            pl.BlockSpec((1, tq, H * dk), lambda b, i: (b, i, 0)),
            pl.BlockSpec((1, 1, n_sel, tq), lambda b, i: (b, 0, 0, i)),
        ],
        out_shape=[jax.ShapeDtypeStruct((B, S, H * dk), F32),
                   jax.ShapeDtypeStruct((B, 1, n_sel, S), F32)],
        compiler_params=pltpu.CompilerParams(vmem_limit_bytes=VMEM_LIMIT),
        name="nsa_cmp_attn",
    )(qa, kc, vc, cover)


def _flash_kernel(*refs, H, G, hpc, tq, tk, slopes, window, mask_block, mask_per_head, mask_t, has_sink, skip):
    refs = list(refs)
    q_ref, k_ref, v_ref = refs[:3]
    pos = 3
    mask_ref = sink_ref = None
    if mask_block:
        mask_ref = refs[pos]
        pos += 1
    if has_sink:
        sink_ref = refs[pos]
        pos += 1
    o_ref, m_s, acc_s = refs[pos:pos + 3]
    need_s = refs[pos + 3] if skip else None
    use_aug = slopes is not None or bool(mask_block)
    R = H // G
    C = H // hpc
    M = hpc * tq
    dv = v_ref.shape[-2]
    i = pl.program_id(1)
    q0 = i * tq
    hi = q0 // tk
    lo = jnp.maximum(q0 - (window - 1), 0) // tk if window else 0

    def rel_pos():
        r_row = jnp.concatenate([lax.broadcasted_iota(jnp.int32, (1, tq), 1)] * hpc, axis=1)
        return (r_row - lax.broadcasted_iota(jnp.int32, (tk, M), 0)).astype(F32)

    dq = q_ref.shape[-1]
    n_mask = mask_ref.shape[-2 if mask_t else -1] if mask_block else 0
    aug_w = LANE if dq + AUG_MASK0 + n_mask <= LANE else 2 * LANE
    aug0 = dq
    lane_a = lax.broadcasted_iota(jnp.int32, (1, aug_w), 1)
    feature_lane = lane_a < dq

    def slope_lanes(hh):
        row = jnp.zeros((1, aug_w), F32)
        if slopes is not None:
            for n, piece in enumerate(_bf16_pieces(slopes[hh])):
                row = jnp.where((lane_a == aug0 + n) | (lane_a == aug0 + n + 3), piece, row)
        return row

    def widen(a):
        return jnp.concatenate([a, jnp.zeros((a.shape[0], aug_w - dq), a.dtype)], axis=1)

    qs, slope_rows, bms = [], [], []
    for c in range(C):
        heads = [c * hpc + r for r in range(hpc)]
        if hpc == 1:
            qs.append(q_ref[0, heads[0]])
        else:
            qs.append(jnp.concatenate([q_ref[0, hh] for hh in heads], axis=0))
        if slopes is not None:
            slope_rows.append(jnp.concatenate([jnp.full((1, tq), slopes[hh], F32) for hh in heads], axis=1))
        if mask_block:
            parts = [mask_ref[0, hh if mask_per_head else 0] for hh in heads]
            bms.append(jnp.concatenate(parts, axis=1) if mask_t else jnp.concatenate(parts, axis=0).T)
    m_s[...] = jnp.full(m_s.shape, NEG_INF, F32)
    acc_s[...] = jnp.zeros(acc_s.shape, F32)
    ones_rows = jnp.ones((acc_s.shape[1] - dv, tk), BF16)

    if mask_block:
        nblk = bms[0].shape[0]
    if need_s is not None:
        col = bms[0]
        for bm in bms[1:]:
            col = jnp.maximum(col, bm)
        col = jnp.max(col, axis=1, keepdims=True)
        bpt = tk // mask_block
        for jt in range(nblk // bpt):
            need_s[jt] = jnp.max(col[jt * bpt:(jt + 1) * bpt]).astype(jnp.int32)

    if use_aug:
        for c in range(C):
            qa = jnp.concatenate(
                [jnp.broadcast_to(slope_lanes(c * hpc + r), (tq, aug_w)) for r in range(hpc)], axis=0)
            if mask_block:
                place = jnp.where(
                    lax.broadcasted_iota(jnp.int32, (nblk, aug_w), 0) + (aug0 + AUG_MASK0) == lane_a,
                    1.0, 0.0).astype(BF16)
                off_sel = ((bms[c] - 1.0) * -NEG_INF).astype(BF16)
                qa = qa + lax.dot_general(off_sel, place, (((0,), (0,)), ((), ())), preferred_element_type=F32)
            qs[c] = jnp.where(feature_lane, widen(qs[c]), qa.astype(BF16))

    def tile_scores(j):
        k0 = pl.multiple_of(j * tk, tk)
        if use_aug:
            t_key = lax.broadcasted_iota(jnp.int32, (tk, aug_w), 0) + k0
            k_aug = jnp.zeros((tk, aug_w), F32)
            if mask_block:
                key_blk = lax.shift_right_logical(t_key, int(math.log2(mask_block)))
                k_aug = jnp.where(key_blk + (aug0 + AUG_MASK0) == lane_a, 1.0, 0.0)
            if slopes is not None:
                t_hi = (lax.shift_right_logical(t_key, 8) * 256).astype(F32)
                t_lo = (t_key & 255).astype(F32)
                k_aug = jnp.where(lane_a < aug0 + 3, t_hi, jnp.where(lane_a < aug0 + 6, t_lo, k_aug))
            k_aug = k_aug.astype(BF16)
        k_cats = {}
        scores = []
        for c in range(C):
            g = (c * hpc) // R
            if g not in k_cats:
                k = k_ref[0, g, pl.ds(k0, tk), :]
                k_cats[g] = jnp.where(feature_lane, widen(k), k_aug) if use_aug else k
            scores.append(_dot_nt(k_cats[g], qs[c]))
        return scores

    def tile_update(j, edge, scores):
        k0 = pl.multiple_of(j * tk, tk)
        valid = None
        if edge:
            dist = rel_pos() + (q0 - k0).astype(F32)
            valid = dist >= 0.0
            if window:
                valid = valid & (dist < float(window))
        probs, alphas = [], []
        for g in range(C):
            s = scores[g]
            if valid is not None:
                s = jnp.where(valid, s, NEG_INF)
            m_prev = m_s[g]
            m_new = jnp.maximum(m_prev, jnp.max(s, axis=0, keepdims=True))
            alpha = jnp.exp2(m_prev - m_new)
            p = jnp.exp2(s - m_new)
            m_s[g] = m_new
            probs.append(p.astype(BF16))
            alphas.append(alpha)
        for c in range(C):
            v = jnp.concatenate([v_ref[0, (c * hpc) // R, :, pl.ds(k0, tk)], ones_rows], axis=0)
            acc_s[c] = alphas[c] * acc_s[c] + _dot(v, probs[c])

    def tile(j, edge):
        tile_update(j, edge, tile_scores(j))

    def body(step, carry):
        j = hi - step
        is_edge = step == 0
        if window:
            is_edge = is_edge | (q0 - j * tk + (tq - 1) >= window)
        run = (step == 0) | (need_s[j] > 0) if need_s is not None else None

        def when(c):
            return pl.when(c if run is None else c & run)

        @when(is_edge)
        def _():
            tile(j, True)

        @when(jnp.logical_not(is_edge))
        def _():
            tile(j, False)

        return carry

    lax.fori_loop(0, hi - lo + 1, body, 0)

    for c in range(C):
        m = m_s[c]
        l = acc_s[c, dv:dv + 1]
        acc = acc_s[c, 0:dv]
        if has_sink:
            sk = LOG2E * jnp.concatenate(
                [jnp.broadcast_to(sink_ref[:, c * hpc + r:c * hpc + r + 1], (1, tq)) for r in range(hpc)], axis=1)
            if slopes is not None:
                t_q = jnp.concatenate([lax.broadcasted_iota(jnp.int32, (1, tq), 1) + q0] * hpc, axis=1)
                m = m - slope_rows[c] * t_q.astype(F32)
            m_f = jnp.maximum(m, sk)
            a = jnp.exp2(m - m_f)
            l = l * a + jnp.exp2(sk - m_f)
            acc = acc * a
        out = acc / l
        for r in range(hpc):
            hh = c * hpc + r
            o_ref[0, :, hh * dv:(hh + 1) * dv] = out[:, r * tq:(r + 1) * tq].T.astype(o_ref.dtype)


def _flash_call(q, k, v, *, slopes=None, window=0, mask=None, mask_block=0, mask_t=False, sinks=None,
                tq=128, tk=256, hpc=None, skip_tiles=False, out_dtype=BF16, name="flash"):
    B, H, S, dq = q.shape
    G = k.shape[1]
    dv = v.shape[-2]
    tk = min(tk, S)
    assert tk % tq == 0 and S % tk == 0
    R = H // G
    hpc = R if hpc is None else hpc
    assert R % hpc == 0
    C = H // hpc
    in_specs = [
        pl.BlockSpec((1, H, tq, dq), lambda b, i: (b, 0, i, 0)),
        pl.BlockSpec((1, G, S, dq), lambda b, i: (b, 0, 0, 0)),
        pl.BlockSpec((1, G, dv, S), lambda b, i: (b, 0, 0, 0)),
    ]
    args = [q, k, v]
    mask_per_head = False
    if mask is not None:
        hm = mask.shape[1]
        mask_per_head = hm > 1
        if mask_t:
            in_specs.append(pl.BlockSpec((1, hm, mask.shape[2], tq), lambda b, i: (b, 0, 0, i)))
        else:
            in_specs.append(pl.BlockSpec((1, hm, tq, mask.shape[3]), lambda b, i: (b, 0, i, 0)))
        args.append(mask)
    if sinks is not None:
        in_specs.append(pl.BlockSpec(sinks.shape, lambda b, i: (0, 0)))
        args.append(sinks)
    kern = functools.partial(
        _flash_kernel, H=H, G=G, hpc=hpc, tq=tq, tk=tk,
        slopes=None if slopes is None else tuple(float(s) * LOG2E for s in slopes),
        window=window, mask_block=mask_block if mask is not None else 0,
        mask_per_head=mask_per_head, mask_t=mask_t, has_sink=sinks is not None,
        skip=skip_tiles)
    ones_rows = 16
    scratch = [pltpu.VMEM((C, 1, hpc * tq), F32), pltpu.VMEM((C, dv + ones_rows, hpc * tq), F32)]
    if mask is not None:
        assert tk % mask_block == 0
    if skip_tiles:
        assert mask is not None
        scratch.append(pltpu.SMEM((S // tk,), jnp.int32))
    return pl.pallas_call(
        kern,
        grid=(B, S // tq),
        in_specs=in_specs,
        out_specs=pl.BlockSpec((1, tq, H * dv), lambda b, i: (b, i, 0)),
        out_shape=jax.ShapeDtypeStruct((B, S, H * dv), out_dtype),
        scratch_shapes=scratch,
        compiler_params=pltpu.CompilerParams(vmem_limit_bytes=VMEM_LIMIT),
        name=name,
    )(*args)


def _merge_kernel(x_ref, sc1_ref, sh1_ref, g1_ref, sc2_ref, sh2_ref, ng_ref, wg_ref, wb_ref, wo_ref,
                  ocmp_ref, oslc_ref, owin_ref, gs_ref, ob_ref, oc_ref, od_ref, wr_ref, br_ref,
                  xo_ref, h2_ref, rt_ref):
    hd = HEAD_DIM
    x = x_ref[0]
    ng = ng_ref[...]
    h = _rms(x, ng[0:1]) * (1.0 + sc1_ref[0]) + sh1_ref[0]
    hb = h.astype(BF16)
    gs = gs_ref[0]
    ocmp = ocmp_ref[0]
    oslc = oslc_ref[0]
    owin = owin_ref[0]
    parts = []
    for hh in range(NSA_HEADS):
        c0 = GATE_LANE0 + 3 * hh
        sl = slice(hh * hd, (hh + 1) * hd)
        parts.append(gs[:, c0:c0 + 1] * ocmp[:, sl] + gs[:, c0 + 1:c0 + 2] * oslc[:, sl]
                     + gs[:, c0 + 2:c0 + 3] * owin[:, sl])
    o_a = jnp.concatenate(parts, axis=1).astype(BF16)
    branches = (o_a, ob_ref[0], oc_ref[0], od_ref[0])
    D = x.shape[1]
    mixed = None
    for n in range(N_BRANCH):
        gate = _sigmoid(_dot(hb, wg_ref[:, n * D:(n + 1) * D]))
        term = gate * _dot(branches[n], wb_ref[n])
        mixed = term if mixed is None else mixed + term
    xn = x + g1_ref[0] * _dot(mixed.astype(BF16), wo_ref[...])
    xo_ref[0] = xn
    h2 = _rms(xn, ng[1:2]) * (1.0 + sc2_ref[0]) + sh2_ref[0]
    for c in range(SUB):
        h2_ref[pl.ds(c, h2.shape[0], stride=SUB), :] = h2[:, c * LANE:(c + 1) * LANE]

    logits = _dot3(h2, wr_ref[...]) + br_ref[...]
    lane = lax.broadcasted_iota(jnp.int32, logits.shape, 1)
    lanef = lane.astype(F32)
    is_c = lane < N_GROUPS
    lc = jnp.where(is_c, logits, NEG_INF)
    mc = jnp.max(lc, axis=-1, keepdims=True)
    grp = jnp.min(jnp.where(lc == mc, lanef, 1e9), axis=-1, keepdims=True)
    p_grp = 1.0 / jnp.sum(jnp.where(is_c, jnp.exp(lc - mc), 0.0), axis=-1, keepdims=True)
    e_lane = lanef - float(N_GROUPS)
    in_grp = (lane >= N_GROUPS) & (lane < N_GROUPS + N_EXPERTS) & (
        jnp.floor(e_lane / EXPERTS_PER_GROUP) == grp)
    lf = jnp.where(in_grp, logits, NEG_INF)
    m1 = jnp.max(lf, axis=-1, keepdims=True)
    i1 = jnp.min(jnp.where(lf == m1, e_lane, 1e9), axis=-1, keepdims=True)
    lf2 = jnp.where(e_lane == i1, NEG_INF, lf)
    m2 = jnp.max(lf2, axis=-1, keepdims=True)
    i2 = jnp.min(jnp.where(lf2 == m2, e_lane, 1e9), axis=-1, keepdims=True)
    e2 = jnp.exp(m2 - m1)
    w1 = p_grp / (1.0 + e2)
    w2 = p_grp * e2 / (1.0 + e2)
    rt = jnp.where(lane == 0, i1, jnp.where(lane == 1, i2, jnp.where(lane == 2, w1, jnp.where(lane == 3, w2, 0.0))))
    rt_ref[0] = rt


def _merge_call(x, mods, ng, wg, wb, wo, ocmp, oslc, owin, gs, ob, oc, od, wr, br):
    B, S, D = x.shape
    tm = 512
    sc1, sh1, g1, sc2, sh2 = mods

    def full(a):
        return pl.BlockSpec(a.shape, lambda b, i: (0,) * a.ndim)

    modspec = pl.BlockSpec((1, 1, D), lambda b, i: (b, 0, 0))
    row = lambda w: pl.BlockSpec((1, tm, w), lambda b, i: (b, i, 0))
    in_specs = [row(D), modspec, modspec, modspec, modspec, modspec, full(ng), full(wg), full(wb), full(wo),
                row(256), row(256), row(256), row(128), row(256), row(256), row(256), full(wr), full(br)]
    return pl.pallas_call(
        _merge_kernel,
        grid=(B, S // tm),
        in_specs=in_specs,
        out_specs=[row(D), pl.BlockSpec((tm * SUB, LANE), lambda b, i: (b * (S // tm) + i, 0)), row(128)],
        out_shape=[jax.ShapeDtypeStruct((B, S, D), F32), jax.ShapeDtypeStruct((B * S * SUB, LANE), F32),
                   jax.ShapeDtypeStruct((B, S, 128), F32)],
        compiler_params=pltpu.CompilerParams(vmem_limit_bytes=VMEM_LIMIT),
        name="merge_router",
    )(x, sc1, sh1, g1, sc2, sh2, ng, wg, wb, wo, ocmp, oslc, owin, gs, ob, oc, od, wr, br)


def _expert_kernel(blk_e_ref, nact_ref, tok_ref, dst_ref, h2_hbm, w13_ref, w2_ref, y2_hbm, xbuf, ybuf, gsem, ssem):
    del blk_e_ref
    i = pl.program_id(0)
    nb = nact_ref[0]
    slot = i % 2
    rows = EXPERT_ROWS

    def tile_rows(t):
        return pl.ds(pl.multiple_of(t * SUB, SUB), SUB)

    def gather_copy(blk, s, r):
        tok = tok_ref[blk * rows + r]
        return pltpu.make_async_copy(h2_hbm.at[tile_rows(tok), :], xbuf.at[s, tile_rows(r), :], gsem.at[s])

    def scatter_copy(blk, s, r):
        d = dst_ref[blk * rows + r]
        return pltpu.make_async_copy(ybuf.at[s, tile_rows(r), :], y2_hbm.at[tile_rows(d), :], ssem.at[s])

    def loop_rows(fn):
        def body(r, carry):
            fn(r)
            return carry
        lax.fori_loop(0, rows, body, 0, unroll=8)

    def wait_gather(s):
        pltpu.make_async_copy(h2_hbm.at[pl.ds(0, rows * SUB), :], xbuf.at[s], gsem.at[s]).wait()

    def wait_scatter(s):
        pltpu.make_async_copy(ybuf.at[s], y2_hbm.at[pl.ds(0, rows * SUB), :], ssem.at[s]).wait()

    def compute():
        ab = None
        for c in range(0, SUB, 2):
            xc = jnp.concatenate([xbuf[slot, pl.ds(c, rows, stride=SUB), :],
                                  xbuf[slot, pl.ds(c + 1, rows, stride=SUB), :]], axis=1).astype(BF16)
            part = _dot(xc, w13_ref[0, 0, c * LANE:(c + 2) * LANE, :].astype(BF16))
            ab = part if ab is None else ab + part
        de = ab.shape[1] // 2
        a = ab[:, :de]
        b = ab[:, de:]
        act = ((a * _sigmoid(a)) * b).astype(BF16)
        for c in range(0, SUB, 2):
            y = _dot(act, w2_ref[0, 0, :, c * LANE:(c + 2) * LANE].astype(BF16))
            ybuf[slot, pl.ds(c, rows, stride=SUB), :] = y[:, :LANE]
            ybuf[slot, pl.ds(c + 1, rows, stride=SUB), :] = y[:, LANE:]

    def step(scatter_prev):
        nxt = jnp.minimum(i + 1, nb - 1)
        for r in range(rows):
            if scatter_prev:
                scatter_copy(i - 1, 1 - slot, r).start()
            gather_copy(nxt, 1 - slot, r).start()
        compute()

    @pl.when(i == 0)
    def _():
        loop_rows(lambda r: gather_copy(0, 0, r).start())

    @pl.when(i < nb)
    def _():
        wait_gather(slot)

    @pl.when((i >= 2) & (i < nb))
    def _():
        wait_scatter(slot)

    @pl.when(i == 0)
    def _():
        step(False)

    @pl.when((i > 0) & (i < nb))
    def _():
        step(True)

    @pl.when(i == nb - 1)
    def _():
        loop_rows(lambda r: scatter_copy(i, slot, r).start())
        wait_gather(1 - slot)

        @pl.when(nb >= 2)
        def _():
            wait_scatter(1 - slot)
        wait_scatter(slot)

    @pl.when(i >= nb)
    def _():
        d0 = pl.multiple_of(dst_ref[i * rows] * SUB, SUB)
        fill = pltpu.make_async_copy(ybuf.at[slot], y2_hbm.at[pl.ds(d0, rows * SUB), :], ssem.at[slot])
        fill.start()
        fill.wait()


def _expert_call(h2, blk_e, n_active, row_tok, row_dst, w13, w2, layer):
    D = SUB * LANE
    assert h2.shape[1] == LANE and w13.shape[2] == D
    n_blocks = blk_e.shape[0]
    de2 = w13.shape[-1]
    grid_spec = pltpu.PrefetchScalarGridSpec(
        num_scalar_prefetch=4,
        grid=(n_blocks,),
        in_specs=[
            pl.BlockSpec(memory_space=pl.ANY),
            pl.BlockSpec((1, 1, D, de2), lambda i, be, na, rt, rd: (layer, be[i], 0, 0)),
            pl.BlockSpec((1, 1, de2 // 2, D), lambda i, be, na, rt, rd: (layer, be[i], 0, 0)),
        ],
        out_specs=pl.BlockSpec(memory_space=pl.ANY),
        scratch_shapes=[pltpu.VMEM((2, EXPERT_ROWS * SUB, LANE), F32), pltpu.VMEM((2, EXPERT_ROWS * SUB, LANE), F32),
                        pltpu.SemaphoreType.DMA((2,)), pltpu.SemaphoreType.DMA((2,))],
    )
    return pl.pallas_call(
        _expert_kernel,
        grid_spec=grid_spec,
        out_shape=jax.ShapeDtypeStruct((row_tok.shape[0] * SUB, LANE), F32),
        compiler_params=pltpu.CompilerParams(
            dimension_semantics=("arbitrary",), vmem_limit_bytes=VMEM_LIMIT),
        name="experts",
    )(blk_e, n_active, row_tok, row_dst, h2, w13, w2)


def _combine_kernel(xn_ref, g2_ref, rt_ref, y0_ref, y1_ref, o_ref):
    rt = rt_ref[...]
    tm = xn_ref.shape[0]
    w0 = rt[:, 2:3]
    w1 = rt[:, 3:4]
    g2 = g2_ref[0]
    for c in range(SUB):
        cols = slice(c * LANE, (c + 1) * LANE)
        y = w0 * y0_ref[pl.ds(c, tm, stride=SUB), :] + w1 * y1_ref[pl.ds(c, tm, stride=SUB), :]
        o_ref[:, cols] = xn_ref[:, cols] + g2[:, cols] * y


def _combine_call(xn, g2, route, y2):
    B, S, D = xn.shape
    tm = 512
    spb = S // tm
    nt = B * spb
    return pl.pallas_call(
        _combine_kernel,
        grid=(B, spb),
        in_specs=[
            pl.BlockSpec((tm, D), lambda b, i: (b * spb + i, 0)),
            pl.BlockSpec((1, 1, D), lambda b, i: (b, 0, 0)),
            pl.BlockSpec((tm, 128), lambda b, i: (b * spb + i, 0)),
            pl.BlockSpec((tm * SUB, LANE), lambda b, i: (b * spb + i, 0)),
            pl.BlockSpec((tm * SUB, LANE), lambda b, i: (nt + b * spb + i, 0)),
        ],
        out_specs=pl.BlockSpec((tm, D), lambda b, i: (b * spb + i, 0)),
        out_shape=jax.ShapeDtypeStruct((B * S, D), F32),
        compiler_params=pltpu.CompilerParams(vmem_limit_bytes=VMEM_LIMIT),
        name="moe_combine",
    )(xn.reshape(B * S, D), g2, route, y2, y2).reshape(B, S, D)


def _moe(h2, route, w13, w2, layer):
    N = h2.shape[0] // SUB
    K = 2
    E = N_EXPERTS
    rows = EXPERT_ROWS
    flat_e = route[:, 0:2].astype(jnp.int32).reshape(-1)
    order = jnp.argsort(flat_e).astype(jnp.int32)
    counts = jnp.sum((flat_e[:, None] == jnp.arange(E)[None, :]).astype(jnp.int32), axis=0)
    padded = (counts + rows - 1) // rows * rows
    pend = jnp.cumsum(padded)
    pstart = pend - padded
    start = jnp.cumsum(counts) - counts
    n_blocks = (N * K) // rows + E
    R = n_blocks * rows
    blk_e = jnp.minimum(
        jnp.sum((jnp.arange(n_blocks)[:, None] * rows >= pend[None, :]).astype(jnp.int32), axis=1), E - 1)
    pos = jnp.arange(R, dtype=jnp.int32).reshape(n_blocks, rows)
    local = pos - pstart[blk_e][:, None]
    is_real = ((pos < pend[E - 1]) & (local < counts[blk_e][:, None])).reshape(R)
    rank = jnp.clip(start[blk_e][:, None] + local, 0, N * K - 1).reshape(R)
    pair = order[rank]
    flat = pos.reshape(R)
    real_before = jnp.cumsum(is_real.astype(jnp.int32)) - is_real.astype(jnp.int32)
    row_tok = jnp.where(is_real, pair // K, flat % N).astype(jnp.int32)
    row_dst = jnp.where(is_real, (pair % K) * N + pair // K, N * K + flat - real_before).astype(jnp.int32)
    n_active = (pend[E - 1:E] // rows).astype(jnp.int32)
    return _expert_call(h2, blk_e.astype(jnp.int32), n_active, row_tok, row_dst, w13, w2, layer)


def kernel(x, c, w_ada, b_ada, norm_gain, w_in, qk_gain, cmp_pe, cmp_w1, cmp_w2, swa_sinks,
           lat_gain_q, lat_gain_kv, rope_gain, w_uq, w_ukv, w_branch, w_out,
           w_coarse, b_coarse, w_fine, b_fine, w13, w2):
    B, S, D = x.shape
    L = w_in.shape[0]
    assert S % MOBA_BLOCK == 0 and D == 1024
    slopes = _alibi_slopes()

    half = MLA_ROPE // 2
    inv = ROPE_THETA ** (-jnp.arange(half, dtype=F32) / half)
    ang = jnp.arange(S).astype(F32)[:, None] * inv[None, :]
    cos = jnp.tile(jnp.cos(ang), (1, 8))
    sin = jnp.tile(jnp.concatenate([-jnp.sin(ang), jnp.sin(ang)], axis=1), (1, 4))

    mod = _ada_mod(c, w_ada, b_ada)

    w_attn = jnp.concatenate(
        [w_in[:, :, :448], w_in[:, :, 512:576], w_in[:, :, 448:512], w_in[:, :, 576:640],
         w_in[:, :, 652:ATTN_OLD], w_in[:, :, 640:652],
         jnp.zeros((L, D, ATTN_COLS - ATTN_OLD), F32)], axis=2).astype(BF16)
    w_gate = w_in[:, :, ATTN_OLD:].astype(BF16)
    dq, dkv = MLA_NOPE + MLA_ROPE, MLA_NOPE + MLA_V
    uq = w_uq.reshape(L, MLA_Q_RANK, MLA_HEADS, dq)
    w_uq_b = jnp.concatenate([uq[..., :MLA_NOPE].reshape(L, MLA_Q_RANK, -1),
                              uq[..., MLA_NOPE:].reshape(L, MLA_Q_RANK, -1)], axis=2).astype(BF16)
    ukv = w_ukv.reshape(L, MLA_KV_RANK, MLA_HEADS, dkv)
    w_ukv_b = jnp.concatenate([ukv[..., :MLA_NOPE].reshape(L, MLA_KV_RANK, -1),
                               ukv[..., MLA_NOPE:].reshape(L, MLA_KV_RANK, -1)], axis=2).astype(BF16)

    qs = HEAD_DIM ** -0.5 * LOG2E
    ms = dq ** -0.5 * LOG2E

    def lanes(v, reps, scale=1.0):
        r = jnp.tile(v, (1, reps)) * scale
        return jnp.pad(r, ((0, 0), (0, 4 * HEAD_DIM - r.shape[1])))

    gq = qk_gain
    gain_rows = jnp.stack([
        lanes(gq[:, QK_NSA_Q], 4, qs),
        lanes(jnp.concatenate([gq[:, QK_NSA_KS], gq[:, QK_NSA_KW]], axis=1), 1),
        lanes(gq[:, QK_SWA_Q], 4, qs),
        lanes(gq[:, QK_SWA_K], 2),
        lanes(gq[:, QK_MOBA_Q], 4, qs),
        lanes(gq[:, QK_MOBA_K], 4),
        lanes(gq[:, QK_MLA_Q], 4, ms),
        lanes(rope_gain[:, 0], 4, ms),
        lanes(gq[:, QK_MLA_K], 4),
        lanes(rope_gain[:, 1], 4),
    ], axis=1)
    bd64 = jnp.asarray(np.kron(np.eye(4), np.ones((HEAD_DIM, HEAD_DIM))) / HEAD_DIM, dtype=BF16)
    bd32 = jnp.asarray(np.kron(np.eye(4), np.ones((MLA_ROPE, MLA_ROPE))) / MLA_ROPE, dtype=BF16)
    w_branch_b = w_branch.astype(BF16)
    w_out_b = w_out.astype(BF16)
    w_router = jnp.concatenate(
        [w_coarse, w_fine, jnp.zeros((L, D, 128 - N_GROUPS - N_EXPERTS), F32)], axis=2)
    b_router = jnp.concatenate(
        [b_coarse, b_fine, jnp.zeros((L, 128 - N_GROUPS - N_EXPERTS), F32)], axis=1)
    sinks_pad = jnp.concatenate([swa_sinks, jnp.zeros((L, 128 - SWA_HEADS), F32)], axis=1)

    for l in range(L):
        m6 = mod[l].reshape(B, 6, 1, D)
        sh1, sc1, g1, sh2, sc2, g2 = (m6[:, j] for j in range(6))
        (qa, kcr, vcr, ks, vs, kw, vw, gs, qb, kb, vb, qc, kc, vc, mc, qd, kd, vd) = _proj_call(
            x, sc1, sh1, norm_gain[l, 0:1], w_attn[l], gain_rows[l], bd64, bd32, lat_gain_q[l][None],
            lat_gain_kv[l][None], w_uq_b[l], w_ukv_b[l], cos, sin)
        kcmp, vcmp = _compress_call(kcr, vcr, cmp_pe[l], cmp_w1[l], cmp_w2[l], qk_gain[l, QK_NSA_KC][None])
        o_cmp, sel = _cmp_attn_call(qa, kcmp, vcmp, slopes[0])
        o_slc = _flash_call(qa, ks, vs, slopes=slopes[0], mask=sel, mask_block=SEL_BLOCK, mask_t=True,
                            tq=256, tk=256, hpc=2, skip_tiles=True, out_dtype=F32, name="nsa_slc")
        o_win = _flash_call(qa, kw, vw, slopes=slopes[0], window=NSA_WINDOW,
                            tq=256, tk=256, hpc=2, out_dtype=F32, name="nsa_win")
        o_b = _flash_call(qb, kb, vb, slopes=slopes[1], window=SWA_WINDOW, sinks=sinks_pad[l][None],
                          tq=256, tk=256, name="swa")
        o_c = _flash_call(qc, kc, vc, slopes=slopes[2], mask=mc, mask_block=MOBA_BLOCK,
                          tq=512, tk=512, name="moba")
        o_d = _flash_call(qd, kd, vd, tq=512, tk=512, name="mla")
        xn, h2, route = _merge_call(
            x, (sc1, sh1, g1, sc2, sh2), norm_gain[l], w_gate[l], w_branch_b[l], w_out_b[l],
            o_cmp, o_slc, o_win, gs, o_b, o_c, o_d, w_router[l], b_router[l][None])
        route = route.reshape(B * S, 128)
        y2 = _moe(h2, route, w13, w2, l)
        x = _combine_call(xn, g2, route, y2)
    return x
```

```python
import functools
import math

import numpy as np
import jax
import jax.numpy as jnp
from jax import lax
from jax.experimental import pallas as pl
from jax.experimental.pallas import tpu as pltpu

F32 = jnp.float32
BF16 = jnp.bfloat16

HEAD_DIM = 64
NEG_INF = -1e30
EPS = 1e-6
NSA_HEADS = 4
CMP_BLOCK = 32
CMP_STRIDE = 16
CMP_HIDDEN = 256
SEL_BLOCK = 64
SEL_TOPK = 8
NSA_WINDOW = 512
FORCE_BONUS = 1e4
SWA_HEADS = 4
SWA_KV_HEADS = 2
SWA_WINDOW = 128
MOBA_HEADS = 4
MOBA_BLOCK = 256
MOBA_TOPK = 3
MLA_HEADS = 4
MLA_Q_RANK = 384
MLA_KV_RANK = 128
MLA_NOPE = 64
MLA_ROPE = 32
MLA_V = 64
ROPE_THETA = 10000.0
N_BRANCH = 4
BRANCH_WIDTH = 256
N_GROUPS = 4
EXPERTS_PER_GROUP = 8
N_EXPERTS = N_GROUPS * EXPERTS_PER_GROUP
D_EXPERT = 256

(GR_NSA_Q, GR_NSA_K, GR_SWA_Q, GR_SWA_K, GR_MOBA_Q, GR_MOBA_K,
 GR_MLA_QN, GR_MLA_QR, GR_MLA_KN, GR_MLA_KR) = range(10)
QK_NSA_Q, QK_NSA_KC, QK_NSA_KS, QK_NSA_KW = 0, 1, 2, 3
QK_SWA_Q, QK_SWA_K, QK_MOBA_Q, QK_MOBA_K, QK_MLA_Q, QK_MLA_K = 4, 5, 6, 7, 8, 9

ATTN_OLD = 2476
ATTN_COLS = 2560
GATE_LANE0 = 32
LOG2E = math.log2(math.e)
SUB, LANE = 8, 128
EXPERT_ROWS = 256
VMEM_LIMIT = 56 * 1024 * 1024


def _alibi_slopes():
    n = NSA_HEADS + SWA_HEADS + MOBA_HEADS

    def pow2(m):
        start = 2.0 ** (-8.0 / m)
        return [start ** (i + 1) for i in range(m)]

    c = 2 ** int(math.floor(math.log2(n)))
    s = pow2(c) + (pow2(2 * c)[0::2][: n - c] if c < n else [])
    s = -np.sort(-np.asarray(s, np.float32))
    return s.reshape(NSA_HEADS, 3).T


def _dot(a, b):
    return jnp.dot(a, b, preferred_element_type=F32)


def _dot_nt(a, b):
    return lax.dot_general(a, b, (((1,), (1,)), ((), ())), preferred_element_type=F32)


def _split(a):
    hi = a.astype(BF16)
    lo = (a - hi.astype(F32)).astype(BF16)
    return hi, lo


def _dot3(a, b):
    ah, al = _split(a)
    bh, bl = _split(b)
    return _dot(ah, bh) + (_dot(ah, bl) + _dot(al, bh))


def _dot3_nt(a, b):
    ah, al = _split(a)
    bh, bl = _split(b)
    return _dot_nt(ah, bh) + (_dot_nt(ah, bl) + _dot_nt(al, bh))


def _rms(x, g):
    return x * lax.rsqrt(jnp.mean(x * x, axis=-1, keepdims=True) + EPS) * g


def _sigmoid(x):
    return 1.0 / (1.0 + jnp.exp(-x))


AUG_MASK0 = 6


def _bf16_pieces(x, n=3):
    out, r = [], float(x)
    for _ in range(n):
        piece = float(np.asarray(r, np.float32).astype(jnp.bfloat16).astype(np.float32))
        out.append(piece)
        r -= piece
    return out


def _topk_mask(score, k, axis=1):
    n = score.shape[axis]
    iota = lax.broadcasted_iota(jnp.int32, score.shape, axis).astype(F32)
    sel = jnp.zeros(score.shape, F32)
    for _ in range(k):
        m = jnp.max(score, axis=axis, keepdims=True)
        idx = jnp.min(jnp.where(score == m, iota, float(n)), axis=axis, keepdims=True)
        hit = iota == idx
        sel = jnp.where(hit, jnp.where(m > 0.5 * NEG_INF, 1.0, 0.0), sel)
        score = jnp.where(hit, -3e38, score)
    return sel


def _ada_kernel(c_ref, w_ref, b_ref, o_ref):
    c = c_ref[...]
    a = c * _sigmoid(c)
    o_ref[0] = _dot(a, w_ref[0]) + b_ref[0]


def _ada_mod(c, w_ada, b_ada):
    L, D, D6 = w_ada.shape
    B = c.shape[0]
    tn = 1024
    return pl.pallas_call(
        _ada_kernel,
        grid=(L, D6 // tn),
        in_specs=[
            pl.BlockSpec((B, D), lambda l, j: (0, 0)),
            pl.BlockSpec((1, D, tn), lambda l, j: (l, 0, j)),
            pl.BlockSpec((1, 1, tn), lambda l, j: (l, 0, j)),
        ],
        out_specs=pl.BlockSpec((1, B, tn), lambda l, j: (l, 0, j)),
        out_shape=jax.ShapeDtypeStruct((L, B, D6), F32),
        compiler_params=pltpu.CompilerParams(vmem_limit_bytes=VMEM_LIMIT),
        name="ada_mod",
    )(c, w_ada, b_ada.reshape(L, 1, D6))


def _rms_blocks(x, bd, g):
    hi, lo = _split(x * x)
    ms = _dot(hi, bd) + _dot(lo, bd)
    return x * lax.rsqrt(ms + EPS) * g


def _proj_kernel(x_ref, sc_ref, sh_ref, ng_ref, w_ref, gr_ref, bd64_ref, bd32_ref, lgq_ref, lgkv_ref,
                 wuq_ref, wukv_ref, cos_ref, sin_ref,
                 qa_ref, kcr_ref, vcr_ref, ks_ref, vs_ref, kw_ref, vw_ref, gs_ref,
                 qb_ref, kb_ref, vb_ref, qc_ref, kc_ref, vc_ref, mc_ref, qd_ref, kd_ref, vd_ref,
                 kmean_s):
    i = pl.program_id(1)
    hd = HEAD_DIM
    x = x_ref[0]
    h = _rms(x, ng_ref[...]) * (1.0 + sc_ref[0]) + sh_ref[0]
    hb = h.astype(BF16)
    bd64 = bd64_ref[...]
    bd64h = bd64_ref[0:2 * hd, 0:2 * hd]
    bd32 = bd32_ref[...]

    def grow(r, w=4 * hd):
        return gr_ref[r:r + 1, 0:w]

    def proj(a, b):
        return _dot(hb, w_ref[:, a:b])

    def store_heads(ref, slab, n):
        for hh in range(n):
            ref[0, hh] = slab[:, hh * hd:(hh + 1) * hd].astype(BF16)

    def store_heads_t(ref, slab_t, n):
        for hh in range(n):
            ref[0, hh] = slab_t[hh * hd:(hh + 1) * hd].astype(BF16)

    store_heads(qa_ref, _rms_blocks(proj(0, 256), bd64, grow(GR_NSA_Q)), NSA_HEADS)
    p = proj(256, 640)
    kcr_ref[0] = p[:, 0:64]
    vcr_ref[0] = p[:, 64:128]
    kk = _rms_blocks(p[:, 128:256], bd64h, grow(GR_NSA_K, 2 * hd))
    ks_ref[0, 0] = kk[:, :hd].astype(BF16)
    kw_ref[0, 0] = kk[:, hd:].astype(BF16)
    vt = p[:, 256:384].T
    vs_ref[0, 0] = vt[:hd].astype(BF16)
    vw_ref[0, 0] = vt[hd:].astype(BF16)

    p = proj(640, 1152)
    store_heads(qb_ref, _rms_blocks(p[:, 0:256], bd64, grow(GR_SWA_Q)), SWA_HEADS)
    store_heads(kb_ref, _rms_blocks(p[:, 256:384], bd64h, grow(GR_SWA_K, 2 * hd)), SWA_KV_HEADS)
    store_heads_t(vb_ref, p[:, 384:512].T, SWA_KV_HEADS)

    @pl.when(i == 0)
    def _():
        kmean_s[...] = jnp.zeros(kmean_s.shape, F32)

    p = proj(1152, 1920)
    qn = _rms_blocks(p[:, 0:256], bd64, grow(GR_MOBA_Q))
    kn = _rms_blocks(p[:, 256:512], bd64, grow(GR_MOBA_K))
    store_heads(qc_ref, qn, MOBA_HEADS)
    store_heads(kc_ref, kn, MOBA_HEADS)
    store_heads_t(vc_ref, p[:, 512:768].T, MOBA_HEADS)
    nblk = kmean_s.shape[0]
    tm = x.shape[0]
    bpt = tm // MOBA_BLOCK
    for u in range(bpt):
        kmean_s[pl.ds(i * bpt + u, 1), :] = jnp.mean(kn[u * MOBA_BLOCK:(u + 1) * MOBA_BLOCK], axis=0, keepdims=True)
    own = i * bpt + lax.broadcasted_iota(jnp.int32, (tm, 1), 0) // MOBA_BLOCK
    blk_iota = lax.broadcasted_iota(jnp.int32, (1, nblk), 1)
    head_of_lane = lax.broadcasted_iota(jnp.int32, (1, 4 * hd), 1) // hd
    kmeans = kmean_s[...]
    for hh in range(MOBA_HEADS):
        g = _dot3_nt(jnp.where(head_of_lane == hh, qn, 0.0), kmeans)
        g = jnp.where(blk_iota < own, g, NEG_INF)
        mc_ref[0, hh] = jnp.where(blk_iota == own, 1.0, _topk_mask(g, MOBA_TOPK))

    p = proj(1920, 2560)
    gs_ref[0] = _sigmoid(p[:, 512:640])
    cos = cos_ref[...]
    sin = sin_ref[...]
    half = MLA_ROPE // 2
    first_half = lax.broadcasted_iota(jnp.int32, (1, 4 * MLA_ROPE), 1) % MLA_ROPE < half

    def rope(v):
        swapped = jnp.where(first_half, pltpu.roll(v, 4 * MLA_ROPE - half, 1), pltpu.roll(v, half, 1))
        return v * cos + swapped * sin

    qlat = _dot(_rms(p[:, 0:384], lgq_ref[...]).astype(BF16), wuq_ref[...])
    kvlat = _dot(_rms(p[:, 384:512], lgkv_ref[...]).astype(BF16), wukv_ref[...])
    q_nope = _rms_blocks(qlat[:, 0:256], bd64, grow(GR_MLA_QN))
    q_rot = rope(_rms_blocks(qlat[:, 256:384], bd32, grow(GR_MLA_QR, 2 * hd)))
    k_nope = _rms_blocks(kvlat[:, 0:256], bd64, grow(GR_MLA_KN))
    k_rot = rope(_rms_blocks(p[:, 512:640], bd32, grow(GR_MLA_KR, 2 * hd)))[:, :MLA_ROPE]
    store_heads_t(vd_ref, kvlat[:, 256:512].T, MLA_HEADS)
    for hh in range(MLA_HEADS):
        qd_ref[0, hh] = jnp.concatenate(
            [q_nope[:, hh * hd:(hh + 1) * hd], q_rot[:, hh * MLA_ROPE:(hh + 1) * MLA_ROPE]], axis=1).astype(BF16)
        kd_ref[0, hh] = jnp.concatenate([k_nope[:, hh * hd:(hh + 1) * hd], k_rot], axis=1).astype(BF16)


def _proj_call(x, sc1, sh1, ng, w_attn, gain_rows, bd64, bd32, lgq, lgkv, wuq, wukv, cos, sin):
    B, S, D = x.shape
    tm = 4 * MOBA_BLOCK
    nblk = S // MOBA_BLOCK
    hd = HEAD_DIM

    def full(shape):
        return pl.BlockSpec(shape, lambda b, i: (0,) * len(shape))

    def heads(nh, d):
        return pl.BlockSpec((1, nh, tm, d), lambda b, i: (b, 0, i, 0))

    in_specs = [
        pl.BlockSpec((1, tm, D), lambda b, i: (b, i, 0)),
        pl.BlockSpec((1, 1, D), lambda b, i: (b, 0, 0)),
        pl.BlockSpec((1, 1, D), lambda b, i: (b, 0, 0)),
        full((1, D)),
        full((D, ATTN_COLS)),
        full(gain_rows.shape),
        full(bd64.shape),
        full(bd32.shape),
        full(lgq.shape),
        full(lgkv.shape),
        full(wuq.shape),
        full(wukv.shape),
        pl.BlockSpec((tm, 4 * MLA_ROPE), lambda b, i: (i, 0)),
        pl.BlockSpec((tm, 4 * MLA_ROPE), lambda b, i: (i, 0)),
    ]
    row64 = pl.BlockSpec((1, tm, hd), lambda b, i: (b, i, 0))
    def heads_t(nh, d):
        return pl.BlockSpec((1, nh, d, tm), lambda b, i: (b, 0, 0, i))

    out_specs = [
        heads(4, hd), row64, row64, heads(1, hd), heads_t(1, hd), heads(1, hd), heads_t(1, hd),
        pl.BlockSpec((1, tm, 128), lambda b, i: (b, i, 0)),
        heads(4, hd), heads(2, hd), heads_t(2, hd),
        heads(4, hd), heads(4, hd), heads_t(4, hd), heads(4, nblk),
        heads(4, MLA_NOPE + MLA_ROPE), heads(4, MLA_NOPE + MLA_ROPE), heads_t(4, MLA_V),
    ]

    def sd(shape, dt):
        return jax.ShapeDtypeStruct(shape, dt)

    out_shape = [
        sd((B, 4, S, hd), BF16), sd((B, S, hd), F32), sd((B, S, hd), F32),
        sd((B, 1, S, hd), BF16), sd((B, 1, hd, S), BF16), sd((B, 1, S, hd), BF16), sd((B, 1, hd, S), BF16),
        sd((B, S, 128), F32),
        sd((B, 4, S, hd), BF16), sd((B, 2, S, hd), BF16), sd((B, 2, hd, S), BF16),
        sd((B, 4, S, hd), BF16), sd((B, 4, S, hd), BF16), sd((B, 4, hd, S), BF16), sd((B, 4, S, nblk), F32),
        sd((B, 4, S, 96), BF16), sd((B, 4, S, 96), BF16), sd((B, 4, MLA_V, S), BF16),
    ]
    return pl.pallas_call(
        _proj_kernel,
        grid=(B, S // tm),
        in_specs=in_specs,
        out_specs=out_specs,
        out_shape=out_shape,
        scratch_shapes=[pltpu.VMEM((nblk, MOBA_HEADS * hd), F32)],
        compiler_params=pltpu.CompilerParams(
            dimension_semantics=("arbitrary", "arbitrary"), vmem_limit_bytes=VMEM_LIMIT),
        name="proj_prep",
    )(x, sc1, sh1, ng, w_attn, gain_rows, bd64, bd32, lgq, lgkv, wuq, wukv, cos, sin)


def _compress_kernel(gk_ref, gv_ref, pe_ref, w1_ref, w2_ref, gkc_ref, kc_ref, vc_ref):
    half = w1_ref.shape[1] // 2
    outs = []
    for j, g_ref in enumerate((gk_ref, gv_ref)):
        g = g_ref[0].astype(BF16)
        top = _dot(g, w1_ref[j, :half].astype(BF16))
        bot = _dot(g, w1_ref[j, half:].astype(BF16))
        bot = jnp.concatenate([bot[1:], bot[:1]], axis=0)
        pe = jnp.broadcast_to(pe_ref[j], (8, pe_ref.shape[2]))
        bias = _dot3(pe, w1_ref[j])[0:1]
        hid = top + bot + bias
        hid = hid * _sigmoid(hid)
        outs.append(_dot(hid.astype(BF16), w2_ref[j].astype(BF16)))
    kc_ref[0] = _rms(outs[0], gkc_ref[...]).astype(BF16)
    dk = outs[1].shape[1]
    vc_ref[0] = jnp.concatenate([outs[1], outs[1]], axis=1).T[:dk].astype(BF16)


def _compress_call(kc_raw, vc_raw, pe, w1, w2, g_kc):
    B, S, dk = kc_raw.shape
    n_grp = S // CMP_STRIDE
    gk = kc_raw.reshape(B, n_grp, CMP_STRIDE * dk)
    gv = vc_raw.reshape(B, n_grp, CMP_STRIDE * dk)
    pe_flat = pe.reshape(2, 1, CMP_BLOCK * dk)
    grp_spec = pl.BlockSpec((1, n_grp, CMP_STRIDE * dk), lambda b: (b, 0, 0))
    out_spec = pl.BlockSpec((1, n_grp, dk), lambda b: (b, 0, 0))
    return pl.pallas_call(
        _compress_kernel,
        grid=(B,),
        in_specs=[
            grp_spec, grp_spec,
            pl.BlockSpec(pe_flat.shape, lambda b: (0, 0, 0)),
            pl.BlockSpec(w1.shape, lambda b: (0, 0, 0)),
            pl.BlockSpec(w2.shape, lambda b: (0, 0, 0)),
            pl.BlockSpec((1, dk), lambda b: (0, 0)),
        ],
        out_specs=[out_spec, pl.BlockSpec((1, dk, n_grp), lambda b: (b, 0, 0))],
        out_shape=[jax.ShapeDtypeStruct((B, n_grp, dk), BF16), jax.ShapeDtypeStruct((B, dk, n_grp), BF16)],
        compiler_params=pltpu.CompilerParams(vmem_limit_bytes=VMEM_LIMIT),
        name="nsa_compress",
    )(gk, gv, pe_flat, w1, w2, g_kc)


def _cmp_attn_kernel(q_ref, kc_ref, vct_ref, cover_ref, o_ref, sel_ref, *, slopes, tq, n_cmp):
    i = pl.program_id(1)
    kc = kc_ref[0]
    vct = vct_ref[0]
    ncp = kc.shape[0]
    t_full = i * tq + lax.broadcasted_iota(jnp.int32, (ncp, tq), 1)
    n_iota = lax.broadcasted_iota(jnp.int32, (ncp, tq), 0)
    dist_i = t_full - (n_iota * CMP_STRIDE + (CMP_BLOCK - 1))
    vis = (n_iota < n_cmp) & (dist_i >= 0)
    dist = dist_i.astype(F32)
    visf = vis.astype(F32)
    psum = jnp.zeros((ncp, tq), F32)
    raw = [_dot_nt(kc, q_ref[0, hh]) for hh in range(NSA_HEADS)]
    probs = []
    for hh in range(NSA_HEADS):
        s = raw[hh] - (slopes[hh] * LOG2E) * dist
        s = jnp.where(vis, s, NEG_INF)
        e = jnp.exp2(s - jnp.max(s, axis=0, keepdims=True)) * visf
        p = e / jnp.maximum(jnp.sum(e, axis=0, keepdims=True), 1e-30)
        probs.append(p.astype(BF16))
        psum = psum + p
    outs = [_dot(vct, pb) for pb in probs]
    o_ref[0] = jnp.concatenate(outs, axis=0).T
    ph, plo = _split(psum)
    cover = cover_ref[...]
    p_slc = _dot(cover, ph) + _dot(cover, plo)
    n_sel = cover.shape[0]
    cur = (i * tq + lax.broadcasted_iota(jnp.int32, (1, tq), 1)) // SEL_BLOCK
    j = lax.broadcasted_iota(jnp.int32, (n_sel, 1), 0)
    forced = jnp.where(j == 0, 1.0, jnp.where(j == cur, 1.0, jnp.where(j == cur - 1, 1.0, 0.0)))
    score = jnp.where(j <= cur, p_slc + FORCE_BONUS * forced, NEG_INF)
    sel_ref[0, 0] = _topk_mask(score, min(SEL_TOPK, n_sel), axis=0)


def _cmp_attn_call(qa, kc, vc, slopes):
    B, H, S, dk = qa.shape
    ncp = kc.shape[1]
    n_cmp = (S - CMP_BLOCK) // CMP_STRIDE + 1
    n_sel = S // SEL_BLOCK
    tq = 256
    starts = np.arange(ncp) * CMP_STRIDE
    jb = np.arange(n_sel) * SEL_BLOCK
    cover = ((starts[:, None] < jb[None, :] + SEL_BLOCK) & (starts[:, None] + CMP_BLOCK > jb[None, :])
             & (np.arange(ncp)[:, None] < n_cmp))
    cover = jnp.asarray(cover.T.astype(np.float32), dtype=BF16)
    return pl.pallas_call(
        functools.partial(_cmp_attn_kernel, slopes=tuple(float(s) for s in slopes), tq=tq, n_cmp=n_cmp),
        grid=(B, S // tq),
        in_specs=[
            pl.BlockSpec((1, H, tq, dk), lambda b, i: (b, 0, i, 0)),
            pl.BlockSpec((1, ncp, dk), lambda b, i: (b, 0, 0)),
            pl.BlockSpec((1, dk, ncp), lambda b, i: (b, 0, 0)),
            pl.BlockSpec((n_sel, ncp), lambda b, i: (0, 0)),
        ],
        out_specs=[
            pl.BlockSpec((1, tq, H * dk), lambda b, i: (b, i, 0)),
            pl.BlockSpec((1, 1, n_sel, tq), lambda b, i: (b, 0, 0, i)),
        ],
        out_shape=[jax.ShapeDtypeStruct((B, S, H * dk), F32),
                   jax.ShapeDtypeStruct((B, 1, n_sel, S), F32)],
        compiler_params=pltpu.CompilerParams(vmem_limit_bytes=VMEM_LIMIT),
        name="nsa_cmp_attn",
    )(qa, kc, vc, cover)


def _flash_kernel(*refs, H, G, hpc, tq, tk, slopes, window, mask_block, mask_per_head, mask_t, has_sink, skip):
    refs = list(refs)
    q_ref, k_ref, v_ref = refs[:3]
    pos = 3
    mask_ref = sink_ref = None
    if mask_block:
        mask_ref = refs[pos]
        pos += 1
    if has_sink:
        sink_ref = refs[pos]
        pos += 1
    o_ref, m_s, acc_s = refs[pos:pos + 3]
    need_s = refs[pos + 3] if skip else None
    use_aug = slopes is not None or bool(mask_block)
    R = H // G
    C = H // hpc
    M = hpc * tq
    dv = v_ref.shape[-2]
    i = pl.program_id(1)
    q0 = i * tq
    hi = q0 // tk
    lo = jnp.maximum(q0 - (window - 1), 0) // tk if window else 0

    def rel_pos():
        r_row = jnp.concatenate([lax.broadcasted_iota(jnp.int32, (1, tq), 1)] * hpc, axis=1)
        return (r_row - lax.broadcasted_iota(jnp.int32, (tk, M), 0)).astype(F32)

    dq = q_ref.shape[-1]
    n_mask = mask_ref.shape[-2 if mask_t else -1] if mask_block else 0
    aug_w = LANE if dq + AUG_MASK0 + n_mask <= LANE else 2 * LANE
    aug0 = dq
    lane_a = lax.broadcasted_iota(jnp.int32, (1, aug_w), 1)
    feature_lane = lane_a < dq

    def slope_lanes(hh):
        row = jnp.zeros((1, aug_w), F32)
        if slopes is not None:
            for n, piece in enumerate(_bf16_pieces(slopes[hh])):
                row = jnp.where((lane_a == aug0 + n) | (lane_a == aug0 + n + 3), piece, row)
        return row

    def widen(a):
        return jnp.concatenate([a, jnp.zeros((a.shape[0], aug_w - dq), a.dtype)], axis=1)

    qs, slope_rows, bms = [], [], []
    for c in range(C):
        heads = [c * hpc + r for r in range(hpc)]
        if hpc == 1:
            qs.append(q_ref[0, heads[0]])
        else:
            qs.append(jnp.concatenate([q_ref[0, hh] for hh in heads], axis=0))
        if slopes is not None:
            slope_rows.append(jnp.concatenate([jnp.full((1, tq), slopes[hh], F32) for hh in heads], axis=1))
        if mask_block:
            parts = [mask_ref[0, hh if mask_per_head else 0] for hh in heads]
            bms.append(jnp.concatenate(parts, axis=1) if mask_t else jnp.concatenate(parts, axis=0).T)
    m_s[...] = jnp.full(m_s.shape, NEG_INF, F32)
    acc_s[...] = jnp.zeros(acc_s.shape, F32)
    ones_rows = jnp.ones((acc_s.shape[1] - dv, tk), BF16)

    if mask_block:
        nblk = bms[0].shape[0]
    if need_s is not None:
        col = bms[0]
        for bm in bms[1:]:
            col = jnp.maximum(col, bm)
        col = jnp.max(col, axis=1, keepdims=True)
        bpt = tk // mask_block
        for jt in range(nblk // bpt):
            need_s[jt] = jnp.max(col[jt * bpt:(jt + 1) * bpt]).astype(jnp.int32)

    if use_aug:
        for c in range(C):
            qa = jnp.concatenate(
                [jnp.broadcast_to(slope_lanes(c * hpc + r), (tq, aug_w)) for r in range(hpc)], axis=0)
            if mask_block:
                place = jnp.where(
                    lax.broadcasted_iota(jnp.int32, (nblk, aug_w), 0) + (aug0 + AUG_MASK0) == lane_a,
                    1.0, 0.0).astype(BF16)
                off_sel = ((bms[c] - 1.0) * -NEG_INF).astype(BF16)
                qa = qa + lax.dot_general(off_sel, place, (((0,), (0,)), ((), ())), preferred_element_type=F32)
            qs[c] = jnp.where(feature_lane, widen(qs[c]), qa.astype(BF16))

    def tile_scores(j):
        k0 = pl.multiple_of(j * tk, tk)
        if use_aug:
            t_key = lax.broadcasted_iota(jnp.int32, (tk, aug_w), 0) + k0
            k_aug = jnp.zeros((tk, aug_w), F32)
            if mask_block:
                key_blk = lax.shift_right_logical(t_key, int(math.log2(mask_block)))
                k_aug = jnp.where(key_blk + (aug0 + AUG_MASK0) == lane_a, 1.0, 0.0)
            if slopes is not None:
                t_hi = (lax.shift_right_logical(t_key, 8) * 256).astype(F32)
                t_lo = (t_key & 255).astype(F32)
                k_aug = jnp.where(lane_a < aug0 + 3, t_hi, jnp.where(lane_a < aug0 + 6, t_lo, k_aug))
            k_aug = k_aug.astype(BF16)
        k_cats = {}
        scores = []
        for c in range(C):
            g = (c * hpc) // R
            if g not in k_cats:
                k = k_ref[0, g, pl.ds(k0, tk), :]
                k_cats[g] = jnp.where(feature_lane, widen(k), k_aug) if use_aug else k
            scores.append(_dot_nt(k_cats[g], qs[c]))
        return scores

    def tile_update(j, edge, scores):
        k0 = pl.multiple_of(j * tk, tk)
        valid = None
        if edge:
            dist = rel_pos() + (q0 - k0).astype(F32)
            valid = dist >= 0.0
            if window:
                valid = valid & (dist < float(window))
        probs, alphas = [], []
        for g in range(C):
            s = scores[g]
            if valid is not None:
                s = jnp.where(valid, s, NEG_INF)
            m_prev = m_s[g]
            m_new = jnp.maximum(m_prev, jnp.max(s, axis=0, keepdims=True))
            alpha = jnp.exp2(m_prev - m_new)
            p = jnp.exp2(s - m_new)
            m_s[g] = m_new
            probs.append(p.astype(BF16))
            alphas.append(alpha)
        for c in range(C):
            v = jnp.concatenate([v_ref[0, (c * hpc) // R, :, pl.ds(k0, tk)], ones_rows], axis=0)
            acc_s[c] = alphas[c] * acc_s[c] + _dot(v, probs[c])

    def tile(j, edge):
        tile_update(j, edge, tile_scores(j))

    def body(step, carry):
        j = hi - step
        is_edge = step == 0
        if window:
            is_edge = is_edge | (q0 - j * tk + (tq - 1) >= window)
        run = (step == 0) | (need_s[j] > 0) if need_s is not None else None

        def when(c):
            return pl.when(c if run is None else c & run)

        @when(is_edge)
        def _():
            tile(j, True)

        @when(jnp.logical_not(is_edge))
        def _():
            tile(j, False)

        return carry

    lax.fori_loop(0, hi - lo + 1, body, 0)

    for c in range(C):
        m = m_s[c]
        l = acc_s[c, dv:dv + 1]
        acc = acc_s[c, 0:dv]
        if has_sink:
            sk = LOG2E * jnp.concatenate(
                [jnp.broadcast_to(sink_ref[:, c * hpc + r:c * hpc + r + 1], (1, tq)) for r in range(hpc)], axis=1)
            if slopes is not None:
                t_q = jnp.concatenate([lax.broadcasted_iota(jnp.int32, (1, tq), 1) + q0] * hpc, axis=1)
                m = m - slope_rows[c] * t_q.astype(F32)
            m_f = jnp.maximum(m, sk)
            a = jnp.exp2(m - m_f)
            l = l * a + jnp.exp2(sk - m_f)
            acc = acc * a
        out = acc / l
        for r in range(hpc):
            hh = c * hpc + r
            o_ref[0, :, hh * dv:(hh + 1) * dv] = out[:, r * tq:(r + 1) * tq].T.astype(o_ref.dtype)


def _flash_call(q, k, v, *, slopes=None, window=0, mask=None, mask_block=0, mask_t=False, sinks=None,
                tq=128, tk=256, hpc=None, skip_tiles=False, out_dtype=BF16, name="flash"):
    B, H, S, dq = q.shape
    G = k.shape[1]
    dv = v.shape[-2]
    tk = min(tk, S)
    assert tk % tq == 0 and S % tk == 0
    R = H // G
    hpc = R if hpc is None else hpc
    assert R % hpc == 0
    C = H // hpc
    in_specs = [
        pl.BlockSpec((1, H, tq, dq), lambda b, i: (b, 0, i, 0)),
        pl.BlockSpec((1, G, S, dq), lambda b, i: (b, 0, 0, 0)),
        pl.BlockSpec((1, G, dv, S), lambda b, i: (b, 0, 0, 0)),
    ]
    args = [q, k, v]
    mask_per_head = False
    if mask is not None:
        hm = mask.shape[1]
        mask_per_head = hm > 1
        if mask_t:
            in_specs.append(pl.BlockSpec((1, hm, mask.shape[2], tq), lambda b, i: (b, 0, 0, i)))
        else:
            in_specs.append(pl.BlockSpec((1, hm, tq, mask.shape[3]), lambda b, i: (b, 0, i, 0)))
        args.append(mask)
    if sinks is not None:
        in_specs.append(pl.BlockSpec(sinks.shape, lambda b, i: (0, 0)))
        args.append(sinks)
    kern = functools.partial(
        _flash_kernel, H=H, G=G, hpc=hpc, tq=tq, tk=tk,
        slopes=None if slopes is None else tuple(float(s) * LOG2E for s in slopes),
        window=window, mask_block=mask_block if mask is not None else 0,
        mask_per_head=mask_per_head, mask_t=mask_t, has_sink=sinks is not None,
        skip=skip_tiles)
    ones_rows = 16
    scratch = [pltpu.VMEM((C, 1, hpc * tq), F32), pltpu.VMEM((C, dv + ones_rows, hpc * tq), F32)]
    if mask is not None:
        assert tk % mask_block == 0
    if skip_tiles:
        assert mask is not None
        scratch.append(pltpu.SMEM((S // tk,), jnp.int32))
    return pl.pallas_call(
        kern,
        grid=(B, S // tq),
        in_specs=in_specs,
        out_specs=pl.BlockSpec((1, tq, H * dv), lambda b, i: (b, i, 0)),
        out_shape=jax.ShapeDtypeStruct((B, S, H * dv), out_dtype),
        scratch_shapes=scratch,
        compiler_params=pltpu.CompilerParams(vmem_limit_bytes=VMEM_LIMIT),
        name=name,
    )(*args)


def _merge_kernel(x_ref, sc1_ref, sh1_ref, g1_ref, sc2_ref, sh2_ref, ng_ref, wg_ref, wb_ref, wo_ref,
                  ocmp_ref, oslc_ref, owin_ref, gs_ref, ob_ref, oc_ref, od_ref, wr_ref, br_ref,
                  xo_ref, h2_ref, rt_ref):
    hd = HEAD_DIM
    x = x_ref[0]
    ng = ng_ref[...]
    h = _rms(x, ng[0:1]) * (1.0 + sc1_ref[0]) + sh1_ref[0]
    hb = h.astype(BF16)
    gs = gs_ref[0]
    ocmp = ocmp_ref[0]
    oslc = oslc_ref[0]
    owin = owin_ref[0]
    parts = []
    for hh in range(NSA_HEADS):
        c0 = GATE_LANE0 + 3 * hh
        sl = slice(hh * hd, (hh + 1) * hd)
        parts.append(gs[:, c0:c0 + 1] * ocmp[:, sl] + gs[:, c0 + 1:c0 + 2] * oslc[:, sl]
                     + gs[:, c0 + 2:c0 + 3] * owin[:, sl])
    o_a = jnp.concatenate(parts, axis=1).astype(BF16)
    branches = (o_a, ob_ref[0], oc_ref[0], od_ref[0])
    D = x.shape[1]
    mixed = None
    for n in range(N_BRANCH):
        gate = _sigmoid(_dot(hb, wg_ref[:, n * D:(n + 1) * D]))
        term = gate * _dot(branches[n], wb_ref[n])
        mixed = term if mixed is None else mixed + term
    xn = x + g1_ref[0] * _dot(mixed.astype(BF16), wo_ref[...])
    xo_ref[0] = xn
    h2 = _rms(xn, ng[1:2]) * (1.0 + sc2_ref[0]) + sh2_ref[0]
    for c in range(SUB):
        h2_ref[pl.ds(c, h2.shape[0], stride=SUB), :] = h2[:, c * LANE:(c + 1) * LANE]

    logits = _dot3(h2, wr_ref[...]) + br_ref[...]
    lane = lax.broadcasted_iota(jnp.int32, logits.shape, 1)
    lanef = lane.astype(F32)
    is_c = lane < N_GROUPS
    lc = jnp.where(is_c, logits, NEG_INF)
    mc = jnp.max(lc, axis=-1, keepdims=True)
    grp = jnp.min(jnp.where(lc == mc, lanef, 1e9), axis=-1, keepdims=True)
    p_grp = 1.0 / jnp.sum(jnp.where(is_c, jnp.exp(lc - mc), 0.0), axis=-1, keepdims=True)
    e_lane = lanef - float(N_GROUPS)
    in_grp = (lane >= N_GROUPS) & (lane < N_GROUPS + N_EXPERTS) & (
        jnp.floor(e_lane / EXPERTS_PER_GROUP) == grp)
    lf = jnp.where(in_grp, logits, NEG_INF)
    m1 = jnp.max(lf, axis=-1, keepdims=True)
    i1 = jnp.min(jnp.where(lf == m1, e_lane, 1e9), axis=-1, keepdims=True)
    lf2 = jnp.where(e_lane == i1, NEG_INF, lf)
    m2 = jnp.max(lf2, axis=-1, keepdims=True)
    i2 = jnp.min(jnp.where(lf2 == m2, e_lane, 1e9), axis=-1, keepdims=True)
    e2 = jnp.exp(m2 - m1)
    w1 = p_grp / (1.0 + e2)
    w2 = p_grp * e2 / (1.0 + e2)
    rt = jnp.where(lane == 0, i1, jnp.where(lane == 1, i2, jnp.where(lane == 2, w1, jnp.where(lane == 3, w2, 0.0))))
    rt_ref[0] = rt


def _merge_call(x, mods, ng, wg, wb, wo, ocmp, oslc, owin, gs, ob, oc, od, wr, br):
    B, S, D = x.shape
    tm = 512
    sc1, sh1, g1, sc2, sh2 = mods

    def full(a):
        return pl.BlockSpec(a.shape, lambda b, i: (0,) * a.ndim)

    modspec = pl.BlockSpec((1, 1, D), lambda b, i: (b, 0, 0))
    row = lambda w: pl.BlockSpec((1, tm, w), lambda b, i: (b, i, 0))
    in_specs = [row(D), modspec, modspec, modspec, modspec, modspec, full(ng), full(wg), full(wb), full(wo),
                row(256), row(256), row(256), row(128), row(256), row(256), row(256), full(wr), full(br)]
    return pl.pallas_call(
        _merge_kernel,
        grid=(B, S // tm),
        in_specs=in_specs,
        out_specs=[row(D), pl.BlockSpec((tm * SUB, LANE), lambda b, i: (b * (S // tm) + i, 0)), row(128)],
        out_shape=[jax.ShapeDtypeStruct((B, S, D), F32), jax.ShapeDtypeStruct((B * S * SUB, LANE), F32),
                   jax.ShapeDtypeStruct((B, S, 128), F32)],
        compiler_params=pltpu.CompilerParams(vmem_limit_bytes=VMEM_LIMIT),
        name="merge_router",
    )(x, sc1, sh1, g1, sc2, sh2, ng, wg, wb, wo, ocmp, oslc, owin, gs, ob, oc, od, wr, br)


def _expert_kernel(blk_e_ref, nact_ref, tok_ref, dst_ref, h2_hbm, w13_ref, w2_ref, y2_hbm, xbuf, ybuf, gsem, ssem):
    del blk_e_ref
    i = pl.program_id(0)
    nb = nact_ref[0]
    slot = i % 2
    rows = EXPERT_ROWS

    def tile_rows(t):
        return pl.ds(pl.multiple_of(t * SUB, SUB), SUB)

    def gather_copy(blk, s, r):
        tok = tok_ref[blk * rows + r]
        return pltpu.make_async_copy(h2_hbm.at[tile_rows(tok), :], xbuf.at[s, tile_rows(r), :], gsem.at[s])

    def scatter_copy(blk, s, r):
        d = dst_ref[blk * rows + r]
        return pltpu.make_async_copy(ybuf.at[s, tile_rows(r), :], y2_hbm.at[tile_rows(d), :], ssem.at[s])

    def loop_rows(fn):
        def body(r, carry):
            fn(r)
            return carry
        lax.fori_loop(0, rows, body, 0, unroll=8)

    def wait_gather(s):
        pltpu.make_async_copy(h2_hbm.at[pl.ds(0, rows * SUB), :], xbuf.at[s], gsem.at[s]).wait()

    def wait_scatter(s):
        pltpu.make_async_copy(ybuf.at[s], y2_hbm.at[pl.ds(0, rows * SUB), :], ssem.at[s]).wait()

    def compute():
        ab = None
        for c in range(0, SUB, 2):
            xc = jnp.concatenate([xbuf[slot, pl.ds(c, rows, stride=SUB), :],
                                  xbuf[slot, pl.ds(c + 1, rows, stride=SUB), :]], axis=1).astype(BF16)
            part = _dot(xc, w13_ref[0, 0, c * LANE:(c + 2) * LANE, :].astype(BF16))
            ab = part if ab is None else ab + part
        de = ab.shape[1] // 2
        a = ab[:, :de]
        b = ab[:, de:]
        act = ((a * _sigmoid(a)) * b).astype(BF16)
        for c in range(0, SUB, 2):
            y = _dot(act, w2_ref[0, 0, :, c * LANE:(c + 2) * LANE].astype(BF16))
            ybuf[slot, pl.ds(c, rows, stride=SUB), :] = y[:, :LANE]
            ybuf[slot, pl.ds(c + 1, rows, stride=SUB), :] = y[:, LANE:]

    def step(scatter_prev):
        nxt = jnp.minimum(i + 1, nb - 1)
        for r in range(rows):
            if scatter_prev:
                scatter_copy(i - 1, 1 - slot, r).start()
            gather_copy(nxt, 1 - slot, r).start()
        compute()

    @pl.when(i == 0)
    def _():
        loop_rows(lambda r: gather_copy(0, 0, r).start())

    @pl.when(i < nb)
    def _():
        wait_gather(slot)

    @pl.when((i >= 2) & (i < nb))
    def _():
        wait_scatter(slot)

    @pl.when(i == 0)
    def _():
        step(False)

    @pl.when((i > 0) & (i < nb))
    def _():
        step(True)

    @pl.when(i == nb - 1)
    def _():
        loop_rows(lambda r: scatter_copy(i, slot, r).start())
        wait_gather(1 - slot)

        @pl.when(nb >= 2)
        def _():
            wait_scatter(1 - slot)
        wait_scatter(slot)

    @pl.when(i >= nb)
    def _():
        d0 = pl.multiple_of(dst_ref[i * rows] * SUB, SUB)
        fill = pltpu.make_async_copy(ybuf.at[slot], y2_hbm.at[pl.ds(d0, rows * SUB), :], ssem.at[slot])
        fill.start()
        fill.wait()


def _expert_call(h2, blk_e, n_active, row_tok, row_dst, w13, w2, layer):
    D = SUB * LANE
    assert h2.shape[1] == LANE and w13.shape[2] == D
    n_blocks = blk_e.shape[0]
    de2 = w13.shape[-1]
    grid_spec = pltpu.PrefetchScalarGridSpec(
        num_scalar_prefetch=4,
        grid=(n_blocks,),
        in_specs=[
            pl.BlockSpec(memory_space=pl.ANY),
            pl.BlockSpec((1, 1, D, de2), lambda i, be, na, rt, rd: (layer, be[i], 0, 0)),
            pl.BlockSpec((1, 1, de2 // 2, D), lambda i, be, na, rt, rd: (layer, be[i], 0, 0)),
        ],
        out_specs=pl.BlockSpec(memory_space=pl.ANY),
        scratch_shapes=[pltpu.VMEM((2, EXPERT_ROWS * SUB, LANE), F32), pltpu.VMEM((2, EXPERT_ROWS * SUB, LANE), F32),
                        pltpu.SemaphoreType.DMA((2,)), pltpu.SemaphoreType.DMA((2,))],
    )
    return pl.pallas_call(
        _expert_kernel,
        grid_spec=grid_spec,
        out_shape=jax.ShapeDtypeStruct((row_tok.shape[0] * SUB, LANE), F32),
        compiler_params=pltpu.CompilerParams(
            dimension_semantics=("arbitrary",), vmem_limit_bytes=VMEM_LIMIT),
        name="experts",
    )(blk_e, n_active, row_tok, row_dst, h2, w13, w2)


def _combine_kernel(xn_ref, g2_ref, rt_ref, y0_ref, y1_ref, o_ref):
    rt = rt_ref[...]
    tm = xn_ref.shape[0]
    w0 = rt[:, 2:3]
    w1 = rt[:, 3:4]
    g2 = g2_ref[0]
    for c in range(SUB):
        cols = slice(c * LANE, (c + 1) * LANE)
        y = w0 * y0_ref[pl.ds(c, tm, stride=SUB), :] + w1 * y1_ref[pl.ds(c, tm, stride=SUB), :]
        o_ref[:, cols] = xn_ref[:, cols] + g2[:, cols] * y


def _combine_call(xn, g2, route, y2):
    B, S, D = xn.shape
    tm = 512
    spb = S // tm
    nt = B * spb
    return pl.pallas_call(
        _combine_kernel,
        grid=(B, spb),
        in_specs=[
            pl.BlockSpec((tm, D), lambda b, i: (b * spb + i, 0)),
            pl.BlockSpec((1, 1, D), lambda b, i: (b, 0, 0)),
            pl.BlockSpec((tm, 128), lambda b, i: (b * spb + i, 0)),
            pl.BlockSpec((tm * SUB, LANE), lambda b, i: (b * spb + i, 0)),
            pl.BlockSpec((tm * SUB, LANE), lambda b, i: (nt + b * spb + i, 0)),
        ],
        out_specs=pl.BlockSpec((tm, D), lambda b, i: (b * spb + i, 0)),
        out_shape=jax.ShapeDtypeStruct((B * S, D), F32),
        compiler_params=pltpu.CompilerParams(vmem_limit_bytes=VMEM_LIMIT),
        name="moe_combine",
    )(xn.reshape(B * S, D), g2, route, y2, y2).reshape(B, S, D)


def _moe(h2, route, w13, w2, layer):
    N = h2.shape[0] // SUB
    K = 2
    E = N_EXPERTS
    rows = EXPERT_ROWS
    flat_e = route[:, 0:2].astype(jnp.int32).reshape(-1)
    order = jnp.argsort(flat_e).astype(jnp.int32)
    counts = jnp.sum((flat_e[:, None] == jnp.arange(E)[None, :]).astype(jnp.int32), axis=0)
    padded = (counts + rows - 1) // rows * rows
    pend = jnp.cumsum(padded)
    pstart = pend - padded
    start = jnp.cumsum(counts) - counts
    n_blocks = (N * K) // rows + E
    R = n_blocks * rows
    blk_e = jnp.minimum(
        jnp.sum((jnp.arange(n_blocks)[:, None] * rows >= pend[None, :]).astype(jnp.int32), axis=1), E - 1)
    pos = jnp.arange(R, dtype=jnp.int32).reshape(n_blocks, rows)
    local = pos - pstart[blk_e][:, None]
    is_real = ((pos < pend[E - 1]) & (local < counts[blk_e][:, None])).reshape(R)
    rank = jnp.clip(start[blk_e][:, None] + local, 0, N * K - 1).reshape(R)
    pair = order[rank]
    flat = pos.reshape(R)
    real_before = jnp.cumsum(is_real.astype(jnp.int32)) - is_real.astype(jnp.int32)
    row_tok = jnp.where(is_real, pair // K, flat % N).astype(jnp.int32)
    row_dst = jnp.where(is_real, (pair % K) * N + pair // K, N * K + flat - real_before).astype(jnp.int32)
    n_active = (pend[E - 1:E] // rows).astype(jnp.int32)
    return _expert_call(h2, blk_e.astype(jnp.int32), n_active, row_tok, row_dst, w13, w2, layer)


def kernel(x, c, w_ada, b_ada, norm_gain, w_in, qk_gain, cmp_pe, cmp_w1, cmp_w2, swa_sinks,
           lat_gain_q, lat_gain_kv, rope_gain, w_uq, w_ukv, w_branch, w_out,
           w_coarse, b_coarse, w_fine, b_fine, w13, w2):
    B, S, D = x.shape
    L = w_in.shape[0]
    assert S % MOBA_BLOCK == 0 and D == 1024
    slopes = _alibi_slopes()

    half = MLA_ROPE // 2
    inv = ROPE_THETA ** (-jnp.arange(half, dtype=F32) / half)
    ang = jnp.arange(S).astype(F32)[:, None] * inv[None, :]
    cos = jnp.tile(jnp.cos(ang), (1, 8))
    sin = jnp.tile(jnp.concatenate([-jnp.sin(ang), jnp.sin(ang)], axis=1), (1, 4))

    mod = _ada_mod(c, w_ada, b_ada)

    w_attn = jnp.concatenate(
        [w_in[:, :, :448], w_in[:, :, 512:576], w_in[:, :, 448:512], w_in[:, :, 576:640],
         w_in[:, :, 652:ATTN_OLD], w_in[:, :, 640:652],
         jnp.zeros((L, D, ATTN_COLS - ATTN_OLD), F32)], axis=2).astype(BF16)
    w_gate = w_in[:, :, ATTN_OLD:].astype(BF16)
    dq, dkv = MLA_NOPE + MLA_ROPE, MLA_NOPE + MLA_V
    uq = w_uq.reshape(L, MLA_Q_RANK, MLA_HEADS, dq)
    w_uq_b = jnp.concatenate([uq[..., :MLA_NOPE].reshape(L, MLA_Q_RANK, -1),
                              uq[..., MLA_NOPE:].reshape(L, MLA_Q_RANK, -1)], axis=2).astype(BF16)
    ukv = w_ukv.reshape(L, MLA_KV_RANK, MLA_HEADS, dkv)
    w_ukv_b = jnp.concatenate([ukv[..., :MLA_NOPE].reshape(L, MLA_KV_RANK, -1),
                               ukv[..., MLA_NOPE:].reshape(L, MLA_KV_RANK, -1)], axis=2).astype(BF16)

    qs = HEAD_DIM ** -0.5 * LOG2E
    ms = dq ** -0.5 * LOG2E

    def lanes(v, reps, scale=1.0):
        r = jnp.tile(v, (1, reps)) * scale
        return jnp.pad(r, ((0, 0), (0, 4 * HEAD_DIM - r.shape[1])))

    gq = qk_gain
    gain_rows = jnp.stack([
        lanes(gq[:, QK_NSA_Q], 4, qs),
        lanes(jnp.concatenate([gq[:, QK_NSA_KS], gq[:, QK_NSA_KW]], axis=1), 1),
        lanes(gq[:, QK_SWA_Q], 4, qs),
        lanes(gq[:, QK_SWA_K], 2),
        lanes(gq[:, QK_MOBA_Q], 4, qs),
        lanes(gq[:, QK_MOBA_K], 4),
        lanes(gq[:, QK_MLA_Q], 4, ms),
        lanes(rope_gain[:, 0], 4, ms),
        lanes(gq[:, QK_MLA_K], 4),
        lanes(rope_gain[:, 1], 4),
    ], axis=1)
    bd64 = jnp.asarray(np.kron(np.eye(4), np.ones((HEAD_DIM, HEAD_DIM))) / HEAD_DIM, dtype=BF16)
    bd32 = jnp.asarray(np.kron(np.eye(4), np.ones((MLA_ROPE, MLA_ROPE))) / MLA_ROPE, dtype=BF16)
    w_branch_b = w_branch.astype(BF16)
    w_out_b = w_out.astype(BF16)
    w_router = jnp.concatenate(
        [w_coarse, w_fine, jnp.zeros((L, D, 128 - N_GROUPS - N_EXPERTS), F32)], axis=2)
    b_router = jnp.concatenate(
        [b_coarse, b_fine, jnp.zeros((L, 128 - N_GROUPS - N_EXPERTS), F32)], axis=1)
    sinks_pad = jnp.concatenate([swa_sinks, jnp.zeros((L, 128 - SWA_HEADS), F32)], axis=1)

    for l in range(L):
        m6 = mod[l].reshape(B, 6, 1, D)
        sh1, sc1, g1, sh2, sc2, g2 = (m6[:, j] for j in range(6))
        (qa, kcr, vcr, ks, vs, kw, vw, gs, qb, kb, vb, qc, kc, vc, mc, qd, kd, vd) = _proj_call(
            x, sc1, sh1, norm_gain[l, 0:1], w_attn[l], gain_rows[l], bd64, bd32, lat_gain_q[l][None],
            lat_gain_kv[l][None], w_uq_b[l], w_ukv_b[l], cos, sin)
        kcmp, vcmp = _compress_call(kcr, vcr, cmp_pe[l], cmp_w1[l], cmp_w2[l], qk_gain[l, QK_NSA_KC][None])
        o_cmp, sel = _cmp_attn_call(qa, kcmp, vcmp, slopes[0])
        o_slc = _flash_call(qa, ks, vs, slopes=slopes[0], mask=sel, mask_block=SEL_BLOCK, mask_t=True,
                            tq=256, tk=256, hpc=2, skip_tiles=True, out_dtype=F32, name="nsa_slc")
        o_win = _flash_call(qa, kw, vw, slopes=slopes[0], window=NSA_WINDOW,
                            tq=256, tk=256, hpc=2, out_dtype=F32, name="nsa_win")
        o_b = _flash_call(qb, kb, vb, slopes=slopes[1], window=SWA_WINDOW, sinks=sinks_pad[l][None],
                          tq=256, tk=256, name="swa")
        o_c = _flash_call(qc, kc, vc, slopes=slopes[2], mask=mc, mask_block=MOBA_BLOCK,
                          tq=512, tk=512, name="moba")
        o_d = _flash_call(qd, kd, vd, tq=512, tk=512, name="mla")
        xn, h2, route = _merge_call(
            x, (sc1, sh1, g1, sc2, sh2), norm_gain[l], w_gate[l], w_branch_b[l], w_out_b[l],
            o_cmp, o_slc, o_win, gs, o_b, o_c, o_d, w_router[l], b_router[l][None])
        route = route.reshape(B * S, 128)
        y2 = _moe(h2, route, w13, w2, l)
        x = _combine_call(xn, g2, route, y2)
    return x
```

```python
import functools
import math

import numpy as np
import jax
import jax.numpy as jnp
from jax import lax
from jax.experimental import pallas as pl
from jax.experimental.pallas import tpu as pltpu

F32 = jnp.float32
BF16 = jnp.bfloat16

HEAD_DIM = 64
NEG_INF = -1e30
EPS = 1e-6
NSA_HEADS = 4
CMP_BLOCK = 32
CMP_STRIDE = 16
CMP_HIDDEN = 256
SEL_BLOCK = 64
SEL_TOPK = 8
NSA_WINDOW = 512
FORCE_BONUS = 1e4
SWA_HEADS = 4
SWA_KV_HEADS = 2
SWA_WINDOW = 128
MOBA_HEADS = 4
MOBA_BLOCK = 256
MOBA_TOPK = 3
MLA_HEADS = 4
MLA_Q_RANK = 384
MLA_KV_RANK = 128
MLA_NOPE = 64
MLA_ROPE = 32
MLA_V = 64
ROPE_THETA = 10000.0
N_BRANCH = 4
BRANCH_WIDTH = 256
N_GROUPS = 4
EXPERTS_PER_GROUP = 8
N_EXPERTS = N_GROUPS * EXPERTS_PER_GROUP
D_EXPERT = 256

(GR_NSA_Q, GR_NSA_K, GR_SWA_Q, GR_SWA_K, GR_MOBA_Q, GR_MOBA_K,
 GR_MLA_QN, GR_MLA_QR, GR_MLA_KN, GR_MLA_KR) = range(10)
QK_NSA_Q, QK_NSA_KC, QK_NSA_KS, QK_NSA_KW = 0, 1, 2, 3
QK_SWA_Q, QK_SWA_K, QK_MOBA_Q, QK_MOBA_K, QK_MLA_Q, QK_MLA_K = 4, 5, 6, 7, 8, 9

ATTN_OLD = 2476
ATTN_COLS = 2560
GATE_LANE0 = 32
LOG2E = math.log2(math.e)
SUB, LANE = 8, 128
EXPERT_ROWS = 256
VMEM_LIMIT = 56 * 1024 * 1024


def _alibi_slopes():
    n = NSA_HEADS + SWA_HEADS + MOBA_HEADS

    def pow2(m):
        start = 2.0 ** (-8.0 / m)
        return [start ** (i + 1) for i in range(m)]

    c = 2 ** int(math.floor(math.log2(n)))
    s = pow2(c) + (pow2(2 * c)[0::2][: n - c] if c < n else [])
    s = -np.sort(-np.asarray(s, np.float32))
    return s.reshape(NSA_HEADS, 3).T


def _dot(a, b):
    return jnp.dot(a, b, preferred_element_type=F32)


def _dot_nt(a, b):
    return lax.dot_general(a, b, (((1,), (1,)), ((), ())), preferred_element_type=F32)


def _split(a):
    hi = a.astype(BF16)
    lo = (a - hi.astype(F32)).astype(BF16)
    return hi, lo


def _dot3(a, b):
    ah, al = _split(a)
    bh, bl = _split(b)
    return _dot(ah, bh) + (_dot(ah, bl) + _dot(al, bh))


def _dot3_nt(a, b):
    ah, al = _split(a)
    bh, bl = _split(b)
    return _dot_nt(ah, bh) + (_dot_nt(ah, bl) + _dot_nt(al, bh))


def _rms(x, g):
    return x * lax.rsqrt(jnp.mean(x * x, axis=-1, keepdims=True) + EPS) * g


def _sigmoid(x):
    return 1.0 / (1.0 + jnp.exp(-x))


AUG_MASK0 = 6


def _bf16_pieces(x, n=3):
    out, r = [], float(x)
    for _ in range(n):
        piece = float(np.asarray(r, np.float32).astype(jnp.bfloat16).astype(np.float32))
        out.append(piece)
        r -= piece
    return out


def _topk_mask(score, k, axis=1):
    n = score.shape[axis]
    iota = lax.broadcasted_iota(jnp.int32, score.shape, axis).astype(F32)
    sel = jnp.zeros(score.shape, F32)
    for _ in range(k):
        m = jnp.max(score, axis=axis, keepdims=True)
        idx = jnp.min(jnp.where(score == m, iota, float(n)), axis=axis, keepdims=True)
        hit = iota == idx
        sel = jnp.where(hit, jnp.where(m > 0.5 * NEG_INF, 1.0, 0.0), sel)
        score = jnp.where(hit, -3e38, score)
    return sel


def _ada_kernel(c_ref, w_ref, b_ref, o_ref):
    c = c_ref[...]
    a = c * _sigmoid(c)
    o_ref[0] = _dot(a, w_ref[0]) + b_ref[0]


def _ada_mod(c, w_ada, b_ada):
    L, D, D6 = w_ada.shape
    B = c.shape[0]
    tn = 1024
    return pl.pallas_call(
        _ada_kernel,
        grid=(L, D6 // tn),
        in_specs=[
            pl.BlockSpec((B, D), lambda l, j: (0, 0)),
            pl.BlockSpec((1, D, tn), lambda l, j: (l, 0, j)),
            pl.BlockSpec((1, 1, tn), lambda l, j: (l, 0, j)),
        ],
        out_specs=pl.BlockSpec((1, B, tn), lambda l, j: (l, 0, j)),
        out_shape=jax.ShapeDtypeStruct((L, B, D6), F32),
        compiler_params=pltpu.CompilerParams(vmem_limit_bytes=VMEM_LIMIT),
        name="ada_mod",
    )(c, w_ada, b_ada.reshape(L, 1, D6))


def _rms_blocks(x, bd, g):
    hi, lo = _split(x * x)
    ms = _dot(hi, bd) + _dot(lo, bd)
    return x * lax.rsqrt(ms + EPS) * g


def _proj_kernel(x_ref, sc_ref, sh_ref, ng_ref, w_ref, gr_ref, bd64_ref, bd32_ref, lgq_ref, lgkv_ref,
                 wuq_ref, wukv_ref, cos_ref, sin_ref,
                 qa_ref, kcr_ref, vcr_ref, ks_ref, vs_ref, kw_ref, vw_ref, gs_ref,
                 qb_ref, kb_ref, vb_ref, qc_ref, kc_ref, vc_ref, mc_ref, qd_ref, kd_ref, vd_ref,
                 kmean_s):
    i = pl.program_id(1)
    hd = HEAD_DIM
    x = x_ref[0]
    h = _rms(x, ng_ref[...]) * (1.0 + sc_ref[0]) + sh_ref[0]
    hb = h.astype(BF16)
    bd64 = bd64_ref[...]
    bd64h = bd64_ref[0:2 * hd, 0:2 * hd]
    bd32 = bd32_ref[...]

    def grow(r, w=4 * hd):
        return gr_ref[r:r + 1, 0:w]

    def proj(a, b):
        return _dot(hb, w_ref[:, a:b])

    def store_heads(ref, slab, n):
        for hh in range(n):
            ref[0, hh] = slab[:, hh * hd:(hh + 1) * hd].astype(BF16)

    def store_heads_t(ref, slab_t, n):
        for hh in range(n):
            ref[0, hh] = slab_t[hh * hd:(hh + 1) * hd].astype(BF16)

    store_heads(qa_ref, _rms_blocks(proj(0, 256), bd64, grow(GR_NSA_Q)), NSA_HEADS)
    p = proj(256, 640)
    kcr_ref[0] = p[:, 0:64]
    vcr_ref[0] = p[:, 64:128]
    kk = _rms_blocks(p[:, 128:256], bd64h, grow(GR_NSA_K, 2 * hd))
    ks_ref[0, 0] = kk[:, :hd].astype(BF16)
    kw_ref[0, 0] = kk[:, hd:].astype(BF16)
    vt = p[:, 256:384].T
    vs_ref[0, 0] = vt[:hd].astype(BF16)
    vw_ref[0, 0] = vt[hd:].astype(BF16)

    p = proj(640, 1152)
    store_heads(qb_ref, _rms_blocks(p[:, 0:256], bd64, grow(GR_SWA_Q)), SWA_HEADS)
    store_heads(kb_ref, _rms_blocks(p[:, 256:384], bd64h, grow(GR_SWA_K, 2 * hd)), SWA_KV_HEADS)
    store_heads_t(vb_ref, p[:, 384:512].T, SWA_KV_HEADS)

    @pl.when(i == 0)
    def _():
        kmean_s[...] = jnp.zeros(kmean_s.shape, F32)

    p = proj(1152, 1920)
    qn = _rms_blocks(p[:, 0:256], bd64, grow(GR_MOBA_Q))
    kn = _rms_blocks(p[:, 256:512], bd64, grow(GR_MOBA_K))
    store_heads(qc_ref, qn, MOBA_HEADS)
    store_heads(kc_ref, kn, MOBA_HEADS)
    store_heads_t(vc_ref, p[:, 512:768].T, MOBA_HEADS)
    nblk = kmean_s.shape[0]
    tm = x.shape[0]
    bpt = tm // MOBA_BLOCK
    for u in range(bpt):
        kmean_s[pl.ds(i * bpt + u, 1), :] = jnp.mean(kn[u * MOBA_BLOCK:(u + 1) * MOBA_BLOCK], axis=0, keepdims=True)
    own = i * bpt + lax.broadcasted_iota(jnp.int32, (tm, 1), 0) // MOBA_BLOCK
    blk_iota = lax.broadcasted_iota(jnp.int32, (1, nblk), 1)
    head_of_lane = lax.broadcasted_iota(jnp.int32, (1, 4 * hd), 1) // hd
    kmeans = kmean_s[...]
    for hh in range(MOBA_HEADS):
        g = _dot3_nt(jnp.where(head_of_lane == hh, qn, 0.0), kmeans)
        g = jnp.where(blk_iota < own, g, NEG_INF)
        mc_ref[0, hh] = jnp.where(blk_iota == own, 1.0, _topk_mask(g, MOBA_TOPK))

    p = proj(1920, 2560)
    gs_ref[0] = _sigmoid(p[:, 512:640])
    cos = cos_ref[...]
    sin = sin_ref[...]
    half = MLA_ROPE // 2
    first_half = lax.broadcasted_iota(jnp.int32, (1, 4 * MLA_ROPE), 1) % MLA_ROPE < half

    def rope(v):
        swapped = jnp.where(first_half, pltpu.roll(v, 4 * MLA_ROPE - half, 1), pltpu.roll(v, half, 1))
        return v * cos + swapped * sin

    qlat = _dot(_rms(p[:, 0:384], lgq_ref[...]).astype(BF16), wuq_ref[...])
    kvlat = _dot(_rms(p[:, 384:512], lgkv_ref[...]).astype(BF16), wukv_ref[...])
    q_nope = _rms_blocks(qlat[:, 0:256], bd64, grow(GR_MLA_QN))
    q_rot = rope(_rms_blocks(qlat[:, 256:384], bd32, grow(GR_MLA_QR, 2 * hd)))
    k_nope = _rms_blocks(kvlat[:, 0:256], bd64, grow(GR_MLA_KN))
    k_rot = rope(_rms_blocks(p[:, 512:640], bd32, grow(GR_MLA_KR, 2 * hd)))[:, :MLA_ROPE]
    store_heads_t(vd_ref, kvlat[:, 256:512].T, MLA_HEADS)
    for hh in range(MLA_HEADS):
        qd_ref[0, hh] = jnp.concatenate(
            [q_nope[:, hh * hd:(hh + 1) * hd], q_rot[:, hh * MLA_ROPE:(hh + 1) * MLA_ROPE]], axis=1).astype(BF16)
        kd_ref[0, hh] = jnp.concatenate([k_nope[:, hh * hd:(hh + 1) * hd], k_rot], axis=1).astype(BF16)


def _proj_call(x, sc1, sh1, ng, w_attn, gain_rows, bd64, bd32, lgq, lgkv, wuq, wukv, cos, sin):
    B, S, D = x.shape
    tm = 4 * MOBA_BLOCK
    nblk = S // MOBA_BLOCK
    hd = HEAD_DIM

    def full(shape):
        return pl.BlockSpec(shape, lambda b, i: (0,) * len(shape))

    def heads(nh, d):
        return pl.BlockSpec((1, nh, tm, d), lambda b, i: (b, 0, i, 0))

    in_specs = [
        pl.BlockSpec((1, tm, D), lambda b, i: (b, i, 0)),
        pl.BlockSpec((1, 1, D), lambda b, i: (b, 0, 0)),
        pl.BlockSpec((1, 1, D), lambda b, i: (b, 0, 0)),
        full((1, D)),
        full((D, ATTN_COLS)),
        full(gain_rows.shape),
        full(bd64.shape),
        full(bd32.shape),
        full(lgq.shape),
        full(lgkv.shape),
        full(wuq.shape),
        full(wukv.shape),
        pl.BlockSpec((tm, 4 * MLA_ROPE), lambda b, i: (i, 0)),
        pl.BlockSpec((tm, 4 * MLA_ROPE), lambda b, i: (i, 0)),
    ]
    row64 = pl.BlockSpec((1, tm, hd), lambda b, i: (b, i, 0))
    def heads_t(nh, d):
        return pl.BlockSpec((1, nh, d, tm), lambda b, i: (b, 0, 0, i))

    out_specs = [
        heads(4, hd), row64, row64, heads(1, hd), heads_t(1, hd), heads(1, hd), heads_t(1, hd),
        pl.BlockSpec((1, tm, 128), lambda b, i: (b, i, 0)),
        heads(4, hd), heads(2, hd), heads_t(2, hd),
        heads(4, hd), heads(4, hd), heads_t(4, hd), heads(4, nblk),
        heads(4, MLA_NOPE + MLA_ROPE), heads(4, MLA_NOPE + MLA_ROPE), heads_t(4, MLA_V),
    ]

    def sd(shape, dt):
        return jax.ShapeDtypeStruct(shape, dt)

    out_shape = [
        sd((B, 4, S, hd), BF16), sd((B, S, hd), F32), sd((B, S, hd), F32),
        sd((B, 1, S, hd), BF16), sd((B, 1, hd, S), BF16), sd((B, 1, S, hd), BF16), sd((B, 1, hd, S), BF16),
        sd((B, S, 128), F32),
        sd((B, 4, S, hd), BF16), sd((B, 2, S, hd), BF16), sd((B, 2, hd, S), BF16),
        sd((B, 4, S, hd), BF16), sd((B, 4, S, hd), BF16), sd((B, 4, hd, S), BF16), sd((B, 4, S, nblk), F32),
        sd((B, 4, S, 96), BF16), sd((B, 4, S, 96), BF16), sd((B, 4, MLA_V, S), BF16),
    ]
    return pl.pallas_call(
        _proj_kernel,
        grid=(B, S // tm),
        in_specs=in_specs,
        out_specs=out_specs,
        out_shape=out_shape,
        scratch_shapes=[pltpu.VMEM((nblk, MOBA_HEADS * hd), F32)],
        compiler_params=pltpu.CompilerParams(
            dimension_semantics=("arbitrary", "arbitrary"), vmem_limit_bytes=VMEM_LIMIT),
        name="proj_prep",
    )(x, sc1, sh1, ng, w_attn, gain_rows, bd64, bd32, lgq, lgkv, wuq, wukv, cos, sin)


def _compress_kernel(gk_ref, gv_ref, pe_ref, w1_ref, w2_ref, gkc_ref, kc_ref, vc_ref):
    half = w1_ref.shape[1] // 2
    outs = []
    for j, g_ref in enumerate((gk_ref, gv_ref)):
        g = g_ref[0].astype(BF16)
        top = _dot(g, w1_ref[j, :half].astype(BF16))
        bot = _dot(g, w1_ref[j, half:].astype(BF16))
        bot = jnp.concatenate([bot[1:], bot[:1]], axis=0)
        pe = jnp.broadcast_to(pe_ref[j], (8, pe_ref.shape[2]))
        bias = _dot3(pe, w1_ref[j])[0:1]
        hid = top + bot + bias
        hid = hid * _sigmoid(hid)
        outs.append(_dot(hid.astype(BF16), w2_ref[j].astype(BF16)))
    kc_ref[0] = _rms(outs[0], gkc_ref[...]).astype(BF16)
    dk = outs[1].shape[1]
    vc_ref[0] = jnp.concatenate([outs[1], outs[1]], axis=1).T[:dk].astype(BF16)


def _compress_call(kc_raw, vc_raw, pe, w1, w2, g_kc):
    B, S, dk = kc_raw.shape
    n_grp = S // CMP_STRIDE
    gk = kc_raw.reshape(B, n_grp, CMP_STRIDE * dk)
    gv = vc_raw.reshape(B, n_grp, CMP_STRIDE * dk)
    pe_flat = pe.reshape(2, 1, CMP_BLOCK * dk)
    grp_spec = pl.BlockSpec((1, n_grp, CMP_STRIDE * dk), lambda b: (b, 0, 0))
    out_spec = pl.BlockSpec((1, n_grp, dk), lambda b: (b, 0, 0))
    return pl.pallas_call(
        _compress_kernel,
        grid=(B,),
        in_specs=[
            grp_spec, grp_spec,
            pl.BlockSpec(pe_flat.shape, lambda b: (0, 0, 0)),
            pl.BlockSpec(w1.shape, lambda b: (0, 0, 0)),
            pl.BlockSpec(w2.shape, lambda b: (0, 0, 0)),
            pl.BlockSpec((1, dk), lambda b: (0, 0)),
        ],
        out_specs=[out_spec, pl.BlockSpec((1, dk, n_grp), lambda b: (b, 0, 0))],
        out_shape=[jax.ShapeDtypeStruct((B, n_grp, dk), BF16), jax.ShapeDtypeStruct((B, dk, n_grp), BF16)],
        compiler_params=pltpu.CompilerParams(vmem_limit_bytes=VMEM_LIMIT),
        name="nsa_compress",
    )(gk, gv, pe_flat, w1, w2, g_kc)


def _cmp_attn_kernel(q_ref, kc_ref, vct_ref, cover_ref, o_ref, sel_ref, *, slopes, tq, n_cmp):
    i = pl.program_id(1)
    kc = kc_ref[0]
    vct = vct_ref[0]
    ncp = kc.shape[0]
    t_full = i * tq + lax.broadcasted_iota(jnp.int32, (ncp, tq), 1)
    n_iota = lax.broadcasted_iota(jnp.int32, (ncp, tq), 0)
    dist_i = t_full - (n_iota * CMP_STRIDE + (CMP_BLOCK - 1))
    vis = (n_iota < n_cmp) & (dist_i >= 0)
    dist = dist_i.astype(F32)
    visf = vis.astype(F32)
    psum = jnp.zeros((ncp, tq), F32)
    raw = [_dot_nt(kc, q_ref[0, hh]) for hh in range(NSA_HEADS)]
    probs = []
    for hh in range(NSA_HEADS):
        s = raw[hh] - (slopes[hh] * LOG2E) * dist
        s = jnp.where(vis, s, NEG_INF)
        e = jnp.exp2(s - jnp.max(s, axis=0, keepdims=True)) * visf
        p = e / jnp.maximum(jnp.sum(e, axis=0, keepdims=True), 1e-30)
        probs.append(p.astype(BF16))
        psum = psum + p
    outs = [_dot(vct, pb) for pb in probs]
    o_ref[0] = jnp.concatenate(outs, axis=0).T
    ph, plo = _split(psum)
    cover = cover_ref[...]
    p_slc = _dot(cover, ph) + _dot(cover, plo)
    n_sel = cover.shape[0]
    cur = (i * tq + lax.broadcasted_iota(jnp.int32, (1, tq), 1)) // SEL_BLOCK
    j = lax.broadcasted_iota(jnp.int32, (n_sel, 1), 0)
    forced = jnp.where(j == 0, 1.0, jnp.where(j == cur, 1.0, jnp.where(j == cur - 1, 1.0, 0.0)))
    score = jnp.where(j <= cur, p_slc + FORCE_BONUS * forced, NEG_INF)
    sel_ref[0, 0] = _topk_mask(score, min(SEL_TOPK, n_sel), axis=0)


def _cmp_attn_call(qa, kc, vc, slopes):
    B, H, S, dk = qa.shape
    ncp = kc.shape[1]
    n_cmp = (S - CMP_BLOCK) // CMP_STRIDE + 1
    n_sel = S // SEL_BLOCK
    tq = 256
    starts = np.arange(ncp) * CMP_STRIDE
    jb = np.arange(n_sel) * SEL_BLOCK
    cover = ((starts[:, None] < jb[None, :] + SEL_BLOCK) & (starts[:, None] + CMP_BLOCK > jb[None, :])
             & (np.arange(ncp)[:, None] < n_cmp))
    cover = jnp.asarray(cover.T.astype(np.float32), dtype=BF16)
    return pl.pallas_call(
        functools.partial(_cmp_attn_kernel, slopes=tuple(float(s) for s in slopes), tq=tq, n_cmp=n_cmp),
        grid=(B, S // tq),
        in_specs=[
            pl.BlockSpec((1, H, tq, dk), lambda b, i: (b, 0, i, 0)),
            pl.BlockSpec((1, ncp, dk), lambda b, i: (b, 0, 0)),
            pl.BlockSpec((1, dk, ncp), lambda b, i: (b, 0, 0)),
            pl.BlockSpec((n_sel, ncp), lambda b, i: (0, 0)),
        ],
        out_specs=[
            pl.BlockSpec((1, tq, H * dk), lambda b, i: (b, i, 0)),
            pl.BlockSpec((1, 1, n_sel, tq), lambda b, i: (b, 0, 0, i)),
        ],
        out_shape=[jax.ShapeDtypeStruct((B, S, H * dk), F32),
                   jax.ShapeDtypeStruct((B, 1, n_sel, S), F32)],
        compiler_params=pltpu.CompilerParams(vmem_limit_bytes=VMEM_LIMIT),
        name="nsa_cmp_attn",
    )(qa, kc, vc, cover)


def _flash_kernel(*refs, H, G, hpc, tq, tk, slopes, window, mask_block, mask_per_head, mask_t, has_sink, skip):
    refs = list(refs)
    q_ref, k_ref, v_ref = refs[:3]
    pos = 3
    mask_ref = sink_ref = None
    if mask_block:
        mask_ref = refs[pos]
        pos += 1
    if has_sink:
        sink_ref = refs[pos]
        pos += 1
    o_ref, m_s, acc_s = refs[pos:pos + 3]
    need_s = refs[pos + 3] if skip else None
    use_aug = slopes is not None or bool(mask_block)
    R = H // G
    C = H // hpc
    M = hpc * tq
    dv = v_ref.shape[-2]
    i = pl.program_id(1)
    q0 = i * tq
    hi = q0 // tk
    lo = jnp.maximum(q0 - (window - 1), 0) // tk if window else 0

    def rel_pos():
        r_row = jnp.concatenate([lax.broadcasted_iota(jnp.int32, (1, tq), 1)] * hpc, axis=1)
        return (r_row - lax.broadcasted_iota(jnp.int32, (tk, M), 0)).astype(F32)

    dq = q_ref.shape[-1]
    n_mask = mask_ref.shape[-2 if mask_t else -1] if mask_block else 0
    aug_w = LANE if dq + AUG_MASK0 + n_mask <= LANE else 2 * LANE
    aug0 = dq
    lane_a = lax.broadcasted_iota(jnp.int32, (1, aug_w), 1)
    feature_lane = lane_a < dq

    def slope_lanes(hh):
        row = jnp.zeros((1, aug_w), F32)
        if slopes is not None:
            for n, piece in enumerate(_bf16_pieces(slopes[hh])):
                row = jnp.where((lane_a == aug0 + n) | (lane_a == aug0 + n + 3), piece, row)
        return row

    def widen(a):
        return jnp.concatenate([a, jnp.zeros((a.shape[0], aug_w - dq), a.dtype)], axis=1)

    qs, slope_rows, bms = [], [], []
    for c in range(C):
        heads = [c * hpc + r for r in range(hpc)]
        if hpc == 1:
            qs.append(q_ref[0, heads[0]])
        else:
            qs.append(jnp.concatenate([q_ref[0, hh] for hh in heads], axis=0))
        if slopes is not None:
            slope_rows.append(jnp.concatenate([jnp.full((1, tq), slopes[hh], F32) for hh in heads], axis=1))
        if mask_block:
            parts = [mask_ref[0, hh if mask_per_head else 0] for hh in heads]
            bms.append(jnp.concatenate(parts, axis=1) if mask_t else jnp.concatenate(parts, axis=0).T)
    m_s[...] = jnp.full(m_s.shape, NEG_INF, F32)
    acc_s[...] = jnp.zeros(acc_s.shape, F32)
    ones_rows = jnp.ones((acc_s.shape[1] - dv, tk), BF16)

    if mask_block:
        nblk = bms[0].shape[0]
    if need_s is not None:
        col = bms[0]
        for bm in bms[1:]:
            col = jnp.maximum(col, bm)
        col = jnp.max(col, axis=1, keepdims=True)
        bpt = tk // mask_block
        for jt in range(nblk // bpt):
            need_s[jt] = jnp.max(col[jt * bpt:(jt + 1) * bpt]).astype(jnp.int32)

    if use_aug:
        for c in range(C):
            qa = jnp.concatenate(
                [jnp.broadcast_to(slope_lanes(c * hpc + r), (tq, aug_w)) for r in range(hpc)], axis=0)
            if mask_block:
                place = jnp.where(
                    lax.broadcasted_iota(jnp.int32, (nblk, aug_w), 0) + (aug0 + AUG_MASK0) == lane_a,
                    1.0, 0.0).astype(BF16)
                off_sel = ((bms[c] - 1.0) * -NEG_INF).astype(BF16)
                qa = qa + lax.dot_general(off_sel, place, (((0,), (0,)), ((), ())), preferred_element_type=F32)
            qs[c] = jnp.where(feature_lane, widen(qs[c]), qa.astype(BF16))

    def tile_scores(j):
        k0 = pl.multiple_of(j * tk, tk)
        if use_aug:
            t_key = lax.broadcasted_iota(jnp.int32, (tk, aug_w), 0) + k0
            k_aug = jnp.zeros((tk, aug_w), F32)
            if mask_block:
                key_blk = lax.shift_right_logical(t_key, int(math.log2(mask_block)))
                k_aug = jnp.where(key_blk + (aug0 + AUG_MASK0) == lane_a, 1.0, 0.0)
            if slopes is not None:
                t_hi = (lax.shift_right_logical(t_key, 8) * 256).astype(F32)
                t_lo = (t_key & 255).astype(F32)
                k_aug = jnp.where(lane_a < aug0 + 3, t_hi, jnp.where(lane_a < aug0 + 6, t_lo, k_aug))
            k_aug = k_aug.astype(BF16)
        k_cats = {}
        scores = []
        for c in range(C):
            g = (c * hpc) // R
            if g not in k_cats:
                k = k_ref[0, g, pl.ds(k0, tk), :]
                k_cats[g] = jnp.where(feature_lane, widen(k), k_aug) if use_aug else k
            scores.append(_dot_nt(k_cats[g], qs[c]))
        return scores

    def tile_update(j, edge, scores):
        k0 = pl.multiple_of(j * tk, tk)
        valid = None
        if edge:
            dist = rel_pos() + (q0 - k0).astype(F32)
            valid = dist >= 0.0
            if window:
                valid = valid & (dist < float(window))
        probs, alphas = [], []
        for g in range(C):
            s = scores[g]
            if valid is not None:
                s = jnp.where(valid, s, NEG_INF)
            m_prev = m_s[g]
            m_new = jnp.maximum(m_prev, jnp.max(s, axis=0, keepdims=True))
            alpha = jnp.exp2(m_prev - m_new)
            p = jnp.exp2(s - m_new)
            m_s[g] = m_new
            probs.append(p.astype(BF16))
            alphas.append(alpha)
        for c in range(C):
            v = jnp.concatenate([v_ref[0, (c * hpc) // R, :, pl.ds(k0, tk)], ones_rows], axis=0)
            acc_s[c] = alphas[c] * acc_s[c] + _dot(v, probs[c])

    def tile(j, edge):
        tile_update(j, edge, tile_scores(j))

    def body(step, carry):
        j = hi - step
        is_edge = step == 0
        if window:
            is_edge = is_edge | (q0 - j * tk + (tq - 1) >= window)
        run = (step == 0) | (need_s[j] > 0) if need_s is not None else None

        def when(c):
            return pl.when(c if run is None else c & run)

        @when(is_edge)
        def _():
            tile(j, True)

        @when(jnp.logical_not(is_edge))
        def _():
            tile(j, False)

        return carry

    lax.fori_loop(0, hi - lo + 1, body, 0)

    for c in range(C):
        m = m_s[c]
        l = acc_s[c, dv:dv + 1]
        acc = acc_s[c, 0:dv]
        if has_sink:
            sk = LOG2E * jnp.concatenate(
                [jnp.broadcast_to(sink_ref[:, c * hpc + r:c * hpc + r + 1], (1, tq)) for r in range(hpc)], axis=1)
            if slopes is not None:
                t_q = jnp.concatenate([lax.broadcasted_iota(jnp.int32, (1, tq), 1) + q0] * hpc, axis=1)
                m = m - slope_rows[c] * t_q.astype(F32)
            m_f = jnp.maximum(m, sk)
            a = jnp.exp2(m - m_f)
            l = l * a + jnp.exp2(sk - m_f)
            acc = acc * a
        out = acc / l
        for r in range(hpc):
            hh = c * hpc + r
            o_ref[0, :, hh * dv:(hh + 1) * dv] = out[:, r * tq:(r + 1) * tq].T.astype(o_ref.dtype)


def _flash_call(q, k, v, *, slopes=None, window=0, mask=None, mask_block=0, mask_t=False, sinks=None,
                tq=128, tk=256, hpc=None, skip_tiles=False, out_dtype=BF16, name="flash"):
    B, H, S, dq = q.shape
    G = k.shape[1]
    dv = v.shape[-2]
    tk = min(tk, S)
    assert tk % tq == 0 and S % tk == 0
    R = H // G
    hpc = R if hpc is None else hpc
    assert R % hpc == 0
    C = H // hpc
    in_specs = [
        pl.BlockSpec((1, H, tq, dq), lambda b, i: (b, 0, i, 0)),
        pl.BlockSpec((1, G, S, dq), lambda b, i: (b, 0, 0, 0)),
        pl.BlockSpec((1, G, dv, S), lambda b, i: (b, 0, 0, 0)),
    ]
    args = [q, k, v]
    mask_per_head = False
    if mask is not None:
        hm = mask.shape[1]
        mask_per_head = hm > 1
        if mask_t:
            in_specs.append(pl.BlockSpec((1, hm, mask.shape[2], tq), lambda b, i: (b, 0, 0, i)))
        else:
            in_specs.append(pl.BlockSpec((1, hm, tq, mask.shape[3]), lambda b, i: (b, 0, i, 0)))
        args.append(mask)
    if sinks is not None:
        in_specs.append(pl.BlockSpec(sinks.shape, lambda b, i: (0, 0)))
        args.append(sinks)
    kern = functools.partial(
        _flash_kernel, H=H, G=G, hpc=hpc, tq=tq, tk=tk,
        slopes=None if slopes is None else tuple(float(s) * LOG2E for s in slopes),
        window=window, mask_block=mask_block if mask is not None else 0,
        mask_per_head=mask_per_head, mask_t=mask_t, has_sink=sinks is not None,
        skip=skip_tiles)
    ones_rows = 16
    scratch = [pltpu.VMEM((C, 1, hpc * tq), F32), pltpu.VMEM((C, dv + ones_rows, hpc * tq), F32)]
    if mask is not None:
        assert tk % mask_block == 0
    if skip_tiles:
        assert mask is not None
        scratch.append(pltpu.SMEM((S // tk,), jnp.int32))
    return pl.pallas_call(
        kern,
        grid=(B, S // tq),
        in_specs=in_specs,
        out_specs=pl.BlockSpec((1, tq, H * dv), lambda b, i: (b, i, 0)),
        out_shape=jax.ShapeDtypeStruct((B, S, H * dv), out_dtype),
        scratch_shapes=scratch,
        compiler_params=pltpu.CompilerParams(vmem_limit_bytes=VMEM_LIMIT),
        name=name,
    )(*args)


def _merge_kernel(x_ref, sc1_ref, sh1_ref, g1_ref, sc2_ref, sh2_ref, ng_ref, wg_ref, wb_ref, wo_ref,
                  ocmp_ref, oslc_ref, owin_ref, gs_ref, ob_ref, oc_ref, od_ref, wr_ref, br_ref,
                  xo_ref, h2_ref, rt_ref):
    hd = HEAD_DIM
    x = x_ref[0]
    ng = ng_ref[...]
    h = _rms(x, ng[0:1]) * (1.0 + sc1_ref[0]) + sh1_ref[0]
    hb = h.astype(BF16)
    gs = gs_ref[0]
    ocmp = ocmp_ref[0]
    oslc = oslc_ref[0]
    owin = owin_ref[0]
    parts = []
    for hh in range(NSA_HEADS):
        c0 = GATE_LANE0 + 3 * hh
        sl = slice(hh * hd, (hh + 1) * hd)
        parts.append(gs[:, c0:c0 + 1] * ocmp[:, sl] + gs[:, c0 + 1:c0 + 2] * oslc[:, sl]
                     + gs[:, c0 + 2:c0 + 3] * owin[:, sl])
    o_a = jnp.concatenate(parts, axis=1).astype(BF16)
    branches = (o_a, ob_ref[0], oc_ref[0], od_ref[0])
    D = x.shape[1]
    mixed = None
    for n in range(N_BRANCH):
        gate = _sigmoid(_dot(hb, wg_ref[:, n * D:(n + 1) * D]))
        term = gate * _dot(branches[n], wb_ref[n])
        mixed = term if mixed is None else mixed + term
    xn = x + g1_ref[0] * _dot(mixed.astype(BF16), wo_ref[...])
    xo_ref[0] = xn
    h2 = _rms(xn, ng[1:2]) * (1.0 + sc2_ref[0]) + sh2_ref[0]
    for c in range(SUB):
        h2_ref[pl.ds(c, h2.shape[0], stride=SUB), :] = h2[:, c * LANE:(c + 1) * LANE]

    logits = _dot3(h2, wr_ref[...]) + br_ref[...]
    lane = lax.broadcasted_iota(jnp.int32, logits.shape, 1)
    lanef = lane.astype(F32)
    is_c = lane < N_GROUPS
    lc = jnp.where(is_c, logits, NEG_INF)
    mc = jnp.max(lc, axis=-1, keepdims=True)
    grp = jnp.min(jnp.where(lc == mc, lanef, 1e9), axis=-1, keepdims=True)
    p_grp = 1.0 / jnp.sum(jnp.where(is_c, jnp.exp(lc - mc), 0.0), axis=-1, keepdims=True)
    e_lane = lanef - float(N_GROUPS)
    in_grp = (lane >= N_GROUPS) & (lane < N_GROUPS + N_EXPERTS) & (
        jnp.floor(e_lane / EXPERTS_PER_GROUP) == grp)
    lf = jnp.where(in_grp, logits, NEG_INF)
    m1 = jnp.max(lf, axis=-1, keepdims=True)
    i1 = jnp.min(jnp.where(lf == m1, e_lane, 1e9), axis=-1, keepdims=True)
    lf2 = jnp.where(e_lane == i1, NEG_INF, lf)
    m2 = jnp.max(lf2, axis=-1, keepdims=True)
    i2 = jnp.min(jnp.where(lf2 == m2, e_lane, 1e9), axis=-1, keepdims=True)
    e2 = jnp.exp(m2 - m1)
    w1 = p_grp / (1.0 + e2)
    w2 = p_grp * e2 / (1.0 + e2)
    rt = jnp.where(lane == 0, i1, jnp.where(lane == 1, i2, jnp.where(lane == 2, w1, jnp.where(lane == 3, w2, 0.0))))
    rt_ref[0] = rt


def _merge_call(x, mods, ng, wg, wb, wo, ocmp, oslc, owin, gs, ob, oc, od, wr, br):
    B, S, D = x.shape
    tm = 512
    sc1, sh1, g1, sc2, sh2 = mods

    def full(a):
        return pl.BlockSpec(a.shape, lambda b, i: (0,) * a.ndim)

    modspec = pl.BlockSpec((1, 1, D), lambda b, i: (b, 0, 0))
    row = lambda w: pl.BlockSpec((1, tm, w), lambda b, i: (b, i, 0))
    in_specs = [row(D), modspec, modspec, modspec, modspec, modspec, full(ng), full(wg), full(wb), full(wo),
                row(256), row(256), row(256), row(128), row(256), row(256), row(256), full(wr), full(br)]
    return pl.pallas_call(
        _merge_kernel,
        grid=(B, S // tm),
        in_specs=in_specs,
        out_specs=[row(D), pl.BlockSpec((tm * SUB, LANE), lambda b, i: (b * (S // tm) + i, 0)), row(128)],
        out_shape=[jax.ShapeDtypeStruct((B, S, D), F32), jax.ShapeDtypeStruct((B * S * SUB, LANE), F32),
                   jax.ShapeDtypeStruct((B, S, 128), F32)],
        compiler_params=pltpu.CompilerParams(vmem_limit_bytes=VMEM_LIMIT),
        name="merge_router",
    )(x, sc1, sh1, g1, sc2, sh2, ng, wg, wb, wo, ocmp, oslc, owin, gs, ob, oc, od, wr, br)


def _expert_kernel(blk_e_ref, nact_ref, tok_ref, dst_ref, h2_hbm, w13_ref, w2_ref, y2_hbm, xbuf, ybuf, gsem, ssem):
    del blk_e_ref
    i = pl.program_id(0)
    nb = nact_ref[0]
    slot = i % 2
    rows = EXPERT_ROWS

    def tile_rows(t):
        return pl.ds(pl.multiple_of(t * SUB, SUB), SUB)

    def gather_copy(blk, s, r):
        tok = tok_ref[blk * rows + r]
        return pltpu.make_async_copy(h2_hbm.at[tile_rows(tok), :], xbuf.at[s, tile_rows(r), :], gsem.at[s])

    def scatter_copy(blk, s, r):
        d = dst_ref[blk * rows + r]
        return pltpu.make_async_copy(ybuf.at[s, tile_rows(r), :], y2_hbm.at[tile_rows(d), :], ssem.at[s])

    def loop_rows(fn):
        def body(r, carry):
            fn(r)
            return carry
        lax.fori_loop(0, rows, body, 0, unroll=8)

    def wait_gather(s):
        pltpu.make_async_copy(h2_hbm.at[pl.ds(0, rows * SUB), :], xbuf.at[s], gsem.at[s]).wait()

    def wait_scatter(s):
        pltpu.make_async_copy(ybuf.at[s], y2_hbm.at[pl.ds(0, rows * SUB), :], ssem.at[s]).wait()

    def compute():
        ab = None
        for c in range(0, SUB, 2):
            xc = jnp.concatenate([xbuf[slot, pl.ds(c, rows, stride=SUB), :],
                                  xbuf[slot, pl.ds(c + 1, rows, stride=SUB), :]], axis=1).astype(BF16)
            part = _dot(xc, w13_ref[0, 0, c * LANE:(c + 2) * LANE, :].astype(BF16))
            ab = part if ab is None else ab + part
        de = ab.shape[1] // 2
        a = ab[:, :de]
        b = ab[:, de:]
        act = ((a * _sigmoid(a)) * b).astype(BF16)
        for c in range(0, SUB, 2):
            y = _dot(act, w2_ref[0, 0, :, c * LANE:(c + 2) * LANE].astype(BF16))
            ybuf[slot, pl.ds(c, rows, stride=SUB), :] = y[:, :LANE]
            ybuf[slot, pl.ds(c + 1, rows, stride=SUB), :] = y[:, LANE:]

    def step(scatter_prev):
        nxt = jnp.minimum(i + 1, nb - 1)
        for r in range(rows):
            if scatter_prev:
                scatter_copy(i - 1, 1 - slot, r).start(priority=r % 2)
            gather_copy(nxt, 1 - slot, r).start(priority=r % 2)
        compute()

    @pl.when(i == 0)
    def _():
        loop_rows(lambda r: gather_copy(0, 0, r).start())

    @pl.when(i < nb)
    def _():
        wait_gather(slot)

    @pl.when((i >= 2) & (i < nb))
    def _():
        wait_scatter(slot)

    @pl.when(i == 0)
    def _():
        step(False)

    @pl.when((i > 0) & (i < nb))
    def _():
        step(True)

    @pl.when(i == nb - 1)
    def _():
        loop_rows(lambda r: scatter_copy(i, slot, r).start())
        wait_gather(1 - slot)

        @pl.when(nb >= 2)
        def _():
            wait_scatter(1 - slot)
        wait_scatter(slot)

    @pl.when(i >= nb)
    def _():
        d0 = pl.multiple_of(dst_ref[i * rows] * SUB, SUB)
        fill = pltpu.make_async_copy(ybuf.at[slot], y2_hbm.at[pl.ds(d0, rows * SUB), :], ssem.at[slot])
        fill.start()
        fill.wait()


def _expert_call(h2, blk_e, n_active, row_tok, row_dst, w13, w2, layer):
    D = SUB * LANE
    assert h2.shape[1] == LANE and w13.shape[2] == D
    n_blocks = blk_e.shape[0]
    de2 = w13.shape[-1]
    grid_spec = pltpu.PrefetchScalarGridSpec(
        num_scalar_prefetch=4,
        grid=(n_blocks,),
        in_specs=[
            pl.BlockSpec(memory_space=pl.ANY),
            pl.BlockSpec((1, 1, D, de2), lambda i, be, na, rt, rd: (layer, be[i], 0, 0)),
            pl.BlockSpec((1, 1, de2 // 2, D), lambda i, be, na, rt, rd: (layer, be[i], 0, 0)),
        ],
        out_specs=pl.BlockSpec(memory_space=pl.ANY),
        scratch_shapes=[pltpu.VMEM((2, EXPERT_ROWS * SUB, LANE), F32), pltpu.VMEM((2, EXPERT_ROWS * SUB, LANE), F32),
                        pltpu.SemaphoreType.DMA((2,)), pltpu.SemaphoreType.DMA((2,))],
    )
    return pl.pallas_call(
        _expert_kernel,
        grid_spec=grid_spec,
        out_shape=jax.ShapeDtypeStruct((row_tok.shape[0] * SUB, LANE), F32),
        compiler_params=pltpu.CompilerParams(
            dimension_semantics=("arbitrary",), vmem_limit_bytes=VMEM_LIMIT),
        name="experts",
    )(blk_e, n_active, row_tok, row_dst, h2, w13, w2)


def _combine_kernel(xn_ref, g2_ref, rt_ref, y0_ref, y1_ref, o_ref):
    rt = rt_ref[...]
    tm = xn_ref.shape[0]
    w0 = rt[:, 2:3]
    w1 = rt[:, 3:4]
    g2 = g2_ref[0]
    for c in range(SUB):
        cols = slice(c * LANE, (c + 1) * LANE)
        y = w0 * y0_ref[pl.ds(c, tm, stride=SUB), :] + w1 * y1_ref[pl.ds(c, tm, stride=SUB), :]
        o_ref[:, cols] = xn_ref[:, cols] + g2[:, cols] * y


def _combine_call(xn, g2, route, y2):
    B, S, D = xn.shape
    tm = 512
    spb = S // tm
    nt = B * spb
    return pl.pallas_call(
        _combine_kernel,
        grid=(B, spb),
        in_specs=[
            pl.BlockSpec((tm, D), lambda b, i: (b * spb + i, 0)),
            pl.BlockSpec((1, 1, D), lambda b, i: (b, 0, 0)),
            pl.BlockSpec((tm, 128), lambda b, i: (b * spb + i, 0)),
            pl.BlockSpec((tm * SUB, LANE), lambda b, i: (b * spb + i, 0)),
            pl.BlockSpec((tm * SUB, LANE), lambda b, i: (nt + b * spb + i, 0)),
        ],
        out_specs=pl.BlockSpec((tm, D), lambda b, i: (b * spb + i, 0)),
        out_shape=jax.ShapeDtypeStruct((B * S, D), F32),
        compiler_params=pltpu.CompilerParams(vmem_limit_bytes=VMEM_LIMIT),
        name="moe_combine",
    )(xn.reshape(B * S, D), g2, route, y2, y2).reshape(B, S, D)


def _moe(h2, route, w13, w2, layer):
    N = h2.shape[0] // SUB
    K = 2
    E = N_EXPERTS
    rows = EXPERT_ROWS
    flat_e = route[:, 0:2].astype(jnp.int32).reshape(-1)
    order = jnp.argsort(flat_e).astype(jnp.int32)
    counts = jnp.sum((flat_e[:, None] == jnp.arange(E)[None, :]).astype(jnp.int32), axis=0)
    padded = (counts + rows - 1) // rows * rows
    pend = jnp.cumsum(padded)
    pstart = pend - padded
    start = jnp.cumsum(counts) - counts
    n_blocks = (N * K) // rows + E
    R = n_blocks * rows
    blk_e = jnp.minimum(
        jnp.sum((jnp.arange(n_blocks)[:, None] * rows >= pend[None, :]).astype(jnp.int32), axis=1), E - 1)
    pos = jnp.arange(R, dtype=jnp.int32).reshape(n_blocks, rows)
    local = pos - pstart[blk_e][:, None]
    is_real = ((pos < pend[E - 1]) & (local < counts[blk_e][:, None])).reshape(R)
    rank = jnp.clip(start[blk_e][:, None] + local, 0, N * K - 1).reshape(R)
    pair = order[rank]
    flat = pos.reshape(R)
    real_before = jnp.cumsum(is_real.astype(jnp.int32)) - is_real.astype(jnp.int32)
    row_tok = jnp.where(is_real, pair // K, flat % N).astype(jnp.int32)
    row_dst = jnp.where(is_real, (pair % K) * N + pair // K, N * K + flat - real_before).astype(jnp.int32)
    n_active = (pend[E - 1:E] // rows).astype(jnp.int32)
    return _expert_call(h2, blk_e.astype(jnp.int32), n_active, row_tok, row_dst, w13, w2, layer)


def kernel(x, c, w_ada, b_ada, norm_gain, w_in, qk_gain, cmp_pe, cmp_w1, cmp_w2, swa_sinks,
           lat_gain_q, lat_gain_kv, rope_gain, w_uq, w_ukv, w_branch, w_out,
           w_coarse, b_coarse, w_fine, b_fine, w13, w2):
    B, S, D = x.shape
    L = w_in.shape[0]
    assert S % MOBA_BLOCK == 0 and D == 1024
    slopes = _alibi_slopes()

    half = MLA_ROPE // 2
    inv = ROPE_THETA ** (-jnp.arange(half, dtype=F32) / half)
    ang = jnp.arange(S).astype(F32)[:, None] * inv[None, :]
    cos = jnp.tile(jnp.cos(ang), (1, 8))
    sin = jnp.tile(jnp.concatenate([-jnp.sin(ang), jnp.sin(ang)], axis=1), (1, 4))

    mod = _ada_mod(c, w_ada, b_ada)

    w_attn = jnp.concatenate(
        [w_in[:, :, :448], w_in[:, :, 512:576], w_in[:, :, 448:512], w_in[:, :, 576:640],
         w_in[:, :, 652:ATTN_OLD], w_in[:, :, 640:652],
         jnp.zeros((L, D, ATTN_COLS - ATTN_OLD), F32)], axis=2).astype(BF16)
    w_gate = w_in[:, :, ATTN_OLD:].astype(BF16)
    dq, dkv = MLA_NOPE + MLA_ROPE, MLA_NOPE + MLA_V
    uq = w_uq.reshape(L, MLA_Q_RANK, MLA_HEADS, dq)
    w_uq_b = jnp.concatenate([uq[..., :MLA_NOPE].reshape(L, MLA_Q_RANK, -1),
                              uq[..., MLA_NOPE:].reshape(L, MLA_Q_RANK, -1)], axis=2).astype(BF16)
    ukv = w_ukv.reshape(L, MLA_KV_RANK, MLA_HEADS, dkv)
    w_ukv_b = jnp.concatenate([ukv[..., :MLA_NOPE].reshape(L, MLA_KV_RANK, -1),
                               ukv[..., MLA_NOPE:].reshape(L, MLA_KV_RANK, -1)], axis=2).astype(BF16)

    qs = HEAD_DIM ** -0.5 * LOG2E
    ms = dq ** -0.5 * LOG2E

    def lanes(v, reps, scale=1.0):
        r = jnp.tile(v, (1, reps)) * scale
        return jnp.pad(r, ((0, 0), (0, 4 * HEAD_DIM - r.shape[1])))

    gq = qk_gain
    gain_rows = jnp.stack([
        lanes(gq[:, QK_NSA_Q], 4, qs),
        lanes(jnp.concatenate([gq[:, QK_NSA_KS], gq[:, QK_NSA_KW]], axis=1), 1),
        lanes(gq[:, QK_SWA_Q], 4, qs),
        lanes(gq[:, QK_SWA_K], 2),
        lanes(gq[:, QK_MOBA_Q], 4, qs),
        lanes(gq[:, QK_MOBA_K], 4),
        lanes(gq[:, QK_MLA_Q], 4, ms),
        lanes(rope_gain[:, 0], 4, ms),
        lanes(gq[:, QK_MLA_K], 4),
        lanes(rope_gain[:, 1], 4),
    ], axis=1)
    bd64 = jnp.asarray(np.kron(np.eye(4), np.ones((HEAD_DIM, HEAD_DIM))) / HEAD_DIM, dtype=BF16)
    bd32 = jnp.asarray(np.kron(np.eye(4), np.ones((MLA_ROPE, MLA_ROPE))) / MLA_ROPE, dtype=BF16)
    w_branch_b = w_branch.astype(BF16)
    w_out_b = w_out.astype(BF16)
    w_router = jnp.concatenate(
        [w_coarse, w_fine, jnp.zeros((L, D, 128 - N_GROUPS - N_EXPERTS), F32)], axis=2)
    b_router = jnp.concatenate(
        [b_coarse, b_fine, jnp.zeros((L, 128 - N_GROUPS - N_EXPERTS), F32)], axis=1)
    sinks_pad = jnp.concatenate([swa_sinks, jnp.zeros((L, 128 - SWA_HEADS), F32)], axis=1)

    for l in range(L):
        m6 = mod[l].reshape(B, 6, 1, D)
        sh1, sc1, g1, sh2, sc2, g2 = (m6[:, j] for j in range(6))
        (qa, kcr, vcr, ks, vs, kw, vw, gs, qb, kb, vb, qc, kc, vc, mc, qd, kd, vd) = _proj_call(
            x, sc1, sh1, norm_gain[l, 0:1], w_attn[l], gain_rows[l], bd64, bd32, lat_gain_q[l][None],
            lat_gain_kv[l][None], w_uq_b[l], w_ukv_b[l], cos, sin)
        kcmp, vcmp = _compress_call(kcr, vcr, cmp_pe[l], cmp_w1[l], cmp_w2[l], qk_gain[l, QK_NSA_KC][None])
        o_cmp, sel = _cmp_attn_call(qa, kcmp, vcmp, slopes[0])
        o_slc = _flash_call(qa, ks, vs, slopes=slopes[0], mask=sel, mask_block=SEL_BLOCK, mask_t=True,
                            tq=256, tk=256, hpc=2, skip_tiles=True, out_dtype=F32, name="nsa_slc")
        o_win = _flash_call(qa, kw, vw, slopes=slopes[0], window=NSA_WINDOW,
                            tq=256, tk=256, hpc=2, out_dtype=F32, name="nsa_win")
        o_b = _flash_call(qb, kb, vb, slopes=slopes[1], window=SWA_WINDOW, sinks=sinks_pad[l][None],
                          tq=256, tk=256, name="swa")
        o_c = _flash_call(qc, kc, vc, slopes=slopes[2], mask=mc, mask_block=MOBA_BLOCK,
                          tq=512, tk=512, name="moba")
        o_d = _flash_call(qd, kd, vd, tq=512, tk=512, name="mla")
        xn, h2, route = _merge_call(
            x, (sc1, sh1, g1, sc2, sh2), norm_gain[l], w_gate[l], w_branch_b[l], w_out_b[l],
            o_cmp, o_slc, o_win, gs, o_b, o_c, o_d, w_router[l], b_router[l][None])
        route = route.reshape(B * S, 128)
        y2 = _moe(h2, route, w13, w2, l)
        x = _combine_call(xn, g2, route, y2)
    return x
```
